```python
import jax, jax.numpy as jnp
from jax import lax
import numpy as np

D_MODEL = 1024
BATCH = 8
SEQ = 4096
DEPTH = 2
DEC_BATCH = 128
DEC_SEQ = 1
PAST_LEN = 16384
PAGE_SIZE = 128

MIX_WIDTH = D_MODEL
ATTN_WIDTH = MIX_WIDTH // 2
GM_WIDTH = MIX_WIDTH - ATTN_WIDTH
HEAD_DIM = 64
N_HEADS = ATTN_WIDTH // HEAD_DIM
N_KV_HEADS = 2
KV_GROUP = N_HEADS // N_KV_HEADS
KV_WIDTH = N_KV_HEADS * HEAD_DIM
WINDOW = 128
ATTN_BLOCK = WINDOW
ROPE_THETA = 10000.0
CHUNK = 128
GM_GROUPS = 4
GM_HEAD = GM_WIDTH // GM_GROUPS
D_FF = 2816
N_EXPERTS = 8
TOP_K = 2
N_ADA = 6
EPS = 1e-6
NEG_INF = -1e30
IN_COLS = ATTN_WIDTH + 2 * KV_WIDTH + 2 * GM_WIDTH
N_DENSE = (DEPTH + 1) // 2
N_MOE = DEPTH // 2

kernel_name = "hymba_swa_sink_gmlp_moe_decode_step"


def rms_norm(x, w):
    xf = x.astype(jnp.float32)
    y = xf * lax.rsqrt(jnp.mean(xf * xf, axis=-1, keepdims=True) + EPS)
    return (y * w.astype(jnp.float32)).astype(x.dtype)


def rope(x, pos):
    inv = ROPE_THETA ** (-jnp.arange(0, HEAD_DIM, 2, dtype=jnp.float32) / HEAD_DIM)
    ang = pos[:, None] * inv[None, :]
    cos = jnp.concatenate([jnp.cos(ang), jnp.cos(ang)], axis=-1)[:, None, :]
    sin = jnp.concatenate([jnp.sin(ang), jnp.sin(ang)], axis=-1)[:, None, :]
    xf = x.astype(jnp.float32)
    x1, x2 = xf[..., :HEAD_DIM // 2], xf[..., HEAD_DIM // 2:]
    rot = jnp.concatenate([-x2, x1], axis=-1)
    return (xf * cos + rot * sin).astype(x.dtype)


def adaln(c, w, b):
    m = jax.nn.silu(c) @ w + b
    return [t[:, None, :] for t in jnp.split(m, N_ADA, axis=-1)]


def project(h, w_in, qn, kn, gmn, pos):
    b, s = h.shape[:2]
    z = h @ w_in
    q, k, v, gm = jnp.split(z, [ATTN_WIDTH, ATTN_WIDTH + KV_WIDTH, ATTN_WIDTH + 2 * KV_WIDTH], axis=-1)
    q = rope(rms_norm(q.reshape(b, s, N_HEADS, HEAD_DIM), qn), pos)
    k = rope(rms_norm(k.reshape(b, s, N_KV_HEADS, HEAD_DIM), kn), pos)
    v = v.reshape(b, s, N_KV_HEADS, HEAD_DIM)
    u, gv = jnp.split(jax.nn.gelu(gm), 2, axis=-1)
    gv = rms_norm(gv, gmn)
    return q, k, v, u, gv


def sink_attention(q, keys, vals, mask, sinks):
    s = jnp.einsum('bnqhgd,bnkhd->bnhgqk', q, keys).astype(jnp.float32) * (HEAD_DIM ** -0.5)
    s = jnp.where(mask[None, :, None, None], s, NEG_INF)
    sink = jnp.broadcast_to(sinks.astype(jnp.float32).reshape(1, 1, N_KV_HEADS, KV_GROUP, 1, 1),
                            s.shape[:-1] + (1,))
    p = jax.nn.softmax(jnp.concatenate([s, sink], axis=-1), axis=-1)[..., :-1]
    return jnp.einsum('bnhgqk,bnkhd->bnqhgd', p.astype(vals.dtype), vals)


def window_attention_prompt(q, k, v, sinks):
    b, s = q.shape[:2]
    nb = s // ATTN_BLOCK
    qb = q.reshape(b, nb, ATTN_BLOCK, N_KV_HEADS, KV_GROUP, HEAD_DIM)
    kb = k.reshape(b, nb, ATTN_BLOCK, N_KV_HEADS, HEAD_DIM)
    vb = v.reshape(b, nb, ATTN_BLOCK, N_KV_HEADS, HEAD_DIM)
    pad_k = jnp.zeros_like(kb[:, :1])
    pad_v = jnp.zeros_like(vb[:, :1])
    keys = jnp.concatenate([jnp.concatenate([pad_k, kb[:, :-1]], axis=1), kb], axis=2)
    vals = jnp.concatenate([jnp.concatenate([pad_v, vb[:, :-1]], axis=1), vb], axis=2)
    i = jnp.arange(ATTN_BLOCK)[:, None]
    j = jnp.arange(2 * ATTN_BLOCK)[None, :]
    diff = i + ATTN_BLOCK - j
    band = (diff >= 0) & (diff < WINDOW)
    key_pos = jnp.arange(nb)[:, None, None] * ATTN_BLOCK + j[None] - ATTN_BLOCK
    mask = band[None] & (key_pos >= 0)
    o = sink_attention(qb, keys, vals, mask, sinks)
    return o.reshape(b, s, ATTN_WIDTH)


def window_attention_sample(q, k, v, cache_k, cache_v, sinks):
    b, t = q.shape[:2]
    w = cache_k.shape[1]
    keys = jnp.concatenate([cache_k, k], axis=1)
    vals = jnp.concatenate([cache_v, v], axis=1)
    q_pos = PAST_LEN + jnp.arange(t)
    k_pos = PAST_LEN - w + jnp.arange(w + t)
    diff = q_pos[:, None] - k_pos[None, :]
    mask = ((diff >= 0) & (diff < WINDOW))[None]
    o = sink_attention(q.reshape(b, 1, t, N_KV_HEADS, KV_GROUP, HEAD_DIM),
                       keys[:, None], vals[:, None], mask, sinks)
    return o.reshape(b, t, ATTN_WIDTH), keys[:, -w:], vals[:, -w:]


def spatial_gate_prompt(u, gv, ws, bs):
    b, s = u.shape[:2]
    nc = s // CHUNK
    wm = ws * jnp.tril(jnp.ones((CHUNK, CHUNK), ws.dtype))
    vc = gv.reshape(b, nc, CHUNK, GM_GROUPS, GM_HEAD)
    sp = jnp.einsum('gts,bcsgd->bctgd', wm, vc) + bs.T[None, None, :, :, None]
    return u * sp.reshape(b, s, GM_WIDTH)


def spatial_gate_sample(u, gv, ws, bs):
    b, t = u.shape[:2]
    wm = ws[:, :t, :t] * jnp.tril(jnp.ones((t, t), ws.dtype))
    vc = gv.reshape(b, t, GM_GROUPS, GM_HEAD)
    sp = jnp.einsum('gts,bsgd->btgd', wm, vc) + bs[:, :t].T[None, :, :, None]
    return u * sp.reshape(b, t, GM_WIDTH)


def swiglu(x, wg, wu, wd):
    return (jax.nn.silu(x @ wg) * (x @ wu)) @ wd


def moe_ffn(x, router_w, router_b, wg, wu, wd):
    shp = x.shape
    xf = x.reshape(-1, D_MODEL)
    logits = (xf @ router_w).astype(jnp.float32) + router_b.astype(jnp.float32)
    probs = jax.nn.softmax(logits, axis=-1)
    top_p, top_i = lax.top_k(probs, TOP_K)
    top_p = top_p / jnp.sum(top_p, axis=-1, keepdims=True)
    gates = jnp.sum(jax.nn.one_hot(top_i, N_EXPERTS, dtype=jnp.float32) * top_p[..., None], axis=1)
    y = jnp.zeros_like(xf)
    for e in range(N_EXPERTS):
        y = y + gates[:, e:e + 1].astype(xf.dtype) * swiglu(xf, wg[e], wu[e], wd[e])
    return y.reshape(shp)


def setup_inputs(seed: int = 0) -> dict:
    key = jax.random.key(seed)
    ks = jax.random.split(key, 26)
    f32 = jnp.float32

    def nrm(k, shape, scale):
        return scale * jax.random.normal(k, shape, f32)

    kv_rows = min(WINDOW, PAST_LEN)
    return {
        "x_prompt": nrm(ks[0], (BATCH, SEQ, D_MODEL), 1.0),
        "x_sample": nrm(ks[1], (DEC_BATCH, DEC_SEQ, D_MODEL), 1.0),
        "cache_k": nrm(ks[2], (DEPTH, DEC_BATCH, kv_rows, N_KV_HEADS, HEAD_DIM), 1.0),
        "cache_v": nrm(ks[3], (DEPTH, DEC_BATCH, kv_rows, N_KV_HEADS, HEAD_DIM), 1.0),
        "c_prompt": nrm(ks[4], (BATCH, D_MODEL), 1.0),
        "c_sample": nrm(ks[5], (DEC_BATCH, D_MODEL), 1.0),
        "w_ada": nrm(ks[6], (DEPTH, D_MODEL, N_ADA * D_MODEL), 0.5 * D_MODEL ** -0.5),
        "b_ada": nrm(ks[7], (DEPTH, N_ADA * D_MODEL), 0.02),
        "norm1_w": 1.0 + nrm(ks[8], (DEPTH, D_MODEL), 0.1),
        "norm2_w": 1.0 + nrm(ks[9], (DEPTH, D_MODEL), 0.1),
        "w_in": nrm(ks[10], (DEPTH, D_MODEL, IN_COLS), D_MODEL ** -0.5),
        "q_norm_w": 1.0 + nrm(ks[11], (DEPTH, HEAD_DIM), 0.1),
        "k_norm_w": 1.0 + nrm(ks[12], (DEPTH, HEAD_DIM), 0.1),
        "attn_sinks": nrm(ks[13], (DEPTH, N_HEADS), 0.5),
        "gm_norm_w": 1.0 + nrm(ks[14], (DEPTH, GM_WIDTH), 0.1),
        "gm_ws": nrm(ks[15], (DEPTH, GM_GROUPS, CHUNK, CHUNK), CHUNK ** -0.5),
        "gm_bs": 1.0 + nrm(ks[16], (DEPTH, GM_GROUPS, CHUNK), 0.1),
        "w_out": nrm(ks[17], (DEPTH, MIX_WIDTH, D_MODEL), MIX_WIDTH ** -0.5),
        "dense_w_gate": nrm(ks[18], (N_DENSE, D_MODEL, D_FF), D_MODEL ** -0.5),
        "dense_w_up": nrm(ks[19], (N_DENSE, D_MODEL, D_FF), D_MODEL ** -0.5),
        "dense_w_down": nrm(ks[20], (N_DENSE, D_FF, D_MODEL), D_FF ** -0.5),
        "router_w": nrm(ks[21], (N_MOE, D_MODEL, N_EXPERTS), D_MODEL ** -0.5),
        "router_b": nrm(ks[22], (N_MOE, N_EXPERTS), 0.01),
        "moe_w_gate": nrm(ks[23], (N_MOE, N_EXPERTS, D_MODEL, D_FF), D_MODEL ** -0.5),
        "moe_w_up": nrm(ks[24], (N_MOE, N_EXPERTS, D_MODEL, D_FF), D_MODEL ** -0.5),
        "moe_w_down": nrm(ks[25], (N_MOE, N_EXPERTS, D_FF, D_MODEL), D_FF ** -0.5),
    }


def reference(x_prompt, x_sample, cache_k, cache_v, c_prompt, c_sample, w_ada, b_ada, norm1_w, norm2_w,
              w_in, q_norm_w, k_norm_w, attn_sinks, gm_norm_w, gm_ws, gm_bs, w_out,
              dense_w_gate, dense_w_up, dense_w_down, router_w, router_b, moe_w_gate, moe_w_up, moe_w_down):
    pos_p = jnp.arange(SEQ, dtype=jnp.float32)
    pos_s = PAST_LEN + jnp.arange(DEC_SEQ, dtype=jnp.float32)
    kp_rows = min(WINDOW, SEQ)
    gv_start = ((SEQ - 1) // CHUNK) * CHUNK

    def channel_mixer(h, l):
        i = l // 2
        if l % 2 == 0:
            return swiglu(h, dense_w_gate[i], dense_w_up[i], dense_w_down[i])
        return moe_ffn(h, router_w[i], router_b[i], moe_w_gate[i], moe_w_up[i], moe_w_down[i])

    xp, xs = x_prompt, x_sample
    k_p, v_p, g_p, k_s, v_s, g_s = [], [], [], [], [], []
    for l in range(DEPTH):
        sh1, sc1, ga1, sh2, sc2, ga2 = adaln(c_prompt, w_ada[l], b_ada[l])
        h = rms_norm(xp, norm1_w[l]) * (1.0 + sc1) + sh1
        q, k, v, u, gv = project(h, w_in[l], q_norm_w[l], k_norm_w[l], gm_norm_w[l], pos_p)
        mix = jnp.concatenate([window_attention_prompt(q, k, v, attn_sinks[l]),
                               spatial_gate_prompt(u, gv, gm_ws[l], gm_bs[l])], axis=-1)
        xp = xp + ga1 * (mix @ w_out[l])
        h = rms_norm(xp, norm2_w[l]) * (1.0 + sc2) + sh2
        xp = xp + ga2 * channel_mixer(h, l)
        k_p.append(k[:, SEQ - kp_rows:])
        v_p.append(v[:, SEQ - kp_rows:])
        g_p.append(gv[:, gv_start:])

        sh1, sc1, ga1, sh2, sc2, ga2 = adaln(c_sample, w_ada[l], b_ada[l])
        h = rms_norm(xs, norm1_w[l]) * (1.0 + sc1) + sh1
        q, k, v, u, gv = project(h, w_in[l], q_norm_w[l], k_norm_w[l], gm_norm_w[l], pos_s)
        attn_o, new_k, new_v = window_attention_sample(q, k, v, cache_k[l], cache_v[l], attn_sinks[l])
        mix = jnp.concatenate([attn_o, spatial_gate_sample(u, gv, gm_ws[l], gm_bs[l])], axis=-1)
        xs = xs + ga1 * (mix @ w_out[l])
        h = rms_norm(xs, norm2_w[l]) * (1.0 + sc2) + sh2
        xs = xs + ga2 * channel_mixer(h, l)
        k_s.append(new_k)
        v_s.append(new_v)
        g_s.append(gv)

    return (xp, xs, jnp.stack(k_p), jnp.stack(v_p), jnp.stack(g_p),
            jnp.stack(k_s), jnp.stack(v_s), jnp.stack(g_s))
```

```python
import functools

import numpy as np
import jax
import jax.numpy as jnp
from jax import lax
from jax.experimental import pallas as pl
from jax.experimental.pallas import tpu as pltpu

D_MODEL = 1024
HEAD_DIM = 64
N_HEADS = 8
N_KV_HEADS = 2
KV_GROUP = N_HEADS // N_KV_HEADS
ATTN_WIDTH = N_HEADS * HEAD_DIM
KV_WIDTH = N_KV_HEADS * HEAD_DIM
GM_WIDTH = 512
GM_GROUPS = 4
WINDOW = 128
CHUNK = 128
D_FF = 2816
FF_HALF = D_FF // 2
N_EXPERTS = 8
N_ADA = 6
IN_COLS = ATTN_WIDTH + 2 * KV_WIDTH + 2 * GM_WIDTH
PAST_LEN = 16384
ROPE_THETA = 10000.0
EPS = 1e-6
NEG_INF = -1e30
LANES = 128

BF = jnp.bfloat16
F32 = jnp.float32
MIB = 1024 * 1024


def _params(sem, vmem_mib):
    return pltpu.CompilerParams(dimension_semantics=sem, vmem_limit_bytes=vmem_mib * MIB)


def _dot(a, b):
    return jnp.dot(a, b, preferred_element_type=F32)


def _dot_nt(a, b):
    return lax.dot_general(a, b, (((1,), (1,)), ((), ())), preferred_element_type=F32)


def _rms(x, w):
    ms = jnp.mean(x * x, axis=-1, keepdims=True)
    return x * lax.rsqrt(ms + EPS) * w


def _ada_body(c_ref, w_ref, b_ref, o_ref):
    s = jax.nn.silu(c_ref[...]).astype(BF)
    o_ref[...] = _dot(s, w_ref[...].astype(BF)) + b_ref[...]


def _ada(c_all, w_ada, b_ada):
    depth, d, cols = w_ada.shape
    n = c_all.shape[0]
    tn = 1024
    return pl.pallas_call(
        _ada_body,
        grid=(depth, cols // tn),
        in_specs=[
            pl.BlockSpec((n, d), lambda l, j: (0, 0)),
            pl.BlockSpec((None, d, tn), lambda l, j: (l, 0, j)),
            pl.BlockSpec((None, 1, tn), lambda l, j: (l, 0, j)),
        ],
        out_specs=pl.BlockSpec((None, n, tn), lambda l, j: (l, 0, j)),
        out_shape=jax.ShapeDtypeStruct((depth, n, cols), F32),
        compiler_params=_params(("arbitrary", "arbitrary"), 32),
        name="ada",
    )(c_all, w_ada, b_ada.reshape(depth, 1, cols))


def _swap_halves(t):
    n = t.shape[-1]
    lane = lax.broadcasted_iota(jnp.int32, (1, n), 1)
    first = (lane % HEAD_DIM) < (HEAD_DIM // 2)
    return jnp.where(first, pltpu.roll(t, n - HEAD_DIM // 2, axis=1), pltpu.roll(t, HEAD_DIM // 2, axis=1))


def _inproj_compute(x, sh, sc, n1, w_ref, qn, kn, gmn, seg_ref, cos, sin):
    h = _rms(x, n1) * (1.0 + sc) + sh
    z = _dot(h.astype(BF), w_ref[...])
    q = z[:, :ATTN_WIDTH]
    k = z[:, ATTN_WIDTH:ATTN_WIDTH + KV_WIDTH]
    v = z[:, ATTN_WIDTH + KV_WIDTH:ATTN_WIDTH + 2 * KV_WIDTH]
    gm = z[:, ATTN_WIDTH + 2 * KV_WIDTH:]

    def head_norm(t, seg, wn):
        ms = _dot((t * t).astype(BF), seg) * (1.0 / HEAD_DIM)
        return t * lax.rsqrt(ms + EPS) * wn

    def rope(t):
        reps = t.shape[-1] // LANES
        c = jnp.concatenate([cos] * reps, axis=-1) if reps > 1 else cos
        s = jnp.concatenate([sin] * reps, axis=-1) if reps > 1 else sin
        return t * c + _swap_halves(t) * s

    q = rope(head_norm(q, seg_ref[...], qn)) * (HEAD_DIM ** -0.5)
    k = rope(head_norm(k, seg_ref[:KV_WIDTH, :KV_WIDTH], kn))
    g = jax.nn.gelu(gm)
    u = g[:, :GM_WIDTH]
    gv = _rms(g[:, GM_WIDTH:], gmn)
    return q, k, v, u, gv


def _dup_heads(t):
    lane = lax.broadcasted_iota(jnp.int32, (1, LANES), 1)
    lo = lane < HEAD_DIM
    r = pltpu.roll(t, HEAD_DIM, axis=1)
    return jnp.concatenate([jnp.where(lo, t, r), jnp.where(lo, r, t)], axis=-1)


def _inproj_p_body(tiles_per_batch, x_ref, mod_ref, n1_ref, w_ref, qn_ref, kn_ref, gmn_ref, seg_ref,
                   cos_ref, sin_ref, q_ref, kd_ref, vd_ref, u_ref, gv_ref, kl_ref, vl_ref, gvl_ref):
    i = pl.program_id(0)
    b = i // tiles_per_batch
    sh = mod_ref[0, pl.ds(b, 1), :]
    sc = mod_ref[1, pl.ds(b, 1), :]
    q, k, v, u, gv = _inproj_compute(x_ref[...], sh, sc, n1_ref[...], w_ref, qn_ref[...], kn_ref[...],
                                     gmn_ref[...], seg_ref, cos_ref[...], sin_ref[...])
    q_ref[...] = q.astype(BF)
    kd_ref[...] = _dup_heads(k).astype(BF)
    vd_ref[...] = _dup_heads(v).astype(BF)
    u_ref[...] = u.astype(BF)
    gv_ref[...] = gv.astype(BF)

    @pl.when(i % tiles_per_batch == tiles_per_batch - 1)
    def _():
        r = k.shape[0]
        kl_ref[...] = k[r - WINDOW:, :]
        vl_ref[...] = v[r - WINDOW:, :]
        gvl_ref[...] = gv[r - CHUNK:, :]


def _inproj_p(x, mod, n1, w_bf, qn, kn, gmn, seg, cos, sin, batch, seq):
    t = x.shape[0]
    tm = min(512, seq)
    tpb = seq // tm
    row = lambda i: (i, 0)
    full = lambda i: (0, 0)
    last = lambda i: (i // tpb, 0, 0)
    return pl.pallas_call(
        functools.partial(_inproj_p_body, tpb),
        grid=(t // tm,),
        in_specs=[
            pl.BlockSpec((tm, D_MODEL), row),
            pl.BlockSpec((N_ADA, batch, D_MODEL), lambda i: (0, 0, 0)),
            pl.BlockSpec((1, D_MODEL), full),
            pl.BlockSpec((D_MODEL, IN_COLS), full),
            pl.BlockSpec((1, ATTN_WIDTH), full),
            pl.BlockSpec((1, KV_WIDTH), full),
            pl.BlockSpec((1, GM_WIDTH), full),
            pl.BlockSpec((ATTN_WIDTH, ATTN_WIDTH), full),
            pl.BlockSpec((tm, LANES), lambda i: (i % tpb, 0)),
            pl.BlockSpec((tm, LANES), lambda i: (i % tpb, 0)),
        ],
        out_specs=[
            pl.BlockSpec((tm, ATTN_WIDTH), row),
            pl.BlockSpec((tm, 2 * KV_WIDTH), row),
            pl.BlockSpec((tm, 2 * KV_WIDTH), row),
            pl.BlockSpec((tm, GM_WIDTH), row),
            pl.BlockSpec((tm, GM_WIDTH), row),
            pl.BlockSpec((None, WINDOW, KV_WIDTH), last),
            pl.BlockSpec((None, WINDOW, KV_WIDTH), last),
            pl.BlockSpec((None, CHUNK, GM_WIDTH), last),
        ],
        out_shape=[
            jax.ShapeDtypeStruct((t, ATTN_WIDTH), BF),
            jax.ShapeDtypeStruct((t, 2 * KV_WIDTH), BF),
            jax.ShapeDtypeStruct((t, 2 * KV_WIDTH), BF),
            jax.ShapeDtypeStruct((t, GM_WIDTH), BF),
            jax.ShapeDtypeStruct((t, GM_WIDTH), BF),
            jax.ShapeDtypeStruct((batch, WINDOW, KV_WIDTH), F32),
            jax.ShapeDtypeStruct((batch, WINDOW, KV_WIDTH), F32),
            jax.ShapeDtypeStruct((batch, CHUNK, GM_WIDTH), F32),
        ],
        compiler_params=_params(("arbitrary",), 48),
        name="inproj_p",
    )(x, mod, n1, w_bf, qn, kn, gmn, seg, cos, sin)


def _inproj_s_body(x_ref, mod_ref, n1_ref, w_ref, qn_ref, kn_ref, gmn_ref, seg_ref, cos_ref, sin_ref,
                   q_ref, k_ref, v_ref, u_ref, gv_ref):
    q, k, v, u, gv = _inproj_compute(x_ref[...], mod_ref[0], mod_ref[1], n1_ref[...], w_ref, qn_ref[...],
                                     kn_ref[...], gmn_ref[...], seg_ref, cos_ref[...], sin_ref[...])
    q_ref[...] = q
    k_ref[...] = k
    v_ref[...] = v
    u_ref[...] = u
    gv_ref[...] = gv


def _inproj_s(x, mod, n1, w_bf, qn, kn, gmn, seg, cos, sin):
    n = x.shape[0]
    widths = (ATTN_WIDTH, KV_WIDTH, KV_WIDTH, GM_WIDTH, GM_WIDTH)
    return pl.pallas_call(
        _inproj_s_body,
        out_shape=[jax.ShapeDtypeStruct((n, w), F32) for w in widths],
        compiler_params=pltpu.CompilerParams(vmem_limit_bytes=48 * MIB),
        name="inproj_s",
    )(x, mod, n1, w_bf, qn, kn, gmn, seg, cos, sin)


def _mix_p_body(nblk, q_ref, kc_ref, kp_ref, vc_ref, vp_ref, u_ref, gv_ref, ws_ref, bst_ref, sink_ref, o_ref):
    i = pl.program_id(1)
    blk = WINDOW
    lane = lax.broadcasted_iota(jnp.int32, (1, LANES), 1)
    lo = lane < HEAD_DIM
    rows = KV_GROUP * blk
    iq = lax.broadcasted_iota(jnp.int32, (rows, 2 * blk), 0) % blk
    jk = lax.broadcasted_iota(jnp.int32, (rows, 2 * blk), 1)
    band = (jk > iq) & (jk <= iq + blk)
    band_first = band & ((jk >= blk) | (i > 0))
    tri = (lax.broadcasted_iota(jnp.int32, (CHUNK, CHUNK), 0)
           >= lax.broadcasted_iota(jnp.int32, (CHUNK, CHUNK), 1))
    wm = [jnp.where(tri, ws_ref[g], 0.0).astype(BF) for g in range(GM_GROUPS)]

    for n in range(nblk):
        r0 = n * blk
        if n == 0:
            kk = jnp.concatenate([kp_ref[...], kc_ref[0:blk, :]], axis=0)
            vv = jnp.concatenate([vp_ref[...], vc_ref[0:blk, :]], axis=0)
            mask = band_first
        else:
            kk = kc_ref[r0 - blk:r0 + blk, :]
            vv = vc_ref[r0 - blk:r0 + blk, :]
            mask = band
        for kvh in range(N_KV_HEADS):
            c0 = 2 * kvh
            qa = q_ref[r0:r0 + blk, c0 * LANES:(c0 + 1) * LANES]
            qb = q_ref[r0:r0 + blk, (c0 + 1) * LANES:(c0 + 2) * LANES]
            zero = jnp.zeros_like(qa)
            qq = jnp.concatenate([jnp.where(lo, qa, zero), jnp.where(lo, zero, qa),
                                  jnp.where(lo, qb, zero), jnp.where(lo, zero, qb)], axis=0)
            s = _dot_nt(qq, kk[:, kvh * LANES:(kvh + 1) * LANES])
            s = jnp.where(mask, s, NEG_INF)
            sink = jnp.concatenate(
                [jnp.full((blk, 1), sink_ref[kvh * KV_GROUP + g], F32) for g in range(KV_GROUP)], axis=0)
            m = jnp.maximum(jnp.max(s, axis=-1, keepdims=True), sink)
            p = jnp.exp(s - m)
            den = jnp.sum(p, axis=-1, keepdims=True) + jnp.exp(sink - m)
            o = _dot(p.astype(BF), vv[:, kvh * LANES:(kvh + 1) * LANES]) * (1.0 / den)
            o_ref[r0:r0 + blk, c0 * LANES:(c0 + 1) * LANES] = jnp.where(
                lo, o[0:blk], o[blk:2 * blk]).astype(BF)
            o_ref[r0:r0 + blk, (c0 + 1) * LANES:(c0 + 2) * LANES] = jnp.where(
                lo, o[2 * blk:3 * blk], o[3 * blk:4 * blk]).astype(BF)
        for g in range(GM_GROUPS):
            cs = slice(g * LANES, (g + 1) * LANES)
            sp = _dot(wm[g], gv_ref[r0:r0 + blk, cs]) + bst_ref[:, g:g + 1]
            o_ref[r0:r0 + blk, ATTN_WIDTH + g * LANES:ATTN_WIDTH + (g + 1) * LANES] = (
                u_ref[r0:r0 + blk, cs].astype(F32) * sp).astype(BF)


def _mix_p(q, kd, vd, u, gv, ws, bst, sinks, batch, seq):
    t = q.shape[0]
    tq = min(512, seq)
    nblk = tq // WINDOW
    tpb = seq // tq
    cur = lambda b, i: (b * tpb + i, 0)
    prev = lambda b, i: (jnp.maximum((b * tpb + i) * nblk - 1, b * tpb * nblk), 0)
    return pl.pallas_call(
        functools.partial(_mix_p_body, nblk),
        grid=(batch, tpb),
        in_specs=[
            pl.BlockSpec((tq, ATTN_WIDTH), cur),
            pl.BlockSpec((tq, 2 * KV_WIDTH), cur),
            pl.BlockSpec((WINDOW, 2 * KV_WIDTH), prev),
            pl.BlockSpec((tq, 2 * KV_WIDTH), cur),
            pl.BlockSpec((WINDOW, 2 * KV_WIDTH), prev),
            pl.BlockSpec((tq, GM_WIDTH), cur),
            pl.BlockSpec((tq, GM_WIDTH), cur),
            pl.BlockSpec((GM_GROUPS, CHUNK, CHUNK), lambda b, i: (0, 0, 0)),
            pl.BlockSpec((CHUNK, GM_GROUPS), lambda b, i: (0, 0)),
            pl.BlockSpec(memory_space=pltpu.SMEM),
        ],
        out_specs=pl.BlockSpec((tq, D_MODEL), cur),
        out_shape=jax.ShapeDtypeStruct((t, D_MODEL), BF),
        compiler_params=_params(("arbitrary", "arbitrary"), 48),
        name="mix_p",
    )(q, kd, kd, vd, vd, u, gv, ws, bst, sinks)


def _attn_s_body(q_ref, kn_ref, vn_ref, ck_ref, cv_ref, sink_ref, o_ref, nk_ref, nv_ref):
    w = ck_ref.shape[1]
    row = lax.broadcasted_iota(jnp.int32, (1, w, 1), 1)
    nk = jnp.where(row == w - 1, kn_ref[...], pltpu.roll(ck_ref[...], w - 1, axis=1))
    nv = jnp.where(row == w - 1, vn_ref[...], pltpu.roll(cv_ref[...], w - 1, axis=1))
    nk_ref[...] = nk
    nv_ref[...] = nv
    s = jnp.einsum('bhd,bjd->bhj', q_ref[...].astype(BF), nk.astype(BF), preferred_element_type=F32)
    sink = sink_ref[...][None, :, 0:1]
    m = jnp.maximum(jnp.max(s, axis=-1, keepdims=True), sink)
    p = jnp.exp(s - m)
    den = jnp.sum(p, axis=-1, keepdims=True) + jnp.exp(sink - m)
    o = jnp.einsum('bhj,bjd->bhd', p.astype(BF), nv.astype(BF), preferred_element_type=F32)
    o_ref[...] = o * (1.0 / den)


def _attn_s(qpad, k_new, v_new, ck, cv, sink_tile):
    n, w, kw = ck.shape
    bb = min(16, n)
    blk3 = lambda r, c: pl.BlockSpec((bb, r, c), lambda i: (i, 0, 0))
    return pl.pallas_call(
        _attn_s_body,
        grid=(n // bb,),
        in_specs=[blk3(N_HEADS, LANES), blk3(1, kw), blk3(1, kw), blk3(w, kw), blk3(w, kw),
                  pl.BlockSpec((N_HEADS, LANES), lambda i: (0, 0))],
        out_specs=[blk3(N_HEADS, LANES), blk3(w, kw), blk3(w, kw)],
        out_shape=[jax.ShapeDtypeStruct((n, N_HEADS, LANES), F32),
                   jax.ShapeDtypeStruct((n, w, kw), F32),
                   jax.ShapeDtypeStruct((n, w, kw), F32)],
        compiler_params=_params(("arbitrary",), 32),
        name="attn_s",
    )(qpad, k_new, v_new, ck, cv, sink_tile)


def _route(h2, rw_ref, rb_ref):
    hi = h2.astype(BF)
    lo = (h2 - hi.astype(F32)).astype(BF)
    rw = rw_ref[...]
    whi = rw.astype(BF)
    wlo = (rw - whi.astype(F32)).astype(BF)
    logits = _dot(hi, whi) + _dot(lo, whi) + _dot(hi, wlo) + rb_ref[...]
    lane = lax.broadcasted_iota(jnp.int32, logits.shape, 1).astype(F32)
    e = jnp.exp(logits - jnp.max(logits, axis=-1, keepdims=True))
    p = e / jnp.sum(e, axis=-1, keepdims=True)
    m1 = jnp.max(p, axis=-1, keepdims=True)
    i1 = jnp.min(jnp.where(p == m1, lane, float(LANES)), axis=-1, keepdims=True)
    p2 = jnp.where(lane == i1, -1.0, p)
    m2 = jnp.max(p2, axis=-1, keepdims=True)
    i2 = jnp.min(jnp.where(p2 == m2, lane, float(LANES)), axis=-1, keepdims=True)
    tot = m1 + m2
    return jnp.where(lane == i1, m1 / tot, 0.0) + jnp.where(lane == i2, m2 / tot, 0.0)


def _outproj_compute(mix_bf, x, ga1, sh2, sc2, w_ref, n2):
    xn = x + ga1 * _dot(mix_bf, w_ref[...])
    h2 = _rms(xn, n2) * (1.0 + sc2) + sh2
    return xn, h2


def _outproj_p_body(tiles_per_batch, with_router, mix_ref, x_ref, mod_ref, w_ref, n2_ref, *rest):
    b = pl.program_id(0) // tiles_per_batch
    mrow = lambda j: mod_ref[j, pl.ds(b, 1), :]
    xn, h2 = _outproj_compute(mix_ref[...], x_ref[...], mrow(2), mrow(3), mrow(4), w_ref, n2_ref[...])
    if with_router:
        rw_ref, rb_ref, xn_ref, h2_ref, gates_ref = rest
        gates_ref[...] = _route(h2, rw_ref, rb_ref)
    else:
        xn_ref, h2_ref = rest
    xn_ref[...] = xn
    h2_ref[...] = h2.astype(BF)


def _outproj_p(mix, x, mod, w_bf, n2, router, batch, seq):
    t = x.shape[0]
    tm = min(512, seq)
    tpb = seq // tm
    row = lambda i: (i, 0)
    full = lambda i: (0, 0)
    in_specs = [
        pl.BlockSpec((tm, D_MODEL), row),
        pl.BlockSpec((tm, D_MODEL), row),
        pl.BlockSpec((N_ADA, batch, D_MODEL), lambda i: (0, 0, 0)),
        pl.BlockSpec((D_MODEL, D_MODEL), full),
        pl.BlockSpec((1, D_MODEL), full),
    ]
    out_specs = [pl.BlockSpec((tm, D_MODEL), row), pl.BlockSpec((tm, D_MODEL), row)]
    out_shape = [jax.ShapeDtypeStruct((t, D_MODEL), F32), jax.ShapeDtypeStruct((t, D_MODEL), BF)]
    args = [mix, x, mod, w_bf, n2]
    if router is not None:
        in_specs += [pl.BlockSpec((D_MODEL, LANES), full), pl.BlockSpec((1, LANES), full)]
        out_specs.append(pl.BlockSpec((tm, LANES), row))
        out_shape.append(jax.ShapeDtypeStruct((t, LANES), F32))
        args += list(router)
    return pl.pallas_call(
        functools.partial(_outproj_p_body, tpb, router is not None),
        grid=(t // tm,),
        in_specs=in_specs, out_specs=out_specs, out_shape=out_shape,
        compiler_params=_params(("arbitrary",), 48),
        name="outproj_p",
    )(*args)


def _outproj_s_body(with_router, o_ref, u_ref, gv_ref, wdiag_ref, bsrow_ref, x_ref, mod_ref, w_ref, n2_ref, *rest):
    gate = u_ref[...] * (wdiag_ref[...] * gv_ref[...] + bsrow_ref[...])
    mix = jnp.concatenate([o_ref[...], gate], axis=-1).astype(BF)
    xn, h2 = _outproj_compute(mix, x_ref[...], mod_ref[2], mod_ref[3], mod_ref[4], w_ref, n2_ref[...])
    if with_router:
        rw_ref, rb_ref, xn_ref, h2_ref, gates_ref = rest
        gates_ref[...] = _route(h2, rw_ref, rb_ref)
    else:
        xn_ref, h2_ref = rest
    xn_ref[...] = xn
    h2_ref[...] = h2.astype(BF)


def _outproj_s(o, u, gv, wdiag, bsrow, x, mod, w_bf, n2, router):
    n = x.shape[0]
    out_shape = [jax.ShapeDtypeStruct((n, D_MODEL), F32), jax.ShapeDtypeStruct((n, D_MODEL), BF)]
    args = [o, u, gv, wdiag, bsrow, x, mod, w_bf, n2]
    if router is not None:
        out_shape.append(jax.ShapeDtypeStruct((n, LANES), F32))
        args += list(router)
    return pl.pallas_call(
        functools.partial(_outproj_s_body, router is not None),
        out_shape=out_shape,
        compiler_params=pltpu.CompilerParams(vmem_limit_bytes=32 * MIB),
        name="outproj_s",
    )(*args)


def _swiglu(h_bf, wg_ref, wu_ref, wd_ref):
    y = None
    for c in range(2):
        sl = slice(c * FF_HALF, (c + 1) * FF_HALF)
        a = (jax.nn.silu(_dot(h_bf, wg_ref[:, sl])) * _dot(h_bf, wu_ref[:, sl])).astype(BF)
        part = _dot(a, wd_ref[sl, :])
        y = part if y is None else y + part
    return y


def _ffn_body(tiles_per_batch, h_ref, x_ref, mod_ref, wg_ref, wu_ref, wd_ref, o_ref):
    if tiles_per_batch:
        b = pl.program_id(0) // tiles_per_batch
        ga2 = mod_ref[5, pl.ds(b, 1), :]
    else:
        ga2 = mod_ref[5]
    o_ref[...] = x_ref[...] + ga2 * _swiglu(h_ref[...], wg_ref, wu_ref, wd_ref)


def _ffn(h2, x, mod, wg, wu, wd, seq):
    t = x.shape[0]
    tm = min(512, t if seq is None else seq)
    tpb = 0 if seq is None else seq // tm
    row = lambda i: (i, 0)
    full = lambda i: (0, 0)
    return pl.pallas_call(
        functools.partial(_ffn_body, tpb),
        grid=(t // tm,),
        in_specs=[
            pl.BlockSpec((tm, D_MODEL), row),
            pl.BlockSpec((tm, D_MODEL), row),
            pl.BlockSpec(mod.shape, lambda i: (0, 0, 0)),
            pl.BlockSpec((D_MODEL, D_FF), full),
            pl.BlockSpec((D_MODEL, D_FF), full),
            pl.BlockSpec((D_FF, D_MODEL), full),
        ],
        out_specs=pl.BlockSpec((tm, D_MODEL), row),
        out_shape=jax.ShapeDtypeStruct((t, D_MODEL), F32),
        compiler_params=_params(("arbitrary",), 56),
        name="ffn",
    )(h2, x, mod, wg, wu, wd)


def _moe_all_body(tiles_per_batch, h_ref, x_ref, gates_ref, mod_ref, wg_ref, wu_ref, wd_ref, o_ref):
    e = pl.program_id(1)
    if tiles_per_batch:
        b = pl.program_id(0) // tiles_per_batch
        ga2 = mod_ref[5, pl.ds(b, 1), :]
    else:
        ga2 = mod_ref[5]

    @pl.when(e == 0)
    def _():
        o_ref[...] = jnp.zeros_like(o_ref)

    lane = lax.broadcasted_iota(jnp.int32, (1, LANES), 1)
    gate = jnp.sum(jnp.where(lane == e, gates_ref[...], 0.0), axis=-1, keepdims=True)
    o_ref[...] += gate * _swiglu(h_ref[...], wg_ref, wu_ref, wd_ref)

    @pl.when(e == N_EXPERTS - 1)
    def _():
        o_ref[...] = x_ref[...] + ga2 * o_ref[...]


def _moe_all(h2, x, gates, mod, wg, wu, wd, seq):
    t = x.shape[0]
    tm = min(512, t if seq is None else seq)
    tpb = 0 if seq is None else seq // tm
    row = lambda i, e: (i, 0)
    return pl.pallas_call(
        functools.partial(_moe_all_body, tpb),
        grid=(t // tm, N_EXPERTS),
        in_specs=[
            pl.BlockSpec((tm, D_MODEL), row),
            pl.BlockSpec((tm, D_MODEL), row),
            pl.BlockSpec((tm, LANES), row),
            pl.BlockSpec(mod.shape, lambda i, e: (0, 0, 0)),
            pl.BlockSpec((None, D_MODEL, D_FF), lambda i, e: (e, 0, 0)),
            pl.BlockSpec((None, D_MODEL, D_FF), lambda i, e: (e, 0, 0)),
            pl.BlockSpec((None, D_FF, D_MODEL), lambda i, e: (e, 0, 0)),
        ],
        out_specs=pl.BlockSpec((tm, D_MODEL), row),
        out_shape=jax.ShapeDtypeStruct((t, D_MODEL), F32),
        compiler_params=_params(("arbitrary", "arbitrary"), 56),
        name="moe_all",
    )(h2, x, gates, mod, wg, wu, wd)


def _rope_tables(pos):
    inv = ROPE_THETA ** (-np.arange(0, HEAD_DIM, 2, dtype=np.float64) / HEAD_DIM)
    ang = np.asarray(pos, np.float64)[:, None] * inv[None, :]
    cos = np.concatenate([np.cos(ang), np.cos(ang)], axis=-1)
    sin = np.concatenate([-np.sin(ang), np.sin(ang)], axis=-1)
    reps = LANES // HEAD_DIM
    return (jnp.asarray(np.tile(cos, (1, reps)), F32), jnp.asarray(np.tile(sin, (1, reps)), F32))


def kernel(x_prompt, x_sample, cache_k, cache_v, c_prompt, c_sample, w_ada, b_ada, norm1_w, norm2_w, w_in,
           q_norm_w, k_norm_w, attn_sinks, gm_norm_w, gm_ws, gm_bs, w_out, dense_w_gate, dense_w_up,
           dense_w_down, router_w, router_b, moe_w_gate, moe_w_up, moe_w_down):
    batch, seq, d = x_prompt.shape
    nd = x_sample.shape[0]
    depth = w_in.shape[0]
    t = batch * seq

    mod = _ada(jnp.concatenate([c_prompt, c_sample], axis=0), w_ada, b_ada)
    mod_p = mod[:, :batch].reshape(depth, batch, N_ADA, d).transpose(0, 2, 1, 3)
    mod_s = mod[:, batch:].reshape(depth, nd, N_ADA, d).transpose(0, 2, 1, 3)

    cos_p, sin_p = _rope_tables(np.arange(seq))
    cos_s, sin_s = _rope_tables(np.array([PAST_LEN]))
    head_of = np.arange(ATTN_WIDTH) // HEAD_DIM
    seg = jnp.asarray(head_of[:, None] == head_of[None, :], BF)
    kv_of_head = (jnp.arange(N_HEADS) // KV_GROUP)[None, :, None]

    w_in_bf = w_in.astype(BF)
    w_out_bf = w_out.astype(BF)
    dense_bf = [w.astype(BF) for w in (dense_w_gate, dense_w_up, dense_w_down)]
    moe_bf = [w.astype(BF) for w in (moe_w_gate, moe_w_up, moe_w_down)]
    router_w_pad = jnp.pad(router_w, ((0, 0), (0, 0), (0, LANES - N_EXPERTS)))
    router_b_pad = jnp.pad(router_b, ((0, 0), (0, LANES - N_EXPERTS)), constant_values=NEG_INF)

    xp = x_prompt.reshape(t, d)
    xs = x_sample.reshape(nd, d)
    k_p, v_p, g_p, k_s, v_s, g_s = [], [], [], [], [], []
    for l in range(depth):
        i = l // 2
        n1 = norm1_w[l][None, :]
        n2 = norm2_w[l][None, :]
        qn = jnp.tile(q_norm_w[l], N_HEADS)[None, :]
        kn = jnp.tile(k_norm_w[l], N_KV_HEADS)[None, :]
        gmn = gm_norm_w[l][None, :]
        router = None if l % 2 == 0 else (router_w_pad[i], router_b_pad[i][None, :])

        q, kd, vd, u, gv, kl, vl, gvl = _inproj_p(xp, mod_p[l], n1, w_in_bf[l], qn, kn, gmn, seg,
                                                  cos_p, sin_p, batch, seq)
        mix = _mix_p(q, kd, vd, u, gv, gm_ws[l], gm_bs[l].T, attn_sinks[l], batch, seq)
        res = _outproj_p(mix, xp, mod_p[l], w_out_bf[l], n2, router, batch, seq)
        if router is None:
            xp = _ffn(res[1], res[0], mod_p[l], dense_bf[0][i], dense_bf[1][i], dense_bf[2][i], seq)
        else:
            xp = _moe_all(res[1], res[0], res[2], mod_p[l], moe_bf[0][i], moe_bf[1][i], moe_bf[2][i], seq)
        k_p.append(kl.reshape(batch, WINDOW, N_KV_HEADS, HEAD_DIM))
        v_p.append(vl.reshape(batch, WINDOW, N_KV_HEADS, HEAD_DIM))
        g_p.append(gvl)

        q, k, v, u, gv = _inproj_s(xs, mod_s[l], n1, w_in_bf[l], qn, kn, gmn, seg, cos_s, sin_s)
        qh = q.reshape(nd, N_HEADS, HEAD_DIM)
        zq = jnp.zeros_like(qh)
        qpad = jnp.where(kv_of_head == 0, jnp.concatenate([qh, zq], -1), jnp.concatenate([zq, qh], -1))
        w = cache_k.shape[2]
        o, nk, nv = _attn_s(qpad, k[:, None, :], v[:, None, :], cache_k[l].reshape(nd, w, KV_WIDTH),
                            cache_v[l].reshape(nd, w, KV_WIDTH),
                            jnp.broadcast_to(attn_sinks[l][:, None], (N_HEADS, LANES)))
        o = jnp.where(kv_of_head == 0, o[..., :HEAD_DIM], o[..., HEAD_DIM:]).reshape(nd, ATTN_WIDTH)
        wdiag = jnp.repeat(gm_ws[l][:, 0, 0], GM_WIDTH // GM_GROUPS)[None, :]
        bsrow = jnp.repeat(gm_bs[l][:, 0], GM_WIDTH // GM_GROUPS)[None, :]
        res = _outproj_s(o, u, gv, wdiag, bsrow, xs, mod_s[l], w_out_bf[l], n2, router)
        if router is None:
            xs = _ffn(res[1], res[0], mod_s[l], dense_bf[0][i], dense_bf[1][i], dense_bf[2][i], None)
        else:
            xs = _moe_all(res[1], res[0], res[2], mod_s[l], moe_bf[0][i], moe_bf[1][i], moe_bf[2][i], None)
        k_s.append(nk.reshape(nd, w, N_KV_HEADS, HEAD_DIM))
        v_s.append(nv.reshape(nd, w, N_KV_HEADS, HEAD_DIM))
        g_s.append(gv[:, None, :])

    return (xp.reshape(batch, seq, d), xs.reshape(nd, 1, d), jnp.stack(k_p), jnp.stack(v_p), jnp.stack(g_p),
            jnp.stack(k_s), jnp.stack(v_s), jnp.stack(g_s))
```

```python
import functools

import numpy as np
import jax
import jax.numpy as jnp
from jax import lax
from jax.experimental import pallas as pl
from jax.experimental.pallas import tpu as pltpu

D_MODEL = 1024
HEAD_DIM = 64
N_HEADS = 8
N_KV_HEADS = 2
KV_GROUP = N_HEADS // N_KV_HEADS
ATTN_WIDTH = N_HEADS * HEAD_DIM
KV_WIDTH = N_KV_HEADS * HEAD_DIM
GM_WIDTH = 512
GM_GROUPS = 4
WINDOW = 128
CHUNK = 128
D_FF = 2816
FF_HALF = D_FF // 2
N_EXPERTS = 8
N_ADA = 6
IN_COLS = ATTN_WIDTH + 2 * KV_WIDTH + 2 * GM_WIDTH
PAST_LEN = 16384
ROPE_THETA = 10000.0
EPS = 1e-6
NEG_INF = -1e30
LANES = 128

BF = jnp.bfloat16
F32 = jnp.float32
MIB = 1024 * 1024


def _params(sem, vmem_mib):
    return pltpu.CompilerParams(dimension_semantics=sem, vmem_limit_bytes=vmem_mib * MIB)


def _dot(a, b):
    return jnp.dot(a, b, preferred_element_type=F32)


def _dot_nt(a, b):
    return lax.dot_general(a, b, (((1,), (1,)), ((), ())), preferred_element_type=F32)


def _rms(x, w):
    ms = jnp.mean(x * x, axis=-1, keepdims=True)
    return x * lax.rsqrt(ms + EPS) * w


def _ada_body(c_ref, w_ref, b_ref, o_ref):
    s = jax.nn.silu(c_ref[...]).astype(BF)
    o_ref[...] = _dot(s, w_ref[...].astype(BF)) + b_ref[...]


def _ada(c_all, w_ada, b_ada):
    depth, d, cols = w_ada.shape
    n = c_all.shape[0]
    tn = 1024
    return pl.pallas_call(
        _ada_body,
        grid=(depth, cols // tn),
        in_specs=[
            pl.BlockSpec((n, d), lambda l, j: (0, 0)),
            pl.BlockSpec((None, d, tn), lambda l, j: (l, 0, j)),
            pl.BlockSpec((None, 1, tn), lambda l, j: (l, 0, j)),
        ],
        out_specs=pl.BlockSpec((None, n, tn), lambda l, j: (l, 0, j)),
        out_shape=jax.ShapeDtypeStruct((depth, n, cols), F32),
        compiler_params=_params(("arbitrary", "arbitrary"), 32),
        name="ada",
    )(c_all, w_ada, b_ada.reshape(depth, 1, cols))


def _swap_halves(t):
    n = t.shape[-1]
    lane = lax.broadcasted_iota(jnp.int32, (1, n), 1)
    first = (lane % HEAD_DIM) < (HEAD_DIM // 2)
    return jnp.where(first, pltpu.roll(t, n - HEAD_DIM // 2, axis=1), pltpu.roll(t, HEAD_DIM // 2, axis=1))


def _inproj_compute(x, sh, sc, n1, w_ref, qn, kn, gmn, seg_ref, cos, sin):
    h = _rms(x, n1) * (1.0 + sc) + sh
    z = _dot(h.astype(BF), w_ref[...])
    q = z[:, :ATTN_WIDTH]
    k = z[:, ATTN_WIDTH:ATTN_WIDTH + KV_WIDTH]
    v = z[:, ATTN_WIDTH + KV_WIDTH:ATTN_WIDTH + 2 * KV_WIDTH]
    gm = z[:, ATTN_WIDTH + 2 * KV_WIDTH:]

    def head_norm(t, seg, wn):
        ms = _dot((t * t).astype(BF), seg) * (1.0 / HEAD_DIM)
        return t * lax.rsqrt(ms + EPS) * wn

    def rope(t):
        reps = t.shape[-1] // LANES
        c = jnp.concatenate([cos] * reps, axis=-1) if reps > 1 else cos
        s = jnp.concatenate([sin] * reps, axis=-1) if reps > 1 else sin
        return t * c + _swap_halves(t) * s

    q = rope(head_norm(q, seg_ref[...], qn)) * (HEAD_DIM ** -0.5)
    k = rope(head_norm(k, seg_ref[:KV_WIDTH, :KV_WIDTH], kn))
    g = jax.nn.gelu(gm)
    u = g[:, :GM_WIDTH]
    gv = _rms(g[:, GM_WIDTH:], gmn)
    return q, k, v, u, gv


def _dup_heads(t):
    lane = lax.broadcasted_iota(jnp.int32, (1, LANES), 1)
    lo = lane < HEAD_DIM
    r = pltpu.roll(t, HEAD_DIM, axis=1)
    return jnp.concatenate([jnp.where(lo, t, r), jnp.where(lo, r, t)], axis=-1)


def _inproj_p_body(tiles_per_batch, x_ref, mod_ref, n1_ref, w_ref, qn_ref, kn_ref, gmn_ref, seg_ref,
                   cos_ref, sin_ref, q_ref, kd_ref, vd_ref, u_ref, gv_ref, kl_ref, vl_ref, gvl_ref):
    i = pl.program_id(0)
    b = i // tiles_per_batch
    sh = mod_ref[0, pl.ds(b, 1), :]
    sc = mod_ref[1, pl.ds(b, 1), :]
    q, k, v, u, gv = _inproj_compute(x_ref[...], sh, sc, n1_ref[...], w_ref, qn_ref[...], kn_ref[...],
                                     gmn_ref[...], seg_ref, cos_ref[...], sin_ref[...])
    q_ref[...] = q.astype(BF)
    kd_ref[...] = _dup_heads(k).astype(BF)
    vd_ref[...] = _dup_heads(v).astype(BF)
    u_ref[...] = u.astype(BF)
    gv_ref[...] = gv.astype(BF)

    @pl.when(i % tiles_per_batch == tiles_per_batch - 1)
    def _():
        r = k.shape[0]
        kl_ref[...] = k[r - WINDOW:, :]
        vl_ref[...] = v[r - WINDOW:, :]
        gvl_ref[...] = gv[r - CHUNK:, :]


def _inproj_p(x, mod, n1, w_bf, qn, kn, gmn, seg, cos, sin, batch, seq):
    t = x.shape[0]
    tm = min(512, seq)
    tpb = seq // tm
    row = lambda i: (i, 0)
    full = lambda i: (0, 0)
    last = lambda i: (i // tpb, 0, 0)
    return pl.pallas_call(
        functools.partial(_inproj_p_body, tpb),
        grid=(t // tm,),
        in_specs=[
            pl.BlockSpec((tm, D_MODEL), row),
            pl.BlockSpec((N_ADA, batch, D_MODEL), lambda i: (0, 0, 0)),
            pl.BlockSpec((1, D_MODEL), full),
            pl.BlockSpec((D_MODEL, IN_COLS), full),
            pl.BlockSpec((1, ATTN_WIDTH), full),
            pl.BlockSpec((1, KV_WIDTH), full),
            pl.BlockSpec((1, GM_WIDTH), full),
            pl.BlockSpec((ATTN_WIDTH, ATTN_WIDTH), full),
            pl.BlockSpec((tm, LANES), lambda i: (i % tpb, 0)),
            pl.BlockSpec((tm, LANES), lambda i: (i % tpb, 0)),
        ],
        out_specs=[
            pl.BlockSpec((tm, ATTN_WIDTH), row),
            pl.BlockSpec((tm, 2 * KV_WIDTH), row),
            pl.BlockSpec((tm, 2 * KV_WIDTH), row),
            pl.BlockSpec((tm, GM_WIDTH), row),
            pl.BlockSpec((tm, GM_WIDTH), row),
            pl.BlockSpec((None, WINDOW, KV_WIDTH), last),
            pl.BlockSpec((None, WINDOW, KV_WIDTH), last),
            pl.BlockSpec((None, CHUNK, GM_WIDTH), last),
        ],
        out_shape=[
            jax.ShapeDtypeStruct((t, ATTN_WIDTH), BF),
            jax.ShapeDtypeStruct((t, 2 * KV_WIDTH), BF),
            jax.ShapeDtypeStruct((t, 2 * KV_WIDTH), BF),
            jax.ShapeDtypeStruct((t, GM_WIDTH), BF),
            jax.ShapeDtypeStruct((t, GM_WIDTH), BF),
            jax.ShapeDtypeStruct((batch, WINDOW, KV_WIDTH), F32),
            jax.ShapeDtypeStruct((batch, WINDOW, KV_WIDTH), F32),
            jax.ShapeDtypeStruct((batch, CHUNK, GM_WIDTH), F32),
        ],
        compiler_params=_params(("arbitrary",), 48),
        name="inproj_p",
    )(x, mod, n1, w_bf, qn, kn, gmn, seg, cos, sin)


def _inproj_s_body(x_ref, mod_ref, n1_ref, w_ref, qn_ref, kn_ref, gmn_ref, seg_ref, cos_ref, sin_ref,
                   q_ref, k_ref, v_ref, u_ref, gv_ref):
    q, k, v, u, gv = _inproj_compute(x_ref[...], mod_ref[0], mod_ref[1], n1_ref[...], w_ref, qn_ref[...],
                                     kn_ref[...], gmn_ref[...], seg_ref, cos_ref[...], sin_ref[...])
    q_ref[...] = q
    k_ref[...] = k
    v_ref[...] = v
    u_ref[...] = u
    gv_ref[...] = gv


def _inproj_s(x, mod, n1, w_bf, qn, kn, gmn, seg, cos, sin):
    n = x.shape[0]
    widths = (ATTN_WIDTH, KV_WIDTH, KV_WIDTH, GM_WIDTH, GM_WIDTH)
    return pl.pallas_call(
        _inproj_s_body,
        out_shape=[jax.ShapeDtypeStruct((n, w), F32) for w in widths],
        compiler_params=pltpu.CompilerParams(vmem_limit_bytes=48 * MIB),
        name="inproj_s",
    )(x, mod, n1, w_bf, qn, kn, gmn, seg, cos, sin)


def _mix_p_body(nblk, q_ref, kc_ref, kp_ref, vc_ref, vp_ref, u_ref, gv_ref, ws_ref, bst_ref, sink_ref, o_ref):
    i = pl.program_id(1)
    blk = WINDOW
    lane = lax.broadcasted_iota(jnp.int32, (1, LANES), 1)
    lo = lane < HEAD_DIM
    rows = KV_GROUP * blk
    iq = lax.broadcasted_iota(jnp.int32, (rows, 2 * blk), 0) % blk
    jk = lax.broadcasted_iota(jnp.int32, (rows, 2 * blk), 1)
    band = (jk > iq) & (jk <= iq + blk)
    band_first = band & ((jk >= blk) | (i > 0))
    tri = (lax.broadcasted_iota(jnp.int32, (CHUNK, CHUNK), 0)
           >= lax.broadcasted_iota(jnp.int32, (CHUNK, CHUNK), 1))
    wm = [jnp.where(tri, ws_ref[g], 0.0).astype(BF) for g in range(GM_GROUPS)]

    for n in range(nblk):
        r0 = n * blk
        if n == 0:
            kk = jnp.concatenate([kp_ref[...], kc_ref[0:blk, :]], axis=0)
            vv = jnp.concatenate([vp_ref[...], vc_ref[0:blk, :]], axis=0)
            mask = band_first
        else:
            kk = kc_ref[r0 - blk:r0 + blk, :]
            vv = vc_ref[r0 - blk:r0 + blk, :]
            mask = band
        for kvh in range(N_KV_HEADS):
            c0 = 2 * kvh
            qa = q_ref[r0:r0 + blk, c0 * LANES:(c0 + 1) * LANES]
            qb = q_ref[r0:r0 + blk, (c0 + 1) * LANES:(c0 + 2) * LANES]
            zero = jnp.zeros_like(qa)
            qq = jnp.concatenate([jnp.where(lo, qa, zero), jnp.where(lo, zero, qa),
                                  jnp.where(lo, qb, zero), jnp.where(lo, zero, qb)], axis=0)
            s = _dot_nt(qq, kk[:, kvh * LANES:(kvh + 1) * LANES])
            s = jnp.where(mask, s, NEG_INF)
            sink = jnp.concatenate(
                [jnp.full((blk, 1), sink_ref[kvh * KV_GROUP + g], F32) for g in range(KV_GROUP)], axis=0)
            m = jnp.maximum(jnp.max(s, axis=-1, keepdims=True), sink)
            p = jnp.exp(s - m)
            den = jnp.sum(p, axis=-1, keepdims=True) + jnp.exp(sink - m)
            o = _dot(p.astype(BF), vv[:, kvh * LANES:(kvh + 1) * LANES]) * (1.0 / den)
            o_ref[r0:r0 + blk, c0 * LANES:(c0 + 1) * LANES] = jnp.where(
                lo, o[0:blk], o[blk:2 * blk]).astype(BF)
            o_ref[r0:r0 + blk, (c0 + 1) * LANES:(c0 + 2) * LANES] = jnp.where(
                lo, o[2 * blk:3 * blk], o[3 * blk:4 * blk]).astype(BF)
        for g in range(GM_GROUPS):
            cs = slice(g * LANES, (g + 1) * LANES)
            sp = _dot(wm[g], gv_ref[r0:r0 + blk, cs]) + bst_ref[:, g:g + 1]
            o_ref[r0:r0 + blk, ATTN_WIDTH + g * LANES:ATTN_WIDTH + (g + 1) * LANES] = (
                u_ref[r0:r0 + blk, cs].astype(F32) * sp).astype(BF)


def _mix_p(q, kd, vd, u, gv, ws, bst, sinks, batch, seq):
    t = q.shape[0]
    tq = min(512, seq)
    nblk = tq // WINDOW
    tpb = seq // tq
    cur = lambda b, i: (b * tpb + i, 0)
    prev = lambda b, i: (jnp.maximum((b * tpb + i) * nblk - 1, b * tpb * nblk), 0)
    return pl.pallas_call(
        functools.partial(_mix_p_body, nblk),
        grid=(batch, tpb),
        in_specs=[
            pl.BlockSpec((tq, ATTN_WIDTH), cur),
            pl.BlockSpec((tq, 2 * KV_WIDTH), cur),
            pl.BlockSpec((WINDOW, 2 * KV_WIDTH), prev),
            pl.BlockSpec((tq, 2 * KV_WIDTH), cur),
            pl.BlockSpec((WINDOW, 2 * KV_WIDTH), prev),
            pl.BlockSpec((tq, GM_WIDTH), cur),
            pl.BlockSpec((tq, GM_WIDTH), cur),
            pl.BlockSpec((GM_GROUPS, CHUNK, CHUNK), lambda b, i: (0, 0, 0)),
            pl.BlockSpec((CHUNK, GM_GROUPS), lambda b, i: (0, 0)),
            pl.BlockSpec(memory_space=pltpu.SMEM),
        ],
        out_specs=pl.BlockSpec((tq, D_MODEL), cur),
        out_shape=jax.ShapeDtypeStruct((t, D_MODEL), BF),
        compiler_params=_params(("arbitrary", "arbitrary"), 48),
        name="mix_p",
    )(q, kd, kd, vd, vd, u, gv, ws, bst, sinks)


def _attn_s_body(q_ref, kn_ref, vn_ref, ck_ref, cv_ref, sink_ref, o_ref, nk_ref, nv_ref):
    w = ck_ref.shape[1]
    row = lax.broadcasted_iota(jnp.int32, (1, w, 1), 1)
    nk = jnp.where(row == w - 1, kn_ref[...], pltpu.roll(ck_ref[...], w - 1, axis=1))
    nv = jnp.where(row == w - 1, vn_ref[...], pltpu.roll(cv_ref[...], w - 1, axis=1))
    nk_ref[...] = nk
    nv_ref[...] = nv
    s = jnp.einsum('bhd,bjd->bhj', q_ref[...].astype(BF), nk.astype(BF), preferred_element_type=F32)
    sink = sink_ref[...][None, :, 0:1]
    m = jnp.maximum(jnp.max(s, axis=-1, keepdims=True), sink)
    p = jnp.exp(s - m)
    den = jnp.sum(p, axis=-1, keepdims=True) + jnp.exp(sink - m)
    o = jnp.einsum('bhj,bjd->bhd', p.astype(BF), nv.astype(BF), preferred_element_type=F32)
    o_ref[...] = o * (1.0 / den)


def _attn_s(qpad, k_new, v_new, ck, cv, sink_tile):
    n, w, kw = ck.shape
    bb = min(16, n)
    blk3 = lambda r, c: pl.BlockSpec((bb, r, c), lambda i: (i, 0, 0))
    return pl.pallas_call(
        _attn_s_body,
        grid=(n // bb,),
        in_specs=[blk3(N_HEADS, LANES), blk3(1, kw), blk3(1, kw), blk3(w, kw), blk3(w, kw),
                  pl.BlockSpec((N_HEADS, LANES), lambda i: (0, 0))],
        out_specs=[blk3(N_HEADS, LANES), blk3(w, kw), blk3(w, kw)],
        out_shape=[jax.ShapeDtypeStruct((n, N_HEADS, LANES), F32),
                   jax.ShapeDtypeStruct((n, w, kw), F32),
                   jax.ShapeDtypeStruct((n, w, kw), F32)],
        compiler_params=_params(("arbitrary",), 32),
        name="attn_s",
    )(qpad, k_new, v_new, ck, cv, sink_tile)


def _top2(h2, rw_ref, rb_ref):
    hi = h2.astype(BF)
    lo = (h2 - hi.astype(F32)).astype(BF)
    rw = rw_ref[...]
    whi = rw.astype(BF)
    wlo = (rw - whi.astype(F32)).astype(BF)
    logits = _dot(hi, whi) + _dot(lo, whi) + _dot(hi, wlo) + rb_ref[...]
    lane = lax.broadcasted_iota(jnp.int32, logits.shape, 1).astype(F32)
    e = jnp.exp(logits - jnp.max(logits, axis=-1, keepdims=True))
    p = e / jnp.sum(e, axis=-1, keepdims=True)
    m1 = jnp.max(p, axis=-1, keepdims=True)
    i1 = jnp.min(jnp.where(p == m1, lane, float(LANES)), axis=-1, keepdims=True)
    p2 = jnp.where(lane == i1, -1.0, p)
    m2 = jnp.max(p2, axis=-1, keepdims=True)
    i2 = jnp.min(jnp.where(p2 == m2, lane, float(LANES)), axis=-1, keepdims=True)
    tot = m1 + m2
    return lane, i1, i2, m1 / tot, m2 / tot


def _route_gates(h2, rw_ref, rb_ref):
    lane, i1, i2, g1, g2 = _top2(h2, rw_ref, rb_ref)
    return jnp.where(lane == i1, g1, 0.0) + jnp.where(lane == i2, g2, 0.0)


ROUTE_E, ROUTE_RANK, ROUTE_GATE = 0, 2, 4


def _route_ranked(h2, rw_ref, rb_ref, tri_ref, cnt_ref):
    lane, i1, i2, g1, g2 = _top2(h2, rw_ref, rb_ref)
    oh1 = lane == i1
    oh2 = lane == i2
    hit = jnp.where(oh1, 1.0, 0.0) + jnp.where(oh2, 1.0, 0.0)
    before = cnt_ref[...] + _dot(tri_ref[...], hit.astype(BF))
    r1 = jnp.sum(jnp.where(oh1, before, 0.0), axis=-1, keepdims=True)
    r2 = jnp.sum(jnp.where(oh2, before, 0.0), axis=-1, keepdims=True)
    cnt_ref[...] += jnp.sum(hit, axis=0, keepdims=True)
    cols = (i1, i2, r1, r2, g1, g2)
    out = jnp.zeros_like(lane)
    for j, c in enumerate(cols):
        out = jnp.where(lane == float(j), c, out)
    return out


def _outproj_compute(mix_bf, x, ga1, sh2, sc2, w_ref, n2):
    xn = x + ga1 * _dot(mix_bf, w_ref[...])
    h2 = _rms(xn, n2) * (1.0 + sc2) + sh2
    return xn, h2


def _outproj_p_body(tiles_per_batch, with_router, mix_ref, x_ref, mod_ref, w_ref, n2_ref, *rest):
    i = pl.program_id(0)
    b = i // tiles_per_batch
    mrow = lambda j: mod_ref[j, pl.ds(b, 1), :]
    xn, h2 = _outproj_compute(mix_ref[...], x_ref[...], mrow(2), mrow(3), mrow(4), w_ref, n2_ref[...])
    if with_router:
        rw_ref, rb_ref, tri_ref, xn_ref, h2_ref, route_ref, cnt_ref = rest

        @pl.when(i == 0)
        def _():
            cnt_ref[...] = jnp.zeros_like(cnt_ref)

        route_ref[...] = _route_ranked(h2, rw_ref, rb_ref, tri_ref, cnt_ref)
        h2_ref[...] = h2
    else:
        xn_ref, h2_ref = rest
        h2_ref[...] = h2.astype(BF)
    xn_ref[...] = xn


def _outproj_p(mix, x, mod, w_bf, n2, router, batch, seq):
    t = x.shape[0]
    tm = min(512, seq)
    tpb = seq // tm
    row = lambda i: (i, 0)
    full = lambda i: (0, 0)
    in_specs = [
        pl.BlockSpec((tm, D_MODEL), row),
        pl.BlockSpec((tm, D_MODEL), row),
        pl.BlockSpec((N_ADA, batch, D_MODEL), lambda i: (0, 0, 0)),
        pl.BlockSpec((D_MODEL, D_MODEL), full),
        pl.BlockSpec((1, D_MODEL), full),
    ]
    out_specs = [pl.BlockSpec((tm, D_MODEL), row), pl.BlockSpec((tm, D_MODEL), row)]
    out_shape = [jax.ShapeDtypeStruct((t, D_MODEL), F32),
                 jax.ShapeDtypeStruct((t, D_MODEL), BF if router is None else F32)]
    args = [mix, x, mod, w_bf, n2]
    if router is not None:
        tri = jnp.asarray(np.tri(tm, k=-1), BF)
        in_specs += [pl.BlockSpec((D_MODEL, LANES), full), pl.BlockSpec((1, LANES), full),
                     pl.BlockSpec((tm, tm), full)]
        out_specs += [pl.BlockSpec((tm, LANES), row), pl.BlockSpec((1, LANES), full)]
        out_shape += [jax.ShapeDtypeStruct((t, LANES), F32), jax.ShapeDtypeStruct((1, LANES), F32)]
        args += list(router) + [tri]
    return pl.pallas_call(
        functools.partial(_outproj_p_body, tpb, router is not None),
        grid=(t // tm,),
        in_specs=in_specs, out_specs=out_specs, out_shape=out_shape,
        compiler_params=_params(("arbitrary",), 48),
        name="outproj_p",
    )(*args)


def _outproj_s_body(with_router, o_ref, u_ref, gv_ref, wdiag_ref, bsrow_ref, x_ref, mod_ref, w_ref, n2_ref, *rest):
    gate = u_ref[...] * (wdiag_ref[...] * gv_ref[...] + bsrow_ref[...])
    mix = jnp.concatenate([o_ref[...], gate], axis=-1).astype(BF)
    xn, h2 = _outproj_compute(mix, x_ref[...], mod_ref[2], mod_ref[3], mod_ref[4], w_ref, n2_ref[...])
    if with_router:
        rw_ref, rb_ref, xn_ref, h2_ref, gates_ref = rest
        gates_ref[...] = _route_gates(h2, rw_ref, rb_ref)
    else:
        xn_ref, h2_ref = rest
    xn_ref[...] = xn
    h2_ref[...] = h2.astype(BF)


def _outproj_s(o, u, gv, wdiag, bsrow, x, mod, w_bf, n2, router):
    n = x.shape[0]
    out_shape = [jax.ShapeDtypeStruct((n, D_MODEL), F32), jax.ShapeDtypeStruct((n, D_MODEL), BF)]
    args = [o, u, gv, wdiag, bsrow, x, mod, w_bf, n2]
    if router is not None:
        out_shape.append(jax.ShapeDtypeStruct((n, LANES), F32))
        args += list(router)
    return pl.pallas_call(
        functools.partial(_outproj_s_body, router is not None),
        out_shape=out_shape,
        compiler_params=pltpu.CompilerParams(vmem_limit_bytes=32 * MIB),
        name="outproj_s",
    )(*args)


def _swiglu(h_bf, wg_ref, wu_ref, wd_ref):
    y = None
    for c in range(2):
        sl = slice(c * FF_HALF, (c + 1) * FF_HALF)
        a = (jax.nn.silu(_dot(h_bf, wg_ref[:, sl])) * _dot(h_bf, wu_ref[:, sl])).astype(BF)
        part = _dot(a, wd_ref[sl, :])
        y = part if y is None else y + part
    return y


def _ffn_body(tiles_per_batch, h_ref, x_ref, mod_ref, wg_ref, wu_ref, wd_ref, o_ref):
    if tiles_per_batch:
        b = pl.program_id(0) // tiles_per_batch
        ga2 = mod_ref[5, pl.ds(b, 1), :]
    else:
        ga2 = mod_ref[5]
    o_ref[...] = x_ref[...] + ga2 * _swiglu(h_ref[...], wg_ref, wu_ref, wd_ref)


def _ffn(h2, x, mod, wg, wu, wd, seq):
    t = x.shape[0]
    tm = min(512, t if seq is None else seq)
    tpb = 0 if seq is None else seq // tm
    row = lambda i: (i, 0)
    full = lambda i: (0, 0)
    return pl.pallas_call(
        functools.partial(_ffn_body, tpb),
        grid=(t // tm,),
        in_specs=[
            pl.BlockSpec((tm, D_MODEL), row),
            pl.BlockSpec((tm, D_MODEL), row),
            pl.BlockSpec(mod.shape, lambda i: (0, 0, 0)),
            pl.BlockSpec((D_MODEL, D_FF), full),
            pl.BlockSpec((D_MODEL, D_FF), full),
            pl.BlockSpec((D_FF, D_MODEL), full),
        ],
        out_specs=pl.BlockSpec((tm, D_MODEL), row),
        out_shape=jax.ShapeDtypeStruct((t, D_MODEL), F32),
        compiler_params=_params(("arbitrary",), 56),
        name="ffn",
    )(h2, x, mod, wg, wu, wd)


def _moe_all_body(tiles_per_batch, h_ref, x_ref, gates_ref, mod_ref, wg_ref, wu_ref, wd_ref, o_ref):
    e = pl.program_id(1)
    if tiles_per_batch:
        b = pl.program_id(0) // tiles_per_batch
        ga2 = mod_ref[5, pl.ds(b, 1), :]
    else:
        ga2 = mod_ref[5]

    @pl.when(e == 0)
    def _():
        o_ref[...] = jnp.zeros_like(o_ref)

    lane = lax.broadcasted_iota(jnp.int32, (1, LANES), 1)
    gate = jnp.sum(jnp.where(lane == e, gates_ref[...], 0.0), axis=-1, keepdims=True)
    o_ref[...] += gate * _swiglu(h_ref[...], wg_ref, wu_ref, wd_ref)

    @pl.when(e == N_EXPERTS - 1)
    def _():
        o_ref[...] = x_ref[...] + ga2 * o_ref[...]


def _moe_all(h2, x, gates, mod, wg, wu, wd, seq):
    t = x.shape[0]
    tm = min(512, t if seq is None else seq)
    tpb = 0 if seq is None else seq // tm
    row = lambda i, e: (i, 0)
    return pl.pallas_call(
        functools.partial(_moe_all_body, tpb),
        grid=(t // tm, N_EXPERTS),
        in_specs=[
            pl.BlockSpec((tm, D_MODEL), row),
            pl.BlockSpec((tm, D_MODEL), row),
            pl.BlockSpec((tm, LANES), row),
            pl.BlockSpec(mod.shape, lambda i, e: (0, 0, 0)),
            pl.BlockSpec((None, D_MODEL, D_FF), lambda i, e: (e, 0, 0)),
            pl.BlockSpec((None, D_MODEL, D_FF), lambda i, e: (e, 0, 0)),
            pl.BlockSpec((None, D_FF, D_MODEL), lambda i, e: (e, 0, 0)),
        ],
        out_specs=pl.BlockSpec((tm, D_MODEL), row),
        out_shape=jax.ShapeDtypeStruct((t, D_MODEL), F32),
        compiler_params=_params(("arbitrary", "arbitrary"), 56),
        name="moe_all",
    )(h2, x, gates, mod, wg, wu, wd)


TM_MOE = 512
TD = 256


def _row_copy(src, s, dst, d, sem):
    return pltpu.make_async_copy(src.at[pl.ds(s, 1), :], dst.at[pl.ds(d, 1), :], sem)


def _dispatch_body(pos_ref, pad_ref, h_ref, xs_ref, stage, sem, zsem):
    i = pl.program_id(0)
    n = pl.num_programs(0)
    td = h_ref.shape[0]
    slot = i % 2

    def wait_slot(s):
        for _ in range(2):
            pltpu.make_async_copy(stage.at[s], xs_ref.at[pl.ds(0, td), :], sem.at[s]).wait()

    @pl.when(i >= 2)
    def _():
        wait_slot(slot)

    stage[slot] = h_ref[...]

    def issue(r, c):
        for k in range(2):
            _row_copy(stage.at[slot], r, xs_ref, pos_ref[0, 0, k * td + r], sem.at[slot]).start()
        return c

    lax.fori_loop(0, td, issue, 0, unroll=8)

    @pl.when(i == n - 1)
    def _():
        wait_slot(slot)

        @pl.when(n > 1)
        def _():
            wait_slot(1 - slot)

        stage[0] = jnp.zeros(stage.shape[1:], stage.dtype)
        for e in range(N_EXPERTS):
            lo = pad_ref[0, e]
            hi = pad_ref[1, e]

            def zero_row(r, c):
                _row_copy(stage.at[0], 0, xs_ref, r, zsem).start()
                return c

            def wait_row(r, c):
                _row_copy(stage.at[0], 0, xs_ref, 0, zsem).wait()
                return c

            lax.fori_loop(lo, hi, zero_row, 0)
            lax.fori_loop(lo, hi, wait_row, 0)

        def zero_blk(j, c):
            pltpu.make_async_copy(stage.at[0], xs_ref.at[pl.ds(j * td, td), :], zsem).start()
            return c

        def wait_blk(j, c):
            pltpu.make_async_copy(stage.at[0], xs_ref.at[pl.ds(0, td), :], zsem).wait()
            return c

        lax.fori_loop(pad_ref[0, N_EXPERTS], pad_ref[1, N_EXPERTS], zero_blk, 0)
        lax.fori_loop(pad_ref[0, N_EXPERTS], pad_ref[1, N_EXPERTS], wait_blk, 0)


def _dispatch(h2, pos_t, pad, npad):
    t = h2.shape[0]
    td = min(TD, t)
    return pl.pallas_call(
        _dispatch_body,
        grid=(t // td,),
        in_specs=[
            pl.BlockSpec((1, 1, 2 * td), lambda i: (i, 0, 0), memory_space=pltpu.SMEM),
            pl.BlockSpec(memory_space=pltpu.SMEM),
            pl.BlockSpec((td, D_MODEL), lambda i: (i, 0)),
        ],
        out_specs=pl.BlockSpec(memory_space=pl.ANY),
        out_shape=jax.ShapeDtypeStruct((npad, D_MODEL), F32),
        scratch_shapes=[pltpu.VMEM((2, td, D_MODEL), F32), pltpu.SemaphoreType.DMA((2,)),
                        pltpu.SemaphoreType.DMA(())],
        compiler_params=_params(("arbitrary",), 32),
        name="dispatch",
    )(pos_t, pad, h2)


def _moe_body(te_ref, src_ref, nv_ref, x_ref, wg_ref, wu_ref, wd_ref, o_ref):
    i = pl.program_id(0)

    @pl.when(nv_ref[i] > 0)
    def _():
        o_ref[...] = _swiglu(x_ref[...].astype(BF), wg_ref, wu_ref, wd_ref)

    @pl.when(nv_ref[i] == 0)
    def _():
        o_ref[...] = jnp.zeros_like(o_ref)


def _moe(xs, tile_e, tile_src, tile_nv, wg, wu, wd):
    npad = xs.shape[0]
    wspec = lambda shape: pl.BlockSpec((None,) + shape, lambda i, te, src, nv: (te[i], 0, 0),
                                       pipeline_mode=pl.Buffered(1))
    return pl.pallas_call(
        _moe_body,
        grid_spec=pltpu.PrefetchScalarGridSpec(
            num_scalar_prefetch=3,
            grid=(npad // TM_MOE,),
            in_specs=[
                pl.BlockSpec((TM_MOE, D_MODEL), lambda i, te, src, nv: (src[i], 0)),
                wspec((D_MODEL, D_FF)), wspec((D_MODEL, D_FF)), wspec((D_FF, D_MODEL)),
            ],
            out_specs=pl.BlockSpec((TM_MOE, D_MODEL), lambda i, te, src, nv: (i, 0)),
        ),
        out_shape=jax.ShapeDtypeStruct((npad, D_MODEL), F32),
        compiler_params=_params(("arbitrary",), 56),
        name="moe",
    )(tile_e, tile_src, tile_nv, xs, wg, wu, wd)


def _combine_body(tiles_per_batch, posc_ref, posn_ref, x_ref, route_ref, mod_ref, ys_ref, o_ref, buf, sem):
    i = pl.program_id(0)
    n = pl.num_programs(0)
    tc = x_ref.shape[0]

    def gather(p_ref, s):
        def issue(r, c):
            for k in range(2):
                _row_copy(ys_ref, p_ref[0, 0, k * tc + r], buf.at[s], k * tc + r, sem.at[s]).start()
            return c

        lax.fori_loop(0, tc, issue, 0, unroll=8)

    @pl.when(i == 0)
    def _():
        gather(posc_ref, 0)

    @pl.when(i + 1 < n)
    def _():
        gather(posn_ref, (i + 1) % 2)

    slot = i % 2
    pltpu.make_async_copy(ys_ref.at[pl.ds(0, 2 * tc), :], buf.at[slot], sem.at[slot]).wait()
    lane = lax.broadcasted_iota(jnp.int32, (1, LANES), 1)
    rt = route_ref[...]
    g1 = jnp.sum(jnp.where(lane == ROUTE_GATE, rt, 0.0), axis=-1, keepdims=True)
    g2 = jnp.sum(jnp.where(lane == ROUTE_GATE + 1, rt, 0.0), axis=-1, keepdims=True)
    y = g1 * buf[slot, 0:tc, :] + g2 * buf[slot, tc:2 * tc, :]
    ga2 = mod_ref[5, pl.ds(i // tiles_per_batch, 1), :]
    o_ref[...] = x_ref[...] + ga2 * y


def _combine(ys, pos_t, x, route, mod, seq):
    t = x.shape[0]
    tc = min(TD, t)
    nt = t // tc
    row = lambda i: (i, 0)
    return pl.pallas_call(
        functools.partial(_combine_body, seq // tc),
        grid=(nt,),
        in_specs=[
            pl.BlockSpec((1, 1, 2 * tc), lambda i: (i, 0, 0), memory_space=pltpu.SMEM),
            pl.BlockSpec((1, 1, 2 * tc), lambda i: (jnp.minimum(i + 1, nt - 1), 0, 0), memory_space=pltpu.SMEM),
            pl.BlockSpec((tc, D_MODEL), row),
            pl.BlockSpec((tc, LANES), row),
            pl.BlockSpec(mod.shape, lambda i: (0, 0, 0)),
            pl.BlockSpec(memory_space=pl.ANY),
        ],
        out_specs=pl.BlockSpec((tc, D_MODEL), row),
        out_shape=jax.ShapeDtypeStruct((t, D_MODEL), F32),
        scratch_shapes=[pltpu.VMEM((2, 2 * tc, D_MODEL), F32), pltpu.SemaphoreType.DMA((2,))],
        compiler_params=_params(("arbitrary",), 32),
        name="combine",
    )(pos_t, pos_t, x, route, mod, ys)


def _moe_routed(h2, xn, route, cnt, mod, wg, wu, wd, seq):
    t = h2.shape[0]
    td = min(TD, t)
    nt_max = pl.cdiv(2 * t, TM_MOE) + N_EXPERTS
    npad = nt_max * TM_MOE
    counts = cnt[0, :N_EXPERTS].astype(jnp.int32)
    ntile = (counts + TM_MOE - 1) // TM_MOE
    tile_end = jnp.cumsum(ntile)
    off = (tile_end - ntile) * TM_MOE
    e12 = route[:, ROUTE_E:ROUTE_E + 2].astype(jnp.int32)
    r12 = route[:, ROUTE_RANK:ROUTE_RANK + 2].astype(jnp.int32)
    onehot = e12[:, :, None] == jnp.arange(N_EXPERTS)[None, None, :]
    pos = jnp.sum(jnp.where(onehot, off[None, None, :], 0), axis=-1) + r12
    pos_t = pos.reshape(t // td, td, 2).transpose(0, 2, 1).reshape(t // td, 1, 2 * td)
    total = tile_end[-1]
    tid = jnp.arange(nt_max)
    tile_e = jnp.minimum(jnp.searchsorted(tile_end, tid, side='right'), N_EXPERTS - 1).astype(jnp.int32)
    tile_nv = (tid < total).astype(jnp.int32)
    tile_src = jnp.minimum(tid, total - 1).astype(jnp.int32)
    pad = jnp.stack([jnp.concatenate([off + counts, (total * (TM_MOE // td))[None]]),
                     jnp.concatenate([off + ntile * TM_MOE, jnp.full((1,), npad // td, jnp.int32)])]).astype(jnp.int32)
    xs = _dispatch(h2, pos_t, pad, npad)
    ys = _moe(xs, tile_e, tile_src, tile_nv, wg, wu, wd)
    return _combine(ys, pos_t, xn, route, mod, seq)


def _rope_tables(pos):
    inv = ROPE_THETA ** (-np.arange(0, HEAD_DIM, 2, dtype=np.float64) / HEAD_DIM)
    ang = np.asarray(pos, np.float64)[:, None] * inv[None, :]
    cos = np.concatenate([np.cos(ang), np.cos(ang)], axis=-1)
    sin = np.concatenate([-np.sin(ang), np.sin(ang)], axis=-1)
    reps = LANES // HEAD_DIM
    return (jnp.asarray(np.tile(cos, (1, reps)), F32), jnp.asarray(np.tile(sin, (1, reps)), F32))


def kernel(x_prompt, x_sample, cache_k, cache_v, c_prompt, c_sample, w_ada, b_ada, norm1_w, norm2_w, w_in,
           q_norm_w, k_norm_w, attn_sinks, gm_norm_w, gm_ws, gm_bs, w_out, dense_w_gate, dense_w_up,
           dense_w_down, router_w, router_b, moe_w_gate, moe_w_up, moe_w_down):
    batch, seq, d = x_prompt.shape
    nd = x_sample.shape[0]
    depth = w_in.shape[0]
    t = batch * seq

    mod = _ada(jnp.concatenate([c_prompt, c_sample], axis=0), w_ada, b_ada)
    mod_p = mod[:, :batch].reshape(depth, batch, N_ADA, d).transpose(0, 2, 1, 3)
    mod_s = mod[:, batch:].reshape(depth, nd, N_ADA, d).transpose(0, 2, 1, 3)

    cos_p, sin_p = _rope_tables(np.arange(seq))
    cos_s, sin_s = _rope_tables(np.array([PAST_LEN]))
    head_of = np.arange(ATTN_WIDTH) // HEAD_DIM
    seg = jnp.asarray(head_of[:, None] == head_of[None, :], BF)
    kv_of_head = (jnp.arange(N_HEADS) // KV_GROUP)[None, :, None]

    w_in_bf = w_in.astype(BF)
    w_out_bf = w_out.astype(BF)
    dense_bf = [w.astype(BF) for w in (dense_w_gate, dense_w_up, dense_w_down)]
    moe_bf = [w.astype(BF) for w in (moe_w_gate, moe_w_up, moe_w_down)]
    router_w_pad = jnp.pad(router_w, ((0, 0), (0, 0), (0, LANES - N_EXPERTS)))
    router_b_pad = jnp.pad(router_b, ((0, 0), (0, LANES - N_EXPERTS)), constant_values=NEG_INF)

    xp = x_prompt.reshape(t, d)
    xs = x_sample.reshape(nd, d)
    k_p, v_p, g_p, k_s, v_s, g_s = [], [], [], [], [], []
    for l in range(depth):
        i = l // 2
        n1 = norm1_w[l][None, :]
        n2 = norm2_w[l][None, :]
        qn = jnp.tile(q_norm_w[l], N_HEADS)[None, :]
        kn = jnp.tile(k_norm_w[l], N_KV_HEADS)[None, :]
        gmn = gm_norm_w[l][None, :]
        router = None if l % 2 == 0 else (router_w_pad[i], router_b_pad[i][None, :])

        q, kd, vd, u, gv, kl, vl, gvl = _inproj_p(xp, mod_p[l], n1, w_in_bf[l], qn, kn, gmn, seg,
                                                  cos_p, sin_p, batch, seq)
        mix = _mix_p(q, kd, vd, u, gv, gm_ws[l], gm_bs[l].T, attn_sinks[l], batch, seq)
        res = _outproj_p(mix, xp, mod_p[l], w_out_bf[l], n2, router, batch, seq)
        if router is None:
            xp = _ffn(res[1], res[0], mod_p[l], dense_bf[0][i], dense_bf[1][i], dense_bf[2][i], seq)
        else:
            xn, h2, route, cnt = res
            xp = _moe_routed(h2, xn, route, cnt, mod_p[l], moe_bf[0][i], moe_bf[1][i], moe_bf[2][i], seq)
        k_p.append(kl.reshape(batch, WINDOW, N_KV_HEADS, HEAD_DIM))
        v_p.append(vl.reshape(batch, WINDOW, N_KV_HEADS, HEAD_DIM))
        g_p.append(gvl)

        q, k, v, u, gv = _inproj_s(xs, mod_s[l], n1, w_in_bf[l], qn, kn, gmn, seg, cos_s, sin_s)
        qh = q.reshape(nd, N_HEADS, HEAD_DIM)
        zq = jnp.zeros_like(qh)
        qpad = jnp.where(kv_of_head == 0, jnp.concatenate([qh, zq], -1), jnp.concatenate([zq, qh], -1))
        w = cache_k.shape[2]
        o, nk, nv = _attn_s(qpad, k[:, None, :], v[:, None, :], cache_k[l].reshape(nd, w, KV_WIDTH),
                            cache_v[l].reshape(nd, w, KV_WIDTH),
                            jnp.broadcast_to(attn_sinks[l][:, None], (N_HEADS, LANES)))
        o = jnp.where(kv_of_head == 0, o[..., :HEAD_DIM], o[..., HEAD_DIM:]).reshape(nd, ATTN_WIDTH)
        wdiag = jnp.repeat(gm_ws[l][:, 0, 0], GM_WIDTH // GM_GROUPS)[None, :]
        bsrow = jnp.repeat(gm_bs[l][:, 0], GM_WIDTH // GM_GROUPS)[None, :]
        res = _outproj_s(o, u, gv, wdiag, bsrow, xs, mod_s[l], w_out_bf[l], n2, router)
        if router is None:
            xs = _ffn(res[1], res[0], mod_s[l], dense_bf[0][i], dense_bf[1][i], dense_bf[2][i], None)
        else:
            xs = _moe_all(res[1], res[0], res[2], mod_s[l], moe_bf[0][i], moe_bf[1][i], moe_bf[2][i], None)
        k_s.append(nk.reshape(nd, w, N_KV_HEADS, HEAD_DIM))
        v_s.append(nv.reshape(nd, w, N_KV_HEADS, HEAD_DIM))
        g_s.append(gv[:, None, :])

    return (xp.reshape(batch, seq, d), xs.reshape(nd, 1, d), jnp.stack(k_p), jnp.stack(v_p), jnp.stack(g_p),
            jnp.stack(k_s), jnp.stack(v_s), jnp.stack(g_s))
```

```python
import functools

import numpy as np
import jax
import jax.numpy as jnp
from jax import lax
from jax.experimental import pallas as pl
from jax.experimental.pallas import tpu as pltpu

D_MODEL = 1024
HEAD_DIM = 64
N_HEADS = 8
N_KV_HEADS = 2
KV_GROUP = N_HEADS // N_KV_HEADS
ATTN_WIDTH = N_HEADS * HEAD_DIM
KV_WIDTH = N_KV_HEADS * HEAD_DIM
GM_WIDTH = 512
GM_GROUPS = 4
WINDOW = 128
CHUNK = 128
D_FF = 2816
FF_HALF = D_FF // 2
N_EXPERTS = 8
N_ADA = 6
IN_COLS = ATTN_WIDTH + 2 * KV_WIDTH + 2 * GM_WIDTH
PAST_LEN = 16384
ROPE_THETA = 10000.0
EPS = 1e-6
NEG_INF = -1e30
LANES = 128

BF = jnp.bfloat16
F32 = jnp.float32
MIB = 1024 * 1024


def _params(sem, vmem_mib):
    return pltpu.CompilerParams(dimension_semantics=sem, vmem_limit_bytes=vmem_mib * MIB)


def _dot(a, b):
    return jnp.dot(a, b, preferred_element_type=F32)


def _dot_nt(a, b):
    return lax.dot_general(a, b, (((1,), (1,)), ((), ())), preferred_element_type=F32)


def _rms(x, w):
    ms = jnp.mean(x * x, axis=-1, keepdims=True)
    return x * lax.rsqrt(ms + EPS) * w


def _ada_body(c_ref, w_ref, b_ref, o_ref):
    s = jax.nn.silu(c_ref[...]).astype(BF)
    o_ref[...] = _dot(s, w_ref[...].astype(BF)) + b_ref[...]


def _ada(c_all, w_ada, b_ada):
    depth, d, cols = w_ada.shape
    n = c_all.shape[0]
    tn = 1024
    return pl.pallas_call(
        _ada_body,
        grid=(depth, cols // tn),
        in_specs=[
            pl.BlockSpec((n, d), lambda l, j: (0, 0)),
            pl.BlockSpec((None, d, tn), lambda l, j: (l, 0, j)),
            pl.BlockSpec((None, 1, tn), lambda l, j: (l, 0, j)),
        ],
        out_specs=pl.BlockSpec((None, n, tn), lambda l, j: (l, 0, j)),
        out_shape=jax.ShapeDtypeStruct((depth, n, cols), F32),
        compiler_params=_params(("arbitrary", "arbitrary"), 32),
        name="ada",
    )(c_all, w_ada, b_ada.reshape(depth, 1, cols))


def _swap_halves(t):
    n = t.shape[-1]
    lane = lax.broadcasted_iota(jnp.int32, (1, n), 1)
    first = (lane % HEAD_DIM) < (HEAD_DIM // 2)
    return jnp.where(first, pltpu.roll(t, n - HEAD_DIM // 2, axis=1), pltpu.roll(t, HEAD_DIM // 2, axis=1))


def _inproj_compute(x, sh, sc, n1, w_ref, qn, kn, gmn, seg_ref, cos, sin):
    h = _rms(x, n1) * (1.0 + sc) + sh
    z = _dot(h.astype(BF), w_ref[...])
    q = z[:, :ATTN_WIDTH]
    k = z[:, ATTN_WIDTH:ATTN_WIDTH + KV_WIDTH]
    v = z[:, ATTN_WIDTH + KV_WIDTH:ATTN_WIDTH + 2 * KV_WIDTH]
    gm = z[:, ATTN_WIDTH + 2 * KV_WIDTH:]

    def head_norm(t, seg, wn):
        ms = _dot((t * t).astype(BF), seg) * (1.0 / HEAD_DIM)
        return t * lax.rsqrt(ms + EPS) * wn

    def rope(t):
        reps = t.shape[-1] // LANES
        c = jnp.concatenate([cos] * reps, axis=-1) if reps > 1 else cos
        s = jnp.concatenate([sin] * reps, axis=-1) if reps > 1 else sin
        return t * c + _swap_halves(t) * s

    q = rope(head_norm(q, seg_ref[...], qn)) * (HEAD_DIM ** -0.5)
    k = rope(head_norm(k, seg_ref[:KV_WIDTH, :KV_WIDTH], kn))
    g = jax.nn.gelu(gm)
    u = g[:, :GM_WIDTH]
    gv = _rms(g[:, GM_WIDTH:], gmn)
    return q, k, v, u, gv


def _dup_heads(t):
    lane = lax.broadcasted_iota(jnp.int32, (1, LANES), 1)
    lo = lane < HEAD_DIM
    r = pltpu.roll(t, HEAD_DIM, axis=1)
    return jnp.concatenate([jnp.where(lo, t, r), jnp.where(lo, r, t)], axis=-1)


def _inproj_p_body(tiles_per_batch, x_ref, mod_ref, n1_ref, w_ref, qn_ref, kn_ref, gmn_ref, seg_ref,
                   cos_ref, sin_ref, q_ref, kd_ref, vd_ref, u_ref, gv_ref, kl_ref, vl_ref, gvl_ref):
    i = pl.program_id(0)
    b = i // tiles_per_batch
    sh = mod_ref[0, pl.ds(b, 1), :]
    sc = mod_ref[1, pl.ds(b, 1), :]
    q, k, v, u, gv = _inproj_compute(x_ref[...], sh, sc, n1_ref[...], w_ref, qn_ref[...], kn_ref[...],
                                     gmn_ref[...], seg_ref, cos_ref[...], sin_ref[...])
    q_ref[...] = q.astype(BF)
    kd_ref[...] = _dup_heads(k).astype(BF)
    vd_ref[...] = _dup_heads(v).astype(BF)
    u_ref[...] = u.astype(BF)
    gv_ref[...] = gv.astype(BF)

    @pl.when(i % tiles_per_batch == tiles_per_batch - 1)
    def _():
        r = k.shape[0]
        kl_ref[...] = k[r - WINDOW:, :]
        vl_ref[...] = v[r - WINDOW:, :]
        gvl_ref[...] = gv[r - CHUNK:, :]


def _inproj_p(x, mod, n1, w_bf, qn, kn, gmn, seg, cos, sin, batch, seq):
    t = x.shape[0]
    tm = min(512, seq)
    tpb = seq // tm
    row = lambda i: (i, 0)
    full = lambda i: (0, 0)
    last = lambda i: (i // tpb, 0, 0)
    return pl.pallas_call(
        functools.partial(_inproj_p_body, tpb),
        grid=(t // tm,),
        in_specs=[
            pl.BlockSpec((tm, D_MODEL), row),
            pl.BlockSpec((N_ADA, batch, D_MODEL), lambda i: (0, 0, 0)),
            pl.BlockSpec((1, D_MODEL), full),
            pl.BlockSpec((D_MODEL, IN_COLS), full),
            pl.BlockSpec((1, ATTN_WIDTH), full),
            pl.BlockSpec((1, KV_WIDTH), full),
            pl.BlockSpec((1, GM_WIDTH), full),
            pl.BlockSpec((ATTN_WIDTH, ATTN_WIDTH), full),
            pl.BlockSpec((tm, LANES), lambda i: (i % tpb, 0)),
            pl.BlockSpec((tm, LANES), lambda i: (i % tpb, 0)),
        ],
        out_specs=[
            pl.BlockSpec((tm, ATTN_WIDTH), row),
            pl.BlockSpec((tm, 2 * KV_WIDTH), row),
            pl.BlockSpec((tm, 2 * KV_WIDTH), row),
            pl.BlockSpec((tm, GM_WIDTH), row),
            pl.BlockSpec((tm, GM_WIDTH), row),
            pl.BlockSpec((None, WINDOW, KV_WIDTH), last),
            pl.BlockSpec((None, WINDOW, KV_WIDTH), last),
            pl.BlockSpec((None, CHUNK, GM_WIDTH), last),
        ],
        out_shape=[
            jax.ShapeDtypeStruct((t, ATTN_WIDTH), BF),
            jax.ShapeDtypeStruct((t, 2 * KV_WIDTH), BF),
            jax.ShapeDtypeStruct((t, 2 * KV_WIDTH), BF),
            jax.ShapeDtypeStruct((t, GM_WIDTH), BF),
            jax.ShapeDtypeStruct((t, GM_WIDTH), BF),
            jax.ShapeDtypeStruct((batch, WINDOW, KV_WIDTH), F32),
            jax.ShapeDtypeStruct((batch, WINDOW, KV_WIDTH), F32),
            jax.ShapeDtypeStruct((batch, CHUNK, GM_WIDTH), F32),
        ],
        compiler_params=_params(("arbitrary",), 48),
        name="inproj_p",
    )(x, mod, n1, w_bf, qn, kn, gmn, seg, cos, sin)


def _inproj_s_body(x_ref, mod_ref, n1_ref, w_ref, qn_ref, kn_ref, gmn_ref, seg_ref, cos_ref, sin_ref,
                   q_ref, k_ref, v_ref, u_ref, gv_ref):
    q, k, v, u, gv = _inproj_compute(x_ref[...], mod_ref[0], mod_ref[1], n1_ref[...], w_ref, qn_ref[...],
                                     kn_ref[...], gmn_ref[...], seg_ref, cos_ref[...], sin_ref[...])
    q_ref[...] = q
    k_ref[...] = k
    v_ref[...] = v
    u_ref[...] = u
    gv_ref[...] = gv


def _inproj_s(x, mod, n1, w_bf, qn, kn, gmn, seg, cos, sin):
    n = x.shape[0]
    widths = (ATTN_WIDTH, KV_WIDTH, KV_WIDTH, GM_WIDTH, GM_WIDTH)
    return pl.pallas_call(
        _inproj_s_body,
        out_shape=[jax.ShapeDtypeStruct((n, w), F32) for w in widths],
        compiler_params=pltpu.CompilerParams(vmem_limit_bytes=48 * MIB),
        name="inproj_s",
    )(x, mod, n1, w_bf, qn, kn, gmn, seg, cos, sin)


def _mix_p_body(nblk, q_ref, kc_ref, kp_ref, vc_ref, vp_ref, u_ref, gv_ref, ws_ref, bst_ref, sink_ref, o_ref):
    i = pl.program_id(1)
    blk = WINDOW
    lane = lax.broadcasted_iota(jnp.int32, (1, LANES), 1)
    lo = lane < HEAD_DIM
    cols = KV_GROUP * blk
    iq = lax.broadcasted_iota(jnp.int32, (2 * blk, cols), 1) % blk
    jk = lax.broadcasted_iota(jnp.int32, (2 * blk, cols), 0)
    band = (jk > iq) & (jk <= iq + blk)
    band_first = band & ((jk >= blk) | (i > 0))
    tri = (lax.broadcasted_iota(jnp.int32, (CHUNK, CHUNK), 0)
           >= lax.broadcasted_iota(jnp.int32, (CHUNK, CHUNK), 1))
    wm = [jnp.where(tri, ws_ref[g], 0.0).astype(BF) for g in range(GM_GROUPS)]

    for n in range(nblk):
        r0 = n * blk
        if n == 0:
            kk = jnp.concatenate([kp_ref[...], kc_ref[0:blk, :]], axis=0)
            vv = jnp.concatenate([vp_ref[...], vc_ref[0:blk, :]], axis=0)
            mask = band_first
        else:
            kk = kc_ref[r0 - blk:r0 + blk, :]
            vv = vc_ref[r0 - blk:r0 + blk, :]
            mask = band
        for kvh in range(N_KV_HEADS):
            c0 = 2 * kvh
            qa = q_ref[r0:r0 + blk, c0 * LANES:(c0 + 1) * LANES]
            qb = q_ref[r0:r0 + blk, (c0 + 1) * LANES:(c0 + 2) * LANES]
            zero = jnp.zeros_like(qa)
            qq = jnp.concatenate([jnp.where(lo, qa, zero), jnp.where(lo, zero, qa),
                                  jnp.where(lo, qb, zero), jnp.where(lo, zero, qb)], axis=0)
            s = _dot_nt(kk[:, kvh * LANES:(kvh + 1) * LANES], qq)
            s = jnp.where(mask, s, NEG_INF)
            sink = jnp.concatenate(
                [jnp.full((1, blk), sink_ref[kvh * KV_GROUP + g], F32) for g in range(KV_GROUP)], axis=1)
            m = jnp.maximum(jnp.max(s, axis=0, keepdims=True), sink)
            p = jnp.exp(s - m)
            den = jnp.sum(p, axis=0, keepdims=True) + jnp.exp(sink - m)
            p = (p * (1.0 / den)).astype(BF)
            o = lax.dot_general(p, vv[:, kvh * LANES:(kvh + 1) * LANES], (((0,), (0,)), ((), ())),
                                preferred_element_type=F32)
            o_ref[r0:r0 + blk, c0 * LANES:(c0 + 1) * LANES] = jnp.where(
                lo, o[0:blk], o[blk:2 * blk]).astype(BF)
            o_ref[r0:r0 + blk, (c0 + 1) * LANES:(c0 + 2) * LANES] = jnp.where(
                lo, o[2 * blk:3 * blk], o[3 * blk:4 * blk]).astype(BF)
        for g in range(GM_GROUPS):
            cs = slice(g * LANES, (g + 1) * LANES)
            sp = _dot(wm[g], gv_ref[r0:r0 + blk, cs]) + bst_ref[:, g:g + 1]
            o_ref[r0:r0 + blk, ATTN_WIDTH + g * LANES:ATTN_WIDTH + (g + 1) * LANES] = (
                u_ref[r0:r0 + blk, cs].astype(F32) * sp).astype(BF)


def _mix_p(q, kd, vd, u, gv, ws, bst, sinks, batch, seq):
    t = q.shape[0]
    tq = min(512, seq)
    nblk = tq // WINDOW
    tpb = seq // tq
    cur = lambda b, i: (b * tpb + i, 0)
    prev = lambda b, i: (jnp.maximum((b * tpb + i) * nblk - 1, b * tpb * nblk), 0)
    return pl.pallas_call(
        functools.partial(_mix_p_body, nblk),
        grid=(batch, tpb),
        in_specs=[
            pl.BlockSpec((tq, ATTN_WIDTH), cur),
            pl.BlockSpec((tq, 2 * KV_WIDTH), cur),
            pl.BlockSpec((WINDOW, 2 * KV_WIDTH), prev),
            pl.BlockSpec((tq, 2 * KV_WIDTH), cur),
            pl.BlockSpec((WINDOW, 2 * KV_WIDTH), prev),
            pl.BlockSpec((tq, GM_WIDTH), cur),
            pl.BlockSpec((tq, GM_WIDTH), cur),
            pl.BlockSpec((GM_GROUPS, CHUNK, CHUNK), lambda b, i: (0, 0, 0)),
            pl.BlockSpec((CHUNK, GM_GROUPS), lambda b, i: (0, 0)),
            pl.BlockSpec(memory_space=pltpu.SMEM),
        ],
        out_specs=pl.BlockSpec((tq, D_MODEL), cur),
        out_shape=jax.ShapeDtypeStruct((t, D_MODEL), BF),
        compiler_params=_params(("arbitrary", "arbitrary"), 48),
        name="mix_p",
    )(q, kd, kd, vd, vd, u, gv, ws, bst, sinks)


def _attn_s_body(q_ref, kn_ref, vn_ref, ck_ref, cv_ref, sink_ref, o_ref, nk_ref, nv_ref):
    w = ck_ref.shape[1]
    row = lax.broadcasted_iota(jnp.int32, (1, w, 1), 1)
    nk = jnp.where(row == w - 1, kn_ref[...], pltpu.roll(ck_ref[...], w - 1, axis=1))
    nv = jnp.where(row == w - 1, vn_ref[...], pltpu.roll(cv_ref[...], w - 1, axis=1))
    nk_ref[...] = nk
    nv_ref[...] = nv
    s = jnp.einsum('bhd,bjd->bhj', q_ref[...].astype(BF), nk.astype(BF), preferred_element_type=F32)
    sink = sink_ref[...][None, :, 0:1]
    m = jnp.maximum(jnp.max(s, axis=-1, keepdims=True), sink)
    p = jnp.exp(s - m)
    den = jnp.sum(p, axis=-1, keepdims=True) + jnp.exp(sink - m)
    o = jnp.einsum('bhj,bjd->bhd', p.astype(BF), nv.astype(BF), preferred_element_type=F32)
    o_ref[...] = o * (1.0 / den)


def _attn_s(qpad, k_new, v_new, ck, cv, sink_tile):
    n, w, kw = ck.shape
    bb = min(16, n)
    blk3 = lambda r, c: pl.BlockSpec((bb, r, c), lambda i: (i, 0, 0))
    return pl.pallas_call(
        _attn_s_body,
        grid=(n // bb,),
        in_specs=[blk3(N_HEADS, LANES), blk3(1, kw), blk3(1, kw), blk3(w, kw), blk3(w, kw),
                  pl.BlockSpec((N_HEADS, LANES), lambda i: (0, 0))],
        out_specs=[blk3(N_HEADS, LANES), blk3(w, kw), blk3(w, kw)],
        out_shape=[jax.ShapeDtypeStruct((n, N_HEADS, LANES), F32),
                   jax.ShapeDtypeStruct((n, w, kw), F32),
                   jax.ShapeDtypeStruct((n, w, kw), F32)],
        compiler_params=_params(("arbitrary",), 32),
        name="attn_s",
    )(qpad, k_new, v_new, ck, cv, sink_tile)


def _top2(h2, rw_ref, rb_ref):
    hi = h2.astype(BF)
    lo = (h2 - hi.astype(F32)).astype(BF)
    rw = rw_ref[...]
    whi = rw.astype(BF)
    wlo = (rw - whi.astype(F32)).astype(BF)
    logits = _dot(hi, whi) + _dot(lo, whi) + _dot(hi, wlo) + rb_ref[...]
    lane = lax.broadcasted_iota(jnp.int32, logits.shape, 1).astype(F32)
    e = jnp.exp(logits - jnp.max(logits, axis=-1, keepdims=True))
    p = e / jnp.sum(e, axis=-1, keepdims=True)
    m1 = jnp.max(p, axis=-1, keepdims=True)
    i1 = jnp.min(jnp.where(p == m1, lane, float(LANES)), axis=-1, keepdims=True)
    p2 = jnp.where(lane == i1, -1.0, p)
    m2 = jnp.max(p2, axis=-1, keepdims=True)
    i2 = jnp.min(jnp.where(p2 == m2, lane, float(LANES)), axis=-1, keepdims=True)
    tot = m1 + m2
    return lane, i1, i2, m1 / tot, m2 / tot


def _route_gates(h2, rw_ref, rb_ref):
    lane, i1, i2, g1, g2 = _top2(h2, rw_ref, rb_ref)
    return jnp.where(lane == i1, g1, 0.0) + jnp.where(lane == i2, g2, 0.0)


ROUTE_E, ROUTE_RANK, ROUTE_GATE = 0, 2, 4


def _route_ranked(h2, rw_ref, rb_ref, tri_ref, cnt_ref):
    lane, i1, i2, g1, g2 = _top2(h2, rw_ref, rb_ref)
    oh1 = lane == i1
    oh2 = lane == i2
    hit = jnp.where(oh1, 1.0, 0.0) + jnp.where(oh2, 1.0, 0.0)
    before = cnt_ref[...] + _dot(tri_ref[...], hit.astype(BF))
    r1 = jnp.sum(jnp.where(oh1, before, 0.0), axis=-1, keepdims=True)
    r2 = jnp.sum(jnp.where(oh2, before, 0.0), axis=-1, keepdims=True)
    cnt_ref[...] += jnp.sum(hit, axis=0, keepdims=True)
    cols = (i1, i2, r1, r2, g1, g2)
    out = jnp.zeros_like(lane)
    for j, c in enumerate(cols):
        out = jnp.where(lane == float(j), c, out)
    return out


def _outproj_compute(mix_bf, x, ga1, sh2, sc2, w_ref, n2):
    xn = x + ga1 * _dot(mix_bf, w_ref[...])
    h2 = _rms(xn, n2) * (1.0 + sc2) + sh2
    return xn, h2


def _outproj_p_body(tiles_per_batch, with_router, mix_ref, x_ref, mod_ref, w_ref, n2_ref, *rest):
    i = pl.program_id(0)
    b = i // tiles_per_batch
    mrow = lambda j: mod_ref[j, pl.ds(b, 1), :]
    xn, h2 = _outproj_compute(mix_ref[...], x_ref[...], mrow(2), mrow(3), mrow(4), w_ref, n2_ref[...])
    if with_router:
        rw_ref, rb_ref, tri_ref, xn_ref, h2_ref, route_ref, cnt_ref = rest

        @pl.when(i == 0)
        def _():
            cnt_ref[...] = jnp.zeros_like(cnt_ref)

        route_ref[...] = _route_ranked(h2, rw_ref, rb_ref, tri_ref, cnt_ref)
        h2_ref[...] = h2
    else:
        xn_ref, h2_ref = rest
        h2_ref[...] = h2.astype(BF)
    xn_ref[...] = xn


def _outproj_p(mix, x, mod, w_bf, n2, router, batch, seq):
    t = x.shape[0]
    tm = min(512, seq)
    tpb = seq // tm
    row = lambda i: (i, 0)
    full = lambda i: (0, 0)
    in_specs = [
        pl.BlockSpec((tm, D_MODEL), row),
        pl.BlockSpec((tm, D_MODEL), row),
        pl.BlockSpec((N_ADA, batch, D_MODEL), lambda i: (0, 0, 0)),
        pl.BlockSpec((D_MODEL, D_MODEL), full),
        pl.BlockSpec((1, D_MODEL), full),
    ]
    out_specs = [pl.BlockSpec((tm, D_MODEL), row), pl.BlockSpec((tm, D_MODEL), row)]
    out_shape = [jax.ShapeDtypeStruct((t, D_MODEL), F32),
                 jax.ShapeDtypeStruct((t, D_MODEL), BF if router is None else F32)]
    args = [mix, x, mod, w_bf, n2]
    if router is not None:
        tri = jnp.asarray(np.tri(tm, k=-1), BF)
        in_specs += [pl.BlockSpec((D_MODEL, LANES), full), pl.BlockSpec((1, LANES), full),
                     pl.BlockSpec((tm, tm), full)]
        out_specs += [pl.BlockSpec((tm, LANES), row), pl.BlockSpec((1, LANES), full)]
        out_shape += [jax.ShapeDtypeStruct((t, LANES), F32), jax.ShapeDtypeStruct((1, LANES), F32)]
        args += list(router) + [tri]
    return pl.pallas_call(
        functools.partial(_outproj_p_body, tpb, router is not None),
        grid=(t // tm,),
        in_specs=in_specs, out_specs=out_specs, out_shape=out_shape,
        compiler_params=_params(("arbitrary",), 48),
        name="outproj_p",
    )(*args)


def _outproj_s_body(with_router, o_ref, u_ref, gv_ref, wdiag_ref, bsrow_ref, x_ref, mod_ref, w_ref, n2_ref, *rest):
    gate = u_ref[...] * (wdiag_ref[...] * gv_ref[...] + bsrow_ref[...])
    mix = jnp.concatenate([o_ref[...], gate], axis=-1).astype(BF)
    xn, h2 = _outproj_compute(mix, x_ref[...], mod_ref[2], mod_ref[3], mod_ref[4], w_ref, n2_ref[...])
    if with_router:
        rw_ref, rb_ref, xn_ref, h2_ref, gates_ref = rest
        gates_ref[...] = _route_gates(h2, rw_ref, rb_ref)
    else:
        xn_ref, h2_ref = rest
    xn_ref[...] = xn
    h2_ref[...] = h2.astype(BF)


def _outproj_s(o, u, gv, wdiag, bsrow, x, mod, w_bf, n2, router):
    n = x.shape[0]
    out_shape = [jax.ShapeDtypeStruct((n, D_MODEL), F32), jax.ShapeDtypeStruct((n, D_MODEL), BF)]
    args = [o, u, gv, wdiag, bsrow, x, mod, w_bf, n2]
    if router is not None:
        out_shape.append(jax.ShapeDtypeStruct((n, LANES), F32))
        args += list(router)
    return pl.pallas_call(
        functools.partial(_outproj_s_body, router is not None),
        out_shape=out_shape,
        compiler_params=pltpu.CompilerParams(vmem_limit_bytes=32 * MIB),
        name="outproj_s",
    )(*args)


def _swiglu(h_bf, wg_ref, wu_ref, wd_ref):
    y = None
    for c in range(2):
        sl = slice(c * FF_HALF, (c + 1) * FF_HALF)
        a = (jax.nn.silu(_dot(h_bf, wg_ref[:, sl])) * _dot(h_bf, wu_ref[:, sl])).astype(BF)
        part = _dot(a, wd_ref[sl, :])
        y = part if y is None else y + part
    return y


def _ffn_body(tiles_per_batch, h_ref, x_ref, mod_ref, wg_ref, wu_ref, wd_ref, o_ref):
    if tiles_per_batch:
        b = pl.program_id(0) // tiles_per_batch
        ga2 = mod_ref[5, pl.ds(b, 1), :]
    else:
        ga2 = mod_ref[5]
    o_ref[...] = x_ref[...] + ga2 * _swiglu(h_ref[...], wg_ref, wu_ref, wd_ref)


def _ffn(h2, x, mod, wg, wu, wd, seq):
    t = x.shape[0]
    tm = min(512, t if seq is None else seq)
    tpb = 0 if seq is None else seq // tm
    row = lambda i: (i, 0)
    full = lambda i: (0, 0)
    return pl.pallas_call(
        functools.partial(_ffn_body, tpb),
        grid=(t // tm,),
        in_specs=[
            pl.BlockSpec((tm, D_MODEL), row),
            pl.BlockSpec((tm, D_MODEL), row),
            pl.BlockSpec(mod.shape, lambda i: (0, 0, 0)),
            pl.BlockSpec((D_MODEL, D_FF), full),
            pl.BlockSpec((D_MODEL, D_FF), full),
            pl.BlockSpec((D_FF, D_MODEL), full),
        ],
        out_specs=pl.BlockSpec((tm, D_MODEL), row),
        out_shape=jax.ShapeDtypeStruct((t, D_MODEL), F32),
        compiler_params=_params(("arbitrary",), 56),
        name="ffn",
    )(h2, x, mod, wg, wu, wd)


def _moe_all_body(tiles_per_batch, h_ref, x_ref, gates_ref, mod_ref, wg_ref, wu_ref, wd_ref, o_ref):
    e = pl.program_id(1)
    if tiles_per_batch:
        b = pl.program_id(0) // tiles_per_batch
        ga2 = mod_ref[5, pl.ds(b, 1), :]
    else:
        ga2 = mod_ref[5]

    @pl.when(e == 0)
    def _():
        o_ref[...] = jnp.zeros_like(o_ref)

    lane = lax.broadcasted_iota(jnp.int32, (1, LANES), 1)
    gate = jnp.sum(jnp.where(lane == e, gates_ref[...], 0.0), axis=-1, keepdims=True)
    o_ref[...] += gate * _swiglu(h_ref[...], wg_ref, wu_ref, wd_ref)

    @pl.when(e == N_EXPERTS - 1)
    def _():
        o_ref[...] = x_ref[...] + ga2 * o_ref[...]


def _moe_all(h2, x, gates, mod, wg, wu, wd, seq):
    t = x.shape[0]
    tm = min(512, t if seq is None else seq)
    tpb = 0 if seq is None else seq // tm
    row = lambda i, e: (i, 0)
    return pl.pallas_call(
        functools.partial(_moe_all_body, tpb),
        grid=(t // tm, N_EXPERTS),
        in_specs=[
            pl.BlockSpec((tm, D_MODEL), row),
            pl.BlockSpec((tm, D_MODEL), row),
            pl.BlockSpec((tm, LANES), row),
            pl.BlockSpec(mod.shape, lambda i, e: (0, 0, 0)),
            pl.BlockSpec((None, D_MODEL, D_FF), lambda i, e: (e, 0, 0)),
            pl.BlockSpec((None, D_MODEL, D_FF), lambda i, e: (e, 0, 0)),
            pl.BlockSpec((None, D_FF, D_MODEL), lambda i, e: (e, 0, 0)),
        ],
        out_specs=pl.BlockSpec((tm, D_MODEL), row),
        out_shape=jax.ShapeDtypeStruct((t, D_MODEL), F32),
        compiler_params=_params(("arbitrary", "arbitrary"), 56),
        name="moe_all",
    )(h2, x, gates, mod, wg, wu, wd)


TM_MOE = 512
TD = 256


def _row_copy(src, s, dst, d, sem):
    return pltpu.make_async_copy(src.at[pl.ds(s, 1), :], dst.at[pl.ds(d, 1), :], sem)


def _dispatch_body(pos_ref, pad_ref, h_ref, xs_ref, stage, sem, zsem):
    i = pl.program_id(0)
    n = pl.num_programs(0)
    td = h_ref.shape[0]
    slot = i % 2

    def wait_slot(s):
        for _ in range(2):
            pltpu.make_async_copy(stage.at[s], xs_ref.at[pl.ds(0, td), :], sem.at[s]).wait()

    @pl.when(i >= 2)
    def _():
        wait_slot(slot)

    stage[slot] = h_ref[...]

    def issue(r, c):
        for k in range(2):
            _row_copy(stage.at[slot], r, xs_ref, pos_ref[0, 0, k * td + r], sem.at[slot]).start()
        return c

    lax.fori_loop(0, td, issue, 0, unroll=8)

    @pl.when(i == n - 1)
    def _():
        wait_slot(slot)

        @pl.when(n > 1)
        def _():
            wait_slot(1 - slot)

        stage[0] = jnp.zeros(stage.shape[1:], stage.dtype)
        for e in range(N_EXPERTS):
            lo = pad_ref[0, e]
            hi = pad_ref[1, e]

            def zero_row(r, c):
                _row_copy(stage.at[0], 0, xs_ref, r, zsem).start()
                return c

            def wait_row(r, c):
                _row_copy(stage.at[0], 0, xs_ref, 0, zsem).wait()
                return c

            lax.fori_loop(lo, hi, zero_row, 0)
            lax.fori_loop(lo, hi, wait_row, 0)

        def zero_blk(j, c):
            pltpu.make_async_copy(stage.at[0], xs_ref.at[pl.ds(j * td, td), :], zsem).start()
            return c

        def wait_blk(j, c):
            pltpu.make_async_copy(stage.at[0], xs_ref.at[pl.ds(0, td), :], zsem).wait()
            return c

        lax.fori_loop(pad_ref[0, N_EXPERTS], pad_ref[1, N_EXPERTS], zero_blk, 0)
        lax.fori_loop(pad_ref[0, N_EXPERTS], pad_ref[1, N_EXPERTS], wait_blk, 0)


def _dispatch(h2, pos_t, pad, npad):
    t = h2.shape[0]
    td = min(TD, t)
    return pl.pallas_call(
        _dispatch_body,
        grid=(t // td,),
        in_specs=[
            pl.BlockSpec((1, 1, 2 * td), lambda i: (i, 0, 0), memory_space=pltpu.SMEM),
            pl.BlockSpec(memory_space=pltpu.SMEM),
            pl.BlockSpec((td, D_MODEL), lambda i: (i, 0)),
        ],
        out_specs=pl.BlockSpec(memory_space=pl.ANY),
        out_shape=jax.ShapeDtypeStruct((npad, D_MODEL), F32),
        scratch_shapes=[pltpu.VMEM((2, td, D_MODEL), F32), pltpu.SemaphoreType.DMA((2,)),
                        pltpu.SemaphoreType.DMA(())],
        compiler_params=_params(("arbitrary",), 32),
        name="dispatch",
    )(pos_t, pad, h2)


def _moe_body(te_ref, src_ref, nv_ref, x_ref, wg_ref, wu_ref, wd_ref, o_ref):
    i = pl.program_id(0)

    @pl.when(nv_ref[i] > 0)
    def _():
        o_ref[...] = _swiglu(x_ref[...].astype(BF), wg_ref, wu_ref, wd_ref)

    @pl.when(nv_ref[i] == 0)
    def _():
        o_ref[...] = jnp.zeros_like(o_ref)


def _moe(xs, tile_e, tile_src, tile_nv, wg, wu, wd):
    npad = xs.shape[0]
    wspec = lambda shape: pl.BlockSpec((None,) + shape, lambda i, te, src, nv: (te[i], 0, 0),
                                       pipeline_mode=pl.Buffered(1))
    return pl.pallas_call(
        _moe_body,
        grid_spec=pltpu.PrefetchScalarGridSpec(
            num_scalar_prefetch=3,
            grid=(npad // TM_MOE,),
            in_specs=[
                pl.BlockSpec((TM_MOE, D_MODEL), lambda i, te, src, nv: (src[i], 0)),
                wspec((D_MODEL, D_FF)), wspec((D_MODEL, D_FF)), wspec((D_FF, D_MODEL)),
            ],
            out_specs=pl.BlockSpec((TM_MOE, D_MODEL), lambda i, te, src, nv: (i, 0)),
        ),
        out_shape=jax.ShapeDtypeStruct((npad, D_MODEL), F32),
        compiler_params=_params(("arbitrary",), 56),
        name="moe",
    )(tile_e, tile_src, tile_nv, xs, wg, wu, wd)


def _combine_body(tiles_per_batch, posc_ref, posn_ref, x_ref, route_ref, mod_ref, ys_ref, o_ref, buf, sem):
    i = pl.program_id(0)
    n = pl.num_programs(0)
    tc = x_ref.shape[0]

    def gather(p_ref, s):
        def issue(r, c):
            for k in range(2):
                _row_copy(ys_ref, p_ref[0, 0, k * tc + r], buf.at[s], k * tc + r, sem.at[s]).start()
            return c

        lax.fori_loop(0, tc, issue, 0, unroll=8)

    @pl.when(i == 0)
    def _():
        gather(posc_ref, 0)

    @pl.when(i + 1 < n)
    def _():
        gather(posn_ref, (i + 1) % 2)

    slot = i % 2
    pltpu.make_async_copy(ys_ref.at[pl.ds(0, 2 * tc), :], buf.at[slot], sem.at[slot]).wait()
    lane = lax.broadcasted_iota(jnp.int32, (1, LANES), 1)
    rt = route_ref[...]
    g1 = jnp.sum(jnp.where(lane == ROUTE_GATE, rt, 0.0), axis=-1, keepdims=True)
    g2 = jnp.sum(jnp.where(lane == ROUTE_GATE + 1, rt, 0.0), axis=-1, keepdims=True)
    y = g1 * buf[slot, 0:tc, :] + g2 * buf[slot, tc:2 * tc, :]
    ga2 = mod_ref[5, pl.ds(i // tiles_per_batch, 1), :]
    o_ref[...] = x_ref[...] + ga2 * y


def _combine(ys, pos_t, x, route, mod, seq):
    t = x.shape[0]
    tc = min(TD, t)
    nt = t // tc
    row = lambda i: (i, 0)
    return pl.pallas_call(
        functools.partial(_combine_body, seq // tc),
        grid=(nt,),
        in_specs=[
            pl.BlockSpec((1, 1, 2 * tc), lambda i: (i, 0, 0), memory_space=pltpu.SMEM),
            pl.BlockSpec((1, 1, 2 * tc), lambda i: (jnp.minimum(i + 1, nt - 1), 0, 0), memory_space=pltpu.SMEM),
            pl.BlockSpec((tc, D_MODEL), row),
            pl.BlockSpec((tc, LANES), row),
            pl.BlockSpec(mod.shape, lambda i: (0, 0, 0)),
            pl.BlockSpec(memory_space=pl.ANY),
        ],
        out_specs=pl.BlockSpec((tc, D_MODEL), row),
        out_shape=jax.ShapeDtypeStruct((t, D_MODEL), F32),
        scratch_shapes=[pltpu.VMEM((2, 2 * tc, D_MODEL), F32), pltpu.SemaphoreType.DMA((2,))],
        compiler_params=_params(("arbitrary",), 32),
        name="combine",
    )(pos_t, pos_t, x, route, mod, ys)


def _moe_routed(h2, xn, route, cnt, mod, wg, wu, wd, seq):
    t = h2.shape[0]
    td = min(TD, t)
    nt_max = pl.cdiv(2 * t, TM_MOE) + N_EXPERTS
    npad = nt_max * TM_MOE
    counts = cnt[0, :N_EXPERTS].astype(jnp.int32)
    ntile = (counts + TM_MOE - 1) // TM_MOE
    tile_end = jnp.cumsum(ntile)
    off = (tile_end - ntile) * TM_MOE
    e12 = route[:, ROUTE_E:ROUTE_E + 2].astype(jnp.int32)
    r12 = route[:, ROUTE_RANK:ROUTE_RANK + 2].astype(jnp.int32)
    onehot = e12[:, :, None] == jnp.arange(N_EXPERTS)[None, None, :]
    pos = jnp.sum(jnp.where(onehot, off[None, None, :], 0), axis=-1) + r12
    pos_t = pos.reshape(t // td, td, 2).transpose(0, 2, 1).reshape(t // td, 1, 2 * td)
    total = tile_end[-1]
    tid = jnp.arange(nt_max)
    tile_e = jnp.minimum(jnp.sum(tid[:, None] >= tile_end[None, :], axis=1), N_EXPERTS - 1).astype(jnp.int32)
    tile_nv = (tid < total).astype(jnp.int32)
    tile_src = jnp.minimum(tid, total - 1).astype(jnp.int32)
    pad = jnp.stack([jnp.concatenate([off + counts, (total * (TM_MOE // td))[None]]),
                     jnp.concatenate([off + ntile * TM_MOE, jnp.full((1,), npad // td, jnp.int32)])]).astype(jnp.int32)
    xs = _dispatch(h2, pos_t, pad, npad)
    ys = _moe(xs, tile_e, tile_src, tile_nv, wg, wu, wd)
    return _combine(ys, pos_t, xn, route, mod, seq)


def _rope_tables(pos):
    inv = ROPE_THETA ** (-np.arange(0, HEAD_DIM, 2, dtype=np.float64) / HEAD_DIM)
    ang = np.asarray(pos, np.float64)[:, None] * inv[None, :]
    cos = np.concatenate([np.cos(ang), np.cos(ang)], axis=-1)
    sin = np.concatenate([-np.sin(ang), np.sin(ang)], axis=-1)
    reps = LANES // HEAD_DIM
    return (jnp.asarray(np.tile(cos, (1, reps)), F32), jnp.asarray(np.tile(sin, (1, reps)), F32))


def kernel(x_prompt, x_sample, cache_k, cache_v, c_prompt, c_sample, w_ada, b_ada, norm1_w, norm2_w, w_in,
           q_norm_w, k_norm_w, attn_sinks, gm_norm_w, gm_ws, gm_bs, w_out, dense_w_gate, dense_w_up,
           dense_w_down, router_w, router_b, moe_w_gate, moe_w_up, moe_w_down):
    batch, seq, d = x_prompt.shape
    nd = x_sample.shape[0]
    depth = w_in.shape[0]
    t = batch * seq

    mod = _ada(jnp.concatenate([c_prompt, c_sample], axis=0), w_ada, b_ada)
    mod_p = mod[:, :batch].reshape(depth, batch, N_ADA, d).transpose(0, 2, 1, 3)
    mod_s = mod[:, batch:].reshape(depth, nd, N_ADA, d).transpose(0, 2, 1, 3)

    cos_p, sin_p = _rope_tables(np.arange(seq))
    cos_s, sin_s = _rope_tables(np.array([PAST_LEN]))
    head_of = np.arange(ATTN_WIDTH) // HEAD_DIM
    seg = jnp.asarray(head_of[:, None] == head_of[None, :], BF)
    kv_of_head = (jnp.arange(N_HEADS) // KV_GROUP)[None, :, None]

    w_in_bf = w_in.astype(BF)
    w_out_bf = w_out.astype(BF)
    dense_bf = [w.astype(BF) for w in (dense_w_gate, dense_w_up, dense_w_down)]
    moe_bf = [w.astype(BF) for w in (moe_w_gate, moe_w_up, moe_w_down)]
    router_w_pad = jnp.pad(router_w, ((0, 0), (0, 0), (0, LANES - N_EXPERTS)))
    router_b_pad = jnp.pad(router_b, ((0, 0), (0, LANES - N_EXPERTS)), constant_values=NEG_INF)

    xp = x_prompt.reshape(t, d)
    xs = x_sample.reshape(nd, d)
    k_p, v_p, g_p, k_s, v_s, g_s = [], [], [], [], [], []
    for l in range(depth):
        i = l // 2
        n1 = norm1_w[l][None, :]
        n2 = norm2_w[l][None, :]
        qn = jnp.tile(q_norm_w[l], N_HEADS)[None, :]
        kn = jnp.tile(k_norm_w[l], N_KV_HEADS)[None, :]
        gmn = gm_norm_w[l][None, :]
        router = None if l % 2 == 0 else (router_w_pad[i], router_b_pad[i][None, :])

        q, kd, vd, u, gv, kl, vl, gvl = _inproj_p(xp, mod_p[l], n1, w_in_bf[l], qn, kn, gmn, seg,
                                                  cos_p, sin_p, batch, seq)
        mix = _mix_p(q, kd, vd, u, gv, gm_ws[l], gm_bs[l].T, attn_sinks[l], batch, seq)
        res = _outproj_p(mix, xp, mod_p[l], w_out_bf[l], n2, router, batch, seq)
        if router is None:
            xp = _ffn(res[1], res[0], mod_p[l], dense_bf[0][i], dense_bf[1][i], dense_bf[2][i], seq)
        else:
            xn, h2, route, cnt = res
            xp = _moe_routed(h2, xn, route, cnt, mod_p[l], moe_bf[0][i], moe_bf[1][i], moe_bf[2][i], seq)
        k_p.append(kl.reshape(batch, WINDOW, N_KV_HEADS, HEAD_DIM))
        v_p.append(vl.reshape(batch, WINDOW, N_KV_HEADS, HEAD_DIM))
        g_p.append(gvl)

        q, k, v, u, gv = _inproj_s(xs, mod_s[l], n1, w_in_bf[l], qn, kn, gmn, seg, cos_s, sin_s)
        qh = q.reshape(nd, N_HEADS, HEAD_DIM)
        zq = jnp.zeros_like(qh)
        qpad = jnp.where(kv_of_head == 0, jnp.concatenate([qh, zq], -1), jnp.concatenate([zq, qh], -1))
        w = cache_k.shape[2]
        o, nk, nv = _attn_s(qpad, k[:, None, :], v[:, None, :], cache_k[l].reshape(nd, w, KV_WIDTH),
                            cache_v[l].reshape(nd, w, KV_WIDTH),
                            jnp.broadcast_to(attn_sinks[l][:, None], (N_HEADS, LANES)))
        o = jnp.where(kv_of_head == 0, o[..., :HEAD_DIM], o[..., HEAD_DIM:]).reshape(nd, ATTN_WIDTH)
        wdiag = jnp.repeat(gm_ws[l][:, 0, 0], GM_WIDTH // GM_GROUPS)[None, :]
        bsrow = jnp.repeat(gm_bs[l][:, 0], GM_WIDTH // GM_GROUPS)[None, :]
        res = _outproj_s(o, u, gv, wdiag, bsrow, xs, mod_s[l], w_out_bf[l], n2, router)
        if router is None:
            xs = _ffn(res[1], res[0], mod_s[l], dense_bf[0][i], dense_bf[1][i], dense_bf[2][i], None)
        else:
            xs = _moe_all(res[1], res[0], res[2], mod_s[l], moe_bf[0][i], moe_bf[1][i], moe_bf[2][i], None)
        k_s.append(nk.reshape(nd, w, N_KV_HEADS, HEAD_DIM))
        v_s.append(nv.reshape(nd, w, N_KV_HEADS, HEAD_DIM))
        g_s.append(gv[:, None, :])

    return (xp.reshape(batch, seq, d), xs.reshape(nd, 1, d), jnp.stack(k_p), jnp.stack(v_p), jnp.stack(g_p),
            jnp.stack(k_s), jnp.stack(v_s), jnp.stack(g_s))
```

```python
import functools

import numpy as np
import jax
import jax.numpy as jnp
from jax import lax
from jax.experimental import pallas as pl
from jax.experimental.pallas import tpu as pltpu

D_MODEL = 1024
HEAD_DIM = 64
N_HEADS = 8
N_KV_HEADS = 2
KV_GROUP = N_HEADS // N_KV_HEADS
ATTN_WIDTH = N_HEADS * HEAD_DIM
KV_WIDTH = N_KV_HEADS * HEAD_DIM
GM_WIDTH = 512
GM_GROUPS = 4
WINDOW = 128
CHUNK = 128
D_FF = 2816
FF_HALF = D_FF // 2
N_EXPERTS = 8
N_ADA = 6
IN_COLS = ATTN_WIDTH + 2 * KV_WIDTH + 2 * GM_WIDTH
PAST_LEN = 16384
ROPE_THETA = 10000.0
EPS = 1e-6
NEG_INF = -1e30
LANES = 128

BF = jnp.bfloat16
F32 = jnp.float32
MIB = 1024 * 1024


def _params(sem, vmem_mib):
    return pltpu.CompilerParams(dimension_semantics=sem, vmem_limit_bytes=vmem_mib * MIB)


def _dot(a, b):
    return jnp.dot(a, b, preferred_element_type=F32)


def _dot_nt(a, b):
    return lax.dot_general(a, b, (((1,), (1,)), ((), ())), preferred_element_type=F32)


def _rms(x, w):
    ms = jnp.mean(x * x, axis=-1, keepdims=True)
    return x * lax.rsqrt(ms + EPS) * w


def _ada_body(c_ref, w_ref, b_ref, o_ref):
    s = jax.nn.silu(c_ref[...]).astype(BF)
    o_ref[...] = _dot(s, w_ref[...].astype(BF)) + b_ref[...]


def _ada(c_all, w_ada, b_ada):
    depth, d, cols = w_ada.shape
    n = c_all.shape[0]
    tn = 1024
    return pl.pallas_call(
        _ada_body,
        grid=(depth, cols // tn),
        in_specs=[
            pl.BlockSpec((n, d), lambda l, j: (0, 0)),
            pl.BlockSpec((None, d, tn), lambda l, j: (l, 0, j)),
            pl.BlockSpec((None, 1, tn), lambda l, j: (l, 0, j)),
        ],
        out_specs=pl.BlockSpec((None, n, tn), lambda l, j: (l, 0, j)),
        out_shape=jax.ShapeDtypeStruct((depth, n, cols), F32),
        compiler_params=_params(("arbitrary", "arbitrary"), 32),
        name="ada",
    )(c_all, w_ada, b_ada.reshape(depth, 1, cols))


def _swap_halves(t):
    n = t.shape[-1]
    lane = lax.broadcasted_iota(jnp.int32, (1, n), 1)
    first = (lane % HEAD_DIM) < (HEAD_DIM // 2)
    return jnp.where(first, pltpu.roll(t, n - HEAD_DIM // 2, axis=1), pltpu.roll(t, HEAD_DIM // 2, axis=1))


def _inproj_compute(x, sh, sc, n1, w_ref, qn, kn, gmn, seg_ref, cos, sin):
    h = _rms(x, n1) * (1.0 + sc) + sh
    z = _dot(h.astype(BF), w_ref[...])
    q = z[:, :ATTN_WIDTH]
    k = z[:, ATTN_WIDTH:ATTN_WIDTH + KV_WIDTH]
    v = z[:, ATTN_WIDTH + KV_WIDTH:ATTN_WIDTH + 2 * KV_WIDTH]
    gm = z[:, ATTN_WIDTH + 2 * KV_WIDTH:]

    def head_norm(t, seg, wn):
        ms = _dot((t * t).astype(BF), seg) * (1.0 / HEAD_DIM)
        return t * lax.rsqrt(ms + EPS) * wn

    def rope(t):
        reps = t.shape[-1] // LANES
        c = jnp.concatenate([cos] * reps, axis=-1) if reps > 1 else cos
        s = jnp.concatenate([sin] * reps, axis=-1) if reps > 1 else sin
        return t * c + _swap_halves(t) * s

    q = rope(head_norm(q, seg_ref[...], qn)) * (HEAD_DIM ** -0.5)
    k = rope(head_norm(k, seg_ref[:KV_WIDTH, :KV_WIDTH], kn))
    g = jax.nn.gelu(gm)
    u = g[:, :GM_WIDTH]
    gv = _rms(g[:, GM_WIDTH:], gmn)
    return q, k, v, u, gv


def _dup_heads(t):
    lane = lax.broadcasted_iota(jnp.int32, (1, LANES), 1)
    lo = lane < HEAD_DIM
    r = pltpu.roll(t, HEAD_DIM, axis=1)
    return jnp.concatenate([jnp.where(lo, t, r), jnp.where(lo, r, t)], axis=-1)


def _inproj_p_body(tiles_per_batch, x_ref, mod_ref, n1_ref, w_ref, qn_ref, kn_ref, gmn_ref, seg_ref,
                   cos_ref, sin_ref, q_ref, kd_ref, vd_ref, u_ref, gv_ref, kl_ref, vl_ref, gvl_ref):
    i = pl.program_id(0)
    b = i // tiles_per_batch
    sh = mod_ref[0, pl.ds(b, 1), :]
    sc = mod_ref[1, pl.ds(b, 1), :]
    q, k, v, u, gv = _inproj_compute(x_ref[...], sh, sc, n1_ref[...], w_ref, qn_ref[...], kn_ref[...],
                                     gmn_ref[...], seg_ref, cos_ref[...], sin_ref[...])
    q_ref[...] = q.astype(BF)
    kd_ref[...] = _dup_heads(k).astype(BF)
    vd_ref[...] = _dup_heads(v).astype(BF)
    u_ref[...] = u.astype(BF)
    gv_ref[...] = gv.astype(BF)

    @pl.when(i % tiles_per_batch == tiles_per_batch - 1)
    def _():
        r = k.shape[0]
        kl_ref[...] = k[r - WINDOW:, :]
        vl_ref[...] = v[r - WINDOW:, :]
        gvl_ref[...] = gv[r - CHUNK:, :]


def _inproj_p(x, mod, n1, w_bf, qn, kn, gmn, seg, cos, sin, batch, seq):
    t = x.shape[0]
    tm = min(512, seq)
    tpb = seq // tm
    row = lambda i: (i, 0)
    full = lambda i: (0, 0)
    last = lambda i: (i // tpb, 0, 0)
    return pl.pallas_call(
        functools.partial(_inproj_p_body, tpb),
        grid=(t // tm,),
        in_specs=[
            pl.BlockSpec((tm, D_MODEL), row),
            pl.BlockSpec((N_ADA, batch, D_MODEL), lambda i: (0, 0, 0)),
            pl.BlockSpec((1, D_MODEL), full),
            pl.BlockSpec((D_MODEL, IN_COLS), full),
            pl.BlockSpec((1, ATTN_WIDTH), full),
            pl.BlockSpec((1, KV_WIDTH), full),
            pl.BlockSpec((1, GM_WIDTH), full),
            pl.BlockSpec((ATTN_WIDTH, ATTN_WIDTH), full),
            pl.BlockSpec((tm, LANES), lambda i: (i % tpb, 0)),
            pl.BlockSpec((tm, LANES), lambda i: (i % tpb, 0)),
        ],
        out_specs=[
            pl.BlockSpec((tm, ATTN_WIDTH), row),
            pl.BlockSpec((tm, 2 * KV_WIDTH), row),
            pl.BlockSpec((tm, 2 * KV_WIDTH), row),
            pl.BlockSpec((tm, GM_WIDTH), row),
            pl.BlockSpec((tm, GM_WIDTH), row),
            pl.BlockSpec((None, WINDOW, KV_WIDTH), last),
            pl.BlockSpec((None, WINDOW, KV_WIDTH), last),
            pl.BlockSpec((None, CHUNK, GM_WIDTH), last),
        ],
        out_shape=[
            jax.ShapeDtypeStruct((t, ATTN_WIDTH), BF),
            jax.ShapeDtypeStruct((t, 2 * KV_WIDTH), BF),
            jax.ShapeDtypeStruct((t, 2 * KV_WIDTH), BF),
            jax.ShapeDtypeStruct((t, GM_WIDTH), BF),
            jax.ShapeDtypeStruct((t, GM_WIDTH), BF),
            jax.ShapeDtypeStruct((batch, WINDOW, KV_WIDTH), F32),
            jax.ShapeDtypeStruct((batch, WINDOW, KV_WIDTH), F32),
            jax.ShapeDtypeStruct((batch, CHUNK, GM_WIDTH), F32),
        ],
        compiler_params=_params(("arbitrary",), 48),
        name="inproj_p",
    )(x, mod, n1, w_bf, qn, kn, gmn, seg, cos, sin)


def _inproj_s_body(x_ref, mod_ref, n1_ref, w_ref, qn_ref, kn_ref, gmn_ref, seg_ref, cos_ref, sin_ref,
                   q_ref, k_ref, v_ref, u_ref, gv_ref):
    q, k, v, u, gv = _inproj_compute(x_ref[...], mod_ref[0], mod_ref[1], n1_ref[...], w_ref, qn_ref[...],
                                     kn_ref[...], gmn_ref[...], seg_ref, cos_ref[...], sin_ref[...])
    q_ref[...] = q
    k_ref[...] = k
    v_ref[...] = v
    u_ref[...] = u
    gv_ref[...] = gv


def _inproj_s(x, mod, n1, w_bf, qn, kn, gmn, seg, cos, sin):
    n = x.shape[0]
    widths = (ATTN_WIDTH, KV_WIDTH, KV_WIDTH, GM_WIDTH, GM_WIDTH)
    return pl.pallas_call(
        _inproj_s_body,
        out_shape=[jax.ShapeDtypeStruct((n, w), F32) for w in widths],
        compiler_params=pltpu.CompilerParams(vmem_limit_bytes=48 * MIB),
        name="inproj_s",
    )(x, mod, n1, w_bf, qn, kn, gmn, seg, cos, sin)


def _mix_p_body(nblk, q_ref, kc_ref, kp_ref, vc_ref, vp_ref, u_ref, gv_ref, ws_ref, bst_ref, sink_ref, o_ref):
    i = pl.program_id(1)
    blk = WINDOW
    lane = lax.broadcasted_iota(jnp.int32, (1, LANES), 1)
    lo = lane < HEAD_DIM
    cols = KV_GROUP * blk
    iq = lax.broadcasted_iota(jnp.int32, (2 * blk, cols), 1) % blk
    jk = lax.broadcasted_iota(jnp.int32, (2 * blk, cols), 0)
    band = (jk > iq) & (jk <= iq + blk)
    band_first = band & ((jk >= blk) | (i > 0))
    tri = (lax.broadcasted_iota(jnp.int32, (CHUNK, CHUNK), 0)
           >= lax.broadcasted_iota(jnp.int32, (CHUNK, CHUNK), 1))
    wm = [jnp.where(tri, ws_ref[g], 0.0).astype(BF) for g in range(GM_GROUPS)]

    for n in range(nblk):
        r0 = n * blk
        if n == 0:
            kk = jnp.concatenate([kp_ref[...], kc_ref[0:blk, :]], axis=0)
            vv = jnp.concatenate([vp_ref[...], vc_ref[0:blk, :]], axis=0)
            mask = band_first
        else:
            kk = kc_ref[r0 - blk:r0 + blk, :]
            vv = vc_ref[r0 - blk:r0 + blk, :]
            mask = band
        for kvh in range(N_KV_HEADS):
            c0 = 2 * kvh
            qa = q_ref[r0:r0 + blk, c0 * LANES:(c0 + 1) * LANES]
            qb = q_ref[r0:r0 + blk, (c0 + 1) * LANES:(c0 + 2) * LANES]
            zero = jnp.zeros_like(qa)
            qq = jnp.concatenate([jnp.where(lo, qa, zero), jnp.where(lo, zero, qa),
                                  jnp.where(lo, qb, zero), jnp.where(lo, zero, qb)], axis=0)
            s = _dot_nt(kk[:, kvh * LANES:(kvh + 1) * LANES], qq)
            s = jnp.where(mask, s, NEG_INF)
            sink = jnp.concatenate(
                [jnp.full((1, blk), sink_ref[kvh * KV_GROUP + g], F32) for g in range(KV_GROUP)], axis=1)
            m = jnp.maximum(jnp.max(s, axis=0, keepdims=True), sink)
            p = jnp.exp(s - m)
            den = jnp.sum(p, axis=0, keepdims=True) + jnp.exp(sink - m)
            p = (p * (1.0 / den)).astype(BF)
            o = lax.dot_general(p, vv[:, kvh * LANES:(kvh + 1) * LANES], (((0,), (0,)), ((), ())),
                                preferred_element_type=F32)
            o_ref[r0:r0 + blk, c0 * LANES:(c0 + 1) * LANES] = jnp.where(
                lo, o[0:blk], o[blk:2 * blk]).astype(BF)
            o_ref[r0:r0 + blk, (c0 + 1) * LANES:(c0 + 2) * LANES] = jnp.where(
                lo, o[2 * blk:3 * blk], o[3 * blk:4 * blk]).astype(BF)
        for g in range(GM_GROUPS):
            cs = slice(g * LANES, (g + 1) * LANES)
            sp = _dot(wm[g], gv_ref[r0:r0 + blk, cs]) + bst_ref[:, g:g + 1]
            o_ref[r0:r0 + blk, ATTN_WIDTH + g * LANES:ATTN_WIDTH + (g + 1) * LANES] = (
                u_ref[r0:r0 + blk, cs].astype(F32) * sp).astype(BF)


def _mix_p(q, kd, vd, u, gv, ws, bst, sinks, batch, seq):
    t = q.shape[0]
    tq = min(512, seq)
    nblk = tq // WINDOW
    tpb = seq // tq
    cur = lambda b, i: (b * tpb + i, 0)
    prev = lambda b, i: (jnp.maximum((b * tpb + i) * nblk - 1, b * tpb * nblk), 0)
    return pl.pallas_call(
        functools.partial(_mix_p_body, nblk),
        grid=(batch, tpb),
        in_specs=[
            pl.BlockSpec((tq, ATTN_WIDTH), cur),
            pl.BlockSpec((tq, 2 * KV_WIDTH), cur),
            pl.BlockSpec((WINDOW, 2 * KV_WIDTH), prev),
            pl.BlockSpec((tq, 2 * KV_WIDTH), cur),
            pl.BlockSpec((WINDOW, 2 * KV_WIDTH), prev),
            pl.BlockSpec((tq, GM_WIDTH), cur),
            pl.BlockSpec((tq, GM_WIDTH), cur),
            pl.BlockSpec((GM_GROUPS, CHUNK, CHUNK), lambda b, i: (0, 0, 0)),
            pl.BlockSpec((CHUNK, GM_GROUPS), lambda b, i: (0, 0)),
            pl.BlockSpec(memory_space=pltpu.SMEM),
        ],
        out_specs=pl.BlockSpec((tq, D_MODEL), cur),
        out_shape=jax.ShapeDtypeStruct((t, D_MODEL), BF),
        compiler_params=_params(("arbitrary", "arbitrary"), 48),
        name="mix_p",
    )(q, kd, kd, vd, vd, u, gv, ws, bst, sinks)


def _attn_s_body(q_ref, kn_ref, vn_ref, ck_ref, cv_ref, sink_ref, o_ref, nk_ref, nv_ref):
    w = ck_ref.shape[1]
    row = lax.broadcasted_iota(jnp.int32, (1, w, 1), 1)
    nk = jnp.where(row == w - 1, kn_ref[...], pltpu.roll(ck_ref[...], w - 1, axis=1))
    nv = jnp.where(row == w - 1, vn_ref[...], pltpu.roll(cv_ref[...], w - 1, axis=1))
    nk_ref[...] = nk
    nv_ref[...] = nv
    s = jnp.einsum('bhd,bjd->bhj', q_ref[...].astype(BF), nk.astype(BF), preferred_element_type=F32)
    sink = sink_ref[...][None, :, 0:1]
    m = jnp.maximum(jnp.max(s, axis=-1, keepdims=True), sink)
    p = jnp.exp(s - m)
    den = jnp.sum(p, axis=-1, keepdims=True) + jnp.exp(sink - m)
    o = jnp.einsum('bhj,bjd->bhd', p.astype(BF), nv.astype(BF), preferred_element_type=F32)
    o_ref[...] = o * (1.0 / den)


def _attn_s(qpad, k_new, v_new, ck, cv, sink_tile):
    n, w, kw = ck.shape
    bb = min(16, n)
    blk3 = lambda r, c: pl.BlockSpec((bb, r, c), lambda i: (i, 0, 0))
    return pl.pallas_call(
        _attn_s_body,
        grid=(n // bb,),
        in_specs=[blk3(N_HEADS, LANES), blk3(1, kw), blk3(1, kw), blk3(w, kw), blk3(w, kw),
                  pl.BlockSpec((N_HEADS, LANES), lambda i: (0, 0))],
        out_specs=[blk3(N_HEADS, LANES), blk3(w, kw), blk3(w, kw)],
        out_shape=[jax.ShapeDtypeStruct((n, N_HEADS, LANES), F32),
                   jax.ShapeDtypeStruct((n, w, kw), F32),
                   jax.ShapeDtypeStruct((n, w, kw), F32)],
        compiler_params=_params(("arbitrary",), 32),
        name="attn_s",
    )(qpad, k_new, v_new, ck, cv, sink_tile)


def _top2(h2, rw_ref, rb_ref):
    hi = h2.astype(BF)
    lo = (h2 - hi.astype(F32)).astype(BF)
    rw = rw_ref[...]
    whi = rw.astype(BF)
    wlo = (rw - whi.astype(F32)).astype(BF)
    logits = _dot(hi, whi) + _dot(lo, whi) + _dot(hi, wlo) + rb_ref[...]
    lane = lax.broadcasted_iota(jnp.int32, logits.shape, 1).astype(F32)
    e = jnp.exp(logits - jnp.max(logits, axis=-1, keepdims=True))
    p = e / jnp.sum(e, axis=-1, keepdims=True)
    m1 = jnp.max(p, axis=-1, keepdims=True)
    i1 = jnp.min(jnp.where(p == m1, lane, float(LANES)), axis=-1, keepdims=True)
    p2 = jnp.where(lane == i1, -1.0, p)
    m2 = jnp.max(p2, axis=-1, keepdims=True)
    i2 = jnp.min(jnp.where(p2 == m2, lane, float(LANES)), axis=-1, keepdims=True)
    tot = m1 + m2
    return lane, i1, i2, m1 / tot, m2 / tot


def _route_gates(h2, rw_ref, rb_ref):
    lane, i1, i2, g1, g2 = _top2(h2, rw_ref, rb_ref)
    return jnp.where(lane == i1, g1, 0.0) + jnp.where(lane == i2, g2, 0.0)


ROUTE_E, ROUTE_RANK, ROUTE_GATE = 0, 2, 4


def _route_ranked(h2, rw_ref, rb_ref, tri_ref, cnt_ref):
    lane, i1, i2, g1, g2 = _top2(h2, rw_ref, rb_ref)
    oh1 = lane == i1
    oh2 = lane == i2
    hit = jnp.where(oh1, 1.0, 0.0) + jnp.where(oh2, 1.0, 0.0)
    before = cnt_ref[...] + _dot(tri_ref[...], hit.astype(BF))
    r1 = jnp.sum(jnp.where(oh1, before, 0.0), axis=-1, keepdims=True)
    r2 = jnp.sum(jnp.where(oh2, before, 0.0), axis=-1, keepdims=True)
    cnt_ref[...] += jnp.sum(hit, axis=0, keepdims=True)
    cols = (i1, i2, r1, r2, g1, g2)
    out = jnp.zeros_like(lane)
    for j, c in enumerate(cols):
        out = jnp.where(lane == float(j), c, out)
    return out


def _outproj_compute(mix_bf, x, ga1, sh2, sc2, w_ref, n2):
    xn = x + ga1 * _dot(mix_bf, w_ref[...])
    h2 = _rms(xn, n2) * (1.0 + sc2) + sh2
    return xn, h2


def _outproj_p_body(tiles_per_batch, with_router, mix_ref, x_ref, mod_ref, w_ref, n2_ref, *rest):
    i = pl.program_id(0)
    b = i // tiles_per_batch
    mrow = lambda j: mod_ref[j, pl.ds(b, 1), :]
    xn, h2 = _outproj_compute(mix_ref[...], x_ref[...], mrow(2), mrow(3), mrow(4), w_ref, n2_ref[...])
    if with_router:
        rw_ref, rb_ref, tri_ref, xn_ref, h2_ref, route_ref, cnt_ref = rest

        @pl.when(i == 0)
        def _():
            cnt_ref[...] = jnp.zeros_like(cnt_ref)

        route_ref[...] = _route_ranked(h2, rw_ref, rb_ref, tri_ref, cnt_ref)
        h2_ref[...] = h2
    else:
        xn_ref, h2_ref = rest
        h2_ref[...] = h2.astype(BF)
    xn_ref[...] = xn


def _outproj_p(mix, x, mod, w_bf, n2, router, batch, seq):
    t = x.shape[0]
    tm = min(512, seq)
    tpb = seq // tm
    row = lambda i: (i, 0)
    full = lambda i: (0, 0)
    in_specs = [
        pl.BlockSpec((tm, D_MODEL), row),
        pl.BlockSpec((tm, D_MODEL), row),
        pl.BlockSpec((N_ADA, batch, D_MODEL), lambda i: (0, 0, 0)),
        pl.BlockSpec((D_MODEL, D_MODEL), full),
        pl.BlockSpec((1, D_MODEL), full),
    ]
    out_specs = [pl.BlockSpec((tm, D_MODEL), row), pl.BlockSpec((tm, D_MODEL), row)]
    out_shape = [jax.ShapeDtypeStruct((t, D_MODEL), F32),
                 jax.ShapeDtypeStruct((t, D_MODEL), BF if router is None else F32)]
    args = [mix, x, mod, w_bf, n2]
    if router is not None:
        tri = jnp.asarray(np.tri(tm, k=-1), BF)
        in_specs += [pl.BlockSpec((D_MODEL, LANES), full), pl.BlockSpec((1, LANES), full),
                     pl.BlockSpec((tm, tm), full)]
        out_specs += [pl.BlockSpec((tm, LANES), row), pl.BlockSpec((1, LANES), full)]
        out_shape += [jax.ShapeDtypeStruct((t, LANES), F32), jax.ShapeDtypeStruct((1, LANES), F32)]
        args += list(router) + [tri]
    return pl.pallas_call(
        functools.partial(_outproj_p_body, tpb, router is not None),
        grid=(t // tm,),
        in_specs=in_specs, out_specs=out_specs, out_shape=out_shape,
        compiler_params=_params(("arbitrary",), 48),
        name="outproj_p",
    )(*args)


def _outproj_s_body(with_router, o_ref, u_ref, gv_ref, wdiag_ref, bsrow_ref, x_ref, mod_ref, w_ref, n2_ref, *rest):
    gate = u_ref[...] * (wdiag_ref[...] * gv_ref[...] + bsrow_ref[...])
    mix = jnp.concatenate([o_ref[...], gate], axis=-1).astype(BF)
    xn, h2 = _outproj_compute(mix, x_ref[...], mod_ref[2], mod_ref[3], mod_ref[4], w_ref, n2_ref[...])
    if with_router:
        rw_ref, rb_ref, xn_ref, h2_ref, gates_ref = rest
        gates_ref[...] = _route_gates(h2, rw_ref, rb_ref)
    else:
        xn_ref, h2_ref = rest
    xn_ref[...] = xn
    h2_ref[...] = h2.astype(BF)


def _outproj_s(o, u, gv, wdiag, bsrow, x, mod, w_bf, n2, router):
    n = x.shape[0]
    out_shape = [jax.ShapeDtypeStruct((n, D_MODEL), F32), jax.ShapeDtypeStruct((n, D_MODEL), BF)]
    args = [o, u, gv, wdiag, bsrow, x, mod, w_bf, n2]
    if router is not None:
        out_shape.append(jax.ShapeDtypeStruct((n, LANES), F32))
        args += list(router)
    return pl.pallas_call(
        functools.partial(_outproj_s_body, router is not None),
        out_shape=out_shape,
        compiler_params=pltpu.CompilerParams(vmem_limit_bytes=32 * MIB),
        name="outproj_s",
    )(*args)


def _swiglu(h_bf, wg_ref, wu_ref, wd_ref):
    y = None
    for c in range(2):
        sl = slice(c * FF_HALF, (c + 1) * FF_HALF)
        a = (jax.nn.silu(_dot(h_bf, wg_ref[:, sl])) * _dot(h_bf, wu_ref[:, sl])).astype(BF)
        part = _dot(a, wd_ref[sl, :])
        y = part if y is None else y + part
    return y


def _ffn_body(tiles_per_batch, h_ref, x_ref, mod_ref, wg_ref, wu_ref, wd_ref, o_ref):
    if tiles_per_batch:
        b = pl.program_id(0) // tiles_per_batch
        ga2 = mod_ref[5, pl.ds(b, 1), :]
    else:
        ga2 = mod_ref[5]
    o_ref[...] = x_ref[...] + ga2 * _swiglu(h_ref[...], wg_ref, wu_ref, wd_ref)


def _ffn(h2, x, mod, wg, wu, wd, seq):
    t = x.shape[0]
    tm = min(512, t if seq is None else seq)
    tpb = 0 if seq is None else seq // tm
    row = lambda i: (i, 0)
    full = lambda i: (0, 0)
    return pl.pallas_call(
        functools.partial(_ffn_body, tpb),
        grid=(t // tm,),
        in_specs=[
            pl.BlockSpec((tm, D_MODEL), row),
            pl.BlockSpec((tm, D_MODEL), row),
            pl.BlockSpec(mod.shape, lambda i: (0, 0, 0)),
            pl.BlockSpec((D_MODEL, D_FF), full),
            pl.BlockSpec((D_MODEL, D_FF), full),
            pl.BlockSpec((D_FF, D_MODEL), full),
        ],
        out_specs=pl.BlockSpec((tm, D_MODEL), row),
        out_shape=jax.ShapeDtypeStruct((t, D_MODEL), F32),
        compiler_params=_params(("arbitrary",), 56),
        name="ffn",
    )(h2, x, mod, wg, wu, wd)


def _moe_all_body(tiles_per_batch, h_ref, x_ref, gates_ref, mod_ref, wg_ref, wu_ref, wd_ref, o_ref):
    e = pl.program_id(1)
    if tiles_per_batch:
        b = pl.program_id(0) // tiles_per_batch
        ga2 = mod_ref[5, pl.ds(b, 1), :]
    else:
        ga2 = mod_ref[5]

    @pl.when(e == 0)
    def _():
        o_ref[...] = jnp.zeros_like(o_ref)

    lane = lax.broadcasted_iota(jnp.int32, (1, LANES), 1)
    gate = jnp.sum(jnp.where(lane == e, gates_ref[...], 0.0), axis=-1, keepdims=True)
    o_ref[...] += gate * _swiglu(h_ref[...], wg_ref, wu_ref, wd_ref)

    @pl.when(e == N_EXPERTS - 1)
    def _():
        o_ref[...] = x_ref[...] + ga2 * o_ref[...]


def _moe_all(h2, x, gates, mod, wg, wu, wd, seq):
    t = x.shape[0]
    tm = min(512, t if seq is None else seq)
    tpb = 0 if seq is None else seq // tm
    row = lambda i, e: (i, 0)
    return pl.pallas_call(
        functools.partial(_moe_all_body, tpb),
        grid=(t // tm, N_EXPERTS),
        in_specs=[
            pl.BlockSpec((tm, D_MODEL), row),
            pl.BlockSpec((tm, D_MODEL), row),
            pl.BlockSpec((tm, LANES), row),
            pl.BlockSpec(mod.shape, lambda i, e: (0, 0, 0)),
            pl.BlockSpec((None, D_MODEL, D_FF), lambda i, e: (e, 0, 0)),
            pl.BlockSpec((None, D_MODEL, D_FF), lambda i, e: (e, 0, 0)),
            pl.BlockSpec((None, D_FF, D_MODEL), lambda i, e: (e, 0, 0)),
        ],
        out_specs=pl.BlockSpec((tm, D_MODEL), row),
        out_shape=jax.ShapeDtypeStruct((t, D_MODEL), F32),
        compiler_params=_params(("arbitrary", "arbitrary"), 56),
        name="moe_all",
    )(h2, x, gates, mod, wg, wu, wd)


TM_MOE = 512
TD = 256


SUB = 8
assert D_MODEL == SUB * LANES


def _to_token_tiles(ref, x):
    r = x.shape[0]
    for g in range(SUB):
        ref[pl.ds(g, r, stride=SUB), :] = x[:, g * LANES:(g + 1) * LANES]


def _from_token_tiles(ref, first, r):
    return jnp.concatenate([ref[pl.ds(first * SUB + g, r, stride=SUB), :] for g in range(SUB)], axis=-1)


def _token_copy(src, s, dst, d, sem):
    return pltpu.make_async_copy(src.at[pl.ds(pl.multiple_of(s, SUB), SUB), :],
                                 dst.at[pl.ds(pl.multiple_of(d, SUB), SUB), :], sem)


def _dispatch_body(pos_ref, pad_ref, h_ref, xs_ref, stage, sem, zsem):
    i = pl.program_id(0)
    n = pl.num_programs(0)
    td = h_ref.shape[0]
    slot = i % 2

    def wait_slot(s):
        for _ in range(2):
            pltpu.make_async_copy(stage.at[s], xs_ref.at[pl.ds(0, td * SUB), :], sem.at[s]).wait()

    @pl.when(i >= 2)
    def _():
        wait_slot(slot)

    _to_token_tiles(stage.at[slot], h_ref[...])

    def issue(r, c):
        for k in range(2):
            _token_copy(stage.at[slot], r * SUB, xs_ref, pos_ref[0, 0, k * td + r], sem.at[slot]).start(priority=k)
        return c

    lax.fori_loop(0, td, issue, 0, unroll=8)

    @pl.when(i == n - 1)
    def _():
        wait_slot(slot)

        @pl.when(n > 1)
        def _():
            wait_slot(1 - slot)

        stage[0] = jnp.zeros(stage.shape[1:], stage.dtype)
        for e in range(N_EXPERTS):
            lo = pad_ref[0, e]
            hi = pad_ref[1, e]

            def zero_token(r, c):
                _token_copy(stage.at[0], 0, xs_ref, r * SUB, zsem).start()
                return c

            def wait_token(r, c):
                _token_copy(stage.at[0], 0, xs_ref, 0, zsem).wait()
                return c

            lax.fori_loop(lo, hi, zero_token, 0)
            lax.fori_loop(lo, hi, wait_token, 0)

        def zero_blk(j, c):
            pltpu.make_async_copy(stage.at[0], xs_ref.at[pl.ds(pl.multiple_of(j * (td * SUB), SUB), td * SUB), :],
                                  zsem).start()
            return c

        def wait_blk(j, c):
            pltpu.make_async_copy(stage.at[0], xs_ref.at[pl.ds(0, td * SUB), :], zsem).wait()
            return c

        lax.fori_loop(pad_ref[0, N_EXPERTS], pad_ref[1, N_EXPERTS], zero_blk, 0)
        lax.fori_loop(pad_ref[0, N_EXPERTS], pad_ref[1, N_EXPERTS], wait_blk, 0)


def _dispatch(h2, pos_t, pad, npad):
    t = h2.shape[0]
    td = min(TD, t)
    return pl.pallas_call(
        _dispatch_body,
        grid=(t // td,),
        in_specs=[
            pl.BlockSpec((1, 1, 2 * td), lambda i: (i, 0, 0), memory_space=pltpu.SMEM),
            pl.BlockSpec(memory_space=pltpu.SMEM),
            pl.BlockSpec((td, D_MODEL), lambda i: (i, 0)),
        ],
        out_specs=pl.BlockSpec(memory_space=pl.ANY),
        out_shape=jax.ShapeDtypeStruct((npad * SUB, LANES), F32),
        scratch_shapes=[pltpu.VMEM((2, td * SUB, LANES), F32), pltpu.SemaphoreType.DMA((2,)),
                        pltpu.SemaphoreType.DMA(())],
        compiler_params=_params(("arbitrary",), 32),
        name="dispatch",
    )(pos_t, pad, h2)


def _moe_body(te_ref, src_ref, nv_ref, x_ref, wg_ref, wu_ref, wd_ref, o_ref):
    i = pl.program_id(0)

    @pl.when(nv_ref[i] > 0)
    def _():
        x = _from_token_tiles(x_ref, 0, TM_MOE).astype(BF)
        _to_token_tiles(o_ref, _swiglu(x, wg_ref, wu_ref, wd_ref))

    @pl.when(nv_ref[i] == 0)
    def _():
        o_ref[...] = jnp.zeros_like(o_ref)


def _moe(xs, tile_e, tile_src, tile_nv, wg, wu, wd):
    rows = TM_MOE * SUB
    wspec = lambda shape: pl.BlockSpec((None,) + shape, lambda i, te, src, nv: (te[i], 0, 0),
                                       pipeline_mode=pl.Buffered(1))
    return pl.pallas_call(
        _moe_body,
        grid_spec=pltpu.PrefetchScalarGridSpec(
            num_scalar_prefetch=3,
            grid=(xs.shape[0] // rows,),
            in_specs=[
                pl.BlockSpec((rows, LANES), lambda i, te, src, nv: (src[i], 0)),
                wspec((D_MODEL, D_FF)), wspec((D_MODEL, D_FF)), wspec((D_FF, D_MODEL)),
            ],
            out_specs=pl.BlockSpec((rows, LANES), lambda i, te, src, nv: (i, 0)),
        ),
        out_shape=jax.ShapeDtypeStruct(xs.shape, F32),
        compiler_params=_params(("arbitrary",), 56),
        name="moe",
    )(tile_e, tile_src, tile_nv, xs, wg, wu, wd)


def _combine_body(tiles_per_batch, posc_ref, posn_ref, x_ref, route_ref, mod_ref, ys_ref, o_ref, buf, sem):
    i = pl.program_id(0)
    n = pl.num_programs(0)
    tc = x_ref.shape[0]

    def gather(p_ref, s):
        def issue(r, c):
            for k in range(2):
                _token_copy(ys_ref, p_ref[0, 0, k * tc + r], buf.at[s], (k * tc + r) * SUB,
                            sem.at[s]).start(priority=k)
            return c

        lax.fori_loop(0, tc, issue, 0, unroll=8)

    @pl.when(i == 0)
    def _():
        gather(posc_ref, 0)

    @pl.when(i + 1 < n)
    def _():
        gather(posn_ref, (i + 1) % 2)

    slot = i % 2
    pltpu.make_async_copy(ys_ref.at[pl.ds(0, 2 * tc * SUB), :], buf.at[slot], sem.at[slot]).wait()
    lane = lax.broadcasted_iota(jnp.int32, (1, LANES), 1)
    rt = route_ref[...]
    g1 = jnp.sum(jnp.where(lane == ROUTE_GATE, rt, 0.0), axis=-1, keepdims=True)
    g2 = jnp.sum(jnp.where(lane == ROUTE_GATE + 1, rt, 0.0), axis=-1, keepdims=True)
    y = g1 * _from_token_tiles(buf.at[slot], 0, tc) + g2 * _from_token_tiles(buf.at[slot], tc, tc)
    ga2 = mod_ref[5, pl.ds(i // tiles_per_batch, 1), :]
    o_ref[...] = x_ref[...] + ga2 * y


def _combine(ys, pos_t, x, route, mod, seq):
    t = x.shape[0]
    tc = min(TD, t)
    nt = t // tc
    row = lambda i: (i, 0)
    return pl.pallas_call(
        functools.partial(_combine_body, seq // tc),
        grid=(nt,),
        in_specs=[
            pl.BlockSpec((1, 1, 2 * tc), lambda i: (i, 0, 0), memory_space=pltpu.SMEM),
            pl.BlockSpec((1, 1, 2 * tc), lambda i: (jnp.minimum(i + 1, nt - 1), 0, 0), memory_space=pltpu.SMEM),
            pl.BlockSpec((tc, D_MODEL), row),
            pl.BlockSpec((tc, LANES), row),
            pl.BlockSpec(mod.shape, lambda i: (0, 0, 0)),
            pl.BlockSpec(memory_space=pl.ANY),
        ],
        out_specs=pl.BlockSpec((tc, D_MODEL), row),
        out_shape=jax.ShapeDtypeStruct((t, D_MODEL), F32),
        scratch_shapes=[pltpu.VMEM((2, 2 * tc * SUB, LANES), F32), pltpu.SemaphoreType.DMA((2,))],
        compiler_params=_params(("arbitrary",), 32),
        name="combine",
    )(pos_t, pos_t, x, route, mod, ys)


def _moe_routed(h2, xn, route, cnt, mod, wg, wu, wd, seq):
    t = h2.shape[0]
    td = min(TD, t)
    nt_max = pl.cdiv(2 * t, TM_MOE) + N_EXPERTS
    npad = nt_max * TM_MOE
    counts = cnt[0, :N_EXPERTS].astype(jnp.int32)
    ntile = (counts + TM_MOE - 1) // TM_MOE
    tile_end = jnp.cumsum(ntile)
    off = (tile_end - ntile) * TM_MOE
    e12 = route[:, ROUTE_E:ROUTE_E + 2].astype(jnp.int32)
    r12 = route[:, ROUTE_RANK:ROUTE_RANK + 2].astype(jnp.int32)
    onehot = e12[:, :, None] == jnp.arange(N_EXPERTS)[None, None, :]
    pos = jnp.sum(jnp.where(onehot, off[None, None, :], 0), axis=-1) + r12
    pos_t = (pos * SUB).reshape(t // td, td, 2).transpose(0, 2, 1).reshape(t // td, 1, 2 * td)
    total = tile_end[-1]
    tid = jnp.arange(nt_max)
    tile_e = jnp.minimum(jnp.sum(tid[:, None] >= tile_end[None, :], axis=1), N_EXPERTS - 1).astype(jnp.int32)
    tile_nv = (tid < total).astype(jnp.int32)
    tile_src = jnp.minimum(tid, total - 1).astype(jnp.int32)
    pad = jnp.stack([jnp.concatenate([off + counts, (total * (TM_MOE // td))[None]]),
                     jnp.concatenate([off + ntile * TM_MOE, jnp.full((1,), npad // td, jnp.int32)])]).astype(jnp.int32)
    xs = _dispatch(h2, pos_t, pad, npad)
    ys = _moe(xs, tile_e, tile_src, tile_nv, wg, wu, wd)
    return _combine(ys, pos_t, xn, route, mod, seq)


def _rope_tables(pos):
    inv = ROPE_THETA ** (-np.arange(0, HEAD_DIM, 2, dtype=np.float64) / HEAD_DIM)
    ang = np.asarray(pos, np.float64)[:, None] * inv[None, :]
    cos = np.concatenate([np.cos(ang), np.cos(ang)], axis=-1)
    sin = np.concatenate([-np.sin(ang), np.sin(ang)], axis=-1)
    reps = LANES // HEAD_DIM
    return (jnp.asarray(np.tile(cos, (1, reps)), F32), jnp.asarray(np.tile(sin, (1, reps)), F32))


def kernel(x_prompt, x_sample, cache_k, cache_v, c_prompt, c_sample, w_ada, b_ada, norm1_w, norm2_w, w_in,
           q_norm_w, k_norm_w, attn_sinks, gm_norm_w, gm_ws, gm_bs, w_out, dense_w_gate, dense_w_up,
           dense_w_down, router_w, router_b, moe_w_gate, moe_w_up, moe_w_down):
    batch, seq, d = x_prompt.shape
    nd = x_sample.shape[0]
    depth = w_in.shape[0]
    t = batch * seq

    mod = _ada(jnp.concatenate([c_prompt, c_sample], axis=0), w_ada, b_ada)
    mod_p = mod[:, :batch].reshape(depth, batch, N_ADA, d).transpose(0, 2, 1, 3)
    mod_s = mod[:, batch:].reshape(depth, nd, N_ADA, d).transpose(0, 2, 1, 3)

    cos_p, sin_p = _rope_tables(np.arange(seq))
    cos_s, sin_s = _rope_tables(np.array([PAST_LEN]))
    head_of = np.arange(ATTN_WIDTH) // HEAD_DIM
    seg = jnp.asarray(head_of[:, None] == head_of[None, :], BF)
    kv_of_head = (jnp.arange(N_HEADS) // KV_GROUP)[None, :, None]

    w_in_bf = w_in.astype(BF)
    w_out_bf = w_out.astype(BF)
    dense_bf = [w.astype(BF) for w in (dense_w_gate, dense_w_up, dense_w_down)]
    moe_bf = [w.astype(BF) for w in (moe_w_gate, moe_w_up, moe_w_down)]
    router_w_pad = jnp.pad(router_w, ((0, 0), (0, 0), (0, LANES - N_EXPERTS)))
    router_b_pad = jnp.pad(router_b, ((0, 0), (0, LANES - N_EXPERTS)), constant_values=NEG_INF)

    xp = x_prompt.reshape(t, d)
    xs = x_sample.reshape(nd, d)
    k_p, v_p, g_p, k_s, v_s, g_s = [], [], [], [], [], []
    for l in range(depth):
        i = l // 2
        n1 = norm1_w[l][None, :]
        n2 = norm2_w[l][None, :]
        qn = jnp.tile(q_norm_w[l], N_HEADS)[None, :]
        kn = jnp.tile(k_norm_w[l], N_KV_HEADS)[None, :]
        gmn = gm_norm_w[l][None, :]
        router = None if l % 2 == 0 else (router_w_pad[i], router_b_pad[i][None, :])

        q, kd, vd, u, gv, kl, vl, gvl = _inproj_p(xp, mod_p[l], n1, w_in_bf[l], qn, kn, gmn, seg,
                                                  cos_p, sin_p, batch, seq)
        mix = _mix_p(q, kd, vd, u, gv, gm_ws[l], gm_bs[l].T, attn_sinks[l], batch, seq)
        res = _outproj_p(mix, xp, mod_p[l], w_out_bf[l], n2, router, batch, seq)
        if router is None:
            xp = _ffn(res[1], res[0], mod_p[l], dense_bf[0][i], dense_bf[1][i], dense_bf[2][i], seq)
        else:
            xn, h2, route, cnt = res
            xp = _moe_routed(h2, xn, route, cnt, mod_p[l], moe_bf[0][i], moe_bf[1][i], moe_bf[2][i], seq)
        k_p.append(kl.reshape(batch, WINDOW, N_KV_HEADS, HEAD_DIM))
        v_p.append(vl.reshape(batch, WINDOW, N_KV_HEADS, HEAD_DIM))
        g_p.append(gvl)

        q, k, v, u, gv = _inproj_s(xs, mod_s[l], n1, w_in_bf[l], qn, kn, gmn, seg, cos_s, sin_s)
        qh = q.reshape(nd, N_HEADS, HEAD_DIM)
        zq = jnp.zeros_like(qh)
        qpad = jnp.where(kv_of_head == 0, jnp.concatenate([qh, zq], -1), jnp.concatenate([zq, qh], -1))
        w = cache_k.shape[2]
        o, nk, nv = _attn_s(qpad, k[:, None, :], v[:, None, :], cache_k[l].reshape(nd, w, KV_WIDTH),
                            cache_v[l].reshape(nd, w, KV_WIDTH),
                            jnp.broadcast_to(attn_sinks[l][:, None], (N_HEADS, LANES)))
        o = jnp.where(kv_of_head == 0, o[..., :HEAD_DIM], o[..., HEAD_DIM:]).reshape(nd, ATTN_WIDTH)
        wdiag = jnp.repeat(gm_ws[l][:, 0, 0], GM_WIDTH // GM_GROUPS)[None, :]
        bsrow = jnp.repeat(gm_bs[l][:, 0], GM_WIDTH // GM_GROUPS)[None, :]
        res = _outproj_s(o, u, gv, wdiag, bsrow, xs, mod_s[l], w_out_bf[l], n2, router)
        if router is None:
            xs = _ffn(res[1], res[0], mod_s[l], dense_bf[0][i], dense_bf[1][i], dense_bf[2][i], None)
        else:
            xs = _moe_all(res[1], res[0], res[2], mod_s[l], moe_bf[0][i], moe_bf[1][i], moe_bf[2][i], None)
        k_s.append(nk.reshape(nd, w, N_KV_HEADS, HEAD_DIM))
        v_s.append(nv.reshape(nd, w, N_KV_HEADS, HEAD_DIM))
        g_s.append(gv[:, None, :])

    return (xp.reshape(batch, seq, d), xs.reshape(nd, 1, d), jnp.stack(k_p), jnp.stack(v_p), jnp.stack(g_p),
            jnp.stack(k_s), jnp.stack(v_s), jnp.stack(g_s))
```

```python
import functools

import numpy as np
import jax
import jax.numpy as jnp
from jax import lax
from jax.experimental import pallas as pl
from jax.experimental.pallas import tpu as pltpu

D_MODEL = 1024
HEAD_DIM = 64
N_HEADS = 8
N_KV_HEADS = 2
KV_GROUP = N_HEADS // N_KV_HEADS
ATTN_WIDTH = N_HEADS * HEAD_DIM
KV_WIDTH = N_KV_HEADS * HEAD_DIM
GM_WIDTH = 512
GM_GROUPS = 4
WINDOW = 128
CHUNK = 128
D_FF = 2816
MXU_DIM = 256
FF_SPLIT = (0, 6 * MXU_DIM, D_FF)
N_EXPERTS = 8
N_ADA = 6
IN_COLS = ATTN_WIDTH + 2 * KV_WIDTH + 2 * GM_WIDTH
PAST_LEN = 16384
ROPE_THETA = 10000.0
EPS = 1e-6
NEG_INF = -1e30
LANES = 128

BF = jnp.bfloat16
F32 = jnp.float32
MIB = 1024 * 1024


ROW_SPLIT = 2


def _params(sem, vmem_mib):
    return pltpu.CompilerParams(dimension_semantics=sem, vmem_limit_bytes=vmem_mib * MIB)


def _dot(a, b):
    return jnp.dot(a, b, preferred_element_type=F32)


def _dot_nt(a, b):
    return lax.dot_general(a, b, (((1,), (1,)), ((), ())), preferred_element_type=F32)


def _rms(x, w):
    ms = jnp.mean(x * x, axis=-1, keepdims=True)
    return x * lax.rsqrt(ms + EPS) * w


def _ada_body(c_ref, w_ref, b_ref, o_ref):
    s = jax.nn.silu(c_ref[...]).astype(BF)
    o_ref[...] = _dot(s, w_ref[...].astype(BF)) + b_ref[...]


def _ada(c_all, w_ada, b_ada):
    depth, d, cols = w_ada.shape
    n = c_all.shape[0]
    tn = 1024
    return pl.pallas_call(
        _ada_body,
        grid=(depth, cols // tn),
        in_specs=[
            pl.BlockSpec((n, d), lambda l, j: (0, 0)),
            pl.BlockSpec((None, d, tn), lambda l, j: (l, 0, j)),
            pl.BlockSpec((None, 1, tn), lambda l, j: (l, 0, j)),
        ],
        out_specs=pl.BlockSpec((None, n, tn), lambda l, j: (l, 0, j)),
        out_shape=jax.ShapeDtypeStruct((depth, n, cols), F32),
        compiler_params=_params(("arbitrary", "arbitrary"), 32),
        name="ada",
    )(c_all, w_ada, b_ada.reshape(depth, 1, cols))


def _swap_halves(t):
    n = t.shape[-1]
    lane = lax.broadcasted_iota(jnp.int32, (1, n), 1)
    first = (lane % HEAD_DIM) < (HEAD_DIM // 2)
    return jnp.where(first, pltpu.roll(t, n - HEAD_DIM // 2, axis=1), pltpu.roll(t, HEAD_DIM // 2, axis=1))


def _inproj_compute(x, sh, sc, n1, w_ref, qn, kn, gmn, seg_ref, cos, sin):
    h = _rms(x, n1) * (1.0 + sc) + sh
    z = _dot(h.astype(BF), w_ref[...])
    q = z[:, :ATTN_WIDTH]
    k = z[:, ATTN_WIDTH:ATTN_WIDTH + KV_WIDTH]
    v = z[:, ATTN_WIDTH + KV_WIDTH:ATTN_WIDTH + 2 * KV_WIDTH]
    gm = z[:, ATTN_WIDTH + 2 * KV_WIDTH:]

    def head_norm(t, seg, wn):
        ms = _dot((t * t).astype(BF), seg) * (1.0 / HEAD_DIM)
        return t * lax.rsqrt(ms + EPS) * wn

    def rope(t):
        reps = t.shape[-1] // LANES
        c = jnp.concatenate([cos] * reps, axis=-1) if reps > 1 else cos
        s = jnp.concatenate([sin] * reps, axis=-1) if reps > 1 else sin
        return t * c + _swap_halves(t) * s

    q = rope(head_norm(q, seg_ref[...], qn)) * (HEAD_DIM ** -0.5)
    k = rope(head_norm(k, seg_ref[:KV_WIDTH, :KV_WIDTH], kn))
    g = jax.nn.gelu(gm)
    u = g[:, :GM_WIDTH]
    gv = _rms(g[:, GM_WIDTH:], gmn)
    return q, k, v, u, gv


def _dup_heads(t):
    lane = lax.broadcasted_iota(jnp.int32, (1, LANES), 1)
    lo = lane < HEAD_DIM
    r = pltpu.roll(t, HEAD_DIM, axis=1)
    return jnp.concatenate([jnp.where(lo, t, r), jnp.where(lo, r, t)], axis=-1)


def _inproj_p_body(tiles_per_batch, x_ref, mod_ref, n1_ref, w_ref, qn_ref, kn_ref, gmn_ref, seg_ref,
                   cos_ref, sin_ref, q_ref, kd_ref, vd_ref, u_ref, gv_ref, kl_ref, vl_ref, gvl_ref):
    i = pl.program_id(0)
    b = i // tiles_per_batch
    sh = mod_ref[0, pl.ds(b, 1), :]
    sc = mod_ref[1, pl.ds(b, 1), :]
    hs = x_ref.shape[0] // ROW_SPLIT
    for hh in range(ROW_SPLIT):
        rs = slice(hh * hs, (hh + 1) * hs)
        q, k, v, u, gv = _inproj_compute(x_ref[rs, :], sh, sc, n1_ref[...], w_ref, qn_ref[...], kn_ref[...],
                                         gmn_ref[...], seg_ref, cos_ref[rs, :], sin_ref[rs, :])
        q_ref[rs, :] = q.astype(BF)
        kd_ref[rs, :] = _dup_heads(k).astype(BF)
        vd_ref[rs, :] = _dup_heads(v).astype(BF)
        u_ref[rs, :] = u.astype(BF)
        gv_ref[rs, :] = gv.astype(BF)

    @pl.when(i % tiles_per_batch == tiles_per_batch - 1)
    def _():
        kl_ref[...] = k[hs - WINDOW:, :]
        vl_ref[...] = v[hs - WINDOW:, :]
        gvl_ref[...] = gv[hs - CHUNK:, :]


def _inproj_p(x, mod, n1, w_bf, qn, kn, gmn, seg, cos, sin, batch, seq):
    t = x.shape[0]
    tm = min(512, seq)
    tpb = seq // tm
    row = lambda i: (i, 0)
    full = lambda i: (0, 0)
    last = lambda i: (i // tpb, 0, 0)
    return pl.pallas_call(
        functools.partial(_inproj_p_body, tpb),
        grid=(t // tm,),
        in_specs=[
            pl.BlockSpec((tm, D_MODEL), row),
            pl.BlockSpec((N_ADA, batch, D_MODEL), lambda i: (0, 0, 0)),
            pl.BlockSpec((1, D_MODEL), full),
            pl.BlockSpec((D_MODEL, IN_COLS), full),
            pl.BlockSpec((1, ATTN_WIDTH), full),
            pl.BlockSpec((1, KV_WIDTH), full),
            pl.BlockSpec((1, GM_WIDTH), full),
            pl.BlockSpec((ATTN_WIDTH, ATTN_WIDTH), full),
            pl.BlockSpec((tm, LANES), lambda i: (i % tpb, 0)),
            pl.BlockSpec((tm, LANES), lambda i: (i % tpb, 0)),
        ],
        out_specs=[
            pl.BlockSpec((tm, ATTN_WIDTH), row),
            pl.BlockSpec((tm, 2 * KV_WIDTH), row),
            pl.BlockSpec((tm, 2 * KV_WIDTH), row),
            pl.BlockSpec((tm, GM_WIDTH), row),
            pl.BlockSpec((tm, GM_WIDTH), row),
            pl.BlockSpec((None, WINDOW, KV_WIDTH), last),
            pl.BlockSpec((None, WINDOW, KV_WIDTH), last),
            pl.BlockSpec((None, CHUNK, GM_WIDTH), last),
        ],
        out_shape=[
            jax.ShapeDtypeStruct((t, ATTN_WIDTH), BF),
            jax.ShapeDtypeStruct((t, 2 * KV_WIDTH), BF),
            jax.ShapeDtypeStruct((t, 2 * KV_WIDTH), BF),
            jax.ShapeDtypeStruct((t, GM_WIDTH), BF),
            jax.ShapeDtypeStruct((t, GM_WIDTH), BF),
            jax.ShapeDtypeStruct((batch, WINDOW, KV_WIDTH), F32),
            jax.ShapeDtypeStruct((batch, WINDOW, KV_WIDTH), F32),
            jax.ShapeDtypeStruct((batch, CHUNK, GM_WIDTH), F32),
        ],
        compiler_params=_params(("arbitrary",), 48),
        name="inproj_p",
    )(x, mod, n1, w_bf, qn, kn, gmn, seg, cos, sin)


def _inproj_s_body(x_ref, mod_ref, n1_ref, w_ref, qn_ref, kn_ref, gmn_ref, seg_ref, cos_ref, sin_ref,
                   q_ref, k_ref, v_ref, u_ref, gv_ref):
    q, k, v, u, gv = _inproj_compute(x_ref[...], mod_ref[0], mod_ref[1], n1_ref[...], w_ref, qn_ref[...],
                                     kn_ref[...], gmn_ref[...], seg_ref, cos_ref[...], sin_ref[...])
    q_ref[...] = q
    k_ref[...] = k
    v_ref[...] = v
    u_ref[...] = u
    gv_ref[...] = gv


def _inproj_s(x, mod, n1, w_bf, qn, kn, gmn, seg, cos, sin):
    n = x.shape[0]
    widths = (ATTN_WIDTH, KV_WIDTH, KV_WIDTH, GM_WIDTH, GM_WIDTH)
    return pl.pallas_call(
        _inproj_s_body,
        out_shape=[jax.ShapeDtypeStruct((n, w), F32) for w in widths],
        compiler_params=pltpu.CompilerParams(vmem_limit_bytes=48 * MIB),
        name="inproj_s",
    )(x, mod, n1, w_bf, qn, kn, gmn, seg, cos, sin)


def _mix_p_body(nblk, q_ref, kc_ref, kp_ref, vc_ref, vp_ref, u_ref, gv_ref, ws_ref, bst_ref, sink_ref, o_ref):
    i = pl.program_id(1)
    blk = WINDOW
    lane = lax.broadcasted_iota(jnp.int32, (1, LANES), 1)
    lo = lane < HEAD_DIM
    cols = KV_GROUP * blk
    iq = lax.broadcasted_iota(jnp.int32, (2 * blk, cols), 1) % blk
    jk = lax.broadcasted_iota(jnp.int32, (2 * blk, cols), 0)
    band = (jk > iq) & (jk <= iq + blk)
    band_first = band & ((jk >= blk) | (i > 0))
    tri = (lax.broadcasted_iota(jnp.int32, (CHUNK, CHUNK), 0)
           >= lax.broadcasted_iota(jnp.int32, (CHUNK, CHUNK), 1))
    wm = [jnp.where(tri, ws_ref[g], 0.0).astype(BF) for g in range(GM_GROUPS)]

    for n in range(nblk):
        r0 = n * blk
        if n == 0:
            kk = jnp.concatenate([kp_ref[...], kc_ref[0:blk, :]], axis=0)
            vv = jnp.concatenate([vp_ref[...], vc_ref[0:blk, :]], axis=0)
            mask = band_first
        else:
            kk = kc_ref[r0 - blk:r0 + blk, :]
            vv = vc_ref[r0 - blk:r0 + blk, :]
            mask = band
        for kvh in range(N_KV_HEADS):
            c0 = 2 * kvh
            qa = q_ref[r0:r0 + blk, c0 * LANES:(c0 + 1) * LANES]
            qb = q_ref[r0:r0 + blk, (c0 + 1) * LANES:(c0 + 2) * LANES]
            zero = jnp.zeros_like(qa)
            qq = jnp.concatenate([jnp.where(lo, qa, zero), jnp.where(lo, zero, qa),
                                  jnp.where(lo, qb, zero), jnp.where(lo, zero, qb)], axis=0)
            s = _dot_nt(kk[:, kvh * LANES:(kvh + 1) * LANES], qq)
            s = jnp.where(mask, s, NEG_INF)
            sink = jnp.concatenate(
                [jnp.full((1, blk), sink_ref[kvh * KV_GROUP + g], F32) for g in range(KV_GROUP)], axis=1)
            m = jnp.maximum(jnp.max(s, axis=0, keepdims=True), sink)
            p = jnp.exp(s - m)
            den = jnp.sum(p, axis=0, keepdims=True) + jnp.exp(sink - m)
            p = (p * (1.0 / den)).astype(BF)
            o = lax.dot_general(p, vv[:, kvh * LANES:(kvh + 1) * LANES], (((0,), (0,)), ((), ())),
                                preferred_element_type=F32)
            o_ref[r0:r0 + blk, c0 * LANES:(c0 + 1) * LANES] = jnp.where(
                lo, o[0:blk], o[blk:2 * blk]).astype(BF)
            o_ref[r0:r0 + blk, (c0 + 1) * LANES:(c0 + 2) * LANES] = jnp.where(
                lo, o[2 * blk:3 * blk], o[3 * blk:4 * blk]).astype(BF)
        for g in range(GM_GROUPS):
            cs = slice(g * LANES, (g + 1) * LANES)
            sp = _dot(wm[g], gv_ref[r0:r0 + blk, cs]) + bst_ref[:, g:g + 1]
            o_ref[r0:r0 + blk, ATTN_WIDTH + g * LANES:ATTN_WIDTH + (g + 1) * LANES] = (
                u_ref[r0:r0 + blk, cs].astype(F32) * sp).astype(BF)


def _mix_p(q, kd, vd, u, gv, ws, bst, sinks, batch, seq):
    t = q.shape[0]
    tq = min(512, seq)
    nblk = tq // WINDOW
    tpb = seq // tq
    cur = lambda b, i: (b * tpb + i, 0)
    prev = lambda b, i: (jnp.maximum((b * tpb + i) * nblk - 1, b * tpb * nblk), 0)
    return pl.pallas_call(
        functools.partial(_mix_p_body, nblk),
        grid=(batch, tpb),
        in_specs=[
            pl.BlockSpec((tq, ATTN_WIDTH), cur),
            pl.BlockSpec((tq, 2 * KV_WIDTH), cur),
            pl.BlockSpec((WINDOW, 2 * KV_WIDTH), prev),
            pl.BlockSpec((tq, 2 * KV_WIDTH), cur),
            pl.BlockSpec((WINDOW, 2 * KV_WIDTH), prev),
            pl.BlockSpec((tq, GM_WIDTH), cur),
            pl.BlockSpec((tq, GM_WIDTH), cur),
            pl.BlockSpec((GM_GROUPS, CHUNK, CHUNK), lambda b, i: (0, 0, 0)),
            pl.BlockSpec((CHUNK, GM_GROUPS), lambda b, i: (0, 0)),
            pl.BlockSpec(memory_space=pltpu.SMEM),
        ],
        out_specs=pl.BlockSpec((tq, D_MODEL), cur),
        out_shape=jax.ShapeDtypeStruct((t, D_MODEL), BF),
        compiler_params=_params(("arbitrary", "arbitrary"), 48),
        name="mix_p",
    )(q, kd, kd, vd, vd, u, gv, ws, bst, sinks)


def _attn_s_body(q_ref, kn_ref, vn_ref, ck_ref, cv_ref, sink_ref, o_ref, nk_ref, nv_ref):
    w = ck_ref.shape[1]
    row = lax.broadcasted_iota(jnp.int32, (1, w, 1), 1)
    nk = jnp.where(row == w - 1, kn_ref[...], pltpu.roll(ck_ref[...], w - 1, axis=1))
    nv = jnp.where(row == w - 1, vn_ref[...], pltpu.roll(cv_ref[...], w - 1, axis=1))
    nk_ref[...] = nk
    nv_ref[...] = nv
    s = jnp.einsum('bhd,bjd->bhj', q_ref[...].astype(BF), nk.astype(BF), preferred_element_type=F32)
    sink = sink_ref[...][None, :, 0:1]
    m = jnp.maximum(jnp.max(s, axis=-1, keepdims=True), sink)
    p = jnp.exp(s - m)
    den = jnp.sum(p, axis=-1, keepdims=True) + jnp.exp(sink - m)
    o = jnp.einsum('bhj,bjd->bhd', p.astype(BF), nv.astype(BF), preferred_element_type=F32)
    o_ref[...] = o * (1.0 / den)


def _attn_s(qpad, k_new, v_new, ck, cv, sink_tile):
    n, w, kw = ck.shape
    bb = min(16, n)
    blk3 = lambda r, c: pl.BlockSpec((bb, r, c), lambda i: (i, 0, 0))
    return pl.pallas_call(
        _attn_s_body,
        grid=(n // bb,),
        in_specs=[blk3(N_HEADS, LANES), blk3(1, kw), blk3(1, kw), blk3(w, kw), blk3(w, kw),
                  pl.BlockSpec((N_HEADS, LANES), lambda i: (0, 0))],
        out_specs=[blk3(N_HEADS, LANES), blk3(w, kw), blk3(w, kw)],
        out_shape=[jax.ShapeDtypeStruct((n, N_HEADS, LANES), F32),
                   jax.ShapeDtypeStruct((n, w, kw), F32),
                   jax.ShapeDtypeStruct((n, w, kw), F32)],
        compiler_params=_params(("arbitrary",), 32),
        name="attn_s",
    )(qpad, k_new, v_new, ck, cv, sink_tile)


def _top2(h2, rw_ref, rb_ref):
    hi = h2.astype(BF)
    lo = (h2 - hi.astype(F32)).astype(BF)
    rw = rw_ref[...]
    whi = rw.astype(BF)
    wlo = (rw - whi.astype(F32)).astype(BF)
    logits = _dot(hi, whi) + _dot(lo, whi) + _dot(hi, wlo) + rb_ref[...]
    lane = lax.broadcasted_iota(jnp.int32, logits.shape, 1).astype(F32)
    e = jnp.exp(logits - jnp.max(logits, axis=-1, keepdims=True))
    p = e / jnp.sum(e, axis=-1, keepdims=True)
    m1 = jnp.max(p, axis=-1, keepdims=True)
    i1 = jnp.min(jnp.where(p == m1, lane, float(LANES)), axis=-1, keepdims=True)
    p2 = jnp.where(lane == i1, -1.0, p)
    m2 = jnp.max(p2, axis=-1, keepdims=True)
    i2 = jnp.min(jnp.where(p2 == m2, lane, float(LANES)), axis=-1, keepdims=True)
    tot = m1 + m2
    return lane, i1, i2, m1 / tot, m2 / tot


def _route_gates(h2, rw_ref, rb_ref):
    lane, i1, i2, g1, g2 = _top2(h2, rw_ref, rb_ref)
    return jnp.where(lane == i1, g1, 0.0) + jnp.where(lane == i2, g2, 0.0)


ROUTE_E, ROUTE_RANK, ROUTE_GATE = 0, 2, 4


def _route_ranked(h2, rw_ref, rb_ref, tri_ref, cnt_ref):
    lane, i1, i2, g1, g2 = _top2(h2, rw_ref, rb_ref)
    oh1 = lane == i1
    oh2 = lane == i2
    hit = jnp.where(oh1, 1.0, 0.0) + jnp.where(oh2, 1.0, 0.0)
    before = cnt_ref[...] + _dot(tri_ref[...], hit.astype(BF))
    r1 = jnp.sum(jnp.where(oh1, before, 0.0), axis=-1, keepdims=True)
    r2 = jnp.sum(jnp.where(oh2, before, 0.0), axis=-1, keepdims=True)
    cnt_ref[...] += jnp.sum(hit, axis=0, keepdims=True)
    cols = (i1, i2, r1, r2, g1, g2)
    out = jnp.zeros_like(lane)
    for j, c in enumerate(cols):
        out = jnp.where(lane == float(j), c, out)
    return out


def _outproj_compute(mix_bf, x, ga1, sh2, sc2, w_ref, n2):
    xn = x + ga1 * _dot(mix_bf, w_ref[...])
    h2 = _rms(xn, n2) * (1.0 + sc2) + sh2
    return xn, h2


def _outproj_p_body(tiles_per_batch, with_router, mix_ref, x_ref, mod_ref, w_ref, n2_ref, *rest):
    i = pl.program_id(0)
    b = i // tiles_per_batch
    mrow = lambda j: mod_ref[j, pl.ds(b, 1), :]
    if with_router:
        rw_ref, rb_ref, tri_ref, xn_ref, h2_ref, route_ref, cnt_ref = rest

        @pl.when(i == 0)
        def _():
            cnt_ref[...] = jnp.zeros_like(cnt_ref)
    else:
        xn_ref, h2_ref = rest

    xn, h2 = _outproj_compute(mix_ref[...], x_ref[...], mrow(2), mrow(3), mrow(4), w_ref, n2_ref[...])
    if with_router:
        route_ref[...] = _route_ranked(h2, rw_ref, rb_ref, tri_ref, cnt_ref)
        h2_ref[...] = h2
    else:
        h2_ref[...] = h2.astype(BF)
    xn_ref[...] = xn


def _outproj_p(mix, x, mod, w_bf, n2, router, batch, seq):
    t = x.shape[0]
    tm = min(512, seq)
    tpb = seq // tm
    row = lambda i: (i, 0)
    full = lambda i: (0, 0)
    in_specs = [
        pl.BlockSpec((tm, D_MODEL), row),
        pl.BlockSpec((tm, D_MODEL), row),
        pl.BlockSpec((N_ADA, batch, D_MODEL), lambda i: (0, 0, 0)),
        pl.BlockSpec((D_MODEL, D_MODEL), full),
        pl.BlockSpec((1, D_MODEL), full),
    ]
    out_specs = [pl.BlockSpec((tm, D_MODEL), row), pl.BlockSpec((tm, D_MODEL), row)]
    out_shape = [jax.ShapeDtypeStruct((t, D_MODEL), F32),
                 jax.ShapeDtypeStruct((t, D_MODEL), BF if router is None else F32)]
    args = [mix, x, mod, w_bf, n2]
    if router is not None:
        tri = jnp.asarray(np.tri(tm, k=-1), BF)
        in_specs += [pl.BlockSpec((D_MODEL, LANES), full), pl.BlockSpec((1, LANES), full),
                     pl.BlockSpec((tm, tm), full)]
        out_specs += [pl.BlockSpec((tm, LANES), row), pl.BlockSpec((1, LANES), full)]
        out_shape += [jax.ShapeDtypeStruct((t, LANES), F32), jax.ShapeDtypeStruct((1, LANES), F32)]
        args += list(router) + [tri]
    return pl.pallas_call(
        functools.partial(_outproj_p_body, tpb, router is not None),
        grid=(t // tm,),
        in_specs=in_specs, out_specs=out_specs, out_shape=out_shape,
        compiler_params=_params(("arbitrary",), 48),
        name="outproj_p",
    )(*args)


def _outproj_s_body(with_router, o_ref, u_ref, gv_ref, wdiag_ref, bsrow_ref, x_ref, mod_ref, w_ref, n2_ref, *rest):
    gate = u_ref[...] * (wdiag_ref[...] * gv_ref[...] + bsrow_ref[...])
    mix = jnp.concatenate([o_ref[...], gate], axis=-1).astype(BF)
    xn, h2 = _outproj_compute(mix, x_ref[...], mod_ref[2], mod_ref[3], mod_ref[4], w_ref, n2_ref[...])
    if with_router:
        rw_ref, rb_ref, xn_ref, h2_ref, gates_ref = rest
        gates_ref[...] = _route_gates(h2, rw_ref, rb_ref)
    else:
        xn_ref, h2_ref = rest
    xn_ref[...] = xn
    h2_ref[...] = h2.astype(BF)


def _outproj_s(o, u, gv, wdiag, bsrow, x, mod, w_bf, n2, router):
    n = x.shape[0]
    out_shape = [jax.ShapeDtypeStruct((n, D_MODEL), F32), jax.ShapeDtypeStruct((n, D_MODEL), BF)]
    args = [o, u, gv, wdiag, bsrow, x, mod, w_bf, n2]
    if router is not None:
        out_shape.append(jax.ShapeDtypeStruct((n, LANES), F32))
        args += list(router)
    return pl.pallas_call(
        functools.partial(_outproj_s_body, router is not None),
        out_shape=out_shape,
        compiler_params=pltpu.CompilerParams(vmem_limit_bytes=32 * MIB),
        name="outproj_s",
    )(*args)


def _swiglu(h_bf, wg_ref, wu_ref, wd_ref):
    y = None
    for c in range(len(FF_SPLIT) - 1):
        sl = slice(FF_SPLIT[c], FF_SPLIT[c + 1])
        a = (jax.nn.silu(_dot(h_bf, wg_ref[:, sl])) * _dot(h_bf, wu_ref[:, sl])).astype(BF)
        part = _dot(a, wd_ref[sl, :])
        y = part if y is None else y + part
    return y


def _ffn_body(tiles_per_batch, h_ref, x_ref, mod_ref, wg_ref, wu_ref, wd_ref, o_ref):
    if tiles_per_batch:
        b = pl.program_id(0) // tiles_per_batch
        ga2 = mod_ref[5, pl.ds(b, 1), :]
    else:
        ga2 = mod_ref[5]
    o_ref[...] = x_ref[...] + ga2 * _swiglu(h_ref[...], wg_ref, wu_ref, wd_ref)


def _ffn(h2, x, mod, wg, wu, wd, seq):
    t = x.shape[0]
    tm = min(512, t if seq is None else seq)
    tpb = 0 if seq is None else seq // tm
    row = lambda i: (i, 0)
    full = lambda i: (0, 0)
    return pl.pallas_call(
        functools.partial(_ffn_body, tpb),
        grid=(t // tm,),
        in_specs=[
            pl.BlockSpec((tm, D_MODEL), row),
            pl.BlockSpec((tm, D_MODEL), row),
            pl.BlockSpec(mod.shape, lambda i: (0, 0, 0)),
            pl.BlockSpec((D_MODEL, D_FF), full),
            pl.BlockSpec((D_MODEL, D_FF), full),
            pl.BlockSpec((D_FF, D_MODEL), full),
        ],
        out_specs=pl.BlockSpec((tm, D_MODEL), row),
        out_shape=jax.ShapeDtypeStruct((t, D_MODEL), F32),
        compiler_params=_params(("arbitrary",), 56),
        name="ffn",
    )(h2, x, mod, wg, wu, wd)


def _outffn_p_body(tiles_per_batch, mix_ref, x_ref, mod_ref, w_ref, n2_ref, wg_ref, wu_ref, wd_ref, o_ref):
    b = pl.program_id(0) // tiles_per_batch
    mrow = lambda j: mod_ref[j, pl.ds(b, 1), :]
    xn, h2 = _outproj_compute(mix_ref[...], x_ref[...], mrow(2), mrow(3), mrow(4), w_ref, n2_ref[...])
    o_ref[...] = xn + mrow(5) * _swiglu(h2.astype(BF), wg_ref, wu_ref, wd_ref)


def _outffn_p(mix, x, mod, w_bf, n2, wg, wu, wd, batch, seq):
    t = x.shape[0]
    tm = min(512, seq)
    row = lambda i: (i, 0)
    const = lambda shape: pl.BlockSpec(shape, lambda i: (0,) * len(shape), pipeline_mode=pl.Buffered(1))
    return pl.pallas_call(
        functools.partial(_outffn_p_body, seq // tm),
        grid=(t // tm,),
        in_specs=[
            pl.BlockSpec((tm, D_MODEL), row),
            pl.BlockSpec((tm, D_MODEL), row),
            const((N_ADA, batch, D_MODEL)),
            const((D_MODEL, D_MODEL)),
            const((1, D_MODEL)),
            const((D_MODEL, D_FF)), const((D_MODEL, D_FF)), const((D_FF, D_MODEL)),
        ],
        out_specs=pl.BlockSpec((tm, D_MODEL), row),
        out_shape=jax.ShapeDtypeStruct((t, D_MODEL), F32),
        compiler_params=_params(("arbitrary",), 56),
        name="outffn_p",
    )(mix, x, mod, w_bf, n2, wg, wu, wd)


def _moe_all_body(tiles_per_batch, h_ref, x_ref, gates_ref, mod_ref, wg_ref, wu_ref, wd_ref, o_ref):
    e = pl.program_id(1)
    if tiles_per_batch:
        b = pl.program_id(0) // tiles_per_batch
        ga2 = mod_ref[5, pl.ds(b, 1), :]
    else:
        ga2 = mod_ref[5]

    @pl.when(e == 0)
    def _():
        o_ref[...] = jnp.zeros_like(o_ref)

    lane = lax.broadcasted_iota(jnp.int32, (1, LANES), 1)
    gate = jnp.sum(jnp.where(lane == e, gates_ref[...], 0.0), axis=-1, keepdims=True)
    o_ref[...] += gate * _swiglu(h_ref[...], wg_ref, wu_ref, wd_ref)

    @pl.when(e == N_EXPERTS - 1)
    def _():
        o_ref[...] = x_ref[...] + ga2 * o_ref[...]


def _moe_all(h2, x, gates, mod, wg, wu, wd, seq):
    t = x.shape[0]
    tm = min(512, t if seq is None else seq)
    tpb = 0 if seq is None else seq // tm
    row = lambda i, e: (i, 0)
    return pl.pallas_call(
        functools.partial(_moe_all_body, tpb),
        grid=(t // tm, N_EXPERTS),
        in_specs=[
            pl.BlockSpec((tm, D_MODEL), row),
            pl.BlockSpec((tm, D_MODEL), row),
            pl.BlockSpec((tm, LANES), row),
            pl.BlockSpec(mod.shape, lambda i, e: (0, 0, 0)),
            pl.BlockSpec((None, D_MODEL, D_FF), lambda i, e: (e, 0, 0)),
            pl.BlockSpec((None, D_MODEL, D_FF), lambda i, e: (e, 0, 0)),
            pl.BlockSpec((None, D_FF, D_MODEL), lambda i, e: (e, 0, 0)),
        ],
        out_specs=pl.BlockSpec((tm, D_MODEL), row),
        out_shape=jax.ShapeDtypeStruct((t, D_MODEL), F32),
        compiler_params=_params(("arbitrary", "arbitrary"), 56),
        name="moe_all",
    )(h2, x, gates, mod, wg, wu, wd)


TM_MOE = 512
TD = 256


SUB = 8
assert D_MODEL == SUB * LANES


def _to_token_tiles(ref, x):
    r = x.shape[0]
    for g in range(SUB):
        ref[pl.ds(g, r, stride=SUB), :] = x[:, g * LANES:(g + 1) * LANES]


def _from_token_tiles(ref, first, r):
    return jnp.concatenate([ref[pl.ds(first * SUB + g, r, stride=SUB), :] for g in range(SUB)], axis=-1)


def _token_copy(src, s, dst, d, sem):
    return pltpu.make_async_copy(src.at[pl.ds(pl.multiple_of(s, SUB), SUB), :],
                                 dst.at[pl.ds(pl.multiple_of(d, SUB), SUB), :], sem)


def _dispatch_body(pos_ref, pad_ref, h_ref, xs_ref, stage, sem, zsem):
    i = pl.program_id(0)
    n = pl.num_programs(0)
    td = h_ref.shape[0]
    slot = i % 2

    def wait_slot(s):
        for _ in range(2):
            pltpu.make_async_copy(stage.at[s], xs_ref.at[pl.ds(0, td * SUB), :], sem.at[s]).wait()

    @pl.when(i >= 2)
    def _():
        wait_slot(slot)

    _to_token_tiles(stage.at[slot], h_ref[...])

    def issue(r, c):
        for k in range(2):
            _token_copy(stage.at[slot], r * SUB, xs_ref, pos_ref[0, 0, k * td + r], sem.at[slot]).start(priority=k)
        return c

    lax.fori_loop(0, td, issue, 0, unroll=8)

    @pl.when(i == n - 1)
    def _():
        wait_slot(slot)

        @pl.when(n > 1)
        def _():
            wait_slot(1 - slot)

        stage[0] = jnp.zeros(stage.shape[1:], stage.dtype)
        for e in range(N_EXPERTS):
            lo = pad_ref[0, e]
            hi = pad_ref[1, e]

            def zero_token(r, c):
                _token_copy(stage.at[0], 0, xs_ref, r * SUB, zsem).start()
                return c

            def wait_token(r, c):
                _token_copy(stage.at[0], 0, xs_ref, 0, zsem).wait()
                return c

            lax.fori_loop(lo, hi, zero_token, 0)
            lax.fori_loop(lo, hi, wait_token, 0)

        def zero_blk(j, c):
            pltpu.make_async_copy(stage.at[0], xs_ref.at[pl.ds(pl.multiple_of(j * (td * SUB), SUB), td * SUB), :],
                                  zsem).start()
            return c

        def wait_blk(j, c):
            pltpu.make_async_copy(stage.at[0], xs_ref.at[pl.ds(0, td * SUB), :], zsem).wait()
            return c

        lax.fori_loop(pad_ref[0, N_EXPERTS], pad_ref[1, N_EXPERTS], zero_blk, 0)
        lax.fori_loop(pad_ref[0, N_EXPERTS], pad_ref[1, N_EXPERTS], wait_blk, 0)


def _dispatch(h2, pos_t, pad, npad):
    t = h2.shape[0]
    td = min(TD, t)
    return pl.pallas_call(
        _dispatch_body,
        grid=(t // td,),
        in_specs=[
            pl.BlockSpec((1, 1, 2 * td), lambda i: (i, 0, 0), memory_space=pltpu.SMEM),
            pl.BlockSpec(memory_space=pltpu.SMEM),
            pl.BlockSpec((td, D_MODEL), lambda i: (i, 0)),
        ],
        out_specs=pl.BlockSpec(memory_space=pl.ANY),
        out_shape=jax.ShapeDtypeStruct((npad * SUB, LANES), F32),
        scratch_shapes=[pltpu.VMEM((2, td * SUB, LANES), F32), pltpu.SemaphoreType.DMA((2,)),
                        pltpu.SemaphoreType.DMA(())],
        compiler_params=_params(("arbitrary",), 32),
        name="dispatch",
    )(pos_t, pad, h2)


def _moe_body(te_ref, src_ref, nv_ref, x_ref, wg_ref, wu_ref, wd_ref, o_ref):
    i = pl.program_id(0)

    @pl.when(nv_ref[i] > 0)
    def _():
        x = _from_token_tiles(x_ref, 0, TM_MOE).astype(BF)
        _to_token_tiles(o_ref, _swiglu(x, wg_ref, wu_ref, wd_ref))

    @pl.when(nv_ref[i] == 0)
    def _():
        o_ref[...] = jnp.zeros_like(o_ref)


def _moe(xs, tile_e, tile_src, tile_nv, wg, wu, wd):
    rows = TM_MOE * SUB
    wspec = lambda shape: pl.BlockSpec((None,) + shape, lambda i, te, src, nv: (te[i], 0, 0),
                                       pipeline_mode=pl.Buffered(1))
    return pl.pallas_call(
        _moe_body,
        grid_spec=pltpu.PrefetchScalarGridSpec(
            num_scalar_prefetch=3,
            grid=(xs.shape[0] // rows,),
            in_specs=[
                pl.BlockSpec((rows, LANES), lambda i, te, src, nv: (src[i], 0)),
                wspec((D_MODEL, D_FF)), wspec((D_MODEL, D_FF)), wspec((D_FF, D_MODEL)),
            ],
            out_specs=pl.BlockSpec((rows, LANES), lambda i, te, src, nv: (i, 0)),
        ),
        out_shape=jax.ShapeDtypeStruct(xs.shape, F32),
        compiler_params=_params(("arbitrary",), 56),
        name="moe",
    )(tile_e, tile_src, tile_nv, xs, wg, wu, wd)


def _combine_body(tiles_per_batch, posc_ref, posn_ref, x_ref, route_ref, mod_ref, ys_ref, o_ref, buf, sem):
    i = pl.program_id(0)
    n = pl.num_programs(0)
    tc = x_ref.shape[0]

    def gather(p_ref, s):
        def issue(r, c):
            for k in range(2):
                _token_copy(ys_ref, p_ref[0, 0, k * tc + r], buf.at[s], (k * tc + r) * SUB,
                            sem.at[s]).start(priority=k)
            return c

        lax.fori_loop(0, tc, issue, 0, unroll=8)

    @pl.when(i == 0)
    def _():
        gather(posc_ref, 0)

    @pl.when(i + 1 < n)
    def _():
        gather(posn_ref, (i + 1) % 2)

    slot = i % 2
    pltpu.make_async_copy(ys_ref.at[pl.ds(0, 2 * tc * SUB), :], buf.at[slot], sem.at[slot]).wait()
    lane = lax.broadcasted_iota(jnp.int32, (1, LANES), 1)
    rt = route_ref[...]
    g1 = jnp.sum(jnp.where(lane == ROUTE_GATE, rt, 0.0), axis=-1, keepdims=True)
    g2 = jnp.sum(jnp.where(lane == ROUTE_GATE + 1, rt, 0.0), axis=-1, keepdims=True)
    y = g1 * _from_token_tiles(buf.at[slot], 0, tc) + g2 * _from_token_tiles(buf.at[slot], tc, tc)
    ga2 = mod_ref[5, pl.ds(i // tiles_per_batch, 1), :]
    o_ref[...] = x_ref[...] + ga2 * y


def _combine(ys, pos_t, x, route, mod, seq):
    t = x.shape[0]
    tc = min(TD, t)
    nt = t // tc
    row = lambda i: (i, 0)
    return pl.pallas_call(
        functools.partial(_combine_body, seq // tc),
        grid=(nt,),
        in_specs=[
            pl.BlockSpec((1, 1, 2 * tc), lambda i: (i, 0, 0), memory_space=pltpu.SMEM),
            pl.BlockSpec((1, 1, 2 * tc), lambda i: (jnp.minimum(i + 1, nt - 1), 0, 0), memory_space=pltpu.SMEM),
            pl.BlockSpec((tc, D_MODEL), row),
            pl.BlockSpec((tc, LANES), row),
            pl.BlockSpec(mod.shape, lambda i: (0, 0, 0)),
            pl.BlockSpec(memory_space=pl.ANY),
        ],
        out_specs=pl.BlockSpec((tc, D_MODEL), row),
        out_shape=jax.ShapeDtypeStruct((t, D_MODEL), F32),
        scratch_shapes=[pltpu.VMEM((2, 2 * tc * SUB, LANES), F32), pltpu.SemaphoreType.DMA((2,))],
        compiler_params=_params(("arbitrary",), 32),
        name="combine",
    )(pos_t, pos_t, x, route, mod, ys)


def _moe_routed(h2, xn, route, cnt, mod, wg, wu, wd, seq):
    t = h2.shape[0]
    td = min(TD, t)
    nt_max = pl.cdiv(2 * t, TM_MOE) + N_EXPERTS
    npad = nt_max * TM_MOE
    counts = cnt[0, :N_EXPERTS].astype(jnp.int32)
    ntile = (counts + TM_MOE - 1) // TM_MOE
    eid = jnp.arange(N_EXPERTS)
    tile_end = jnp.sum(jnp.where(eid[None, :] <= eid[:, None], ntile[None, :], 0), axis=1)
    off = (tile_end - ntile) * TM_MOE
    e12 = route[:, ROUTE_E:ROUTE_E + 2].astype(jnp.int32)
    r12 = route[:, ROUTE_RANK:ROUTE_RANK + 2].astype(jnp.int32)
    onehot = e12[:, :, None] == jnp.arange(N_EXPERTS)[None, None, :]
    pos = jnp.sum(jnp.where(onehot, off[None, None, :], 0), axis=-1) + r12
    pos_t = (pos * SUB).reshape(t // td, td, 2).transpose(0, 2, 1).reshape(t // td, 1, 2 * td)
    total = tile_end[-1]
    tid = jnp.arange(nt_max)
    tile_e = jnp.minimum(jnp.sum(tid[:, None] >= tile_end[None, :], axis=1), N_EXPERTS - 1).astype(jnp.int32)
    tile_nv = (tid < total).astype(jnp.int32)
    tile_src = jnp.minimum(tid, total - 1).astype(jnp.int32)
    pad = jnp.stack([jnp.concatenate([off + counts, (total * (TM_MOE // td))[None]]),
                     jnp.concatenate([off + ntile * TM_MOE, jnp.full((1,), npad // td, jnp.int32)])]).astype(jnp.int32)
    xs = _dispatch(h2, pos_t, pad, npad)
    ys = _moe(xs, tile_e, tile_src, tile_nv, wg, wu, wd)
    return _combine(ys, pos_t, xn, route, mod, seq)


def _rope_tables(pos):
    inv = ROPE_THETA ** (-np.arange(0, HEAD_DIM, 2, dtype=np.float64) / HEAD_DIM)
    ang = np.asarray(pos, np.float64)[:, None] * inv[None, :]
    cos = np.concatenate([np.cos(ang), np.cos(ang)], axis=-1)
    sin = np.concatenate([-np.sin(ang), np.sin(ang)], axis=-1)
    reps = LANES // HEAD_DIM
    return (jnp.asarray(np.tile(cos, (1, reps)), F32), jnp.asarray(np.tile(sin, (1, reps)), F32))


def kernel(x_prompt, x_sample, cache_k, cache_v, c_prompt, c_sample, w_ada, b_ada, norm1_w, norm2_w, w_in,
           q_norm_w, k_norm_w, attn_sinks, gm_norm_w, gm_ws, gm_bs, w_out, dense_w_gate, dense_w_up,
           dense_w_down, router_w, router_b, moe_w_gate, moe_w_up, moe_w_down):
    batch, seq, d = x_prompt.shape
    nd = x_sample.shape[0]
    depth = w_in.shape[0]
    t = batch * seq

    mod = _ada(jnp.concatenate([c_prompt, c_sample], axis=0), w_ada, b_ada)
    mod_p = mod[:, :batch].reshape(depth, batch, N_ADA, d).transpose(0, 2, 1, 3)
    mod_s = mod[:, batch:].reshape(depth, nd, N_ADA, d).transpose(0, 2, 1, 3)

    cos_p, sin_p = _rope_tables(np.arange(seq))
    cos_s, sin_s = _rope_tables(np.array([PAST_LEN]))
    head_of = np.arange(ATTN_WIDTH) // HEAD_DIM
    seg = jnp.asarray(head_of[:, None] == head_of[None, :], BF)
    kv_of_head = (jnp.arange(N_HEADS) // KV_GROUP)[None, :, None]

    w_in_bf = w_in.astype(BF)
    w_out_bf = w_out.astype(BF)
    dense_bf = [w.astype(BF) for w in (dense_w_gate, dense_w_up, dense_w_down)]
    moe_bf = [w.astype(BF) for w in (moe_w_gate, moe_w_up, moe_w_down)]
    router_w_pad = jnp.pad(router_w, ((0, 0), (0, 0), (0, LANES - N_EXPERTS)))
    router_b_pad = jnp.pad(router_b, ((0, 0), (0, LANES - N_EXPERTS)), constant_values=NEG_INF)

    xp = x_prompt.reshape(t, d)
    xs = x_sample.reshape(nd, d)
    k_p, v_p, g_p, k_s, v_s, g_s = [], [], [], [], [], []
    for l in range(depth):
        i = l // 2
        n1 = norm1_w[l][None, :]
        n2 = norm2_w[l][None, :]
        qn = jnp.tile(q_norm_w[l], N_HEADS)[None, :]
        kn = jnp.tile(k_norm_w[l], N_KV_HEADS)[None, :]
        gmn = gm_norm_w[l][None, :]
        router = None if l % 2 == 0 else (router_w_pad[i], router_b_pad[i][None, :])

        q, kd, vd, u, gv, kl, vl, gvl = _inproj_p(xp, mod_p[l], n1, w_in_bf[l], qn, kn, gmn, seg,
                                                  cos_p, sin_p, batch, seq)
        mix = _mix_p(q, kd, vd, u, gv, gm_ws[l], gm_bs[l].T, attn_sinks[l], batch, seq)
        if router is None:
            xp = _outffn_p(mix, xp, mod_p[l], w_out_bf[l], n2, dense_bf[0][i], dense_bf[1][i], dense_bf[2][i],
                           batch, seq)
        else:
            xn, h2, route, cnt = _outproj_p(mix, xp, mod_p[l], w_out_bf[l], n2, router, batch, seq)
            xp = _moe_routed(h2, xn, route, cnt, mod_p[l], moe_bf[0][i], moe_bf[1][i], moe_bf[2][i], seq)
        k_p.append(kl.reshape(batch, WINDOW, N_KV_HEADS, HEAD_DIM))
        v_p.append(vl.reshape(batch, WINDOW, N_KV_HEADS, HEAD_DIM))
        g_p.append(gvl)

        q, k, v, u, gv = _inproj_s(xs, mod_s[l], n1, w_in_bf[l], qn, kn, gmn, seg, cos_s, sin_s)
        qh = q.reshape(nd, N_HEADS, HEAD_DIM)
        zq = jnp.zeros_like(qh)
        qpad = jnp.where(kv_of_head == 0, jnp.concatenate([qh, zq], -1), jnp.concatenate([zq, qh], -1))
        w = cache_k.shape[2]
        o, nk, nv = _attn_s(qpad, k[:, None, :], v[:, None, :], cache_k[l].reshape(nd, w, KV_WIDTH),
                            cache_v[l].reshape(nd, w, KV_WIDTH),
                            jnp.broadcast_to(attn_sinks[l][:, None], (N_HEADS, LANES)))
        o = jnp.where(kv_of_head == 0, o[..., :HEAD_DIM], o[..., HEAD_DIM:]).reshape(nd, ATTN_WIDTH)
        wdiag = jnp.repeat(gm_ws[l][:, 0, 0], GM_WIDTH // GM_GROUPS)[None, :]
        bsrow = jnp.repeat(gm_bs[l][:, 0], GM_WIDTH // GM_GROUPS)[None, :]
        res = _outproj_s(o, u, gv, wdiag, bsrow, xs, mod_s[l], w_out_bf[l], n2, router)
        if router is None:
            xs = _ffn(res[1], res[0], mod_s[l], dense_bf[0][i], dense_bf[1][i], dense_bf[2][i], None)
        else:
            xs = _moe_all(res[1], res[0], res[2], mod_s[l], moe_bf[0][i], moe_bf[1][i], moe_bf[2][i], None)
        k_s.append(nk.reshape(nd, w, N_KV_HEADS, HEAD_DIM))
        v_s.append(nv.reshape(nd, w, N_KV_HEADS, HEAD_DIM))
        g_s.append(gv[:, None, :])

    return (xp.reshape(batch, seq, d), xs.reshape(nd, 1, d), jnp.stack(k_p), jnp.stack(v_p), jnp.stack(g_p),
            jnp.stack(k_s), jnp.stack(v_s), jnp.stack(g_s))
```

```python
import functools

import numpy as np
import jax
import jax.numpy as jnp
from jax import lax
from jax.experimental import pallas as pl
from jax.experimental.pallas import tpu as pltpu

D_MODEL = 1024
HEAD_DIM = 64
N_HEADS = 8
N_KV_HEADS = 2
KV_GROUP = N_HEADS // N_KV_HEADS
ATTN_WIDTH = N_HEADS * HEAD_DIM
KV_WIDTH = N_KV_HEADS * HEAD_DIM
GM_WIDTH = 512
GM_GROUPS = 4
WINDOW = 128
CHUNK = 128
D_FF = 2816
MXU_DIM = 256
FF_SPLIT = (0, 6 * MXU_DIM, D_FF)
N_EXPERTS = 8
N_ADA = 6
IN_COLS = ATTN_WIDTH + 2 * KV_WIDTH + 2 * GM_WIDTH
PAST_LEN = 16384
ROPE_THETA = 10000.0
EPS = 1e-6
NEG_INF = -1e30
LANES = 128
SUB = 8
assert D_MODEL == SUB * LANES

BF = jnp.bfloat16
F32 = jnp.float32
MIB = 1024 * 1024


ROW_SPLIT = 2


def _params(sem, vmem_mib):
    return pltpu.CompilerParams(dimension_semantics=sem, vmem_limit_bytes=vmem_mib * MIB)


def _dot(a, b):
    return jnp.dot(a, b, preferred_element_type=F32)


def _dot_nt(a, b):
    return lax.dot_general(a, b, (((1,), (1,)), ((), ())), preferred_element_type=F32)


def _split(a):
    hi = a.astype(BF)
    return hi, (a - hi.astype(F32)).astype(BF)


def _dot_bf(a, w):
    return _dot(a.astype(BF), w)


def _dot3(a, w):
    ah, al = _split(a)
    if w.dtype == BF:
        return _dot(ah, w) + _dot(al, w)
    wh, wl = _split(w)
    return _dot(ah, wh) + _dot(al, wh) + _dot(ah, wl)


def _rms(x, w):
    ms = jnp.mean(x * x, axis=-1, keepdims=True)
    return x * lax.rsqrt(ms + EPS) * w


def _ada_body(c_ref, w_ref, b_ref, o_ref):
    o_ref[...] = _dot3(jax.nn.silu(c_ref[...]), w_ref[...]) + b_ref[...]


def _ada(c_all, w_ada, b_ada):
    depth, d, cols = w_ada.shape
    n = c_all.shape[0]
    tn = 1024
    return pl.pallas_call(
        _ada_body,
        grid=(depth, cols // tn),
        in_specs=[
            pl.BlockSpec((n, d), lambda l, j: (0, 0)),
            pl.BlockSpec((None, d, tn), lambda l, j: (l, 0, j)),
            pl.BlockSpec((None, 1, tn), lambda l, j: (l, 0, j)),
        ],
        out_specs=pl.BlockSpec((None, n, tn), lambda l, j: (l, 0, j)),
        out_shape=jax.ShapeDtypeStruct((depth, n, cols), F32),
        compiler_params=_params(("arbitrary", "arbitrary"), 32),
        name="ada",
    )(c_all, w_ada, b_ada.reshape(depth, 1, cols))


def _swap_halves(t):
    n = t.shape[-1]
    lane = lax.broadcasted_iota(jnp.int32, (1, n), 1)
    first = (lane % HEAD_DIM) < (HEAD_DIM // 2)
    return jnp.where(first, pltpu.roll(t, n - HEAD_DIM // 2, axis=1), pltpu.roll(t, HEAD_DIM // 2, axis=1))


def _inproj_compute(mm, x, sh, sc, n1, w_ref, qn, kn, gmn, seg_ref, cos, sin):
    h = _rms(x, n1) * (1.0 + sc) + sh
    z = mm(h, w_ref[...])
    q = z[:, :ATTN_WIDTH]
    k = z[:, ATTN_WIDTH:ATTN_WIDTH + KV_WIDTH]
    v = z[:, ATTN_WIDTH + KV_WIDTH:ATTN_WIDTH + 2 * KV_WIDTH]
    gm = z[:, ATTN_WIDTH + 2 * KV_WIDTH:]

    def head_norm(t, seg, wn):
        ms = mm(t * t, seg) * (1.0 / HEAD_DIM)
        return t * lax.rsqrt(ms + EPS) * wn

    def rope(t):
        reps = t.shape[-1] // LANES
        c = jnp.concatenate([cos] * reps, axis=-1) if reps > 1 else cos
        s = jnp.concatenate([sin] * reps, axis=-1) if reps > 1 else sin
        return t * c + _swap_halves(t) * s

    q = rope(head_norm(q, seg_ref[...], qn)) * (HEAD_DIM ** -0.5)
    k = rope(head_norm(k, seg_ref[:KV_WIDTH, :KV_WIDTH], kn))
    g = jax.nn.gelu(gm)
    u = g[:, :GM_WIDTH]
    gv = _rms(g[:, GM_WIDTH:], gmn)
    return q, k, v, u, gv


def _dup_heads(t):
    lane = lax.broadcasted_iota(jnp.int32, (1, LANES), 1)
    lo = lane < HEAD_DIM
    r = pltpu.roll(t, HEAD_DIM, axis=1)
    return jnp.concatenate([jnp.where(lo, t, r), jnp.where(lo, r, t)], axis=-1)


def _inproj_p_body(tiles_per_batch, x_ref, mod_ref, n1_ref, w_ref, qn_ref, kn_ref, gmn_ref, seg_ref,
                   cos_ref, sin_ref, q_ref, kd_ref, vd_ref, u_ref, gv_ref, kl_ref, vl_ref, gvl_ref):
    i = pl.program_id(0)
    b = i // tiles_per_batch
    sh = mod_ref[0, pl.ds(b, 1), :]
    sc = mod_ref[1, pl.ds(b, 1), :]
    hs = x_ref.shape[0] // ROW_SPLIT
    for hh in range(ROW_SPLIT):
        rs = slice(hh * hs, (hh + 1) * hs)
        q, k, v, u, gv = _inproj_compute(_dot_bf, x_ref[rs, :], sh, sc, n1_ref[...], w_ref, qn_ref[...], kn_ref[...],
                                         gmn_ref[...], seg_ref, cos_ref[rs, :], sin_ref[rs, :])
        q_ref[rs, :] = q.astype(BF)
        kd_ref[rs, :] = _dup_heads(k).astype(BF)
        vd_ref[rs, :] = _dup_heads(v).astype(BF)
        u_ref[rs, :] = u.astype(BF)
        gv_ref[rs, :] = gv.astype(BF)

    @pl.when(i % tiles_per_batch == tiles_per_batch - 1)
    def _():
        kl_ref[...] = k[hs - WINDOW:, :]
        vl_ref[...] = v[hs - WINDOW:, :]
        gvl_ref[...] = gv[hs - CHUNK:, :]


def _inproj_p(x, mod, n1, w_bf, qn, kn, gmn, seg, cos, sin, batch, seq):
    t = x.shape[0]
    tm = min(512, seq)
    tpb = seq // tm
    row = lambda i: (i, 0)
    full = lambda i: (0, 0)
    last = lambda i: (i // tpb, 0, 0)
    return pl.pallas_call(
        functools.partial(_inproj_p_body, tpb),
        grid=(t // tm,),
        in_specs=[
            pl.BlockSpec((tm, D_MODEL), row),
            pl.BlockSpec((N_ADA, batch, D_MODEL), lambda i: (0, 0, 0)),
            pl.BlockSpec((1, D_MODEL), full),
            pl.BlockSpec((D_MODEL, IN_COLS), full),
            pl.BlockSpec((1, ATTN_WIDTH), full),
            pl.BlockSpec((1, KV_WIDTH), full),
            pl.BlockSpec((1, GM_WIDTH), full),
            pl.BlockSpec((ATTN_WIDTH, ATTN_WIDTH), full),
            pl.BlockSpec((tm, LANES), lambda i: (i % tpb, 0)),
            pl.BlockSpec((tm, LANES), lambda i: (i % tpb, 0)),
        ],
        out_specs=[
            pl.BlockSpec((tm, ATTN_WIDTH), row),
            pl.BlockSpec((tm, 2 * KV_WIDTH), row),
            pl.BlockSpec((tm, 2 * KV_WIDTH), row),
            pl.BlockSpec((tm, GM_WIDTH), row),
            pl.BlockSpec((tm, GM_WIDTH), row),
            pl.BlockSpec((None, WINDOW, KV_WIDTH), last),
            pl.BlockSpec((None, WINDOW, KV_WIDTH), last),
            pl.BlockSpec((None, CHUNK, GM_WIDTH), last),
        ],
        out_shape=[
            jax.ShapeDtypeStruct((t, ATTN_WIDTH), BF),
            jax.ShapeDtypeStruct((t, 2 * KV_WIDTH), BF),
            jax.ShapeDtypeStruct((t, 2 * KV_WIDTH), BF),
            jax.ShapeDtypeStruct((t, GM_WIDTH), BF),
            jax.ShapeDtypeStruct((t, GM_WIDTH), BF),
            jax.ShapeDtypeStruct((batch, WINDOW, KV_WIDTH), F32),
            jax.ShapeDtypeStruct((batch, WINDOW, KV_WIDTH), F32),
            jax.ShapeDtypeStruct((batch, CHUNK, GM_WIDTH), F32),
        ],
        compiler_params=_params(("arbitrary",), 48),
        name="inproj_p",
    )(x, mod, n1, w_bf, qn, kn, gmn, seg, cos, sin)


def _inproj_s_body(x_ref, mod_ref, n1_ref, w_ref, qn_ref, kn_ref, gmn_ref, seg_ref, cos_ref, sin_ref,
                   q_ref, k_ref, v_ref, u_ref, gv_ref):
    q, k, v, u, gv = _inproj_compute(_dot3, x_ref[...], mod_ref[0], mod_ref[1], n1_ref[...], w_ref, qn_ref[...],
                                     kn_ref[...], gmn_ref[...], seg_ref, cos_ref[...], sin_ref[...])
    q_ref[...] = q
    k_ref[...] = k
    v_ref[...] = v
    u_ref[...] = u
    gv_ref[...] = gv


def _inproj_s(x, mod, n1, w_bf, qn, kn, gmn, seg, cos, sin):
    n = x.shape[0]
    widths = (ATTN_WIDTH, KV_WIDTH, KV_WIDTH, GM_WIDTH, GM_WIDTH)
    return pl.pallas_call(
        _inproj_s_body,
        out_shape=[jax.ShapeDtypeStruct((n, w), F32) for w in widths],
        compiler_params=pltpu.CompilerParams(vmem_limit_bytes=48 * MIB),
        name="inproj_s",
    )(x, mod, n1, w_bf, qn, kn, gmn, seg, cos, sin)


def _mix_p_body(nblk, q_ref, kc_ref, kp_ref, vc_ref, vp_ref, u_ref, gv_ref, ws_ref, bst_ref, sink_ref, o_ref):
    i = pl.program_id(1)
    blk = WINDOW
    lane = lax.broadcasted_iota(jnp.int32, (1, LANES), 1)
    lo = lane < HEAD_DIM
    cols = KV_GROUP * blk
    iq = lax.broadcasted_iota(jnp.int32, (2 * blk, cols), 1) % blk
    jk = lax.broadcasted_iota(jnp.int32, (2 * blk, cols), 0)
    band = (jk > iq) & (jk <= iq + blk)
    band_first = band & ((jk >= blk) | (i > 0))
    tri = (lax.broadcasted_iota(jnp.int32, (CHUNK, CHUNK), 0)
           >= lax.broadcasted_iota(jnp.int32, (CHUNK, CHUNK), 1))
    wm = [jnp.where(tri, ws_ref[g], 0.0).astype(BF) for g in range(GM_GROUPS)]

    for n in range(nblk):
        r0 = n * blk
        if n == 0:
            kk = jnp.concatenate([kp_ref[...], kc_ref[0:blk, :]], axis=0)
            vv = jnp.concatenate([vp_ref[...], vc_ref[0:blk, :]], axis=0)
            mask = band_first
        else:
            kk = kc_ref[r0 - blk:r0 + blk, :]
            vv = vc_ref[r0 - blk:r0 + blk, :]
            mask = band
        for kvh in range(N_KV_HEADS):
            c0 = 2 * kvh
            qa = q_ref[r0:r0 + blk, c0 * LANES:(c0 + 1) * LANES]
            qb = q_ref[r0:r0 + blk, (c0 + 1) * LANES:(c0 + 2) * LANES]
            zero = jnp.zeros_like(qa)
            qq = jnp.concatenate([jnp.where(lo, qa, zero), jnp.where(lo, zero, qa),
                                  jnp.where(lo, qb, zero), jnp.where(lo, zero, qb)], axis=0)
            s = _dot_nt(kk[:, kvh * LANES:(kvh + 1) * LANES], qq)
            s = jnp.where(mask, s, NEG_INF)
            sink = jnp.concatenate(
                [jnp.full((1, blk), sink_ref[kvh * KV_GROUP + g], F32) for g in range(KV_GROUP)], axis=1)
            m = jnp.maximum(jnp.max(s, axis=0, keepdims=True), sink)
            p = jnp.exp(s - m)
            den = jnp.sum(p, axis=0, keepdims=True) + jnp.exp(sink - m)
            p = (p * (1.0 / den)).astype(BF)
            o = lax.dot_general(p, vv[:, kvh * LANES:(kvh + 1) * LANES], (((0,), (0,)), ((), ())),
                                preferred_element_type=F32)
            o_ref[r0:r0 + blk, c0 * LANES:(c0 + 1) * LANES] = jnp.where(
                lo, o[0:blk], o[blk:2 * blk]).astype(BF)
            o_ref[r0:r0 + blk, (c0 + 1) * LANES:(c0 + 2) * LANES] = jnp.where(
                lo, o[2 * blk:3 * blk], o[3 * blk:4 * blk]).astype(BF)
        for g in range(GM_GROUPS):
            cs = slice(g * LANES, (g + 1) * LANES)
            sp = _dot(wm[g], gv_ref[r0:r0 + blk, cs]) + bst_ref[:, g:g + 1]
            o_ref[r0:r0 + blk, ATTN_WIDTH + g * LANES:ATTN_WIDTH + (g + 1) * LANES] = (
                u_ref[r0:r0 + blk, cs].astype(F32) * sp).astype(BF)


def _mix_p(q, kd, vd, u, gv, ws, bst, sinks, batch, seq):
    t = q.shape[0]
    tq = min(512, seq)
    nblk = tq // WINDOW
    tpb = seq // tq
    cur = lambda b, i: (b * tpb + i, 0)
    prev = lambda b, i: (jnp.maximum((b * tpb + i) * nblk - 1, b * tpb * nblk), 0)
    return pl.pallas_call(
        functools.partial(_mix_p_body, nblk),
        grid=(batch, tpb),
        in_specs=[
            pl.BlockSpec((tq, ATTN_WIDTH), cur),
            pl.BlockSpec((tq, 2 * KV_WIDTH), cur),
            pl.BlockSpec((WINDOW, 2 * KV_WIDTH), prev),
            pl.BlockSpec((tq, 2 * KV_WIDTH), cur),
            pl.BlockSpec((WINDOW, 2 * KV_WIDTH), prev),
            pl.BlockSpec((tq, GM_WIDTH), cur),
            pl.BlockSpec((tq, GM_WIDTH), cur),
            pl.BlockSpec((GM_GROUPS, CHUNK, CHUNK), lambda b, i: (0, 0, 0)),
            pl.BlockSpec((CHUNK, GM_GROUPS), lambda b, i: (0, 0)),
            pl.BlockSpec(memory_space=pltpu.SMEM),
        ],
        out_specs=pl.BlockSpec((tq, D_MODEL), cur),
        out_shape=jax.ShapeDtypeStruct((t, D_MODEL), BF),
        compiler_params=_params(("arbitrary", "arbitrary"), 48),
        name="mix_p",
    )(q, kd, kd, vd, vd, u, gv, ws, bst, sinks)


def _attn_s_body(q_ref, kn_ref, vn_ref, ck_ref, cv_ref, sink_ref, o_ref, nk_ref, nv_ref):
    w = ck_ref.shape[1]
    row = lax.broadcasted_iota(jnp.int32, (1, w, 1), 1)
    nk = jnp.where(row == w - 1, kn_ref[...], pltpu.roll(ck_ref[...], w - 1, axis=1))
    nv = jnp.where(row == w - 1, vn_ref[...], pltpu.roll(cv_ref[...], w - 1, axis=1))
    nk_ref[...] = nk
    nv_ref[...] = nv
    def bmm3(spec, a, b):
        (ah, al), (bh, bl) = _split(a), _split(b)
        mm = lambda x, y: jnp.einsum(spec, x, y, preferred_element_type=F32)
        return mm(ah, bh) + mm(al, bh) + mm(ah, bl)

    s = bmm3('bhd,bjd->bhj', q_ref[...], nk)
    sink = sink_ref[...][None, :, 0:1]
    m = jnp.maximum(jnp.max(s, axis=-1, keepdims=True), sink)
    p = jnp.exp(s - m)
    den = jnp.sum(p, axis=-1, keepdims=True) + jnp.exp(sink - m)
    o_ref[...] = bmm3('bhj,bjd->bhd', p, nv) * (1.0 / den)


def _attn_s(qpad, k_new, v_new, ck, cv, sink_tile):
    n, w, kw = ck.shape
    bb = min(16, n)
    blk3 = lambda r, c: pl.BlockSpec((bb, r, c), lambda i: (i, 0, 0))
    return pl.pallas_call(
        _attn_s_body,
        grid=(n // bb,),
        in_specs=[blk3(N_HEADS, LANES), blk3(1, kw), blk3(1, kw), blk3(w, kw), blk3(w, kw),
                  pl.BlockSpec((N_HEADS, LANES), lambda i: (0, 0))],
        out_specs=[blk3(N_HEADS, LANES), blk3(w, kw), blk3(w, kw)],
        out_shape=[jax.ShapeDtypeStruct((n, N_HEADS, LANES), F32),
                   jax.ShapeDtypeStruct((n, w, kw), F32),
                   jax.ShapeDtypeStruct((n, w, kw), F32)],
        compiler_params=_params(("arbitrary",), 32),
        name="attn_s",
    )(qpad, k_new, v_new, ck, cv, sink_tile)


def _top2(h2, rw_ref, rb_ref):
    logits = _dot3(h2, rw_ref[...]) + rb_ref[...]
    lane = lax.broadcasted_iota(jnp.int32, logits.shape, 1).astype(F32)
    e = jnp.exp(logits - jnp.max(logits, axis=-1, keepdims=True))
    p = e / jnp.sum(e, axis=-1, keepdims=True)
    m1 = jnp.max(p, axis=-1, keepdims=True)
    i1 = jnp.min(jnp.where(p == m1, lane, float(LANES)), axis=-1, keepdims=True)
    p2 = jnp.where(lane == i1, -1.0, p)
    m2 = jnp.max(p2, axis=-1, keepdims=True)
    i2 = jnp.min(jnp.where(p2 == m2, lane, float(LANES)), axis=-1, keepdims=True)
    tot = m1 + m2
    return lane, i1, i2, m1 / tot, m2 / tot


def _route_gates(h2, rw_ref, rb_ref):
    lane, i1, i2, g1, g2 = _top2(h2, rw_ref, rb_ref)
    return jnp.where(lane == i1, g1, 0.0) + jnp.where(lane == i2, g2, 0.0)


ROUTE_E, ROUTE_RANK, ROUTE_GATE = 0, 2, 4


def _route_ranked(h2, rw_ref, rb_ref, tri_ref, cnt_ref):
    lane, i1, i2, g1, g2 = _top2(h2, rw_ref, rb_ref)
    oh1 = lane == i1
    oh2 = lane == i2
    hit = jnp.where(oh1, 1.0, 0.0) + jnp.where(oh2, 1.0, 0.0)
    before = cnt_ref[...] + _dot(tri_ref[...], hit.astype(BF))
    r1 = jnp.sum(jnp.where(oh1, before, 0.0), axis=-1, keepdims=True)
    r2 = jnp.sum(jnp.where(oh2, before, 0.0), axis=-1, keepdims=True)
    cnt_ref[...] += jnp.sum(hit, axis=0, keepdims=True)
    cols = (i1, i2, r1, r2, g1, g2)
    out = jnp.zeros_like(lane)
    for j, c in enumerate(cols):
        out = jnp.where(lane == float(j), c, out)
    return out


def _outproj_compute(mm, mix, x, ga1, sh2, sc2, w_ref, n2):
    xn = x + ga1 * mm(mix, w_ref[...])
    h2 = _rms(xn, n2) * (1.0 + sc2) + sh2
    return xn, h2


def _outproj_p_body(tiles_per_batch, mix_ref, x_ref, mod_ref, w_ref, n2_ref, rw_ref, rb_ref, tri_ref,
                    xn_ref, h2_ref, route_ref, route_t_ref, cnt_ref):
    i = pl.program_id(0)
    b = i // tiles_per_batch
    mrow = lambda j: mod_ref[j, pl.ds(b, 1), :]

    @pl.when(i == 0)
    def _():
        cnt_ref[...] = jnp.zeros_like(cnt_ref)

    xn, h2 = _outproj_compute(_dot_bf, mix_ref[...], x_ref[...], mrow(2), mrow(3), mrow(4), w_ref, n2_ref[...])
    route = _route_ranked(h2, rw_ref, rb_ref, tri_ref, cnt_ref)
    route_ref[...] = route
    route_t_ref[...] = route.T[:SUB, :]
    h2_ref[...] = h2
    xn_ref[...] = xn


def _outproj_p(mix, x, mod, w_bf, n2, rw, rb, batch, seq):
    t = x.shape[0]
    tm = min(512, seq)
    row = lambda i: (i, 0)
    full = lambda i: (0, 0)
    return pl.pallas_call(
        functools.partial(_outproj_p_body, seq // tm),
        grid=(t // tm,),
        in_specs=[
            pl.BlockSpec((tm, D_MODEL), row),
            pl.BlockSpec((tm, D_MODEL), row),
            pl.BlockSpec((N_ADA, batch, D_MODEL), lambda i: (0, 0, 0)),
            pl.BlockSpec((D_MODEL, D_MODEL), full),
            pl.BlockSpec((1, D_MODEL), full),
            pl.BlockSpec((D_MODEL, LANES), full),
            pl.BlockSpec((1, LANES), full),
            pl.BlockSpec((tm, tm), full),
        ],
        out_specs=[pl.BlockSpec((tm, D_MODEL), row), pl.BlockSpec((tm, D_MODEL), row),
                   pl.BlockSpec((tm, LANES), row), pl.BlockSpec((SUB, tm), lambda i: (0, i)),
                   pl.BlockSpec((1, LANES), full)],
        out_shape=[jax.ShapeDtypeStruct((t, D_MODEL), F32), jax.ShapeDtypeStruct((t, D_MODEL), F32),
                   jax.ShapeDtypeStruct((t, LANES), F32), jax.ShapeDtypeStruct((SUB, t), F32),
                   jax.ShapeDtypeStruct((1, LANES), F32)],
        compiler_params=_params(("arbitrary",), 48),
        name="outproj_p",
    )(mix, x, mod, w_bf, n2, rw, rb, jnp.asarray(np.tri(tm, k=-1), BF))


def _outproj_s_body(with_router, o_ref, u_ref, gv_ref, wdiag_ref, bsrow_ref, x_ref, mod_ref, w_ref, n2_ref, *rest):
    gate = u_ref[...] * (wdiag_ref[...] * gv_ref[...] + bsrow_ref[...])
    mix = jnp.concatenate([o_ref[...], gate], axis=-1)
    xn, h2 = _outproj_compute(_dot3, mix, x_ref[...], mod_ref[2], mod_ref[3], mod_ref[4], w_ref, n2_ref[...])
    if with_router:
        rw_ref, rb_ref, xn_ref, h2_ref, gates_ref = rest
        gates_ref[...] = _route_gates(h2, rw_ref, rb_ref)
    else:
        xn_ref, h2_ref = rest
    xn_ref[...] = xn
    h2_ref[...] = h2


def _outproj_s(o, u, gv, wdiag, bsrow, x, mod, w, n2, router):
    n = x.shape[0]
    out_shape = [jax.ShapeDtypeStruct((n, D_MODEL), F32), jax.ShapeDtypeStruct((n, D_MODEL), F32)]
    args = [o, u, gv, wdiag, bsrow, x, mod, w, n2]
    if router is not None:
        out_shape.append(jax.ShapeDtypeStruct((n, LANES), F32))
        args += list(router)
    return pl.pallas_call(
        functools.partial(_outproj_s_body, router is not None),
        out_shape=out_shape,
        compiler_params=pltpu.CompilerParams(vmem_limit_bytes=32 * MIB),
        name="outproj_s",
    )(*args)


def _swiglu(h_bf, wg_ref, wu_ref, wd_ref):
    y = None
    for c in range(len(FF_SPLIT) - 1):
        sl = slice(FF_SPLIT[c], FF_SPLIT[c + 1])
        a = (jax.nn.silu(_dot(h_bf, wg_ref[:, sl])) * _dot(h_bf, wu_ref[:, sl])).astype(BF)
        part = _dot(a, wd_ref[sl, :])
        y = part if y is None else y + part
    return y


def _ffn_s_body(h_ref, x_ref, mod_ref, wg_ref, wu_ref, wd_ref, o_ref):
    h = h_ref[...]
    y = jnp.zeros_like(h)
    for c in range(D_FF // MXU_DIM):
        sl = slice(c * MXU_DIM, (c + 1) * MXU_DIM)
        a = jax.nn.silu(_dot3(h, wg_ref[:, sl])) * _dot3(h, wu_ref[:, sl])
        y = y + _dot3(a, wd_ref[sl, :])
    o_ref[...] = x_ref[...] + mod_ref[5] * y


def _ffn_s(h2, x, mod, wg, wu, wd):
    return pl.pallas_call(
        _ffn_s_body,
        out_shape=jax.ShapeDtypeStruct(x.shape, F32),
        compiler_params=pltpu.CompilerParams(vmem_limit_bytes=56 * MIB),
        name="ffn_s",
    )(h2, x, mod, wg, wu, wd)


def _outffn_p_body(tiles_per_batch, mix_ref, x_ref, mod_ref, w_ref, n2_ref, wg_ref, wu_ref, wd_ref, o_ref):
    b = pl.program_id(0) // tiles_per_batch
    mrow = lambda j: mod_ref[j, pl.ds(b, 1), :]
    xn, h2 = _outproj_compute(_dot_bf, mix_ref[...], x_ref[...], mrow(2), mrow(3), mrow(4), w_ref, n2_ref[...])
    o_ref[...] = xn + mrow(5) * _swiglu(h2.astype(BF), wg_ref, wu_ref, wd_ref)


def _outffn_p(mix, x, mod, w_bf, n2, wg, wu, wd, batch, seq):
    t = x.shape[0]
    tm = min(512, seq)
    row = lambda i: (i, 0)
    const = lambda shape: pl.BlockSpec(shape, lambda i: (0,) * len(shape), pipeline_mode=pl.Buffered(1))
    return pl.pallas_call(
        functools.partial(_outffn_p_body, seq // tm),
        grid=(t // tm,),
        in_specs=[
            pl.BlockSpec((tm, D_MODEL), row),
            pl.BlockSpec((tm, D_MODEL), row),
            const((N_ADA, batch, D_MODEL)),
            const((D_MODEL, D_MODEL)),
            const((1, D_MODEL)),
            const((D_MODEL, D_FF)), const((D_MODEL, D_FF)), const((D_FF, D_MODEL)),
        ],
        out_specs=pl.BlockSpec((tm, D_MODEL), row),
        out_shape=jax.ShapeDtypeStruct((t, D_MODEL), F32),
        compiler_params=_params(("arbitrary",), 56),
        name="outffn_p",
    )(mix, x, mod, w_bf, n2, wg, wu, wd)


def _moe_s_body(h_ref, x_ref, gates_ref, mod_ref, wg_ref, wu_ref, wd_ref, o_ref):
    e = pl.program_id(0)

    @pl.when(e == 0)
    def _():
        o_ref[...] = jnp.zeros_like(o_ref)

    lane = lax.broadcasted_iota(jnp.int32, (1, LANES), 1)
    gate = jnp.sum(jnp.where(lane == e, gates_ref[...], 0.0), axis=-1, keepdims=True)
    o_ref[...] += gate * _swiglu(h_ref[...].astype(BF), wg_ref, wu_ref, wd_ref)

    @pl.when(e == N_EXPERTS - 1)
    def _():
        o_ref[...] = x_ref[...] + mod_ref[5] * o_ref[...]


def _moe_s(h2, x, gates, mod, wg, wu, wd):
    n = x.shape[0]
    whole = lambda shape: pl.BlockSpec(shape, lambda e: (0,) * len(shape))
    wspec = lambda shape: pl.BlockSpec((None,) + shape, lambda e: (e, 0, 0))
    return pl.pallas_call(
        _moe_s_body,
        grid=(N_EXPERTS,),
        in_specs=[whole((n, D_MODEL)), whole((n, D_MODEL)), whole((n, LANES)), whole(mod.shape),
                  wspec((D_MODEL, D_FF)), wspec((D_MODEL, D_FF)), wspec((D_FF, D_MODEL))],
        out_specs=whole((n, D_MODEL)),
        out_shape=jax.ShapeDtypeStruct((n, D_MODEL), F32),
        compiler_params=_params(("arbitrary",), 56),
        name="moe_s",
    )(h2, x, gates, mod, wg, wu, wd)


TM_MOE = 512
TD = 256


def _to_token_tiles(ref, x):
    r = x.shape[0]
    for g in range(SUB):
        ref[pl.ds(g, r, stride=SUB), :] = x[:, g * LANES:(g + 1) * LANES]


def _from_token_tiles(ref, first, r):
    return jnp.concatenate([ref[pl.ds(first * SUB + g, r, stride=SUB), :] for g in range(SUB)], axis=-1)


def _token_copy(src, s, dst, d, sem):
    aligned = lambda v: v if isinstance(v, int) else pl.multiple_of(v, SUB)
    return pltpu.make_async_copy(src.at[pl.ds(aligned(s), SUB), :], dst.at[pl.ds(aligned(d), SUB), :], sem)


def _dispatch_body(pos_ref, pad_ref, h_ref, xs_ref, stage, sem, zsem):
    i = pl.program_id(0)
    n = pl.num_programs(0)
    td = h_ref.shape[0]
    slot = i % 2

    def wait_slot(s):
        for _ in range(2):
            pltpu.make_async_copy(stage.at[s], xs_ref.at[pl.ds(0, td * SUB), :], sem.at[s]).wait()

    @pl.when(i >= 2)
    def _():
        wait_slot(slot)

    _to_token_tiles(stage.at[slot], h_ref[...])

    def issue(r, c):
        for k in range(2):
            _token_copy(stage.at[slot], r * SUB, xs_ref, pos_ref[0, 0, k * td + r], sem.at[slot]).start(priority=k)
        return c

    lax.fori_loop(0, td, issue, 0, unroll=8)

    @pl.when(i == n - 1)
    def _():
        wait_slot(slot)

        @pl.when(n > 1)
        def _():
            wait_slot(1 - slot)

        stage[0] = jnp.zeros(stage.shape[1:], stage.dtype)
        for e in range(N_EXPERTS):
            lo = pad_ref[0, e]
            hi = pad_ref[1, e]

            def zero_token(r, c):
                _token_copy(stage.at[0], 0, xs_ref, r * SUB, zsem).start()
                return c

            def wait_token(r, c):
                _token_copy(stage.at[0], 0, xs_ref, 0, zsem).wait()
                return c

            lax.fori_loop(lo, hi, zero_token, 0)
            lax.fori_loop(lo, hi, wait_token, 0)

        def zero_blk(j, c):
            pltpu.make_async_copy(stage.at[0], xs_ref.at[pl.ds(pl.multiple_of(j * (td * SUB), SUB), td * SUB), :],
                                  zsem).start()
            return c

        def wait_blk(j, c):
            pltpu.make_async_copy(stage.at[0], xs_ref.at[pl.ds(0, td * SUB), :], zsem).wait()
            return c

        lax.fori_loop(pad_ref[0, N_EXPERTS], pad_ref[1, N_EXPERTS], zero_blk, 0)
        lax.fori_loop(pad_ref[0, N_EXPERTS], pad_ref[1, N_EXPERTS], wait_blk, 0)


def _dispatch(h2, pos_t, pad, npad):
    t = h2.shape[0]
    td = min(TD, t)
    return pl.pallas_call(
        _dispatch_body,
        grid=(t // td,),
        in_specs=[
            pl.BlockSpec((1, 1, 2 * td), lambda i: (i, 0, 0), memory_space=pltpu.SMEM),
            pl.BlockSpec(memory_space=pltpu.SMEM),
            pl.BlockSpec((td, D_MODEL), lambda i: (i, 0)),
        ],
        out_specs=pl.BlockSpec(memory_space=pl.ANY),
        out_shape=jax.ShapeDtypeStruct((npad * SUB, LANES), F32),
        scratch_shapes=[pltpu.VMEM((2, td * SUB, LANES), F32), pltpu.SemaphoreType.DMA((2,)),
                        pltpu.SemaphoreType.DMA(())],
        compiler_params=_params(("arbitrary",), 32),
        name="dispatch",
    )(pos_t, pad, h2)


def _moe_body(te_ref, src_ref, nv_ref, x_ref, wg_ref, wu_ref, wd_ref, o_ref):
    i = pl.program_id(0)

    @pl.when(nv_ref[i] > 0)
    def _():
        x = _from_token_tiles(x_ref, 0, TM_MOE).astype(BF)
        _to_token_tiles(o_ref, _swiglu(x, wg_ref, wu_ref, wd_ref))

    @pl.when(nv_ref[i] == 0)
    def _():
        o_ref[...] = jnp.zeros_like(o_ref)


def _moe(xs, tile_e, tile_src, tile_nv, wg, wu, wd):
    rows = TM_MOE * SUB
    wspec = lambda shape: pl.BlockSpec((None,) + shape, lambda i, te, src, nv: (te[i], 0, 0),
                                       pipeline_mode=pl.Buffered(1))
    return pl.pallas_call(
        _moe_body,
        grid_spec=pltpu.PrefetchScalarGridSpec(
            num_scalar_prefetch=3,
            grid=(xs.shape[0] // rows,),
            in_specs=[
                pl.BlockSpec((rows, LANES), lambda i, te, src, nv: (src[i], 0)),
                wspec((D_MODEL, D_FF)), wspec((D_MODEL, D_FF)), wspec((D_FF, D_MODEL)),
            ],
            out_specs=pl.BlockSpec((rows, LANES), lambda i, te, src, nv: (i, 0)),
        ),
        out_shape=jax.ShapeDtypeStruct(xs.shape, F32),
        compiler_params=_params(("arbitrary",), 56),
        name="moe",
    )(tile_e, tile_src, tile_nv, xs, wg, wu, wd)


def _combine_body(tiles_per_batch, posc_ref, posn_ref, x_ref, route_ref, mod_ref, ys_ref, o_ref, buf, sem):
    i = pl.program_id(0)
    n = pl.num_programs(0)
    tc = x_ref.shape[0]

    def gather(p_ref, s):
        def issue(r, c):
            for k in range(2):
                _token_copy(ys_ref, p_ref[0, 0, k * tc + r], buf.at[s], (k * tc + r) * SUB,
                            sem.at[s]).start(priority=k)
            return c

        lax.fori_loop(0, tc, issue, 0, unroll=8)

    @pl.when(i == 0)
    def _():
        gather(posc_ref, 0)

    @pl.when(i + 1 < n)
    def _():
        gather(posn_ref, (i + 1) % 2)

    slot = i % 2
    pltpu.make_async_copy(ys_ref.at[pl.ds(0, 2 * tc * SUB), :], buf.at[slot], sem.at[slot]).wait()
    lane = lax.broadcasted_iota(jnp.int32, (1, LANES), 1)
    rt = route_ref[...]
    g1 = jnp.sum(jnp.where(lane == ROUTE_GATE, rt, 0.0), axis=-1, keepdims=True)
    g2 = jnp.sum(jnp.where(lane == ROUTE_GATE + 1, rt, 0.0), axis=-1, keepdims=True)
    y = g1 * _from_token_tiles(buf.at[slot], 0, tc) + g2 * _from_token_tiles(buf.at[slot], tc, tc)
    ga2 = mod_ref[5, pl.ds(i // tiles_per_batch, 1), :]
    o_ref[...] = x_ref[...] + ga2 * y


def _combine(ys, pos_t, x, route, mod, seq):
    t = x.shape[0]
    tc = min(TD, t)
    nt = t // tc
    row = lambda i: (i, 0)
    return pl.pallas_call(
        functools.partial(_combine_body, seq // tc),
        grid=(nt,),
        in_specs=[
            pl.BlockSpec((1, 1, 2 * tc), lambda i: (i, 0, 0), memory_space=pltpu.SMEM),
            pl.BlockSpec((1, 1, 2 * tc), lambda i: (jnp.minimum(i + 1, nt - 1), 0, 0), memory_space=pltpu.SMEM),
            pl.BlockSpec((tc, D_MODEL), row),
            pl.BlockSpec((tc, LANES), row),
            pl.BlockSpec(mod.shape, lambda i: (0, 0, 0)),
            pl.BlockSpec(memory_space=pl.ANY),
        ],
        out_specs=pl.BlockSpec((tc, D_MODEL), row),
        out_shape=jax.ShapeDtypeStruct((t, D_MODEL), F32),
        scratch_shapes=[pltpu.VMEM((2, 2 * tc * SUB, LANES), F32), pltpu.SemaphoreType.DMA((2,))],
        compiler_params=_params(("arbitrary",), 32),
        name="combine",
    )(pos_t, pos_t, x, route, mod, ys)


def _moe_routed(h2, xn, route, route_t, cnt, mod, wg, wu, wd, seq):
    t = h2.shape[0]
    td = min(TD, t)
    nt_max = pl.cdiv(2 * t, TM_MOE) + N_EXPERTS
    npad = nt_max * TM_MOE
    counts = cnt[0, :N_EXPERTS].astype(jnp.int32)
    ntile = (counts + TM_MOE - 1) // TM_MOE
    eid = jnp.arange(N_EXPERTS)
    tile_end = jnp.sum(jnp.where(eid[None, :] <= eid[:, None], ntile[None, :], 0), axis=1)
    off = (tile_end - ntile) * TM_MOE
    e12 = route_t[ROUTE_E:ROUTE_E + 2].astype(jnp.int32)
    r12 = route_t[ROUTE_RANK:ROUTE_RANK + 2].astype(jnp.int32)
    onehot = e12[:, :, None] == eid[None, None, :]
    pos = jnp.sum(jnp.where(onehot, off[None, None, :], 0), axis=-1) + r12
    pos_t = (pos * SUB).reshape(2, t // td, td).transpose(1, 0, 2).reshape(t // td, 1, 2 * td)
    total = tile_end[-1]
    tid = jnp.arange(nt_max)
    tile_e = jnp.minimum(jnp.sum(tid[:, None] >= tile_end[None, :], axis=1), N_EXPERTS - 1).astype(jnp.int32)
    tile_nv = (tid < total).astype(jnp.int32)
    tile_src = jnp.minimum(tid, total - 1).astype(jnp.int32)
    pad = jnp.stack([jnp.concatenate([off + counts, (total * (TM_MOE // td))[None]]),
                     jnp.concatenate([off + ntile * TM_MOE, jnp.full((1,), npad // td, jnp.int32)])]).astype(jnp.int32)
    xs = _dispatch(h2, pos_t, pad, npad)
    ys = _moe(xs, tile_e, tile_src, tile_nv, wg, wu, wd)
    return _combine(ys, pos_t, xn, route, mod, seq)


def _rope_tables(pos):
    inv = ROPE_THETA ** (-np.arange(0, HEAD_DIM, 2, dtype=np.float64) / HEAD_DIM)
    ang = np.asarray(pos, np.float64)[:, None] * inv[None, :]
    cos = np.concatenate([np.cos(ang), np.cos(ang)], axis=-1)
    sin = np.concatenate([-np.sin(ang), np.sin(ang)], axis=-1)
    reps = LANES // HEAD_DIM
    return (jnp.asarray(np.tile(cos, (1, reps)), F32), jnp.asarray(np.tile(sin, (1, reps)), F32))


def kernel(x_prompt, x_sample, cache_k, cache_v, c_prompt, c_sample, w_ada, b_ada, norm1_w, norm2_w, w_in,
           q_norm_w, k_norm_w, attn_sinks, gm_norm_w, gm_ws, gm_bs, w_out, dense_w_gate, dense_w_up,
           dense_w_down, router_w, router_b, moe_w_gate, moe_w_up, moe_w_down):
    batch, seq, d = x_prompt.shape
    nd = x_sample.shape[0]
    depth = w_in.shape[0]
    t = batch * seq

    mod = _ada(jnp.concatenate([c_prompt, c_sample], axis=0), w_ada, b_ada)
    mod_p = mod[:, :batch].reshape(depth, batch, N_ADA, d).transpose(0, 2, 1, 3)
    mod_s = mod[:, batch:].reshape(depth, nd, N_ADA, d).transpose(0, 2, 1, 3)

    cos_p, sin_p = _rope_tables(np.arange(seq))
    cos_s, sin_s = _rope_tables(np.array([PAST_LEN]))
    head_of = np.arange(ATTN_WIDTH) // HEAD_DIM
    seg = jnp.asarray(head_of[:, None] == head_of[None, :], BF)
    kv_of_head = (jnp.arange(N_HEADS) // KV_GROUP)[None, :, None]

    w_in_bf = w_in.astype(BF)
    w_out_bf = w_out.astype(BF)
    dense_bf = [w.astype(BF) for w in (dense_w_gate, dense_w_up, dense_w_down)]
    moe_bf = [w.astype(BF) for w in (moe_w_gate, moe_w_up, moe_w_down)]
    router_w_pad = jnp.pad(router_w, ((0, 0), (0, 0), (0, LANES - N_EXPERTS)))
    router_b_pad = jnp.pad(router_b, ((0, 0), (0, LANES - N_EXPERTS)), constant_values=NEG_INF)

    xp = x_prompt.reshape(t, d)
    xs = x_sample.reshape(nd, d)
    k_p, v_p, g_p, k_s, v_s, g_s = [], [], [], [], [], []
    for l in range(depth):
        i = l // 2
        n1 = norm1_w[l][None, :]
        n2 = norm2_w[l][None, :]
        qn = jnp.tile(q_norm_w[l], N_HEADS)[None, :]
        kn = jnp.tile(k_norm_w[l], N_KV_HEADS)[None, :]
        gmn = gm_norm_w[l][None, :]
        router = None if l % 2 == 0 else (router_w_pad[i], router_b_pad[i][None, :])

        q, kd, vd, u, gv, kl, vl, gvl = _inproj_p(xp, mod_p[l], n1, w_in_bf[l], qn, kn, gmn, seg,
                                                  cos_p, sin_p, batch, seq)
        mix = _mix_p(q, kd, vd, u, gv, gm_ws[l], gm_bs[l].T, attn_sinks[l], batch, seq)
        if router is None:
            xp = _outffn_p(mix, xp, mod_p[l], w_out_bf[l], n2, dense_bf[0][i], dense_bf[1][i], dense_bf[2][i],
                           batch, seq)
        else:
            xn, h2, route, route_t, cnt = _outproj_p(mix, xp, mod_p[l], w_out_bf[l], n2, *router, batch, seq)
            xp = _moe_routed(h2, xn, route, route_t, cnt, mod_p[l], moe_bf[0][i], moe_bf[1][i], moe_bf[2][i], seq)
        k_p.append(kl.reshape(batch, WINDOW, N_KV_HEADS, HEAD_DIM))
        v_p.append(vl.reshape(batch, WINDOW, N_KV_HEADS, HEAD_DIM))
        g_p.append(gvl)

        q, k, v, u, gv = _inproj_s(xs, mod_s[l], n1, w_in[l], qn, kn, gmn, seg, cos_s, sin_s)
        qh = q.reshape(nd, N_HEADS, HEAD_DIM)
        zq = jnp.zeros_like(qh)
        qpad = jnp.where(kv_of_head == 0, jnp.concatenate([qh, zq], -1), jnp.concatenate([zq, qh], -1))
        w = cache_k.shape[2]
        o, nk, nv = _attn_s(qpad, k[:, None, :], v[:, None, :], cache_k[l].reshape(nd, w, KV_WIDTH),
                            cache_v[l].reshape(nd, w, KV_WIDTH),
                            jnp.broadcast_to(attn_sinks[l][:, None], (N_HEADS, LANES)))
        o = jnp.where(kv_of_head == 0, o[..., :HEAD_DIM], o[..., HEAD_DIM:]).reshape(nd, ATTN_WIDTH)
        wdiag = jnp.repeat(gm_ws[l][:, 0, 0], GM_WIDTH // GM_GROUPS)[None, :]
        bsrow = jnp.repeat(gm_bs[l][:, 0], GM_WIDTH // GM_GROUPS)[None, :]
        res = _outproj_s(o, u, gv, wdiag, bsrow, xs, mod_s[l], w_out[l], n2, router)
        if router is None:
            xs = _ffn_s(res[1], res[0], mod_s[l], dense_w_gate[i], dense_w_up[i], dense_w_down[i])
        else:
            xs = _moe_s(res[1], res[0], res[2], mod_s[l], moe_bf[0][i], moe_bf[1][i], moe_bf[2][i])
        k_s.append(nk.reshape(nd, w, N_KV_HEADS, HEAD_DIM))
        v_s.append(nv.reshape(nd, w, N_KV_HEADS, HEAD_DIM))
        g_s.append(gv[:, None, :])

    return (xp.reshape(batch, seq, d), xs.reshape(nd, 1, d), jnp.stack(k_p), jnp.stack(v_p), jnp.stack(g_p),
            jnp.stack(k_s), jnp.stack(v_s), jnp.stack(g_s))
```

```python
import functools

import numpy as np
import jax
import jax.numpy as jnp
from jax import lax
from jax.experimental import pallas as pl
from jax.experimental.pallas import tpu as pltpu

D_MODEL = 1024
HEAD_DIM = 64
N_HEADS = 8
N_KV_HEADS = 2
KV_GROUP = N_HEADS // N_KV_HEADS
ATTN_WIDTH = N_HEADS * HEAD_DIM
KV_WIDTH = N_KV_HEADS * HEAD_DIM
GM_WIDTH = 512
GM_GROUPS = 4
WINDOW = 128
CHUNK = 128
D_FF = 2816
MXU_DIM = 256
FF_SPLIT = (0, 6 * MXU_DIM, D_FF)
N_EXPERTS = 8
N_ADA = 6
IN_COLS = ATTN_WIDTH + 2 * KV_WIDTH + 2 * GM_WIDTH
PAST_LEN = 16384
ROPE_THETA = 10000.0
EPS = 1e-6
NEG_INF = -1e30
LANES = 128
SUB = 8
assert D_MODEL == SUB * LANES

BF = jnp.bfloat16
F32 = jnp.float32
MIB = 1024 * 1024


ROW_GROUP = 256


def _params(sem, vmem_mib):
    return pltpu.CompilerParams(dimension_semantics=sem, vmem_limit_bytes=vmem_mib * MIB)


def _dot(a, b):
    return jnp.dot(a, b, preferred_element_type=F32)


def _dot_nt(a, b):
    return lax.dot_general(a, b, (((1,), (1,)), ((), ())), preferred_element_type=F32)


def _split(a):
    hi = a.astype(BF)
    return hi, (a - hi.astype(F32)).astype(BF)


def _dot_bf(a, w):
    return _dot(a.astype(BF), w)


def _dot3(a, w):
    ah, al = _split(a)
    if w.dtype == BF:
        return _dot(ah, w) + _dot(al, w)
    wh, wl = _split(w)
    return _dot(ah, wh) + _dot(al, wh) + _dot(ah, wl)


def _rms(x, w):
    ms = jnp.mean(x * x, axis=-1, keepdims=True)
    return x * lax.rsqrt(ms + EPS) * w


def _ada_body(c_ref, w_ref, b_ref, o_ref):
    o_ref[...] = _dot3(jax.nn.silu(c_ref[...]), w_ref[...]) + b_ref[...]


def _ada(c_all, w_ada, b_ada):
    depth, d, cols = w_ada.shape
    n = c_all.shape[0]
    tn = 1024
    return pl.pallas_call(
        _ada_body,
        grid=(depth, cols // tn),
        in_specs=[
            pl.BlockSpec((n, d), lambda l, j: (0, 0)),
            pl.BlockSpec((None, d, tn), lambda l, j: (l, 0, j)),
            pl.BlockSpec((None, 1, tn), lambda l, j: (l, 0, j)),
        ],
        out_specs=pl.BlockSpec((None, n, tn), lambda l, j: (l, 0, j)),
        out_shape=jax.ShapeDtypeStruct((depth, n, cols), F32),
        compiler_params=_params(("arbitrary", "arbitrary"), 32),
        name="ada",
    )(c_all, w_ada, b_ada.reshape(depth, 1, cols))


def _swap_halves(t):
    n = t.shape[-1]
    lane = lax.broadcasted_iota(jnp.int32, (1, n), 1)
    first = (lane % HEAD_DIM) < (HEAD_DIM // 2)
    return jnp.where(first, pltpu.roll(t, n - HEAD_DIM // 2, axis=1), pltpu.roll(t, HEAD_DIM // 2, axis=1))


def _inproj_compute(mm, x, sh, sc, n1, w_ref, qn, kn, gmn, seg_ref, cos, sin):
    h = _rms(x, n1) * (1.0 + sc) + sh
    z = mm(h, w_ref[...])
    q = z[:, :ATTN_WIDTH]
    k = z[:, ATTN_WIDTH:ATTN_WIDTH + KV_WIDTH]
    v = z[:, ATTN_WIDTH + KV_WIDTH:ATTN_WIDTH + 2 * KV_WIDTH]
    gm = z[:, ATTN_WIDTH + 2 * KV_WIDTH:]

    def head_norm(t, seg, wn):
        ms = mm(t * t, seg) * (1.0 / HEAD_DIM)
        return t * lax.rsqrt(ms + EPS) * wn

    def rope(t):
        reps = t.shape[-1] // LANES
        c = jnp.concatenate([cos] * reps, axis=-1) if reps > 1 else cos
        s = jnp.concatenate([sin] * reps, axis=-1) if reps > 1 else sin
        return t * c + _swap_halves(t) * s

    q = rope(head_norm(q, seg_ref[...], qn)) * (HEAD_DIM ** -0.5)
    k = rope(head_norm(k, seg_ref[:KV_WIDTH, :KV_WIDTH], kn))
    g = jax.nn.gelu(gm)
    u = g[:, :GM_WIDTH]
    gv = _rms(g[:, GM_WIDTH:], gmn)
    return q, k, v, u, gv


def _dup_heads(t):
    lane = lax.broadcasted_iota(jnp.int32, (1, LANES), 1)
    lo = lane < HEAD_DIM
    r = pltpu.roll(t, HEAD_DIM, axis=1)
    return jnp.concatenate([jnp.where(lo, t, r), jnp.where(lo, r, t)], axis=-1)


def _inproj_p_body(tiles_per_batch, x_ref, mod_ref, n1_ref, w_ref, qn_ref, kn_ref, gmn_ref, seg_ref,
                   cos_ref, sin_ref, q_ref, kd_ref, vd_ref, u_ref, gv_ref, kl_ref, vl_ref, gvl_ref):
    i = pl.program_id(0)
    b = i // tiles_per_batch
    sh = mod_ref[0, pl.ds(b, 1), :]
    sc = mod_ref[1, pl.ds(b, 1), :]
    hs = min(ROW_GROUP, x_ref.shape[0])
    for hh in range(x_ref.shape[0] // hs):
        rs = slice(hh * hs, (hh + 1) * hs)
        q, k, v, u, gv = _inproj_compute(_dot_bf, x_ref[rs, :], sh, sc, n1_ref[...], w_ref, qn_ref[...], kn_ref[...],
                                         gmn_ref[...], seg_ref, cos_ref[rs, :], sin_ref[rs, :])
        q_ref[rs, :] = q.astype(BF)
        kd_ref[rs, :] = _dup_heads(k).astype(BF)
        vd_ref[rs, :] = _dup_heads(v).astype(BF)
        u_ref[rs, :] = u.astype(BF)
        gv_ref[rs, :] = gv.astype(BF)

    @pl.when(i % tiles_per_batch == tiles_per_batch - 1)
    def _():
        kl_ref[...] = k[hs - WINDOW:, :]
        vl_ref[...] = v[hs - WINDOW:, :]
        gvl_ref[...] = gv[hs - CHUNK:, :]


def _inproj_p(x, mod, n1, w_bf, qn, kn, gmn, seg, cos, sin, batch, seq):
    t = x.shape[0]
    tm = min(1024, seq)
    tpb = seq // tm
    row = lambda i: (i, 0)
    full = lambda i: (0, 0)
    last = lambda i: (i // tpb, 0, 0)
    return pl.pallas_call(
        functools.partial(_inproj_p_body, tpb),
        grid=(t // tm,),
        in_specs=[
            pl.BlockSpec((tm, D_MODEL), row),
            pl.BlockSpec((N_ADA, batch, D_MODEL), lambda i: (0, 0, 0)),
            pl.BlockSpec((1, D_MODEL), full),
            pl.BlockSpec((D_MODEL, IN_COLS), full),
            pl.BlockSpec((1, ATTN_WIDTH), full),
            pl.BlockSpec((1, KV_WIDTH), full),
            pl.BlockSpec((1, GM_WIDTH), full),
            pl.BlockSpec((ATTN_WIDTH, ATTN_WIDTH), full),
            pl.BlockSpec((tm, LANES), lambda i: (i % tpb, 0)),
            pl.BlockSpec((tm, LANES), lambda i: (i % tpb, 0)),
        ],
        out_specs=[
            pl.BlockSpec((tm, ATTN_WIDTH), row),
            pl.BlockSpec((tm, 2 * KV_WIDTH), row),
            pl.BlockSpec((tm, 2 * KV_WIDTH), row),
            pl.BlockSpec((tm, GM_WIDTH), row),
            pl.BlockSpec((tm, GM_WIDTH), row),
            pl.BlockSpec((None, WINDOW, KV_WIDTH), last),
            pl.BlockSpec((None, WINDOW, KV_WIDTH), last),
            pl.BlockSpec((None, CHUNK, GM_WIDTH), last),
        ],
        out_shape=[
            jax.ShapeDtypeStruct((t, ATTN_WIDTH), BF),
            jax.ShapeDtypeStruct((t, 2 * KV_WIDTH), BF),
            jax.ShapeDtypeStruct((t, 2 * KV_WIDTH), BF),
            jax.ShapeDtypeStruct((t, GM_WIDTH), BF),
            jax.ShapeDtypeStruct((t, GM_WIDTH), BF),
            jax.ShapeDtypeStruct((batch, WINDOW, KV_WIDTH), F32),
            jax.ShapeDtypeStruct((batch, WINDOW, KV_WIDTH), F32),
            jax.ShapeDtypeStruct((batch, CHUNK, GM_WIDTH), F32),
        ],
        compiler_params=_params(("arbitrary",), 48),
        name="inproj_p",
    )(x, mod, n1, w_bf, qn, kn, gmn, seg, cos, sin)


def _inproj_s_body(x_ref, mod_ref, n1_ref, w_ref, qn_ref, kn_ref, gmn_ref, seg_ref, cos_ref, sin_ref,
                   q_ref, k_ref, v_ref, u_ref, gv_ref):
    q, k, v, u, gv = _inproj_compute(_dot3, x_ref[...], mod_ref[0], mod_ref[1], n1_ref[...], w_ref, qn_ref[...],
                                     kn_ref[...], gmn_ref[...], seg_ref, cos_ref[...], sin_ref[...])
    q_ref[...] = q
    k_ref[...] = k
    v_ref[...] = v
    u_ref[...] = u
    gv_ref[...] = gv


def _inproj_s(x, mod, n1, w_bf, qn, kn, gmn, seg, cos, sin):
    n = x.shape[0]
    widths = (ATTN_WIDTH, KV_WIDTH, KV_WIDTH, GM_WIDTH, GM_WIDTH)
    return pl.pallas_call(
        _inproj_s_body,
        out_shape=[jax.ShapeDtypeStruct((n, w), F32) for w in widths],
        compiler_params=pltpu.CompilerParams(vmem_limit_bytes=48 * MIB),
        name="inproj_s",
    )(x, mod, n1, w_bf, qn, kn, gmn, seg, cos, sin)


def _mix_p_body(nblk, q_ref, kc_ref, kp_ref, vc_ref, vp_ref, u_ref, gv_ref, ws_ref, bst_ref, sink_ref, o_ref):
    i = pl.program_id(1)
    blk = WINDOW
    lane = lax.broadcasted_iota(jnp.int32, (1, LANES), 1)
    lo = lane < HEAD_DIM
    cols = KV_GROUP * blk
    iq = lax.broadcasted_iota(jnp.int32, (2 * blk, cols), 1) % blk
    jk = lax.broadcasted_iota(jnp.int32, (2 * blk, cols), 0)
    band = (jk > iq) & (jk <= iq + blk)
    bias = jnp.where(band, 0.0, NEG_INF)
    bias_first = jnp.where(band & ((jk >= blk) | (i > 0)), 0.0, NEG_INF)
    tri = (lax.broadcasted_iota(jnp.int32, (CHUNK, CHUNK), 0)
           >= lax.broadcasted_iota(jnp.int32, (CHUNK, CHUNK), 1))
    wm = [jnp.where(tri, ws_ref[g], 0.0).astype(BF) for g in range(GM_GROUPS)]

    for n in range(nblk):
        r0 = n * blk
        if n == 0:
            kk = jnp.concatenate([kp_ref[...], kc_ref[0:blk, :]], axis=0)
            vv = jnp.concatenate([vp_ref[...], vc_ref[0:blk, :]], axis=0)
            mask_bias = bias_first
        else:
            kk = kc_ref[r0 - blk:r0 + blk, :]
            vv = vc_ref[r0 - blk:r0 + blk, :]
            mask_bias = bias
        for kvh in range(N_KV_HEADS):
            c0 = 2 * kvh
            qa = q_ref[r0:r0 + blk, c0 * LANES:(c0 + 1) * LANES]
            qb = q_ref[r0:r0 + blk, (c0 + 1) * LANES:(c0 + 2) * LANES]
            zero = jnp.zeros_like(qa)
            qq = jnp.concatenate([jnp.where(lo, qa, zero), jnp.where(lo, zero, qa),
                                  jnp.where(lo, qb, zero), jnp.where(lo, zero, qb)], axis=0)
            s = _dot_nt(kk[:, kvh * LANES:(kvh + 1) * LANES], qq) + mask_bias
            sink = jnp.concatenate(
                [jnp.full((1, blk), sink_ref[kvh * KV_GROUP + g], F32) for g in range(KV_GROUP)], axis=1)
            m = jnp.maximum(jnp.max(s, axis=0, keepdims=True), sink)
            p = jnp.exp(s - m)
            den = jnp.sum(p, axis=0, keepdims=True) + jnp.exp(sink - m)
            p = (p * (1.0 / den)).astype(BF)
            o = lax.dot_general(p, vv[:, kvh * LANES:(kvh + 1) * LANES], (((0,), (0,)), ((), ())),
                                preferred_element_type=F32)
            o_ref[r0:r0 + blk, c0 * LANES:(c0 + 1) * LANES] = jnp.where(
                lo, o[0:blk], o[blk:2 * blk]).astype(BF)
            o_ref[r0:r0 + blk, (c0 + 1) * LANES:(c0 + 2) * LANES] = jnp.where(
                lo, o[2 * blk:3 * blk], o[3 * blk:4 * blk]).astype(BF)
        for g in range(GM_GROUPS):
            cs = slice(g * LANES, (g + 1) * LANES)
            sp = _dot(wm[g], gv_ref[r0:r0 + blk, cs]) + bst_ref[:, g:g + 1]
            o_ref[r0:r0 + blk, ATTN_WIDTH + g * LANES:ATTN_WIDTH + (g + 1) * LANES] = (
                u_ref[r0:r0 + blk, cs].astype(F32) * sp).astype(BF)


def _mix_p(q, kd, vd, u, gv, ws, bst, sinks, batch, seq):
    t = q.shape[0]
    tq = min(1024, seq)
    nblk = tq // WINDOW
    tpb = seq // tq
    cur = lambda b, i: (b * tpb + i, 0)
    prev = lambda b, i: (jnp.maximum((b * tpb + i) * nblk - 1, b * tpb * nblk), 0)
    return pl.pallas_call(
        functools.partial(_mix_p_body, nblk),
        grid=(batch, tpb),
        in_specs=[
            pl.BlockSpec((tq, ATTN_WIDTH), cur),
            pl.BlockSpec((tq, 2 * KV_WIDTH), cur),
            pl.BlockSpec((WINDOW, 2 * KV_WIDTH), prev),
            pl.BlockSpec((tq, 2 * KV_WIDTH), cur),
            pl.BlockSpec((WINDOW, 2 * KV_WIDTH), prev),
            pl.BlockSpec((tq, GM_WIDTH), cur),
            pl.BlockSpec((tq, GM_WIDTH), cur),
            pl.BlockSpec((GM_GROUPS, CHUNK, CHUNK), lambda b, i: (0, 0, 0)),
            pl.BlockSpec((CHUNK, GM_GROUPS), lambda b, i: (0, 0)),
            pl.BlockSpec(memory_space=pltpu.SMEM),
        ],
        out_specs=pl.BlockSpec((tq, D_MODEL), cur),
        out_shape=jax.ShapeDtypeStruct((t, D_MODEL), BF),
        compiler_params=_params(("arbitrary", "arbitrary"), 48),
        name="mix_p",
    )(q, kd, kd, vd, vd, u, gv, ws, bst, sinks)


def _attn_s_body(q_ref, kn_ref, vn_ref, ck_ref, cv_ref, sink_ref, o_ref, nk_ref, nv_ref):
    w = ck_ref.shape[1]
    row = lax.broadcasted_iota(jnp.int32, (1, w, 1), 1)
    nk = jnp.where(row == w - 1, kn_ref[...], pltpu.roll(ck_ref[...], w - 1, axis=1))
    nv = jnp.where(row == w - 1, vn_ref[...], pltpu.roll(cv_ref[...], w - 1, axis=1))
    nk_ref[...] = nk
    nv_ref[...] = nv
    def bmm3(spec, a, b):
        (ah, al), (bh, bl) = _split(a), _split(b)
        mm = lambda x, y: jnp.einsum(spec, x, y, preferred_element_type=F32)
        return mm(ah, bh) + mm(al, bh) + mm(ah, bl)

    s = bmm3('bhd,bjd->bhj', q_ref[...], nk)
    sink = sink_ref[...][None, :, 0:1]
    m = jnp.maximum(jnp.max(s, axis=-1, keepdims=True), sink)
    p = jnp.exp(s - m)
    den = jnp.sum(p, axis=-1, keepdims=True) + jnp.exp(sink - m)
    o_ref[...] = bmm3('bhj,bjd->bhd', p, nv) * (1.0 / den)


def _attn_s(qpad, k_new, v_new, ck, cv, sink_tile):
    n, w, kw = ck.shape
    bb = min(16, n)
    blk3 = lambda r, c: pl.BlockSpec((bb, r, c), lambda i: (i, 0, 0))
    return pl.pallas_call(
        _attn_s_body,
        grid=(n // bb,),
        in_specs=[blk3(N_HEADS, LANES), blk3(1, kw), blk3(1, kw), blk3(w, kw), blk3(w, kw),
                  pl.BlockSpec((N_HEADS, LANES), lambda i: (0, 0))],
        out_specs=[blk3(N_HEADS, LANES), blk3(w, kw), blk3(w, kw)],
        out_shape=[jax.ShapeDtypeStruct((n, N_HEADS, LANES), F32),
                   jax.ShapeDtypeStruct((n, w, kw), F32),
                   jax.ShapeDtypeStruct((n, w, kw), F32)],
        compiler_params=_params(("arbitrary",), 32),
        name="attn_s",
    )(qpad, k_new, v_new, ck, cv, sink_tile)


def _top2(h2, rw_ref, rb_ref):
    logits = _dot3(h2, rw_ref[...]) + rb_ref[...]
    lane = lax.broadcasted_iota(jnp.int32, logits.shape, 1).astype(F32)
    e = jnp.exp(logits - jnp.max(logits, axis=-1, keepdims=True))
    p = e / jnp.sum(e, axis=-1, keepdims=True)
    m1 = jnp.max(p, axis=-1, keepdims=True)
    i1 = jnp.min(jnp.where(p == m1, lane, float(LANES)), axis=-1, keepdims=True)
    p2 = jnp.where(lane == i1, -1.0, p)
    m2 = jnp.max(p2, axis=-1, keepdims=True)
    i2 = jnp.min(jnp.where(p2 == m2, lane, float(LANES)), axis=-1, keepdims=True)
    tot = m1 + m2
    return lane, i1, i2, m1 / tot, m2 / tot


def _route_gates(h2, rw_ref, rb_ref):
    lane, i1, i2, g1, g2 = _top2(h2, rw_ref, rb_ref)
    return jnp.where(lane == i1, g1, 0.0) + jnp.where(lane == i2, g2, 0.0)


ROUTE_E, ROUTE_RANK, ROUTE_GATE = 0, 2, 4


def _route_ranked(h2, rw_ref, rb_ref, tri_ref, cnt_ref):
    lane, i1, i2, g1, g2 = _top2(h2, rw_ref, rb_ref)
    oh1 = lane == i1
    oh2 = lane == i2
    hit = jnp.where(oh1, 1.0, 0.0) + jnp.where(oh2, 1.0, 0.0)
    before = cnt_ref[...] + _dot(tri_ref[...], hit.astype(BF))
    r1 = jnp.sum(jnp.where(oh1, before, 0.0), axis=-1, keepdims=True)
    r2 = jnp.sum(jnp.where(oh2, before, 0.0), axis=-1, keepdims=True)
    cnt_ref[...] += jnp.sum(hit, axis=0, keepdims=True)
    cols = (i1, i2, r1, r2, g1, g2)
    out = jnp.zeros_like(lane)
    for j, c in enumerate(cols):
        out = jnp.where(lane == float(j), c, out)
    return out


def _outproj_compute(mm, mix, x, ga1, sh2, sc2, w_ref, n2):
    xn = x + ga1 * mm(mix, w_ref[...])
    h2 = _rms(xn, n2) * (1.0 + sc2) + sh2
    return xn, h2


def _outproj_p_body(tiles_per_batch, mix_ref, x_ref, mod_ref, w_ref, n2_ref, rw_ref, rb_ref, tri_ref,
                    xn_ref, h2_ref, route_ref, route_t_ref, cnt_ref):
    i = pl.program_id(0)
    b = i // tiles_per_batch
    mrow = lambda j: mod_ref[j, pl.ds(b, 1), :]

    @pl.when(i == 0)
    def _():
        cnt_ref[...] = jnp.zeros_like(cnt_ref)

    xn, h2 = _outproj_compute(_dot_bf, mix_ref[...], x_ref[...], mrow(2), mrow(3), mrow(4), w_ref, n2_ref[...])
    route = _route_ranked(h2, rw_ref, rb_ref, tri_ref, cnt_ref)
    route_ref[...] = route
    route_t_ref[...] = route.T[:SUB, :]
    h2_ref[...] = h2
    xn_ref[...] = xn


def _outproj_p(mix, x, mod, w_bf, n2, rw, rb, batch, seq):
    t = x.shape[0]
    tm = min(512, seq)
    row = lambda i: (i, 0)
    full = lambda i: (0, 0)
    return pl.pallas_call(
        functools.partial(_outproj_p_body, seq // tm),
        grid=(t // tm,),
        in_specs=[
            pl.BlockSpec((tm, D_MODEL), row),
            pl.BlockSpec((tm, D_MODEL), row),
            pl.BlockSpec((N_ADA, batch, D_MODEL), lambda i: (0, 0, 0)),
            pl.BlockSpec((D_MODEL, D_MODEL), full),
            pl.BlockSpec((1, D_MODEL), full),
            pl.BlockSpec((D_MODEL, LANES), full),
            pl.BlockSpec((1, LANES), full),
            pl.BlockSpec((tm, tm), full),
        ],
        out_specs=[pl.BlockSpec((tm, D_MODEL), row), pl.BlockSpec((tm, D_MODEL), row),
                   pl.BlockSpec((tm, LANES), row), pl.BlockSpec((SUB, tm), lambda i: (0, i)),
                   pl.BlockSpec((1, LANES), full)],
        out_shape=[jax.ShapeDtypeStruct((t, D_MODEL), F32), jax.ShapeDtypeStruct((t, D_MODEL), F32),
                   jax.ShapeDtypeStruct((t, LANES), F32), jax.ShapeDtypeStruct((SUB, t), F32),
                   jax.ShapeDtypeStruct((1, LANES), F32)],
        compiler_params=_params(("arbitrary",), 48),
        name="outproj_p",
    )(mix, x, mod, w_bf, n2, rw, rb, jnp.asarray(np.tri(tm, k=-1), BF))


def _outproj_s_body(with_router, o_ref, u_ref, gv_ref, wdiag_ref, bsrow_ref, x_ref, mod_ref, w_ref, n2_ref, *rest):
    gate = u_ref[...] * (wdiag_ref[...] * gv_ref[...] + bsrow_ref[...])
    mix = jnp.concatenate([o_ref[...], gate], axis=-1)
    xn, h2 = _outproj_compute(_dot3, mix, x_ref[...], mod_ref[2], mod_ref[3], mod_ref[4], w_ref, n2_ref[...])
    if with_router:
        rw_ref, rb_ref, xn_ref, h2_ref, gates_ref = rest
        gates_ref[...] = _route_gates(h2, rw_ref, rb_ref)
    else:
        xn_ref, h2_ref = rest
    xn_ref[...] = xn
    h2_ref[...] = h2


def _outproj_s(o, u, gv, wdiag, bsrow, x, mod, w, n2, router):
    n = x.shape[0]
    out_shape = [jax.ShapeDtypeStruct((n, D_MODEL), F32), jax.ShapeDtypeStruct((n, D_MODEL), F32)]
    args = [o, u, gv, wdiag, bsrow, x, mod, w, n2]
    if router is not None:
        out_shape.append(jax.ShapeDtypeStruct((n, LANES), F32))
        args += list(router)
    return pl.pallas_call(
        functools.partial(_outproj_s_body, router is not None),
        out_shape=out_shape,
        compiler_params=pltpu.CompilerParams(vmem_limit_bytes=32 * MIB),
        name="outproj_s",
    )(*args)


def _swiglu(h_bf, wg_ref, wu_ref, wd_ref):
    y = None
    for c in range(len(FF_SPLIT) - 1):
        sl = slice(FF_SPLIT[c], FF_SPLIT[c + 1])
        a = (jax.nn.silu(_dot(h_bf, wg_ref[:, sl])) * _dot(h_bf, wu_ref[:, sl])).astype(BF)
        part = _dot(a, wd_ref[sl, :])
        y = part if y is None else y + part
    return y


def _ffn_s_body(h_ref, x_ref, mod_ref, wg_ref, wu_ref, wd_ref, o_ref):
    h = h_ref[...]
    y = jnp.zeros_like(h)
    for c in range(D_FF // MXU_DIM):
        sl = slice(c * MXU_DIM, (c + 1) * MXU_DIM)
        a = jax.nn.silu(_dot3(h, wg_ref[:, sl])) * _dot3(h, wu_ref[:, sl])
        y = y + _dot3(a, wd_ref[sl, :])
    o_ref[...] = x_ref[...] + mod_ref[5] * y


def _ffn_s(h2, x, mod, wg, wu, wd):
    return pl.pallas_call(
        _ffn_s_body,
        out_shape=jax.ShapeDtypeStruct(x.shape, F32),
        compiler_params=pltpu.CompilerParams(vmem_limit_bytes=56 * MIB),
        name="ffn_s",
    )(h2, x, mod, wg, wu, wd)


def _outffn_p_body(tiles_per_batch, mix_ref, x_ref, mod_ref, w_ref, n2_ref, wg_ref, wu_ref, wd_ref, o_ref):
    b = pl.program_id(0) // tiles_per_batch
    mrow = lambda j: mod_ref[j, pl.ds(b, 1), :]
    xn, h2 = _outproj_compute(_dot_bf, mix_ref[...], x_ref[...], mrow(2), mrow(3), mrow(4), w_ref, n2_ref[...])
    o_ref[...] = xn + mrow(5) * _swiglu(h2.astype(BF), wg_ref, wu_ref, wd_ref)


def _outffn_p(mix, x, mod, w_bf, n2, wg, wu, wd, batch, seq):
    t = x.shape[0]
    tm = min(512, seq)
    row = lambda i: (i, 0)
    const = lambda shape: pl.BlockSpec(shape, lambda i: (0,) * len(shape), pipeline_mode=pl.Buffered(1))
    return pl.pallas_call(
        functools.partial(_outffn_p_body, seq // tm),
        grid=(t // tm,),
        in_specs=[
            pl.BlockSpec((tm, D_MODEL), row),
            pl.BlockSpec((tm, D_MODEL), row),
            const((N_ADA, batch, D_MODEL)),
            const((D_MODEL, D_MODEL)),
            const((1, D_MODEL)),
            const((D_MODEL, D_FF)), const((D_MODEL, D_FF)), const((D_FF, D_MODEL)),
        ],
        out_specs=pl.BlockSpec((tm, D_MODEL), row),
        out_shape=jax.ShapeDtypeStruct((t, D_MODEL), F32),
        compiler_params=_params(("arbitrary",), 56),
        name="outffn_p",
    )(mix, x, mod, w_bf, n2, wg, wu, wd)


def _moe_s_body(h_ref, x_ref, gates_ref, mod_ref, wg_ref, wu_ref, wd_ref, o_ref):
    e = pl.program_id(0)

    @pl.when(e == 0)
    def _():
        o_ref[...] = jnp.zeros_like(o_ref)

    lane = lax.broadcasted_iota(jnp.int32, (1, LANES), 1)
    gate = jnp.sum(jnp.where(lane == e, gates_ref[...], 0.0), axis=-1, keepdims=True)
    o_ref[...] += gate * _swiglu(h_ref[...].astype(BF), wg_ref, wu_ref, wd_ref)

    @pl.when(e == N_EXPERTS - 1)
    def _():
        o_ref[...] = x_ref[...] + mod_ref[5] * o_ref[...]


def _moe_s(h2, x, gates, mod, wg, wu, wd):
    n = x.shape[0]
    whole = lambda shape: pl.BlockSpec(shape, lambda e: (0,) * len(shape))
    wspec = lambda shape: pl.BlockSpec((None,) + shape, lambda e: (e, 0, 0))
    return pl.pallas_call(
        _moe_s_body,
        grid=(N_EXPERTS,),
        in_specs=[whole((n, D_MODEL)), whole((n, D_MODEL)), whole((n, LANES)), whole(mod.shape),
                  wspec((D_MODEL, D_FF)), wspec((D_MODEL, D_FF)), wspec((D_FF, D_MODEL))],
        out_specs=whole((n, D_MODEL)),
        out_shape=jax.ShapeDtypeStruct((n, D_MODEL), F32),
        compiler_params=_params(("arbitrary",), 56),
        name="moe_s",
    )(h2, x, gates, mod, wg, wu, wd)


TM_MOE = 512
TD = 512


def _to_token_tiles(ref, x):
    r = x.shape[0]
    for g in range(SUB):
        ref[pl.ds(g, r, stride=SUB), :] = x[:, g * LANES:(g + 1) * LANES]


def _from_token_tiles(ref, first, r):
    return jnp.concatenate([ref[pl.ds(first * SUB + g, r, stride=SUB), :] for g in range(SUB)], axis=-1)


def _token_copy(src, s, dst, d, sem):
    aligned = lambda v: v if isinstance(v, int) else pl.multiple_of(v, SUB)
    return pltpu.make_async_copy(src.at[pl.ds(aligned(s), SUB), :], dst.at[pl.ds(aligned(d), SUB), :], sem)


def _dispatch_body(pos_ref, pad_ref, h_ref, xs_ref, stage, sem, zsem):
    i = pl.program_id(0)
    n = pl.num_programs(0)
    td = h_ref.shape[0]
    slot = i % 2

    def wait_slot(s):
        for _ in range(2):
            pltpu.make_async_copy(stage.at[s], xs_ref.at[pl.ds(0, td * SUB), :], sem.at[s]).wait()

    @pl.when(i >= 2)
    def _():
        wait_slot(slot)

    _to_token_tiles(stage.at[slot], h_ref[...])

    def issue(r, c):
        for k in range(2):
            _token_copy(stage.at[slot], r * SUB, xs_ref, pos_ref[0, 0, k * td + r], sem.at[slot]).start(priority=k)
        return c

    lax.fori_loop(0, td, issue, 0, unroll=8)

    @pl.when(i == n - 1)
    def _():
        wait_slot(slot)

        @pl.when(n > 1)
        def _():
            wait_slot(1 - slot)

        stage[0] = jnp.zeros(stage.shape[1:], stage.dtype)
        for e in range(N_EXPERTS):
            lo = pad_ref[0, e]
            hi = pad_ref[1, e]

            def zero_token(r, c):
                _token_copy(stage.at[0], 0, xs_ref, r * SUB, zsem).start()
                return c

            def wait_token(r, c):
                _token_copy(stage.at[0], 0, xs_ref, 0, zsem).wait()
                return c

            lax.fori_loop(lo, hi, zero_token, 0)
            lax.fori_loop(lo, hi, wait_token, 0)

        def zero_blk(j, c):
            pltpu.make_async_copy(stage.at[0], xs_ref.at[pl.ds(pl.multiple_of(j * (td * SUB), SUB), td * SUB), :],
                                  zsem).start()
            return c

        def wait_blk(j, c):
            pltpu.make_async_copy(stage.at[0], xs_ref.at[pl.ds(0, td * SUB), :], zsem).wait()
            return c

        lax.fori_loop(pad_ref[0, N_EXPERTS], pad_ref[1, N_EXPERTS], zero_blk, 0)
        lax.fori_loop(pad_ref[0, N_EXPERTS], pad_ref[1, N_EXPERTS], wait_blk, 0)


def _dispatch(h2, pos_t, pad, npad):
    t = h2.shape[0]
    td = min(TD, t)
    return pl.pallas_call(
        _dispatch_body,
        grid=(t // td,),
        in_specs=[
            pl.BlockSpec((1, 1, 2 * td), lambda i: (i, 0, 0), memory_space=pltpu.SMEM),
            pl.BlockSpec(memory_space=pltpu.SMEM),
            pl.BlockSpec((td, D_MODEL), lambda i: (i, 0)),
        ],
        out_specs=pl.BlockSpec(memory_space=pl.ANY),
        out_shape=jax.ShapeDtypeStruct((npad * SUB, LANES), F32),
        scratch_shapes=[pltpu.VMEM((2, td * SUB, LANES), F32), pltpu.SemaphoreType.DMA((2,)),
                        pltpu.SemaphoreType.DMA(())],
        compiler_params=_params(("arbitrary",), 32),
        name="dispatch",
    )(pos_t, pad, h2)


def _moe_body(te_ref, src_ref, nv_ref, x_ref, wg_ref, wu_ref, wd_ref, o_ref):
    i = pl.program_id(0)

    @pl.when(nv_ref[i] > 0)
    def _():
        x = _from_token_tiles(x_ref, 0, TM_MOE).astype(BF)
        _to_token_tiles(o_ref, _swiglu(x, wg_ref, wu_ref, wd_ref))

    @pl.when(nv_ref[i] == 0)
    def _():
        o_ref[...] = jnp.zeros_like(o_ref)


def _moe(xs, tile_e, tile_src, tile_nv, wg, wu, wd):
    rows = TM_MOE * SUB
    wspec = lambda shape: pl.BlockSpec((None,) + shape, lambda i, te, src, nv: (te[i], 0, 0))
    return pl.pallas_call(
        _moe_body,
        grid_spec=pltpu.PrefetchScalarGridSpec(
            num_scalar_prefetch=3,
            grid=(xs.shape[0] // rows,),
            in_specs=[
                pl.BlockSpec((rows, LANES), lambda i, te, src, nv: (src[i], 0)),
                wspec((D_MODEL, D_FF)), wspec((D_MODEL, D_FF)), wspec((D_FF, D_MODEL)),
            ],
            out_specs=pl.BlockSpec((rows, LANES), lambda i, te, src, nv: (i, 0)),
        ),
        out_shape=jax.ShapeDtypeStruct(xs.shape, F32),
        compiler_params=_params(("arbitrary",), 56),
        name="moe",
    )(tile_e, tile_src, tile_nv, xs, wg, wu, wd)


def _combine_body(tiles_per_batch, posc_ref, posn_ref, x_ref, route_ref, mod_ref, ys_ref, o_ref, buf, sem):
    i = pl.program_id(0)
    n = pl.num_programs(0)
    tc = x_ref.shape[0]

    def gather(p_ref, s):
        def issue(r, c):
            for k in range(2):
                _token_copy(ys_ref, p_ref[0, 0, k * tc + r], buf.at[s], (k * tc + r) * SUB,
                            sem.at[s]).start(priority=k)
            return c

        lax.fori_loop(0, tc, issue, 0, unroll=8)

    @pl.when(i == 0)
    def _():
        gather(posc_ref, 0)

    @pl.when(i + 1 < n)
    def _():
        gather(posn_ref, (i + 1) % 2)

    slot = i % 2
    pltpu.make_async_copy(ys_ref.at[pl.ds(0, 2 * tc * SUB), :], buf.at[slot], sem.at[slot]).wait()
    lane = lax.broadcasted_iota(jnp.int32, (1, LANES), 1)
    rt = route_ref[...]
    g1 = jnp.sum(jnp.where(lane == ROUTE_GATE, rt, 0.0), axis=-1, keepdims=True)
    g2 = jnp.sum(jnp.where(lane == ROUTE_GATE + 1, rt, 0.0), axis=-1, keepdims=True)
    y = g1 * _from_token_tiles(buf.at[slot], 0, tc) + g2 * _from_token_tiles(buf.at[slot], tc, tc)
    ga2 = mod_ref[5, pl.ds(i // tiles_per_batch, 1), :]
    o_ref[...] = x_ref[...] + ga2 * y


def _combine(ys, pos_t, x, route, mod, seq):
    t = x.shape[0]
    tc = min(TD, t)
    nt = t // tc
    row = lambda i: (i, 0)
    return pl.pallas_call(
        functools.partial(_combine_body, seq // tc),
        grid=(nt,),
        in_specs=[
            pl.BlockSpec((1, 1, 2 * tc), lambda i: (i, 0, 0), memory_space=pltpu.SMEM),
            pl.BlockSpec((1, 1, 2 * tc), lambda i: (jnp.minimum(i + 1, nt - 1), 0, 0), memory_space=pltpu.SMEM),
            pl.BlockSpec((tc, D_MODEL), row),
            pl.BlockSpec((tc, LANES), row),
            pl.BlockSpec(mod.shape, lambda i: (0, 0, 0)),
            pl.BlockSpec(memory_space=pl.ANY),
        ],
        out_specs=pl.BlockSpec((tc, D_MODEL), row),
        out_shape=jax.ShapeDtypeStruct((t, D_MODEL), F32),
        scratch_shapes=[pltpu.VMEM((2, 2 * tc * SUB, LANES), F32), pltpu.SemaphoreType.DMA((2,))],
        compiler_params=_params(("arbitrary",), 32),
        name="combine",
    )(pos_t, pos_t, x, route, mod, ys)


def _moe_routed(h2, xn, route, route_t, cnt, mod, wg, wu, wd, seq):
    t = h2.shape[0]
    td = min(TD, t)
    nt_max = pl.cdiv(2 * t, TM_MOE) + N_EXPERTS
    npad = nt_max * TM_MOE
    counts = cnt[0, :N_EXPERTS].astype(jnp.int32)
    ntile = (counts + TM_MOE - 1) // TM_MOE
    eid = jnp.arange(N_EXPERTS)
    tile_end = jnp.sum(jnp.where(eid[None, :] <= eid[:, None], ntile[None, :], 0), axis=1)
    off = (tile_end - ntile) * TM_MOE
    e12 = route_t[ROUTE_E:ROUTE_E + 2].astype(jnp.int32)
    r12 = route_t[ROUTE_RANK:ROUTE_RANK + 2].astype(jnp.int32)
    onehot = e12[:, :, None] == eid[None, None, :]
    pos = jnp.sum(jnp.where(onehot, off[None, None, :], 0), axis=-1) + r12
    pos_t = (pos * SUB).reshape(2, t // td, td).transpose(1, 0, 2).reshape(t // td, 1, 2 * td)
    total = tile_end[-1]
    tid = jnp.arange(nt_max)
    tile_e = jnp.minimum(jnp.sum(tid[:, None] >= tile_end[None, :], axis=1), N_EXPERTS - 1).astype(jnp.int32)
    tile_nv = (tid < total).astype(jnp.int32)
    tile_src = jnp.minimum(tid, total - 1).astype(jnp.int32)
    pad = jnp.stack([jnp.concatenate([off + counts, (total * (TM_MOE // td))[None]]),
                     jnp.concatenate([off + ntile * TM_MOE, jnp.full((1,), npad // td, jnp.int32)])]).astype(jnp.int32)
    xs = _dispatch(h2, pos_t, pad, npad)
    ys = _moe(xs, tile_e, tile_src, tile_nv, wg, wu, wd)
    return _combine(ys, pos_t, xn, route, mod, seq)


def _rope_tables(pos):
    inv = ROPE_THETA ** (-np.arange(0, HEAD_DIM, 2, dtype=np.float64) / HEAD_DIM)
    ang = np.asarray(pos, np.float64)[:, None] * inv[None, :]
    cos = np.concatenate([np.cos(ang), np.cos(ang)], axis=-1)
    sin = np.concatenate([-np.sin(ang), np.sin(ang)], axis=-1)
    reps = LANES // HEAD_DIM
    return (jnp.asarray(np.tile(cos, (1, reps)), F32), jnp.asarray(np.tile(sin, (1, reps)), F32))


def kernel(x_prompt, x_sample, cache_k, cache_v, c_prompt, c_sample, w_ada, b_ada, norm1_w, norm2_w, w_in,
           q_norm_w, k_norm_w, attn_sinks, gm_norm_w, gm_ws, gm_bs, w_out, dense_w_gate, dense_w_up,
           dense_w_down, router_w, router_b, moe_w_gate, moe_w_up, moe_w_down):
    batch, seq, d = x_prompt.shape
    nd = x_sample.shape[0]
    depth = w_in.shape[0]
    t = batch * seq

    mod = _ada(jnp.concatenate([c_prompt, c_sample], axis=0), w_ada, b_ada)
    mod_p = mod[:, :batch].reshape(depth, batch, N_ADA, d).transpose(0, 2, 1, 3)
    mod_s = mod[:, batch:].reshape(depth, nd, N_ADA, d).transpose(0, 2, 1, 3)

    cos_p, sin_p = _rope_tables(np.arange(seq))
    cos_s, sin_s = _rope_tables(np.array([PAST_LEN]))
    head_of = np.arange(ATTN_WIDTH) // HEAD_DIM
    seg = jnp.asarray(head_of[:, None] == head_of[None, :], BF)
    kv_of_head = (jnp.arange(N_HEADS) // KV_GROUP)[None, :, None]

    w_in_bf = w_in.astype(BF)
    w_out_bf = w_out.astype(BF)
    dense_bf = [w.astype(BF) for w in (dense_w_gate, dense_w_up, dense_w_down)]
    moe_bf = [w.astype(BF) for w in (moe_w_gate, moe_w_up, moe_w_down)]
    router_w_pad = jnp.pad(router_w, ((0, 0), (0, 0), (0, LANES - N_EXPERTS)))
    router_b_pad = jnp.pad(router_b, ((0, 0), (0, LANES - N_EXPERTS)), constant_values=NEG_INF)

    xp = x_prompt.reshape(t, d)
    xs = x_sample.reshape(nd, d)
    k_p, v_p, g_p, k_s, v_s, g_s = [], [], [], [], [], []
    for l in range(depth):
        i = l // 2
        n1 = norm1_w[l][None, :]
        n2 = norm2_w[l][None, :]
        qn = jnp.tile(q_norm_w[l], N_HEADS)[None, :]
        kn = jnp.tile(k_norm_w[l], N_KV_HEADS)[None, :]
        gmn = gm_norm_w[l][None, :]
        router = None if l % 2 == 0 else (router_w_pad[i], router_b_pad[i][None, :])

        q, kd, vd, u, gv, kl, vl, gvl = _inproj_p(xp, mod_p[l], n1, w_in_bf[l], qn, kn, gmn, seg,
                                                  cos_p, sin_p, batch, seq)
        mix = _mix_p(q, kd, vd, u, gv, gm_ws[l], gm_bs[l].T, attn_sinks[l], batch, seq)
        if router is None:
            xp = _outffn_p(mix, xp, mod_p[l], w_out_bf[l], n2, dense_bf[0][i], dense_bf[1][i], dense_bf[2][i],
                           batch, seq)
        else:
            xn, h2, route, route_t, cnt = _outproj_p(mix, xp, mod_p[l], w_out_bf[l], n2, *router, batch, seq)
            xp = _moe_routed(h2, xn, route, route_t, cnt, mod_p[l], moe_bf[0][i], moe_bf[1][i], moe_bf[2][i], seq)
        k_p.append(kl.reshape(batch, WINDOW, N_KV_HEADS, HEAD_DIM))
        v_p.append(vl.reshape(batch, WINDOW, N_KV_HEADS, HEAD_DIM))
        g_p.append(gvl)

        q, k, v, u, gv = _inproj_s(xs, mod_s[l], n1, w_in[l], qn, kn, gmn, seg, cos_s, sin_s)
        qh = q.reshape(nd, N_HEADS, HEAD_DIM)
        zq = jnp.zeros_like(qh)
        qpad = jnp.where(kv_of_head == 0, jnp.concatenate([qh, zq], -1), jnp.concatenate([zq, qh], -1))
        w = cache_k.shape[2]
        o, nk, nv = _attn_s(qpad, k[:, None, :], v[:, None, :], cache_k[l].reshape(nd, w, KV_WIDTH),
                            cache_v[l].reshape(nd, w, KV_WIDTH),
                            jnp.broadcast_to(attn_sinks[l][:, None], (N_HEADS, LANES)))
        o = jnp.where(kv_of_head == 0, o[..., :HEAD_DIM], o[..., HEAD_DIM:]).reshape(nd, ATTN_WIDTH)
        wdiag = jnp.repeat(gm_ws[l][:, 0, 0], GM_WIDTH // GM_GROUPS)[None, :]
        bsrow = jnp.repeat(gm_bs[l][:, 0], GM_WIDTH // GM_GROUPS)[None, :]
        res = _outproj_s(o, u, gv, wdiag, bsrow, xs, mod_s[l], w_out[l], n2, router)
        if router is None:
            xs = _ffn_s(res[1], res[0], mod_s[l], dense_w_gate[i], dense_w_up[i], dense_w_down[i])
        else:
            xs = _moe_s(res[1], res[0], res[2], mod_s[l], moe_bf[0][i], moe_bf[1][i], moe_bf[2][i])
        k_s.append(nk.reshape(nd, w, N_KV_HEADS, HEAD_DIM))
        v_s.append(nv.reshape(nd, w, N_KV_HEADS, HEAD_DIM))
        g_s.append(gv[:, None, :])

    return (xp.reshape(batch, seq, d), xs.reshape(nd, 1, d), jnp.stack(k_p), jnp.stack(v_p), jnp.stack(g_p),
            jnp.stack(k_s), jnp.stack(v_s), jnp.stack(g_s))
```

```python
import functools

import numpy as np
import jax
import jax.numpy as jnp
from jax import lax
from jax.experimental import pallas as pl
from jax.experimental.pallas import tpu as pltpu

D_MODEL = 1024
HEAD_DIM = 64
N_HEADS = 8
N_KV_HEADS = 2
KV_GROUP = N_HEADS // N_KV_HEADS
ATTN_WIDTH = N_HEADS * HEAD_DIM
KV_WIDTH = N_KV_HEADS * HEAD_DIM
GM_WIDTH = 512
GM_GROUPS = 4
WINDOW = 128
CHUNK = 128
D_FF = 2816
MXU_DIM = 256
FF_SPLIT = (0, 6 * MXU_DIM, D_FF)
N_EXPERTS = 8
N_ADA = 6
IN_COLS = ATTN_WIDTH + 2 * KV_WIDTH + 2 * GM_WIDTH
PAST_LEN = 16384
ROPE_THETA = 10000.0
EPS = 1e-6
NEG_INF = -1e30
LANES = 128
SUB = 8
assert D_MODEL == SUB * LANES

BF = jnp.bfloat16
F32 = jnp.float32
MIB = 1024 * 1024


ROW_GROUP = 256


def _params(sem, vmem_mib):
    return pltpu.CompilerParams(dimension_semantics=sem, vmem_limit_bytes=vmem_mib * MIB)


def _dot(a, b):
    return jnp.dot(a, b, preferred_element_type=F32)


def _dot_nt(a, b):
    return lax.dot_general(a, b, (((1,), (1,)), ((), ())), preferred_element_type=F32)


def _split(a):
    hi = a.astype(BF)
    return hi, (a - hi.astype(F32)).astype(BF)


def _dot_bf(a, w):
    return _dot(a.astype(BF), w)


def _dot3(a, w):
    ah, al = _split(a)
    if w.dtype == BF:
        return _dot(ah, w) + _dot(al, w)
    wh, wl = _split(w)
    return _dot(ah, wh) + _dot(al, wh) + _dot(ah, wl)


def _rms(x, w):
    ms = jnp.mean(x * x, axis=-1, keepdims=True)
    return x * lax.rsqrt(ms + EPS) * w


def _ada_body(c_ref, w_ref, b_ref, o_ref):
    o_ref[...] = _dot3(jax.nn.silu(c_ref[...]), w_ref[...]) + b_ref[...]


def _ada(c_all, w_ada, b_ada):
    depth, d, cols = w_ada.shape
    n = c_all.shape[0]
    tn = 1024
    return pl.pallas_call(
        _ada_body,
        grid=(depth, cols // tn),
        in_specs=[
            pl.BlockSpec((n, d), lambda l, j: (0, 0)),
            pl.BlockSpec((None, d, tn), lambda l, j: (l, 0, j)),
            pl.BlockSpec((None, 1, tn), lambda l, j: (l, 0, j)),
        ],
        out_specs=pl.BlockSpec((None, n, tn), lambda l, j: (l, 0, j)),
        out_shape=jax.ShapeDtypeStruct((depth, n, cols), F32),
        compiler_params=_params(("arbitrary", "arbitrary"), 32),
        name="ada",
    )(c_all, w_ada, b_ada.reshape(depth, 1, cols))


def _swap_halves(t):
    n = t.shape[-1]
    lane = lax.broadcasted_iota(jnp.int32, (1, n), 1)
    first = (lane % HEAD_DIM) < (HEAD_DIM // 2)
    return jnp.where(first, pltpu.roll(t, n - HEAD_DIM // 2, axis=1), pltpu.roll(t, HEAD_DIM // 2, axis=1))


def _inproj_compute(mm, x, sh, sc, n1, w_ref, qn, kn, gmn, seg_ref, cos, sin):
    h = _rms(x, n1) * (1.0 + sc) + sh
    z = mm(h, w_ref[...])
    q = z[:, :ATTN_WIDTH]
    k = z[:, ATTN_WIDTH:ATTN_WIDTH + KV_WIDTH]
    v = z[:, ATTN_WIDTH + KV_WIDTH:ATTN_WIDTH + 2 * KV_WIDTH]
    gm = z[:, ATTN_WIDTH + 2 * KV_WIDTH:]

    def head_norm(t, seg, wn):
        ms = mm(t * t, seg) * (1.0 / HEAD_DIM)
        return t * lax.rsqrt(ms + EPS) * wn

    def rope(t):
        reps = t.shape[-1] // LANES
        c = jnp.concatenate([cos] * reps, axis=-1) if reps > 1 else cos
        s = jnp.concatenate([sin] * reps, axis=-1) if reps > 1 else sin
        return t * c + _swap_halves(t) * s

    q = rope(head_norm(q, seg_ref[...], qn)) * (HEAD_DIM ** -0.5)
    k = rope(head_norm(k, seg_ref[:KV_WIDTH, :KV_WIDTH], kn))
    g = jax.nn.gelu(gm)
    u = g[:, :GM_WIDTH]
    gv = _rms(g[:, GM_WIDTH:], gmn)
    return q, k, v, u, gv


def _dup_heads(t):
    lane = lax.broadcasted_iota(jnp.int32, (1, LANES), 1)
    lo = lane < HEAD_DIM
    r = pltpu.roll(t, HEAD_DIM, axis=1)
    return jnp.concatenate([jnp.where(lo, t, r), jnp.where(lo, r, t)], axis=-1)


def _inproj_p_body(tiles_per_batch, x_ref, mod_ref, n1_ref, w_ref, qn_ref, kn_ref, gmn_ref, seg_ref,
                   cos_ref, sin_ref, q_ref, kd_ref, vd_ref, u_ref, gv_ref, kl_ref, vl_ref, gvl_ref):
    i = pl.program_id(0)
    b = i // tiles_per_batch
    sh = mod_ref[0, pl.ds(b, 1), :]
    sc = mod_ref[1, pl.ds(b, 1), :]
    hs = min(ROW_GROUP, x_ref.shape[0])
    for hh in range(x_ref.shape[0] // hs):
        rs = slice(hh * hs, (hh + 1) * hs)
        q, k, v, u, gv = _inproj_compute(_dot_bf, x_ref[rs, :], sh, sc, n1_ref[...], w_ref, qn_ref[...], kn_ref[...],
                                         gmn_ref[...], seg_ref, cos_ref[rs, :], sin_ref[rs, :])
        q_ref[rs, :] = q.astype(BF)
        kd_ref[rs, :] = _dup_heads(k).astype(BF)
        vd_ref[rs, :] = _dup_heads(v).astype(BF)
        u_ref[rs, :] = u.astype(BF)
        gv_ref[rs, :] = gv.astype(BF)

    @pl.when(i % tiles_per_batch == tiles_per_batch - 1)
    def _():
        kl_ref[...] = k[hs - WINDOW:, :]
        vl_ref[...] = v[hs - WINDOW:, :]
        gvl_ref[...] = gv[hs - CHUNK:, :]


def _inproj_p(l, x, mod, n1, w_bf, qn, kn, gmn, seg, cos, sin, batch, seq):
    t = x.shape[0]
    tm = min(1024, seq)
    tpb = seq // tm
    row = lambda i: (i, 0)
    full = lambda i: (0, 0)
    last = lambda i: (i // tpb, 0, 0)
    return pl.pallas_call(
        functools.partial(_inproj_p_body, tpb),
        grid=(t // tm,),
        in_specs=[
            pl.BlockSpec((tm, D_MODEL), row),
            pl.BlockSpec((N_ADA, batch, D_MODEL), lambda i: (0, 0, 0)),
            pl.BlockSpec((1, D_MODEL), full),
            pl.BlockSpec((None, D_MODEL, IN_COLS), lambda i: (l, 0, 0)),
            pl.BlockSpec((1, ATTN_WIDTH), full),
            pl.BlockSpec((1, KV_WIDTH), full),
            pl.BlockSpec((1, GM_WIDTH), full),
            pl.BlockSpec((ATTN_WIDTH, ATTN_WIDTH), full),
            pl.BlockSpec((tm, LANES), lambda i: (i % tpb, 0)),
            pl.BlockSpec((tm, LANES), lambda i: (i % tpb, 0)),
        ],
        out_specs=[
            pl.BlockSpec((tm, ATTN_WIDTH), row),
            pl.BlockSpec((tm, 2 * KV_WIDTH), row),
            pl.BlockSpec((tm, 2 * KV_WIDTH), row),
            pl.BlockSpec((tm, GM_WIDTH), row),
            pl.BlockSpec((tm, GM_WIDTH), row),
            pl.BlockSpec((None, WINDOW, KV_WIDTH), last),
            pl.BlockSpec((None, WINDOW, KV_WIDTH), last),
            pl.BlockSpec((None, CHUNK, GM_WIDTH), last),
        ],
        out_shape=[
            jax.ShapeDtypeStruct((t, ATTN_WIDTH), BF),
            jax.ShapeDtypeStruct((t, 2 * KV_WIDTH), BF),
            jax.ShapeDtypeStruct((t, 2 * KV_WIDTH), BF),
            jax.ShapeDtypeStruct((t, GM_WIDTH), BF),
            jax.ShapeDtypeStruct((t, GM_WIDTH), BF),
            jax.ShapeDtypeStruct((batch, WINDOW, KV_WIDTH), F32),
            jax.ShapeDtypeStruct((batch, WINDOW, KV_WIDTH), F32),
            jax.ShapeDtypeStruct((batch, CHUNK, GM_WIDTH), F32),
        ],
        compiler_params=_params(("arbitrary",), 48),
        name="inproj_p",
    )(x, mod, n1, w_bf, qn, kn, gmn, seg, cos, sin)


def _inproj_s_body(x_ref, mod_ref, n1_ref, w_ref, qn_ref, kn_ref, gmn_ref, seg_ref, cos_ref, sin_ref,
                   q_ref, k_ref, v_ref, u_ref, gv_ref):
    q, k, v, u, gv = _inproj_compute(_dot3, x_ref[...], mod_ref[0], mod_ref[1], n1_ref[...], w_ref, qn_ref[...],
                                     kn_ref[...], gmn_ref[...], seg_ref, cos_ref[...], sin_ref[...])
    q_ref[...] = q
    k_ref[...] = k
    v_ref[...] = v
    u_ref[...] = u
    gv_ref[...] = gv


def _inproj_s(x, mod, n1, w_bf, qn, kn, gmn, seg, cos, sin):
    n = x.shape[0]
    widths = (ATTN_WIDTH, KV_WIDTH, KV_WIDTH, GM_WIDTH, GM_WIDTH)
    return pl.pallas_call(
        _inproj_s_body,
        out_shape=[jax.ShapeDtypeStruct((n, w), F32) for w in widths],
        compiler_params=pltpu.CompilerParams(vmem_limit_bytes=48 * MIB),
        name="inproj_s",
    )(x, mod, n1, w_bf, qn, kn, gmn, seg, cos, sin)


def _mix_p_body(nblk, q_ref, kc_ref, kp_ref, vc_ref, vp_ref, u_ref, gv_ref, ws_ref, bst_ref, sink_ref, o_ref):
    i = pl.program_id(1)
    blk = WINDOW
    lane = lax.broadcasted_iota(jnp.int32, (1, LANES), 1)
    lo = lane < HEAD_DIM
    cols = KV_GROUP * blk
    iq = lax.broadcasted_iota(jnp.int32, (2 * blk, cols), 1) % blk
    jk = lax.broadcasted_iota(jnp.int32, (2 * blk, cols), 0)
    band = (jk > iq) & (jk <= iq + blk)
    bias = jnp.where(band, 0.0, NEG_INF)
    bias_first = jnp.where(band & ((jk >= blk) | (i > 0)), 0.0, NEG_INF)
    tri = (lax.broadcasted_iota(jnp.int32, (CHUNK, CHUNK), 0)
           >= lax.broadcasted_iota(jnp.int32, (CHUNK, CHUNK), 1))
    wm = [jnp.where(tri, ws_ref[g], 0.0).astype(BF) for g in range(GM_GROUPS)]

    for n in range(nblk):
        r0 = n * blk
        if n == 0:
            kk = jnp.concatenate([kp_ref[...], kc_ref[0:blk, :]], axis=0)
            vv = jnp.concatenate([vp_ref[...], vc_ref[0:blk, :]], axis=0)
            mask_bias = bias_first
        else:
            kk = kc_ref[r0 - blk:r0 + blk, :]
            vv = vc_ref[r0 - blk:r0 + blk, :]
            mask_bias = bias
        for kvh in range(N_KV_HEADS):
            c0 = 2 * kvh
            qa = q_ref[r0:r0 + blk, c0 * LANES:(c0 + 1) * LANES]
            qb = q_ref[r0:r0 + blk, (c0 + 1) * LANES:(c0 + 2) * LANES]
            zero = jnp.zeros_like(qa)
            qq = jnp.concatenate([jnp.where(lo, qa, zero), jnp.where(lo, zero, qa),
                                  jnp.where(lo, qb, zero), jnp.where(lo, zero, qb)], axis=0)
            s = _dot_nt(kk[:, kvh * LANES:(kvh + 1) * LANES], qq) + mask_bias
            sink = jnp.concatenate(
                [jnp.full((1, blk), sink_ref[kvh * KV_GROUP + g], F32) for g in range(KV_GROUP)], axis=1)
            m = jnp.maximum(jnp.max(s, axis=0, keepdims=True), sink)
            p = jnp.exp(s - m)
            den = jnp.sum(p, axis=0, keepdims=True) + jnp.exp(sink - m)
            p = (p * (1.0 / den)).astype(BF)
            o = lax.dot_general(p, vv[:, kvh * LANES:(kvh + 1) * LANES], (((0,), (0,)), ((), ())),
                                preferred_element_type=F32)
            o_ref[r0:r0 + blk, c0 * LANES:(c0 + 1) * LANES] = jnp.where(
                lo, o[0:blk], o[blk:2 * blk]).astype(BF)
            o_ref[r0:r0 + blk, (c0 + 1) * LANES:(c0 + 2) * LANES] = jnp.where(
                lo, o[2 * blk:3 * blk], o[3 * blk:4 * blk]).astype(BF)
        for g in range(GM_GROUPS):
            cs = slice(g * LANES, (g + 1) * LANES)
            sp = _dot(wm[g], gv_ref[r0:r0 + blk, cs]) + bst_ref[:, g:g + 1]
            o_ref[r0:r0 + blk, ATTN_WIDTH + g * LANES:ATTN_WIDTH + (g + 1) * LANES] = (
                u_ref[r0:r0 + blk, cs].astype(F32) * sp).astype(BF)


def _mix_p(q, kd, vd, u, gv, ws, bst, sinks, batch, seq):
    t = q.shape[0]
    tq = min(1024, seq)
    nblk = tq // WINDOW
    tpb = seq // tq
    cur = lambda b, i: (b * tpb + i, 0)
    prev = lambda b, i: (jnp.maximum((b * tpb + i) * nblk - 1, b * tpb * nblk), 0)
    return pl.pallas_call(
        functools.partial(_mix_p_body, nblk),
        grid=(batch, tpb),
        in_specs=[
            pl.BlockSpec((tq, ATTN_WIDTH), cur),
            pl.BlockSpec((tq, 2 * KV_WIDTH), cur),
            pl.BlockSpec((WINDOW, 2 * KV_WIDTH), prev),
            pl.BlockSpec((tq, 2 * KV_WIDTH), cur),
            pl.BlockSpec((WINDOW, 2 * KV_WIDTH), prev),
            pl.BlockSpec((tq, GM_WIDTH), cur),
            pl.BlockSpec((tq, GM_WIDTH), cur),
            pl.BlockSpec((GM_GROUPS, CHUNK, CHUNK), lambda b, i: (0, 0, 0)),
            pl.BlockSpec((CHUNK, GM_GROUPS), lambda b, i: (0, 0)),
            pl.BlockSpec(memory_space=pltpu.SMEM),
        ],
        out_specs=pl.BlockSpec((tq, D_MODEL), cur),
        out_shape=jax.ShapeDtypeStruct((t, D_MODEL), BF),
        compiler_params=_params(("arbitrary", "arbitrary"), 48),
        name="mix_p",
    )(q, kd, kd, vd, vd, u, gv, ws, bst, sinks)


def _attn_s_body(q_ref, kn_ref, vn_ref, ck_ref, cv_ref, sink_ref, o_ref, nk_ref, nv_ref):
    w = ck_ref.shape[-1]
    pos = lax.broadcasted_iota(jnp.int32, (1, 1, w), 2)
    nk = jnp.where(pos == w - 1, kn_ref[...], pltpu.roll(ck_ref[...], w - 1, axis=2))
    nv = jnp.where(pos == w - 1, vn_ref[...], pltpu.roll(cv_ref[...], w - 1, axis=2))
    nk_ref[...] = nk
    nv_ref[...] = nv

    def bmm3(spec, a, b):
        (ah, al), (bh, bl) = _split(a), _split(b)
        mm = lambda x, y: jnp.einsum(spec, x, y, preferred_element_type=F32)
        return mm(ah, bh) + mm(al, bh) + mm(ah, bl)

    s = bmm3('ngd,ndj->ngj', q_ref[...], nk)
    sink = sink_ref[...]
    m = jnp.maximum(jnp.max(s, axis=-1, keepdims=True), sink)
    p = jnp.exp(s - m)
    den = jnp.sum(p, axis=-1, keepdims=True) + jnp.exp(sink - m)
    o_ref[...] = bmm3('ngj,ndj->ngd', p, nv) * (1.0 / den)


def _attn_s(l, q, k_new, v_new, ck, cv, sink):
    _, n, hd, w = ck.shape
    nb = min(32, n)
    rows = q.shape[1]
    blk = lambda r, c: pl.BlockSpec((nb, r, c), lambda i: (i, 0, 0))
    cache = pl.BlockSpec((None, nb, hd, w), lambda i: (l, i, 0, 0))
    return pl.pallas_call(
        _attn_s_body,
        grid=(n // nb,),
        in_specs=[blk(rows, hd), blk(hd, 1), blk(hd, 1), cache, cache, blk(rows, 1)],
        out_specs=[blk(rows, hd), blk(hd, w), blk(hd, w)],
        out_shape=[jax.ShapeDtypeStruct((n, rows, hd), F32),
                   jax.ShapeDtypeStruct((n, hd, w), F32),
                   jax.ShapeDtypeStruct((n, hd, w), F32)],
        compiler_params=_params(("arbitrary",), 32),
        name="attn_s",
    )(q, k_new, v_new, ck, cv, sink)


def _top2(h2, rw_ref, rb_ref):
    logits = _dot3(h2, rw_ref[...]) + rb_ref[...]
    lane = lax.broadcasted_iota(jnp.int32, logits.shape, 1).astype(F32)
    e = jnp.exp(logits - jnp.max(logits, axis=-1, keepdims=True))
    p = e / jnp.sum(e, axis=-1, keepdims=True)
    m1 = jnp.max(p, axis=-1, keepdims=True)
    i1 = jnp.min(jnp.where(p == m1, lane, float(LANES)), axis=-1, keepdims=True)
    p2 = jnp.where(lane == i1, -1.0, p)
    m2 = jnp.max(p2, axis=-1, keepdims=True)
    i2 = jnp.min(jnp.where(p2 == m2, lane, float(LANES)), axis=-1, keepdims=True)
    tot = m1 + m2
    return lane, i1, i2, m1 / tot, m2 / tot


def _route_gates(h2, rw_ref, rb_ref):
    lane, i1, i2, g1, g2 = _top2(h2, rw_ref, rb_ref)
    return jnp.where(lane == i1, g1, 0.0) + jnp.where(lane == i2, g2, 0.0)


ROUTE_E, ROUTE_RANK, ROUTE_GATE = 0, 2, 4


def _route_ranked(h2, rw_ref, rb_ref, tri_ref, cnt_ref):
    lane, i1, i2, g1, g2 = _top2(h2, rw_ref, rb_ref)
    oh1 = lane == i1
    oh2 = lane == i2
    hit = jnp.where(oh1, 1.0, 0.0) + jnp.where(oh2, 1.0, 0.0)
    before = cnt_ref[...] + _dot(tri_ref[...], hit.astype(BF))
    r1 = jnp.sum(jnp.where(oh1, before, 0.0), axis=-1, keepdims=True)
    r2 = jnp.sum(jnp.where(oh2, before, 0.0), axis=-1, keepdims=True)
    cnt_ref[...] += jnp.sum(hit, axis=0, keepdims=True)
    cols = (i1, i2, r1, r2, g1, g2)
    out = jnp.zeros_like(lane)
    for j, c in enumerate(cols):
        out = jnp.where(lane == float(j), c, out)
    return out


def _outproj_compute(mm, mix, x, ga1, sh2, sc2, w_ref, n2):
    xn = x + ga1 * mm(mix, w_ref[...])
    h2 = _rms(xn, n2) * (1.0 + sc2) + sh2
    return xn, h2


def _outproj_p_body(tiles_per_batch, mix_ref, x_ref, mod_ref, w_ref, n2_ref, rw_ref, rb_ref, tri_ref,
                    xn_ref, h2_ref, route_ref, route_t_ref, cnt_ref):
    i = pl.program_id(0)
    b = i // tiles_per_batch
    mrow = lambda j: mod_ref[j, pl.ds(b, 1), :]

    @pl.when(i == 0)
    def _():
        cnt_ref[...] = jnp.zeros_like(cnt_ref)

    xn, h2 = _outproj_compute(_dot_bf, mix_ref[...], x_ref[...], mrow(2), mrow(3), mrow(4), w_ref, n2_ref[...])
    route = _route_ranked(h2, rw_ref, rb_ref, tri_ref, cnt_ref)
    route_ref[...] = route
    route_t_ref[...] = route.T[:SUB, :]
    h2_ref[...] = h2
    xn_ref[...] = xn


def _outproj_p(l, mix, x, mod, w_bf, n2, rw, rb, batch, seq):
    t = x.shape[0]
    tm = min(512, seq)
    row = lambda i: (i, 0)
    full = lambda i: (0, 0)
    return pl.pallas_call(
        functools.partial(_outproj_p_body, seq // tm),
        grid=(t // tm,),
        in_specs=[
            pl.BlockSpec((tm, D_MODEL), row),
            pl.BlockSpec((tm, D_MODEL), row),
            pl.BlockSpec((N_ADA, batch, D_MODEL), lambda i: (0, 0, 0)),
            pl.BlockSpec((None, D_MODEL, D_MODEL), lambda i: (l, 0, 0)),
            pl.BlockSpec((1, D_MODEL), full),
            pl.BlockSpec((D_MODEL, LANES), full),
            pl.BlockSpec((1, LANES), full),
            pl.BlockSpec((tm, tm), full),
        ],
        out_specs=[pl.BlockSpec((tm, D_MODEL), row), pl.BlockSpec((tm, D_MODEL), row),
                   pl.BlockSpec((tm, LANES), row), pl.BlockSpec((SUB, tm), lambda i: (0, i)),
                   pl.BlockSpec((1, LANES), full)],
        out_shape=[jax.ShapeDtypeStruct((t, D_MODEL), F32), jax.ShapeDtypeStruct((t, D_MODEL), F32),
                   jax.ShapeDtypeStruct((t, LANES), F32), jax.ShapeDtypeStruct((SUB, t), F32),
                   jax.ShapeDtypeStruct((1, LANES), F32)],
        compiler_params=_params(("arbitrary",), 48),
        name="outproj_p",
    )(mix, x, mod, w_bf, n2, rw, rb, jnp.asarray(np.tri(tm, k=-1), BF))


def _outproj_s_body(with_router, o_ref, u_ref, gv_ref, wdiag_ref, bsrow_ref, x_ref, mod_ref, w_ref, n2_ref, *rest):
    gate = u_ref[...] * (wdiag_ref[...] * gv_ref[...] + bsrow_ref[...])
    mix = jnp.concatenate([o_ref[...], gate], axis=-1)
    xn, h2 = _outproj_compute(_dot3, mix, x_ref[...], mod_ref[2], mod_ref[3], mod_ref[4], w_ref, n2_ref[...])
    if with_router:
        rw_ref, rb_ref, xn_ref, h2_ref, gates_ref = rest
        gates_ref[...] = _route_gates(h2, rw_ref, rb_ref)
    else:
        xn_ref, h2_ref = rest
    xn_ref[...] = xn
    h2_ref[...] = h2


def _outproj_s(o, u, gv, wdiag, bsrow, x, mod, w, n2, router):
    n = x.shape[0]
    out_shape = [jax.ShapeDtypeStruct((n, D_MODEL), F32), jax.ShapeDtypeStruct((n, D_MODEL), F32)]
    args = [o, u, gv, wdiag, bsrow, x, mod, w, n2]
    if router is not None:
        out_shape.append(jax.ShapeDtypeStruct((n, LANES), F32))
        args += list(router)
    return pl.pallas_call(
        functools.partial(_outproj_s_body, router is not None),
        out_shape=out_shape,
        compiler_params=pltpu.CompilerParams(vmem_limit_bytes=32 * MIB),
        name="outproj_s",
    )(*args)


def _swiglu(h_bf, wg_ref, wu_ref, wd_ref):
    y = None
    for c in range(len(FF_SPLIT) - 1):
        sl = slice(FF_SPLIT[c], FF_SPLIT[c + 1])
        a = (jax.nn.silu(_dot(h_bf, wg_ref[:, sl])) * _dot(h_bf, wu_ref[:, sl])).astype(BF)
        part = _dot(a, wd_ref[sl, :])
        y = part if y is None else y + part
    return y


def _ffn_s_body(h_ref, x_ref, mod_ref, wg_ref, wu_ref, wd_ref, o_ref):
    h = h_ref[...]
    y = jnp.zeros_like(h)
    for c in range(D_FF // MXU_DIM):
        sl = slice(c * MXU_DIM, (c + 1) * MXU_DIM)
        a = jax.nn.silu(_dot3(h, wg_ref[:, sl])) * _dot3(h, wu_ref[:, sl])
        y = y + _dot3(a, wd_ref[sl, :])
    o_ref[...] = x_ref[...] + mod_ref[5] * y


def _ffn_s(h2, x, mod, wg, wu, wd):
    return pl.pallas_call(
        _ffn_s_body,
        out_shape=jax.ShapeDtypeStruct(x.shape, F32),
        compiler_params=pltpu.CompilerParams(vmem_limit_bytes=56 * MIB),
        name="ffn_s",
    )(h2, x, mod, wg, wu, wd)


def _outffn_p_body(tiles_per_batch, mix_ref, x_ref, mod_ref, w_ref, n2_ref, wg_ref, wu_ref, wd_ref, o_ref):
    b = pl.program_id(0) // tiles_per_batch
    mrow = lambda j: mod_ref[j, pl.ds(b, 1), :]
    xn, h2 = _outproj_compute(_dot_bf, mix_ref[...], x_ref[...], mrow(2), mrow(3), mrow(4), w_ref, n2_ref[...])
    o_ref[...] = xn + mrow(5) * _swiglu(h2.astype(BF), wg_ref, wu_ref, wd_ref)


def _outffn_p(l, mix, x, mod, w_bf, n2, wg, wu, wd, batch, seq):
    t = x.shape[0]
    tm = min(512, seq)
    row = lambda i: (i, 0)
    const = lambda shape: pl.BlockSpec(shape, lambda i: (0,) * len(shape), pipeline_mode=pl.Buffered(1))
    return pl.pallas_call(
        functools.partial(_outffn_p_body, seq // tm),
        grid=(t // tm,),
        in_specs=[
            pl.BlockSpec((tm, D_MODEL), row),
            pl.BlockSpec((tm, D_MODEL), row),
            const((N_ADA, batch, D_MODEL)),
            pl.BlockSpec((None, D_MODEL, D_MODEL), lambda i: (l, 0, 0), pipeline_mode=pl.Buffered(1)),
            const((1, D_MODEL)),
            const((D_MODEL, D_FF)), const((D_MODEL, D_FF)), const((D_FF, D_MODEL)),
        ],
        out_specs=pl.BlockSpec((tm, D_MODEL), row),
        out_shape=jax.ShapeDtypeStruct((t, D_MODEL), F32),
        compiler_params=_params(("arbitrary",), 56),
        name="outffn_p",
    )(mix, x, mod, w_bf, n2, wg, wu, wd)


def _moe_s_body(h_ref, x_ref, gates_ref, mod_ref, wg_ref, wu_ref, wd_ref, o_ref):
    e = pl.program_id(0)

    @pl.when(e == 0)
    def _():
        o_ref[...] = jnp.zeros_like(o_ref)

    lane = lax.broadcasted_iota(jnp.int32, (1, LANES), 1)
    gate = jnp.sum(jnp.where(lane == e, gates_ref[...], 0.0), axis=-1, keepdims=True)
    o_ref[...] += gate * _swiglu(h_ref[...].astype(BF), wg_ref, wu_ref, wd_ref)

    @pl.when(e == N_EXPERTS - 1)
    def _():
        o_ref[...] = x_ref[...] + mod_ref[5] * o_ref[...]


def _moe_s(h2, x, gates, mod, wg, wu, wd):
    n = x.shape[0]
    whole = lambda shape: pl.BlockSpec(shape, lambda e: (0,) * len(shape))
    wspec = lambda shape: pl.BlockSpec((None,) + shape, lambda e: (e, 0, 0))
    return pl.pallas_call(
        _moe_s_body,
        grid=(N_EXPERTS,),
        in_specs=[whole((n, D_MODEL)), whole((n, D_MODEL)), whole((n, LANES)), whole(mod.shape),
                  wspec((D_MODEL, D_FF)), wspec((D_MODEL, D_FF)), wspec((D_FF, D_MODEL))],
        out_specs=whole((n, D_MODEL)),
        out_shape=jax.ShapeDtypeStruct((n, D_MODEL), F32),
        compiler_params=_params(("arbitrary",), 56),
        name="moe_s",
    )(h2, x, gates, mod, wg, wu, wd)


TM_MOE = 512
TD = 512


def _to_token_tiles(ref, x):
    r = x.shape[0]
    for g in range(SUB):
        ref[pl.ds(g, r, stride=SUB), :] = x[:, g * LANES:(g + 1) * LANES]


def _from_token_tiles(ref, first, r):
    return jnp.concatenate([ref[pl.ds(first * SUB + g, r, stride=SUB), :] for g in range(SUB)], axis=-1)


def _token_copy(src, s, dst, d, sem):
    aligned = lambda v: v if isinstance(v, int) else pl.multiple_of(v, SUB)
    return pltpu.make_async_copy(src.at[pl.ds(aligned(s), SUB), :], dst.at[pl.ds(aligned(d), SUB), :], sem)


def _dispatch_body(pos_ref, pad_ref, h_ref, xs_ref, stage, sem, zsem):
    i = pl.program_id(0)
    n = pl.num_programs(0)
    td = h_ref.shape[0]
    slot = i % 2

    def wait_slot(s):
        for _ in range(2):
            pltpu.make_async_copy(stage.at[s], xs_ref.at[pl.ds(0, td * SUB), :], sem.at[s]).wait()

    @pl.when(i >= 2)
    def _():
        wait_slot(slot)

    _to_token_tiles(stage.at[slot], h_ref[...])

    def issue(r, c):
        for k in range(2):
            _token_copy(stage.at[slot], r * SUB, xs_ref, pos_ref[0, 0, k * td + r], sem.at[slot]).start(priority=k)
        return c

    lax.fori_loop(0, td, issue, 0, unroll=8)

    @pl.when(i == n - 1)
    def _():
        wait_slot(slot)

        @pl.when(n > 1)
        def _():
            wait_slot(1 - slot)

        stage[0] = jnp.zeros(stage.shape[1:], stage.dtype)
        for e in range(N_EXPERTS):
            lo = pad_ref[0, e]
            hi = pad_ref[1, e]

            def zero_token(r, c):
                _token_copy(stage.at[0], 0, xs_ref, r * SUB, zsem).start()
                return c

            def wait_token(r, c):
                _token_copy(stage.at[0], 0, xs_ref, 0, zsem).wait()
                return c

            lax.fori_loop(lo, hi, zero_token, 0)
            lax.fori_loop(lo, hi, wait_token, 0)

        def zero_blk(j, c):
            pltpu.make_async_copy(stage.at[0], xs_ref.at[pl.ds(pl.multiple_of(j * (td * SUB), SUB), td * SUB), :],
                                  zsem).start()
            return c

        def wait_blk(j, c):
            pltpu.make_async_copy(stage.at[0], xs_ref.at[pl.ds(0, td * SUB), :], zsem).wait()
            return c

        lax.fori_loop(pad_ref[0, N_EXPERTS], pad_ref[1, N_EXPERTS], zero_blk, 0)
        lax.fori_loop(pad_ref[0, N_EXPERTS], pad_ref[1, N_EXPERTS], wait_blk, 0)


def _dispatch(h2, pos_t, pad, npad):
    t = h2.shape[0]
    td = min(TD, t)
    return pl.pallas_call(
        _dispatch_body,
        grid=(t // td,),
        in_specs=[
            pl.BlockSpec((1, 1, 2 * td), lambda i: (i, 0, 0), memory_space=pltpu.SMEM),
            pl.BlockSpec(memory_space=pltpu.SMEM),
            pl.BlockSpec((td, D_MODEL), lambda i: (i, 0)),
        ],
        out_specs=pl.BlockSpec(memory_space=pl.ANY),
        out_shape=jax.ShapeDtypeStruct((npad * SUB, LANES), F32),
        scratch_shapes=[pltpu.VMEM((2, td * SUB, LANES), F32), pltpu.SemaphoreType.DMA((2,)),
                        pltpu.SemaphoreType.DMA(())],
        compiler_params=_params(("arbitrary",), 32),
        name="dispatch",
    )(pos_t, pad, h2)


def _moe_body(te_ref, src_ref, nv_ref, x_ref, wg_ref, wu_ref, wd_ref, o_ref):
    i = pl.program_id(0)

    @pl.when(nv_ref[i] > 0)
    def _():
        x = _from_token_tiles(x_ref, 0, TM_MOE).astype(BF)
        _to_token_tiles(o_ref, _swiglu(x, wg_ref, wu_ref, wd_ref))

    @pl.when(nv_ref[i] == 0)
    def _():
        o_ref[...] = jnp.zeros_like(o_ref)


def _moe(xs, tile_e, tile_src, tile_nv, wg, wu, wd):
    rows = TM_MOE * SUB
    wspec = lambda shape: pl.BlockSpec((None,) + shape, lambda i, te, src, nv: (te[i], 0, 0))
    return pl.pallas_call(
        _moe_body,
        grid_spec=pltpu.PrefetchScalarGridSpec(
            num_scalar_prefetch=3,
            grid=(xs.shape[0] // rows,),
            in_specs=[
                pl.BlockSpec((rows, LANES), lambda i, te, src, nv: (src[i], 0)),
                wspec((D_MODEL, D_FF)), wspec((D_MODEL, D_FF)), wspec((D_FF, D_MODEL)),
            ],
            out_specs=pl.BlockSpec((rows, LANES), lambda i, te, src, nv: (i, 0)),
        ),
        out_shape=jax.ShapeDtypeStruct(xs.shape, F32),
        compiler_params=_params(("arbitrary",), 56),
        name="moe",
    )(tile_e, tile_src, tile_nv, xs, wg, wu, wd)


def _combine_body(tiles_per_batch, posc_ref, posn_ref, x_ref, route_ref, mod_ref, ys_ref, o_ref, buf, sem):
    i = pl.program_id(0)
    n = pl.num_programs(0)
    tc = x_ref.shape[0]

    def gather(p_ref, s):
        def issue(r, c):
            for k in range(2):
                _token_copy(ys_ref, p_ref[0, 0, k * tc + r], buf.at[s], (k * tc + r) * SUB,
                            sem.at[s]).start(priority=k)
            return c

        lax.fori_loop(0, tc, issue, 0, unroll=8)

    @pl.when(i == 0)
    def _():
        gather(posc_ref, 0)

    @pl.when(i + 1 < n)
    def _():
        gather(posn_ref, (i + 1) % 2)

    slot = i % 2
    pltpu.make_async_copy(ys_ref.at[pl.ds(0, 2 * tc * SUB), :], buf.at[slot], sem.at[slot]).wait()
    lane = lax.broadcasted_iota(jnp.int32, (1, LANES), 1)
    rt = route_ref[...]
    g1 = jnp.sum(jnp.where(lane == ROUTE_GATE, rt, 0.0), axis=-1, keepdims=True)
    g2 = jnp.sum(jnp.where(lane == ROUTE_GATE + 1, rt, 0.0), axis=-1, keepdims=True)
    y = g1 * _from_token_tiles(buf.at[slot], 0, tc) + g2 * _from_token_tiles(buf.at[slot], tc, tc)
    ga2 = mod_ref[5, pl.ds(i // tiles_per_batch, 1), :]
    o_ref[...] = x_ref[...] + ga2 * y


def _combine(ys, pos_t, x, route, mod, seq):
    t = x.shape[0]
    tc = min(TD, t)
    nt = t // tc
    row = lambda i: (i, 0)
    return pl.pallas_call(
        functools.partial(_combine_body, seq // tc),
        grid=(nt,),
        in_specs=[
            pl.BlockSpec((1, 1, 2 * tc), lambda i: (i, 0, 0), memory_space=pltpu.SMEM),
            pl.BlockSpec((1, 1, 2 * tc), lambda i: (jnp.minimum(i + 1, nt - 1), 0, 0), memory_space=pltpu.SMEM),
            pl.BlockSpec((tc, D_MODEL), row),
            pl.BlockSpec((tc, LANES), row),
            pl.BlockSpec(mod.shape, lambda i: (0, 0, 0)),
            pl.BlockSpec(memory_space=pl.ANY),
        ],
        out_specs=pl.BlockSpec((tc, D_MODEL), row),
        out_shape=jax.ShapeDtypeStruct((t, D_MODEL), F32),
        scratch_shapes=[pltpu.VMEM((2, 2 * tc * SUB, LANES), F32), pltpu.SemaphoreType.DMA((2,))],
        compiler_params=_params(("arbitrary",), 32),
        name="combine",
    )(pos_t, pos_t, x, route, mod, ys)


def _moe_routed(h2, xn, route, route_t, cnt, mod, wg, wu, wd, seq):
    t = h2.shape[0]
    td = min(TD, t)
    nt_max = pl.cdiv(2 * t, TM_MOE) + N_EXPERTS
    npad = nt_max * TM_MOE
    counts = cnt[0, :N_EXPERTS].astype(jnp.int32)
    ntile = (counts + TM_MOE - 1) // TM_MOE
    eid = jnp.arange(N_EXPERTS)
    tile_end = jnp.sum(jnp.where(eid[None, :] <= eid[:, None], ntile[None, :], 0), axis=1)
    off = (tile_end - ntile) * TM_MOE
    e12 = route_t[ROUTE_E:ROUTE_E + 2].astype(jnp.int32)
    r12 = route_t[ROUTE_RANK:ROUTE_RANK + 2].astype(jnp.int32)
    onehot = e12[:, :, None] == eid[None, None, :]
    pos = jnp.sum(jnp.where(onehot, off[None, None, :], 0), axis=-1) + r12
    pos_t = (pos * SUB).reshape(2, t // td, td).transpose(1, 0, 2).reshape(t // td, 1, 2 * td)
    total = tile_end[-1]
    tid = jnp.arange(nt_max)
    tile_e = jnp.minimum(jnp.sum(tid[:, None] >= tile_end[None, :], axis=1), N_EXPERTS - 1).astype(jnp.int32)
    tile_nv = (tid < total).astype(jnp.int32)
    tile_src = jnp.minimum(tid, total - 1).astype(jnp.int32)
    pad = jnp.stack([jnp.concatenate([off + counts, (total * (TM_MOE // td))[None]]),
                     jnp.concatenate([off + ntile * TM_MOE, jnp.full((1,), npad // td, jnp.int32)])]).astype(jnp.int32)
    xs = _dispatch(h2, pos_t, pad, npad)
    ys = _moe(xs, tile_e, tile_src, tile_nv, wg, wu, wd)
    return _combine(ys, pos_t, xn, route, mod, seq)


def _rope_tables(pos):
    inv = ROPE_THETA ** (-np.arange(0, HEAD_DIM, 2, dtype=np.float64) / HEAD_DIM)
    ang = np.asarray(pos, np.float64)[:, None] * inv[None, :]
    cos = np.concatenate([np.cos(ang), np.cos(ang)], axis=-1)
    sin = np.concatenate([-np.sin(ang), np.sin(ang)], axis=-1)
    reps = LANES // HEAD_DIM
    return (jnp.asarray(np.tile(cos, (1, reps)), F32), jnp.asarray(np.tile(sin, (1, reps)), F32))


def kernel(x_prompt, x_sample, cache_k, cache_v, c_prompt, c_sample, w_ada, b_ada, norm1_w, norm2_w, w_in,
           q_norm_w, k_norm_w, attn_sinks, gm_norm_w, gm_ws, gm_bs, w_out, dense_w_gate, dense_w_up,
           dense_w_down, router_w, router_b, moe_w_gate, moe_w_up, moe_w_down):
    batch, seq, d = x_prompt.shape
    nd = x_sample.shape[0]
    depth = w_in.shape[0]
    t = batch * seq

    mod = _ada(jnp.concatenate([c_prompt, c_sample], axis=0), w_ada, b_ada)
    mod_p = mod[:, :batch].reshape(depth, batch, N_ADA, d).transpose(0, 2, 1, 3)
    mod_s = mod[:, batch:].reshape(depth, nd, N_ADA, d).transpose(0, 2, 1, 3)

    cos_p, sin_p = _rope_tables(np.arange(seq))
    cos_s, sin_s = _rope_tables(np.array([PAST_LEN]))
    head_of = np.arange(ATTN_WIDTH) // HEAD_DIM
    seg = jnp.asarray(head_of[:, None] == head_of[None, :], BF)
    pairs = nd * N_KV_HEADS
    w = cache_k.shape[2]
    ck_t = cache_k.transpose(0, 1, 3, 4, 2).reshape(depth, pairs, HEAD_DIM, w)
    cv_t = cache_v.transpose(0, 1, 3, 4, 2).reshape(depth, pairs, HEAD_DIM, w)
    uncache = lambda c: jnp.stack(c).reshape(depth, nd, N_KV_HEADS, HEAD_DIM, w).transpose(0, 1, 4, 2, 3)

    w_in_bf = w_in.astype(BF)
    w_out_bf = w_out.astype(BF)
    dense_bf = [w.astype(BF) for w in (dense_w_gate, dense_w_up, dense_w_down)]
    moe_bf = [w.astype(BF) for w in (moe_w_gate, moe_w_up, moe_w_down)]
    router_w_pad = jnp.pad(router_w, ((0, 0), (0, 0), (0, LANES - N_EXPERTS)))
    router_b_pad = jnp.pad(router_b, ((0, 0), (0, LANES - N_EXPERTS)), constant_values=NEG_INF)

    xp = x_prompt.reshape(t, d)
    xs = x_sample.reshape(nd, d)
    k_p, v_p, g_p, k_s, v_s, g_s = [], [], [], [], [], []
    for l in range(depth):
        i = l // 2
        n1 = norm1_w[l][None, :]
        n2 = norm2_w[l][None, :]
        qn = jnp.tile(q_norm_w[l], N_HEADS)[None, :]
        kn = jnp.tile(k_norm_w[l], N_KV_HEADS)[None, :]
        gmn = gm_norm_w[l][None, :]
        router = None if l % 2 == 0 else (router_w_pad[i], router_b_pad[i][None, :])

        q, kd, vd, u, gv, kl, vl, gvl = _inproj_p(l, xp, mod_p[l], n1, w_in_bf, qn, kn, gmn, seg,
                                                  cos_p, sin_p, batch, seq)
        mix = _mix_p(q, kd, vd, u, gv, gm_ws[l], gm_bs[l].T, attn_sinks[l], batch, seq)
        if router is None:
            xp = _outffn_p(l, mix, xp, mod_p[l], w_out_bf, n2, dense_bf[0][i], dense_bf[1][i], dense_bf[2][i],
                           batch, seq)
        else:
            xn, h2, route, route_t, cnt = _outproj_p(l, mix, xp, mod_p[l], w_out_bf, n2, *router, batch, seq)
            xp = _moe_routed(h2, xn, route, route_t, cnt, mod_p[l], moe_bf[0][i], moe_bf[1][i], moe_bf[2][i], seq)
        k_p.append(kl.reshape(batch, WINDOW, N_KV_HEADS, HEAD_DIM))
        v_p.append(vl.reshape(batch, WINDOW, N_KV_HEADS, HEAD_DIM))
        g_p.append(gvl)

        q, k, v, u, gv = _inproj_s(xs, mod_s[l], n1, w_in[l], qn, kn, gmn, seg, cos_s, sin_s)
        qg = jnp.pad(q.reshape(pairs, KV_GROUP, HEAD_DIM), ((0, 0), (0, SUB - KV_GROUP), (0, 0)))
        sink = jnp.pad(jnp.tile(attn_sinks[l].reshape(N_KV_HEADS, KV_GROUP), (nd, 1)),
                       ((0, 0), (0, SUB - KV_GROUP)))[:, :, None]
        o, nk, nv = _attn_s(l, qg, k.reshape(pairs, HEAD_DIM, 1), v.reshape(pairs, HEAD_DIM, 1), ck_t, cv_t, sink)
        o = o[:, :KV_GROUP, :].reshape(nd, ATTN_WIDTH)
        wdiag = jnp.repeat(gm_ws[l][:, 0, 0], GM_WIDTH // GM_GROUPS)[None, :]
        bsrow = jnp.repeat(gm_bs[l][:, 0], GM_WIDTH // GM_GROUPS)[None, :]
        res = _outproj_s(o, u, gv, wdiag, bsrow, xs, mod_s[l], w_out[l], n2, router)
        if router is None:
            xs = _ffn_s(res[1], res[0], mod_s[l], dense_w_gate[i], dense_w_up[i], dense_w_down[i])
        else:
            xs = _moe_s(res[1], res[0], res[2], mod_s[l], moe_bf[0][i], moe_bf[1][i], moe_bf[2][i])
        k_s.append(nk)
        v_s.append(nv)
        g_s.append(gv[:, None, :])

    return (xp.reshape(batch, seq, d), xs.reshape(nd, 1, d), jnp.stack(k_p), jnp.stack(v_p), jnp.stack(g_p),
            uncache(k_s), uncache(v_s), jnp.stack(g_s))
```

```python
import functools

import numpy as np
import jax
import jax.numpy as jnp
from jax import lax
from jax.experimental import pallas as pl
from jax.experimental.pallas import tpu as pltpu

D_MODEL = 1024
HEAD_DIM = 64
N_HEADS = 8
N_KV_HEADS = 2
KV_GROUP = N_HEADS // N_KV_HEADS
ATTN_WIDTH = N_HEADS * HEAD_DIM
KV_WIDTH = N_KV_HEADS * HEAD_DIM
GM_WIDTH = 512
GM_GROUPS = 4
WINDOW = 128
CHUNK = 128
D_FF = 2816
MXU_DIM = 256
FF_SPLIT = (0, 6 * MXU_DIM, D_FF)
N_EXPERTS = 8
N_ADA = 6
IN_COLS = ATTN_WIDTH + 2 * KV_WIDTH + 2 * GM_WIDTH
PAST_LEN = 16384
ROPE_THETA = 10000.0
EPS = 1e-6
NEG_INF = -1e30
LANES = 128
SUB = 8
assert D_MODEL == SUB * LANES

BF = jnp.bfloat16
F32 = jnp.float32
MIB = 1024 * 1024


ROW_GROUP = 256


def _params(sem, vmem_mib):
    return pltpu.CompilerParams(dimension_semantics=sem, vmem_limit_bytes=vmem_mib * MIB)


def _dot(a, b):
    return jnp.dot(a, b, preferred_element_type=F32)


def _dot_nt(a, b):
    return lax.dot_general(a, b, (((1,), (1,)), ((), ())), preferred_element_type=F32)


def _split(a):
    hi = a.astype(BF)
    return hi, (a - hi.astype(F32)).astype(BF)


def _dot_bf(a, w):
    return _dot(a.astype(BF), w)


def _dot3(a, w):
    ah, al = _split(a)
    if w.dtype == BF:
        return _dot(ah, w) + _dot(al, w)
    wh, wl = _split(w)
    return _dot(ah, wh) + _dot(al, wh) + _dot(ah, wl)


def _rms(x, w):
    ms = jnp.mean(x * x, axis=-1, keepdims=True)
    return x * lax.rsqrt(ms + EPS) * w


def _ada_body(c_ref, w_ref, b_ref, o_ref):
    o_ref[...] = _dot3(jax.nn.silu(c_ref[...]), w_ref[...]) + b_ref[...]


def _ada(c_all, w_ada, b_ada):
    depth, d, cols = w_ada.shape
    n = c_all.shape[0]
    tn = 1024
    return pl.pallas_call(
        _ada_body,
        grid=(depth, cols // tn),
        in_specs=[
            pl.BlockSpec((n, d), lambda l, j: (0, 0)),
            pl.BlockSpec((None, d, tn), lambda l, j: (l, 0, j)),
            pl.BlockSpec((None, 1, tn), lambda l, j: (l, 0, j)),
        ],
        out_specs=pl.BlockSpec((None, n, tn), lambda l, j: (l, 0, j)),
        out_shape=jax.ShapeDtypeStruct((depth, n, cols), F32),
        compiler_params=_params(("arbitrary", "arbitrary"), 32),
        name="ada",
    )(c_all, w_ada, b_ada.reshape(depth, 1, cols))


def _swap_halves(t):
    n = t.shape[-1]
    lane = lax.broadcasted_iota(jnp.int32, (1, n), 1)
    first = (lane % HEAD_DIM) < (HEAD_DIM // 2)
    return jnp.where(first, pltpu.roll(t, n - HEAD_DIM // 2, axis=1), pltpu.roll(t, HEAD_DIM // 2, axis=1))


def _inproj_compute(mm, x, sh, sc, n1, w_ref, qn, kn, gmn, seg_ref, cos, sin):
    h = _rms(x, n1) * (1.0 + sc) + sh
    z = mm(h, w_ref[...])
    q = z[:, :ATTN_WIDTH]
    k = z[:, ATTN_WIDTH:ATTN_WIDTH + KV_WIDTH]
    v = z[:, ATTN_WIDTH + KV_WIDTH:ATTN_WIDTH + 2 * KV_WIDTH]
    gm = z[:, ATTN_WIDTH + 2 * KV_WIDTH:]

    def head_norm(t, seg, wn):
        ms = mm(t * t, seg) * (1.0 / HEAD_DIM)
        return t * lax.rsqrt(ms + EPS) * wn

    def rope(t):
        reps = t.shape[-1] // LANES
        c = jnp.concatenate([cos] * reps, axis=-1) if reps > 1 else cos
        s = jnp.concatenate([sin] * reps, axis=-1) if reps > 1 else sin
        return t * c + _swap_halves(t) * s

    q = rope(head_norm(q, seg_ref[...], qn)) * (HEAD_DIM ** -0.5)
    k = rope(head_norm(k, seg_ref[:KV_WIDTH, :KV_WIDTH], kn))
    g = jax.nn.gelu(gm)
    u = g[:, :GM_WIDTH]
    gv = _rms(g[:, GM_WIDTH:], gmn)
    return q, k, v, u, gv


def _dup_heads(t):
    lane = lax.broadcasted_iota(jnp.int32, (1, LANES), 1)
    lo = lane < HEAD_DIM
    r = pltpu.roll(t, HEAD_DIM, axis=1)
    return jnp.concatenate([jnp.where(lo, t, r), jnp.where(lo, r, t)], axis=-1)


def _inproj_p_body(tiles_per_batch, ncast, x_ref, mod_ref, n1_ref, w_ref, qn_ref, kn_ref, gmn_ref, seg_ref,
                   cos_ref, sin_ref, *rest):
    cast_src = rest[:ncast]
    q_ref, kd_ref, vd_ref, u_ref, gv_ref, kl_ref, vl_ref, gvl_ref = rest[ncast:ncast + 8]
    cast_dst = rest[ncast + 8:]
    for src, dst in zip(cast_src, cast_dst):
        dst[...] = src[...].astype(BF)
    i = pl.program_id(0)
    b = i // tiles_per_batch
    sh = mod_ref[0, pl.ds(b, 1), :]
    sc = mod_ref[1, pl.ds(b, 1), :]
    hs = min(ROW_GROUP, x_ref.shape[0])
    for hh in range(x_ref.shape[0] // hs):
        rs = slice(hh * hs, (hh + 1) * hs)
        q, k, v, u, gv = _inproj_compute(_dot_bf, x_ref[rs, :], sh, sc, n1_ref[...], w_ref, qn_ref[...], kn_ref[...],
                                         gmn_ref[...], seg_ref, cos_ref[rs, :], sin_ref[rs, :])
        q_ref[rs, :] = q.astype(BF)
        kd_ref[rs, :] = _dup_heads(k).astype(BF)
        vd_ref[rs, :] = _dup_heads(v).astype(BF)
        u_ref[rs, :] = u.astype(BF)
        gv_ref[rs, :] = gv.astype(BF)

    @pl.when(i % tiles_per_batch == tiles_per_batch - 1)
    def _():
        kl_ref[...] = k[hs - WINDOW:, :]
        vl_ref[...] = v[hs - WINDOW:, :]
        gvl_ref[...] = gv[hs - CHUNK:, :]


BF16_SUBLANES = 16


def _slab_spec(rows, cols, steps):
    s = steps
    while rows % s or (rows // s) % BF16_SUBLANES:
        s //= 2
    return pl.BlockSpec((rows // s, cols), lambda i: (i // (steps // s), 0))


def _inproj_p(l, x, mod, n1, w_bf, qn, kn, gmn, seg, cos, sin, casts, batch, seq):
    t = x.shape[0]
    tm = min(1024, seq)
    tpb = seq // tm
    steps = t // tm
    row = lambda i: (i, 0)
    full = lambda i: (0, 0)
    last = lambda i: (i // tpb, 0, 0)
    cast_specs = [_slab_spec(c.shape[0], c.shape[1], steps) for c in casts]
    return pl.pallas_call(
        functools.partial(_inproj_p_body, tpb, len(casts)),
        grid=(steps,),
        in_specs=[
            pl.BlockSpec((tm, D_MODEL), row),
            pl.BlockSpec((N_ADA, batch, D_MODEL), lambda i: (0, 0, 0)),
            pl.BlockSpec((1, D_MODEL), full),
            pl.BlockSpec((None, D_MODEL, IN_COLS), lambda i: (l, 0, 0)),
            pl.BlockSpec((1, ATTN_WIDTH), full),
            pl.BlockSpec((1, KV_WIDTH), full),
            pl.BlockSpec((1, GM_WIDTH), full),
            pl.BlockSpec((ATTN_WIDTH, ATTN_WIDTH), full),
            pl.BlockSpec((tm, LANES), lambda i: (i % tpb, 0)),
            pl.BlockSpec((tm, LANES), lambda i: (i % tpb, 0)),
        ] + cast_specs,
        out_specs=[
            pl.BlockSpec((tm, ATTN_WIDTH), row),
            pl.BlockSpec((tm, 2 * KV_WIDTH), row),
            pl.BlockSpec((tm, 2 * KV_WIDTH), row),
            pl.BlockSpec((tm, GM_WIDTH), row),
            pl.BlockSpec((tm, GM_WIDTH), row),
            pl.BlockSpec((None, WINDOW, KV_WIDTH), last),
            pl.BlockSpec((None, WINDOW, KV_WIDTH), last),
            pl.BlockSpec((None, CHUNK, GM_WIDTH), last),
        ] + cast_specs,
        out_shape=[
            jax.ShapeDtypeStruct((t, ATTN_WIDTH), BF),
            jax.ShapeDtypeStruct((t, 2 * KV_WIDTH), BF),
            jax.ShapeDtypeStruct((t, 2 * KV_WIDTH), BF),
            jax.ShapeDtypeStruct((t, GM_WIDTH), BF),
            jax.ShapeDtypeStruct((t, GM_WIDTH), BF),
            jax.ShapeDtypeStruct((batch, WINDOW, KV_WIDTH), F32),
            jax.ShapeDtypeStruct((batch, WINDOW, KV_WIDTH), F32),
            jax.ShapeDtypeStruct((batch, CHUNK, GM_WIDTH), F32),
        ] + [jax.ShapeDtypeStruct(c.shape, BF) for c in casts],
        compiler_params=_params(("arbitrary",), 56),
        name="inproj_p",
    )(x, mod, n1, w_bf, qn, kn, gmn, seg, cos, sin, *casts)


def _inproj_s_body(x_ref, mod_ref, n1_ref, w_ref, qn_ref, kn_ref, gmn_ref, seg_ref, cos_ref, sin_ref,
                   q_ref, k_ref, v_ref, u_ref, gv_ref):
    q, k, v, u, gv = _inproj_compute(_dot3, x_ref[...], mod_ref[0], mod_ref[1], n1_ref[...], w_ref, qn_ref[...],
                                     kn_ref[...], gmn_ref[...], seg_ref, cos_ref[...], sin_ref[...])
    q_ref[...] = q
    k_ref[...] = k
    v_ref[...] = v
    u_ref[...] = u
    gv_ref[...] = gv


def _inproj_s(x, mod, n1, w_bf, qn, kn, gmn, seg, cos, sin):
    n = x.shape[0]
    widths = (ATTN_WIDTH, KV_WIDTH, KV_WIDTH, GM_WIDTH, GM_WIDTH)
    return pl.pallas_call(
        _inproj_s_body,
        out_shape=[jax.ShapeDtypeStruct((n, w), F32) for w in widths],
        compiler_params=pltpu.CompilerParams(vmem_limit_bytes=48 * MIB),
        name="inproj_s",
    )(x, mod, n1, w_bf, qn, kn, gmn, seg, cos, sin)


def _mix_p_body(nblk, q_ref, kc_ref, kp_ref, vc_ref, vp_ref, u_ref, gv_ref, ws_ref, bst_ref, sink_ref, o_ref):
    i = pl.program_id(1)
    blk = WINDOW
    lane = lax.broadcasted_iota(jnp.int32, (1, LANES), 1)
    lo = lane < HEAD_DIM
    cols = KV_GROUP * blk
    iq = lax.broadcasted_iota(jnp.int32, (2 * blk, cols), 1) % blk
    jk = lax.broadcasted_iota(jnp.int32, (2 * blk, cols), 0)
    band = (jk > iq) & (jk <= iq + blk)
    bias = jnp.where(band, 0.0, NEG_INF)
    bias_first = jnp.where(band & ((jk >= blk) | (i > 0)), 0.0, NEG_INF)
    tri = (lax.broadcasted_iota(jnp.int32, (CHUNK, CHUNK), 0)
           >= lax.broadcasted_iota(jnp.int32, (CHUNK, CHUNK), 1))
    wm = [jnp.where(tri, ws_ref[g], 0.0).astype(BF) for g in range(GM_GROUPS)]

    for n in range(nblk):
        r0 = n * blk
        if n == 0:
            kk = jnp.concatenate([kp_ref[...], kc_ref[0:blk, :]], axis=0)
            vv = jnp.concatenate([vp_ref[...], vc_ref[0:blk, :]], axis=0)
            mask_bias = bias_first
        else:
            kk = kc_ref[r0 - blk:r0 + blk, :]
            vv = vc_ref[r0 - blk:r0 + blk, :]
            mask_bias = bias
        for kvh in range(N_KV_HEADS):
            c0 = 2 * kvh
            qa = q_ref[r0:r0 + blk, c0 * LANES:(c0 + 1) * LANES]
            qb = q_ref[r0:r0 + blk, (c0 + 1) * LANES:(c0 + 2) * LANES]
            zero = jnp.zeros_like(qa)
            qq = jnp.concatenate([jnp.where(lo, qa, zero), jnp.where(lo, zero, qa),
                                  jnp.where(lo, qb, zero), jnp.where(lo, zero, qb)], axis=0)
            s = _dot_nt(kk[:, kvh * LANES:(kvh + 1) * LANES], qq) + mask_bias
            sink = jnp.concatenate(
                [jnp.full((1, blk), sink_ref[kvh * KV_GROUP + g], F32) for g in range(KV_GROUP)], axis=1)
            m = jnp.maximum(jnp.max(s, axis=0, keepdims=True), sink)
            p = jnp.exp(s - m)
            den = jnp.sum(p, axis=0, keepdims=True) + jnp.exp(sink - m)
            p = (p * (1.0 / den)).astype(BF)
            o = lax.dot_general(p, vv[:, kvh * LANES:(kvh + 1) * LANES], (((0,), (0,)), ((), ())),
                                preferred_element_type=F32)
            o_ref[r0:r0 + blk, c0 * LANES:(c0 + 1) * LANES] = jnp.where(
                lo, o[0:blk], o[blk:2 * blk]).astype(BF)
            o_ref[r0:r0 + blk, (c0 + 1) * LANES:(c0 + 2) * LANES] = jnp.where(
                lo, o[2 * blk:3 * blk], o[3 * blk:4 * blk]).astype(BF)
        for g in range(GM_GROUPS):
            cs = slice(g * LANES, (g + 1) * LANES)
            sp = _dot(wm[g], gv_ref[r0:r0 + blk, cs]) + bst_ref[:, g:g + 1]
            o_ref[r0:r0 + blk, ATTN_WIDTH + g * LANES:ATTN_WIDTH + (g + 1) * LANES] = (
                u_ref[r0:r0 + blk, cs].astype(F32) * sp).astype(BF)


def _mix_p(q, kd, vd, u, gv, ws, bst, sinks, batch, seq):
    t = q.shape[0]
    tq = min(1024, seq)
    nblk = tq // WINDOW
    tpb = seq // tq
    cur = lambda b, i: (b * tpb + i, 0)
    prev = lambda b, i: (jnp.maximum((b * tpb + i) * nblk - 1, b * tpb * nblk), 0)
    return pl.pallas_call(
        functools.partial(_mix_p_body, nblk),
        grid=(batch, tpb),
        in_specs=[
            pl.BlockSpec((tq, ATTN_WIDTH), cur),
            pl.BlockSpec((tq, 2 * KV_WIDTH), cur),
            pl.BlockSpec((WINDOW, 2 * KV_WIDTH), prev),
            pl.BlockSpec((tq, 2 * KV_WIDTH), cur),
            pl.BlockSpec((WINDOW, 2 * KV_WIDTH), prev),
            pl.BlockSpec((tq, GM_WIDTH), cur),
            pl.BlockSpec((tq, GM_WIDTH), cur),
            pl.BlockSpec((GM_GROUPS, CHUNK, CHUNK), lambda b, i: (0, 0, 0)),
            pl.BlockSpec((CHUNK, GM_GROUPS), lambda b, i: (0, 0)),
            pl.BlockSpec(memory_space=pltpu.SMEM),
        ],
        out_specs=pl.BlockSpec((tq, D_MODEL), cur),
        out_shape=jax.ShapeDtypeStruct((t, D_MODEL), BF),
        compiler_params=_params(("arbitrary", "arbitrary"), 48),
        name="mix_p",
    )(q, kd, kd, vd, vd, u, gv, ws, bst, sinks)


def _attn_s_body(q_ref, kn_ref, vn_ref, ck_ref, cv_ref, sink_ref, o_ref, nk_ref, nv_ref):
    w = ck_ref.shape[-1]
    pos = lax.broadcasted_iota(jnp.int32, (1, 1, w), 2)
    nk = jnp.where(pos == w - 1, kn_ref[...], pltpu.roll(ck_ref[...], w - 1, axis=2))
    nv = jnp.where(pos == w - 1, vn_ref[...], pltpu.roll(cv_ref[...], w - 1, axis=2))
    nk_ref[...] = nk
    nv_ref[...] = nv

    def bmm3(spec, a, b):
        (ah, al), (bh, bl) = _split(a), _split(b)
        mm = lambda x, y: jnp.einsum(spec, x, y, preferred_element_type=F32)
        return mm(ah, bh) + mm(al, bh) + mm(ah, bl)

    s = bmm3('ngd,ndj->ngj', q_ref[...], nk)
    sink = sink_ref[...]
    m = jnp.maximum(jnp.max(s, axis=-1, keepdims=True), sink)
    p = jnp.exp(s - m)
    den = jnp.sum(p, axis=-1, keepdims=True) + jnp.exp(sink - m)
    o_ref[...] = bmm3('ngj,ndj->ngd', p, nv) * (1.0 / den)


def _attn_s(l, q, k_new, v_new, ck, cv, sink):
    _, n, hd, w = ck.shape
    nb = min(32, n)
    rows = q.shape[1]
    blk = lambda r, c: pl.BlockSpec((nb, r, c), lambda i: (i, 0, 0))
    cache = pl.BlockSpec((None, nb, hd, w), lambda i: (l, i, 0, 0))
    return pl.pallas_call(
        _attn_s_body,
        grid=(n // nb,),
        in_specs=[blk(rows, hd), blk(hd, 1), blk(hd, 1), cache, cache, blk(rows, 1)],
        out_specs=[blk(rows, hd), blk(hd, w), blk(hd, w)],
        out_shape=[jax.ShapeDtypeStruct((n, rows, hd), F32),
                   jax.ShapeDtypeStruct((n, hd, w), F32),
                   jax.ShapeDtypeStruct((n, hd, w), F32)],
        compiler_params=_params(("arbitrary",), 32),
        name="attn_s",
    )(q, k_new, v_new, ck, cv, sink)


def _top2(h2, rw_ref, rb_ref):
    logits = _dot3(h2, rw_ref[...]) + rb_ref[...]
    lane = lax.broadcasted_iota(jnp.int32, logits.shape, 1).astype(F32)
    e = jnp.exp(logits - jnp.max(logits, axis=-1, keepdims=True))
    p = e / jnp.sum(e, axis=-1, keepdims=True)
    m1 = jnp.max(p, axis=-1, keepdims=True)
    i1 = jnp.min(jnp.where(p == m1, lane, float(LANES)), axis=-1, keepdims=True)
    p2 = jnp.where(lane == i1, -1.0, p)
    m2 = jnp.max(p2, axis=-1, keepdims=True)
    i2 = jnp.min(jnp.where(p2 == m2, lane, float(LANES)), axis=-1, keepdims=True)
    tot = m1 + m2
    return lane, i1, i2, m1 / tot, m2 / tot


def _route_gates(h2, rw_ref, rb_ref):
    lane, i1, i2, g1, g2 = _top2(h2, rw_ref, rb_ref)
    return jnp.where(lane == i1, g1, 0.0) + jnp.where(lane == i2, g2, 0.0)


ROUTE_E, ROUTE_RANK, ROUTE_GATE = 0, 2, 4


def _route_ranked(h2, rw_ref, rb_ref, tri_ref, cnt_ref):
    lane, i1, i2, g1, g2 = _top2(h2, rw_ref, rb_ref)
    oh1 = lane == i1
    oh2 = lane == i2
    hit = jnp.where(oh1, 1.0, 0.0) + jnp.where(oh2, 1.0, 0.0)
    before = cnt_ref[...] + _dot(tri_ref[...], hit.astype(BF))
    r1 = jnp.sum(jnp.where(oh1, before, 0.0), axis=-1, keepdims=True)
    r2 = jnp.sum(jnp.where(oh2, before, 0.0), axis=-1, keepdims=True)
    cnt_ref[...] += jnp.sum(hit, axis=0, keepdims=True)
    cols = (i1, i2, r1, r2, g1, g2)
    out = jnp.zeros_like(lane)
    for j, c in enumerate(cols):
        out = jnp.where(lane == float(j), c, out)
    return out


def _outproj_compute(mm, mix, x, ga1, sh2, sc2, w_ref, n2):
    xn = x + ga1 * mm(mix, w_ref[...])
    h2 = _rms(xn, n2) * (1.0 + sc2) + sh2
    return xn, h2


def _outproj_p_body(tiles_per_batch, mix_ref, x_ref, mod_ref, w_ref, n2_ref, rw_ref, rb_ref, tri_ref,
                    xn_ref, h2_ref, route_ref, route_t_ref, cnt_ref):
    i = pl.program_id(0)
    b = i // tiles_per_batch
    mrow = lambda j: mod_ref[j, pl.ds(b, 1), :]

    @pl.when(i == 0)
    def _():
        cnt_ref[...] = jnp.zeros_like(cnt_ref)

    xn, h2 = _outproj_compute(_dot_bf, mix_ref[...], x_ref[...], mrow(2), mrow(3), mrow(4), w_ref, n2_ref[...])
    route = _route_ranked(h2, rw_ref, rb_ref, tri_ref, cnt_ref)
    route_ref[...] = route
    route_t_ref[...] = route.T[:SUB, :]
    h2_ref[...] = h2
    xn_ref[...] = xn


def _outproj_p(l, mix, x, mod, w_bf, n2, rw, rb, batch, seq):
    t = x.shape[0]
    tm = min(512, seq)
    row = lambda i: (i, 0)
    full = lambda i: (0, 0)
    return pl.pallas_call(
        functools.partial(_outproj_p_body, seq // tm),
        grid=(t // tm,),
        in_specs=[
            pl.BlockSpec((tm, D_MODEL), row),
            pl.BlockSpec((tm, D_MODEL), row),
            pl.BlockSpec((N_ADA, batch, D_MODEL), lambda i: (0, 0, 0)),
            pl.BlockSpec((None, D_MODEL, D_MODEL), lambda i: (l, 0, 0)),
            pl.BlockSpec((1, D_MODEL), full),
            pl.BlockSpec((D_MODEL, LANES), full),
            pl.BlockSpec((1, LANES), full),
            pl.BlockSpec((tm, tm), full),
        ],
        out_specs=[pl.BlockSpec((tm, D_MODEL), row), pl.BlockSpec((tm, D_MODEL), row),
                   pl.BlockSpec((tm, LANES), row), pl.BlockSpec((SUB, tm), lambda i: (0, i)),
                   pl.BlockSpec((1, LANES), full)],
        out_shape=[jax.ShapeDtypeStruct((t, D_MODEL), F32), jax.ShapeDtypeStruct((t, D_MODEL), F32),
                   jax.ShapeDtypeStruct((t, LANES), F32), jax.ShapeDtypeStruct((SUB, t), F32),
                   jax.ShapeDtypeStruct((1, LANES), F32)],
        compiler_params=_params(("arbitrary",), 48),
        name="outproj_p",
    )(mix, x, mod, w_bf, n2, rw, rb, jnp.asarray(np.tri(tm, k=-1), BF))


def _outproj_s_body(with_router, o_ref, u_ref, gv_ref, wdiag_ref, bsrow_ref, x_ref, mod_ref, w_ref, n2_ref, *rest):
    gate = u_ref[...] * (wdiag_ref[...] * gv_ref[...] + bsrow_ref[...])
    mix = jnp.concatenate([o_ref[...], gate], axis=-1)
    xn, h2 = _outproj_compute(_dot3, mix, x_ref[...], mod_ref[2], mod_ref[3], mod_ref[4], w_ref, n2_ref[...])
    if with_router:
        rw_ref, rb_ref, xn_ref, h2_ref, gates_ref = rest
        gates_ref[...] = _route_gates(h2, rw_ref, rb_ref)
    else:
        xn_ref, h2_ref = rest
    xn_ref[...] = xn
    h2_ref[...] = h2


def _outproj_s(o, u, gv, wdiag, bsrow, x, mod, w, n2, router):
    n = x.shape[0]
    out_shape = [jax.ShapeDtypeStruct((n, D_MODEL), F32), jax.ShapeDtypeStruct((n, D_MODEL), F32)]
    args = [o, u, gv, wdiag, bsrow, x, mod, w, n2]
    if router is not None:
        out_shape.append(jax.ShapeDtypeStruct((n, LANES), F32))
        args += list(router)
    return pl.pallas_call(
        functools.partial(_outproj_s_body, router is not None),
        out_shape=out_shape,
        compiler_params=pltpu.CompilerParams(vmem_limit_bytes=32 * MIB),
        name="outproj_s",
    )(*args)


def _swiglu(h_bf, wg_ref, wu_ref, wd_ref):
    y = None
    for c in range(len(FF_SPLIT) - 1):
        sl = slice(FF_SPLIT[c], FF_SPLIT[c + 1])
        a = (jax.nn.silu(_dot(h_bf, wg_ref[:, sl])) * _dot(h_bf, wu_ref[:, sl])).astype(BF)
        part = _dot(a, wd_ref[sl, :])
        y = part if y is None else y + part
    return y


def _ffn_s_body(h_ref, x_ref, mod_ref, wg_ref, wu_ref, wd_ref, o_ref):
    h = h_ref[...]
    y = jnp.zeros_like(h)
    for c in range(D_FF // MXU_DIM):
        sl = slice(c * MXU_DIM, (c + 1) * MXU_DIM)
        a = jax.nn.silu(_dot3(h, wg_ref[:, sl])) * _dot3(h, wu_ref[:, sl])
        y = y + _dot3(a, wd_ref[sl, :])
    o_ref[...] = x_ref[...] + mod_ref[5] * y


def _ffn_s(h2, x, mod, wg, wu, wd):
    return pl.pallas_call(
        _ffn_s_body,
        out_shape=jax.ShapeDtypeStruct(x.shape, F32),
        compiler_params=pltpu.CompilerParams(vmem_limit_bytes=56 * MIB),
        name="ffn_s",
    )(h2, x, mod, wg, wu, wd)


def _outffn_p_body(tiles_per_batch, mix_ref, x_ref, mod_ref, w_ref, n2_ref, wg_ref, wu_ref, wd_ref, o_ref):
    b = pl.program_id(0) // tiles_per_batch
    mrow = lambda j: mod_ref[j, pl.ds(b, 1), :]
    xn, h2 = _outproj_compute(_dot_bf, mix_ref[...], x_ref[...], mrow(2), mrow(3), mrow(4), w_ref, n2_ref[...])
    o_ref[...] = xn + mrow(5) * _swiglu(h2.astype(BF), wg_ref, wu_ref, wd_ref)


def _outffn_p(l, mix, x, mod, w_bf, n2, wg, wu, wd, batch, seq):
    t = x.shape[0]
    tm = min(512, seq)
    row = lambda i: (i, 0)
    const = lambda shape: pl.BlockSpec(shape, lambda i: (0,) * len(shape), pipeline_mode=pl.Buffered(1))
    return pl.pallas_call(
        functools.partial(_outffn_p_body, seq // tm),
        grid=(t // tm,),
        in_specs=[
            pl.BlockSpec((tm, D_MODEL), row),
            pl.BlockSpec((tm, D_MODEL), row),
            const((N_ADA, batch, D_MODEL)),
            pl.BlockSpec((None, D_MODEL, D_MODEL), lambda i: (l, 0, 0), pipeline_mode=pl.Buffered(1)),
            const((1, D_MODEL)),
            const((D_MODEL, D_FF)), const((D_MODEL, D_FF)), const((D_FF, D_MODEL)),
        ],
        out_specs=pl.BlockSpec((tm, D_MODEL), row),
        out_shape=jax.ShapeDtypeStruct((t, D_MODEL), F32),
        compiler_params=_params(("arbitrary",), 56),
        name="outffn_p",
    )(mix, x, mod, w_bf, n2, wg, wu, wd)


def _moe_s_body(h_ref, x_ref, gates_ref, mod_ref, wg_ref, wu_ref, wd_ref, o_ref):
    e = pl.program_id(0)

    @pl.when(e == 0)
    def _():
        o_ref[...] = jnp.zeros_like(o_ref)

    lane = lax.broadcasted_iota(jnp.int32, (1, LANES), 1)
    gate = jnp.sum(jnp.where(lane == e, gates_ref[...], 0.0), axis=-1, keepdims=True)
    o_ref[...] += gate * _swiglu(h_ref[...].astype(BF), wg_ref, wu_ref, wd_ref)

    @pl.when(e == N_EXPERTS - 1)
    def _():
        o_ref[...] = x_ref[...] + mod_ref[5] * o_ref[...]


def _moe_s(h2, x, gates, mod, wg, wu, wd):
    n = x.shape[0]
    whole = lambda shape: pl.BlockSpec(shape, lambda e: (0,) * len(shape))
    wspec = lambda shape: pl.BlockSpec((None,) + shape, lambda e: (e, 0, 0))
    return pl.pallas_call(
        _moe_s_body,
        grid=(N_EXPERTS,),
        in_specs=[whole((n, D_MODEL)), whole((n, D_MODEL)), whole((n, LANES)), whole(mod.shape),
                  wspec((D_MODEL, D_FF)), wspec((D_MODEL, D_FF)), wspec((D_FF, D_MODEL))],
        out_specs=whole((n, D_MODEL)),
        out_shape=jax.ShapeDtypeStruct((n, D_MODEL), F32),
        compiler_params=_params(("arbitrary",), 56),
        name="moe_s",
    )(h2, x, gates, mod, wg, wu, wd)


TM_MOE = 512
TD = 512


def _to_token_tiles(ref, x):
    r = x.shape[0]
    for g in range(SUB):
        ref[pl.ds(g, r, stride=SUB), :] = x[:, g * LANES:(g + 1) * LANES]


def _from_token_tiles(ref, first, r):
    return jnp.concatenate([ref[pl.ds(first * SUB + g, r, stride=SUB), :] for g in range(SUB)], axis=-1)


def _token_copy(src, s, dst, d, sem):
    aligned = lambda v: v if isinstance(v, int) else pl.multiple_of(v, SUB)
    return pltpu.make_async_copy(src.at[pl.ds(aligned(s), SUB), :], dst.at[pl.ds(aligned(d), SUB), :], sem)


def _dispatch_body(pos_ref, pad_ref, h_ref, xs_ref, stage, sem, zsem):
    i = pl.program_id(0)
    n = pl.num_programs(0)
    td = h_ref.shape[0]
    slot = i % 2

    def wait_slot(s):
        for _ in range(2):
            pltpu.make_async_copy(stage.at[s], xs_ref.at[pl.ds(0, td * SUB), :], sem.at[s]).wait()

    @pl.when(i >= 2)
    def _():
        wait_slot(slot)

    _to_token_tiles(stage.at[slot], h_ref[...])

    def issue(r, c):
        for k in range(2):
            _token_copy(stage.at[slot], r * SUB, xs_ref, pos_ref[0, 0, k * td + r], sem.at[slot]).start(priority=k)
        return c

    lax.fori_loop(0, td, issue, 0, unroll=8)

    @pl.when(i == n - 1)
    def _():
        wait_slot(slot)

        @pl.when(n > 1)
        def _():
            wait_slot(1 - slot)

        stage[0] = jnp.zeros(stage.shape[1:], stage.dtype)
        for e in range(N_EXPERTS):
            lo = pad_ref[0, e]
            hi = pad_ref[1, e]

            def zero_token(r, c):
                _token_copy(stage.at[0], 0, xs_ref, r * SUB, zsem).start()
                return c

            def wait_token(r, c):
                _token_copy(stage.at[0], 0, xs_ref, 0, zsem).wait()
                return c

            lax.fori_loop(lo, hi, zero_token, 0)
            lax.fori_loop(lo, hi, wait_token, 0)

        def zero_blk(j, c):
            pltpu.make_async_copy(stage.at[0], xs_ref.at[pl.ds(pl.multiple_of(j * (td * SUB), SUB), td * SUB), :],
                                  zsem).start()
            return c

        def wait_blk(j, c):
            pltpu.make_async_copy(stage.at[0], xs_ref.at[pl.ds(0, td * SUB), :], zsem).wait()
            return c

        lax.fori_loop(pad_ref[0, N_EXPERTS], pad_ref[1, N_EXPERTS], zero_blk, 0)
        lax.fori_loop(pad_ref[0, N_EXPERTS], pad_ref[1, N_EXPERTS], wait_blk, 0)


def _dispatch(h2, pos_t, pad, npad):
    t = h2.shape[0]
    td = min(TD, t)
    return pl.pallas_call(
        _dispatch_body,
        grid=(t // td,),
        in_specs=[
            pl.BlockSpec((1, 1, 2 * td), lambda i: (i, 0, 0), memory_space=pltpu.SMEM),
            pl.BlockSpec(memory_space=pltpu.SMEM),
            pl.BlockSpec((td, D_MODEL), lambda i: (i, 0)),
        ],
        out_specs=pl.BlockSpec(memory_space=pl.ANY),
        out_shape=jax.ShapeDtypeStruct((npad * SUB, LANES), F32),
        scratch_shapes=[pltpu.VMEM((2, td * SUB, LANES), F32), pltpu.SemaphoreType.DMA((2,)),
                        pltpu.SemaphoreType.DMA(())],
        compiler_params=_params(("arbitrary",), 32),
        name="dispatch",
    )(pos_t, pad, h2)


def _moe_body(te_ref, src_ref, nv_ref, x_ref, wg_ref, wu_ref, wd_ref, o_ref):
    i = pl.program_id(0)

    @pl.when(nv_ref[i] > 0)
    def _():
        x = _from_token_tiles(x_ref, 0, TM_MOE).astype(BF)
        _to_token_tiles(o_ref, _swiglu(x, wg_ref, wu_ref, wd_ref))

    @pl.when(nv_ref[i] == 0)
    def _():
        o_ref[...] = jnp.zeros_like(o_ref)


def _moe(xs, tile_e, tile_src, tile_nv, wg, wu, wd):
    rows = TM_MOE * SUB
    wspec = lambda shape: pl.BlockSpec((None,) + shape, lambda i, te, src, nv: (te[i], 0, 0))
    return pl.pallas_call(
        _moe_body,
        grid_spec=pltpu.PrefetchScalarGridSpec(
            num_scalar_prefetch=3,
            grid=(xs.shape[0] // rows,),
            in_specs=[
                pl.BlockSpec((rows, LANES), lambda i, te, src, nv: (src[i], 0)),
                wspec((D_MODEL, D_FF)), wspec((D_MODEL, D_FF)), wspec((D_FF, D_MODEL)),
            ],
            out_specs=pl.BlockSpec((rows, LANES), lambda i, te, src, nv: (i, 0)),
        ),
        out_shape=jax.ShapeDtypeStruct(xs.shape, F32),
        compiler_params=_params(("arbitrary",), 56),
        name="moe",
    )(tile_e, tile_src, tile_nv, xs, wg, wu, wd)


def _combine_body(tiles_per_batch, posc_ref, posn_ref, x_ref, route_ref, mod_ref, ys_ref, o_ref, buf, sem):
    i = pl.program_id(0)
    n = pl.num_programs(0)
    tc = x_ref.shape[0]

    def gather(p_ref, s):
        def issue(r, c):
            for k in range(2):
                _token_copy(ys_ref, p_ref[0, 0, k * tc + r], buf.at[s], (k * tc + r) * SUB,
                            sem.at[s]).start(priority=k)
            return c

        lax.fori_loop(0, tc, issue, 0, unroll=8)

    @pl.when(i == 0)
    def _():
        gather(posc_ref, 0)

    @pl.when(i + 1 < n)
    def _():
        gather(posn_ref, (i + 1) % 2)

    slot = i % 2
    pltpu.make_async_copy(ys_ref.at[pl.ds(0, 2 * tc * SUB), :], buf.at[slot], sem.at[slot]).wait()
    lane = lax.broadcasted_iota(jnp.int32, (1, LANES), 1)
    rt = route_ref[...]
    g1 = jnp.sum(jnp.where(lane == ROUTE_GATE, rt, 0.0), axis=-1, keepdims=True)
    g2 = jnp.sum(jnp.where(lane == ROUTE_GATE + 1, rt, 0.0), axis=-1, keepdims=True)
    y = g1 * _from_token_tiles(buf.at[slot], 0, tc) + g2 * _from_token_tiles(buf.at[slot], tc, tc)
    ga2 = mod_ref[5, pl.ds(i // tiles_per_batch, 1), :]
    o_ref[...] = x_ref[...] + ga2 * y


def _combine(ys, pos_t, x, route, mod, seq):
    t = x.shape[0]
    tc = min(TD, t)
    nt = t // tc
    row = lambda i: (i, 0)
    return pl.pallas_call(
        functools.partial(_combine_body, seq // tc),
        grid=(nt,),
        in_specs=[
            pl.BlockSpec((1, 1, 2 * tc), lambda i: (i, 0, 0), memory_space=pltpu.SMEM),
            pl.BlockSpec((1, 1, 2 * tc), lambda i: (jnp.minimum(i + 1, nt - 1), 0, 0), memory_space=pltpu.SMEM),
            pl.BlockSpec((tc, D_MODEL), row),
            pl.BlockSpec((tc, LANES), row),
            pl.BlockSpec(mod.shape, lambda i: (0, 0, 0)),
            pl.BlockSpec(memory_space=pl.ANY),
        ],
        out_specs=pl.BlockSpec((tc, D_MODEL), row),
        out_shape=jax.ShapeDtypeStruct((t, D_MODEL), F32),
        scratch_shapes=[pltpu.VMEM((2, 2 * tc * SUB, LANES), F32), pltpu.SemaphoreType.DMA((2,))],
        compiler_params=_params(("arbitrary",), 32),
        name="combine",
    )(pos_t, pos_t, x, route, mod, ys)


def _moe_routed(h2, xn, route, route_t, cnt, mod, wg, wu, wd, seq):
    t = h2.shape[0]
    td = min(TD, t)
    nt_max = pl.cdiv(2 * t, TM_MOE) + N_EXPERTS
    npad = nt_max * TM_MOE
    counts = cnt[0, :N_EXPERTS].astype(jnp.int32)
    ntile = (counts + TM_MOE - 1) // TM_MOE
    eid = jnp.arange(N_EXPERTS)
    tile_end = jnp.sum(jnp.where(eid[None, :] <= eid[:, None], ntile[None, :], 0), axis=1)
    off = (tile_end - ntile) * TM_MOE
    e12 = route_t[ROUTE_E:ROUTE_E + 2].astype(jnp.int32)
    r12 = route_t[ROUTE_RANK:ROUTE_RANK + 2].astype(jnp.int32)
    onehot = e12[:, :, None] == eid[None, None, :]
    pos = jnp.sum(jnp.where(onehot, off[None, None, :], 0), axis=-1) + r12
    pos_t = (pos * SUB).reshape(2, t // td, td).transpose(1, 0, 2).reshape(t // td, 1, 2 * td)
    total = tile_end[-1]
    tid = jnp.arange(nt_max)
    tile_e = jnp.minimum(jnp.sum(tid[:, None] >= tile_end[None, :], axis=1), N_EXPERTS - 1).astype(jnp.int32)
    tile_nv = (tid < total).astype(jnp.int32)
    tile_src = jnp.minimum(tid, total - 1).astype(jnp.int32)
    pad = jnp.stack([jnp.concatenate([off + counts, (total * (TM_MOE // td))[None]]),
                     jnp.concatenate([off + ntile * TM_MOE, jnp.full((1,), npad // td, jnp.int32)])]).astype(jnp.int32)
    xs = _dispatch(h2, pos_t, pad, npad)
    ys = _moe(xs, tile_e, tile_src, tile_nv, wg, wu, wd)
    return _combine(ys, pos_t, xn, route, mod, seq)


def _rope_tables(pos):
    inv = ROPE_THETA ** (-np.arange(0, HEAD_DIM, 2, dtype=np.float64) / HEAD_DIM)
    ang = np.asarray(pos, np.float64)[:, None] * inv[None, :]
    cos = np.concatenate([np.cos(ang), np.cos(ang)], axis=-1)
    sin = np.concatenate([-np.sin(ang), np.sin(ang)], axis=-1)
    reps = LANES // HEAD_DIM
    return (jnp.asarray(np.tile(cos, (1, reps)), F32), jnp.asarray(np.tile(sin, (1, reps)), F32))


def kernel(x_prompt, x_sample, cache_k, cache_v, c_prompt, c_sample, w_ada, b_ada, norm1_w, norm2_w, w_in,
           q_norm_w, k_norm_w, attn_sinks, gm_norm_w, gm_ws, gm_bs, w_out, dense_w_gate, dense_w_up,
           dense_w_down, router_w, router_b, moe_w_gate, moe_w_up, moe_w_down):
    batch, seq, d = x_prompt.shape
    nd = x_sample.shape[0]
    depth = w_in.shape[0]
    t = batch * seq

    mod = _ada(jnp.concatenate([c_prompt, c_sample], axis=0), w_ada, b_ada)
    mod_p = mod[:, :batch].reshape(depth, batch, N_ADA, d).transpose(0, 2, 1, 3)
    mod_s = mod[:, batch:].reshape(depth, nd, N_ADA, d).transpose(0, 2, 1, 3)

    cos_p, sin_p = _rope_tables(np.arange(seq))
    cos_s, sin_s = _rope_tables(np.array([PAST_LEN]))
    head_of = np.arange(ATTN_WIDTH) // HEAD_DIM
    seg = jnp.asarray(head_of[:, None] == head_of[None, :], BF)
    pairs = nd * N_KV_HEADS
    w = cache_k.shape[2]
    ck_t = cache_k.transpose(0, 1, 3, 4, 2).reshape(depth, pairs, HEAD_DIM, w)
    cv_t = cache_v.transpose(0, 1, 3, 4, 2).reshape(depth, pairs, HEAD_DIM, w)
    uncache = lambda c: jnp.stack(c).reshape(depth, nd, N_KV_HEADS, HEAD_DIM, w).transpose(0, 1, 4, 2, 3)

    w_in_bf = w_in.astype(BF)
    w_out_bf = w_out.astype(BF)
    router_w_pad = jnp.pad(router_w, ((0, 0), (0, 0), (0, LANES - N_EXPERTS)))
    router_b_pad = jnp.pad(router_b, ((0, 0), (0, LANES - N_EXPERTS)), constant_values=NEG_INF)

    xp = x_prompt.reshape(t, d)
    xs = x_sample.reshape(nd, d)
    k_p, v_p, g_p, k_s, v_s, g_s = [], [], [], [], [], []
    for l in range(depth):
        i = l // 2
        n1 = norm1_w[l][None, :]
        n2 = norm2_w[l][None, :]
        qn = jnp.tile(q_norm_w[l], N_HEADS)[None, :]
        kn = jnp.tile(k_norm_w[l], N_KV_HEADS)[None, :]
        gmn = gm_norm_w[l][None, :]
        router = None if l % 2 == 0 else (router_w_pad[i], router_b_pad[i][None, :])

        ffn_w = (dense_w_gate, dense_w_up, dense_w_down) if router is None else (moe_w_gate, moe_w_up, moe_w_down)
        res = _inproj_p(l, xp, mod_p[l], n1, w_in_bf, qn, kn, gmn, seg, cos_p, sin_p,
                        [w[i].reshape(-1, w.shape[-1]) for w in ffn_w], batch, seq)
        q, kd, vd, u, gv, kl, vl, gvl = res[:8]
        wg_bf, wu_bf, wd_bf = (c.reshape(w.shape[1:]) for c, w in zip(res[8:], ffn_w))
        mix = _mix_p(q, kd, vd, u, gv, gm_ws[l], gm_bs[l].T, attn_sinks[l], batch, seq)
        if router is None:
            xp = _outffn_p(l, mix, xp, mod_p[l], w_out_bf, n2, wg_bf, wu_bf, wd_bf, batch, seq)
        else:
            xn, h2, route, route_t, cnt = _outproj_p(l, mix, xp, mod_p[l], w_out_bf, n2, *router, batch, seq)
            xp = _moe_routed(h2, xn, route, route_t, cnt, mod_p[l], wg_bf, wu_bf, wd_bf, seq)
        k_p.append(kl.reshape(batch, WINDOW, N_KV_HEADS, HEAD_DIM))
        v_p.append(vl.reshape(batch, WINDOW, N_KV_HEADS, HEAD_DIM))
        g_p.append(gvl)

        q, k, v, u, gv = _inproj_s(xs, mod_s[l], n1, w_in[l], qn, kn, gmn, seg, cos_s, sin_s)
        qg = jnp.pad(q.reshape(pairs, KV_GROUP, HEAD_DIM), ((0, 0), (0, SUB - KV_GROUP), (0, 0)))
        sink = jnp.pad(jnp.tile(attn_sinks[l].reshape(N_KV_HEADS, KV_GROUP), (nd, 1)),
                       ((0, 0), (0, SUB - KV_GROUP)))[:, :, None]
        o, nk, nv = _attn_s(l, qg, k.reshape(pairs, HEAD_DIM, 1), v.reshape(pairs, HEAD_DIM, 1), ck_t, cv_t, sink)
        o = o[:, :KV_GROUP, :].reshape(nd, ATTN_WIDTH)
        wdiag = jnp.repeat(gm_ws[l][:, 0, 0], GM_WIDTH // GM_GROUPS)[None, :]
        bsrow = jnp.repeat(gm_bs[l][:, 0], GM_WIDTH // GM_GROUPS)[None, :]
        res = _outproj_s(o, u, gv, wdiag, bsrow, xs, mod_s[l], w_out[l], n2, router)
        if router is None:
            xs = _ffn_s(res[1], res[0], mod_s[l], dense_w_gate[i], dense_w_up[i], dense_w_down[i])
        else:
            xs = _moe_s(res[1], res[0], res[2], mod_s[l], wg_bf, wu_bf, wd_bf)
        k_s.append(nk)
        v_s.append(nv)
        g_s.append(gv[:, None, :])

    return (xp.reshape(batch, seq, d), xs.reshape(nd, 1, d), jnp.stack(k_p), jnp.stack(v_p), jnp.stack(g_p),
            uncache(k_s), uncache(v_s), jnp.stack(g_s))
```

```python
import functools

import numpy as np
import jax
import jax.numpy as jnp
from jax import lax
from jax.experimental import pallas as pl
from jax.experimental.pallas import tpu as pltpu

D_MODEL = 1024
HEAD_DIM = 64
N_HEADS = 8
N_KV_HEADS = 2
KV_GROUP = N_HEADS // N_KV_HEADS
ATTN_WIDTH = N_HEADS * HEAD_DIM
KV_WIDTH = N_KV_HEADS * HEAD_DIM
GM_WIDTH = 512
GM_GROUPS = 4
WINDOW = 128
CHUNK = 128
D_FF = 2816
MXU_DIM = 256
FF_SPLIT = (0, 6 * MXU_DIM, D_FF)
N_EXPERTS = 8
N_ADA = 6
IN_COLS = ATTN_WIDTH + 2 * KV_WIDTH + 2 * GM_WIDTH
PAST_LEN = 16384
ROPE_THETA = 10000.0
EPS = 1e-6
NEG_INF = -1e30
LANES = 128
SUB = 8
assert D_MODEL == SUB * LANES

BF = jnp.bfloat16
F32 = jnp.float32
MIB = 1024 * 1024


ROW_GROUP = 256


def _params(sem, vmem_mib):
    return pltpu.CompilerParams(dimension_semantics=sem, vmem_limit_bytes=vmem_mib * MIB)


def _dot(a, b):
    return jnp.dot(a, b, preferred_element_type=F32)


def _dot_nt(a, b):
    return lax.dot_general(a, b, (((1,), (1,)), ((), ())), preferred_element_type=F32)


def _split(a):
    hi = a.astype(BF)
    return hi, (a - hi.astype(F32)).astype(BF)


def _dot_bf(a, w):
    return _dot(a.astype(BF), w)


def _dot3(a, w):
    ah, al = _split(a)
    if w.dtype == BF:
        return _dot(ah, w) + _dot(al, w)
    wh, wl = _split(w)
    return _dot(ah, wh) + _dot(al, wh) + _dot(ah, wl)


def _rms(x, w):
    ms = jnp.mean(x * x, axis=-1, keepdims=True)
    return x * lax.rsqrt(ms + EPS) * w


def _ada_body(c_ref, w_ref, b_ref, o_ref):
    o_ref[...] = _dot3(jax.nn.silu(c_ref[...]), w_ref[...]) + b_ref[...]


def _ada(c_all, w_ada, b_ada):
    depth, d, cols = w_ada.shape
    n = c_all.shape[0]
    tn = 1024
    return pl.pallas_call(
        _ada_body,
        grid=(depth, cols // tn),
        in_specs=[
            pl.BlockSpec((n, d), lambda l, j: (0, 0)),
            pl.BlockSpec((None, d, tn), lambda l, j: (l, 0, j)),
            pl.BlockSpec((None, 1, tn), lambda l, j: (l, 0, j)),
        ],
        out_specs=pl.BlockSpec((None, n, tn), lambda l, j: (l, 0, j)),
        out_shape=jax.ShapeDtypeStruct((depth, n, cols), F32),
        compiler_params=_params(("arbitrary", "arbitrary"), 32),
        name="ada",
    )(c_all, w_ada, b_ada.reshape(depth, 1, cols))


def _swap_halves(t):
    n = t.shape[-1]
    lane = lax.broadcasted_iota(jnp.int32, (1, n), 1)
    first = (lane % HEAD_DIM) < (HEAD_DIM // 2)
    return jnp.where(first, pltpu.roll(t, n - HEAD_DIM // 2, axis=1), pltpu.roll(t, HEAD_DIM // 2, axis=1))


def _inproj_compute(mm, x, sh, sc, n1, w_ref, qn, kn, gmn, seg_ref, cos, sin):
    h = _rms(x, n1) * (1.0 + sc) + sh
    z = mm(h, w_ref[...])
    q = z[:, :ATTN_WIDTH]
    k = z[:, ATTN_WIDTH:ATTN_WIDTH + KV_WIDTH]
    v = z[:, ATTN_WIDTH + KV_WIDTH:ATTN_WIDTH + 2 * KV_WIDTH]
    gm = z[:, ATTN_WIDTH + 2 * KV_WIDTH:]

    def head_norm(t, seg, wn):
        ms = mm(t * t, seg) * (1.0 / HEAD_DIM)
        return t * lax.rsqrt(ms + EPS) * wn

    def rope(t):
        reps = t.shape[-1] // LANES
        c = jnp.concatenate([cos] * reps, axis=-1) if reps > 1 else cos
        s = jnp.concatenate([sin] * reps, axis=-1) if reps > 1 else sin
        return t * c + _swap_halves(t) * s

    q = rope(head_norm(q, seg_ref[...], qn)) * (HEAD_DIM ** -0.5)
    k = rope(head_norm(k, seg_ref[:KV_WIDTH, :KV_WIDTH], kn))
    g = jax.nn.gelu(gm)
    u = g[:, :GM_WIDTH]
    gv = _rms(g[:, GM_WIDTH:], gmn)
    return q, k, v, u, gv


def _dup_heads(t):
    lane = lax.broadcasted_iota(jnp.int32, (1, LANES), 1)
    lo = lane < HEAD_DIM
    r = pltpu.roll(t, HEAD_DIM, axis=1)
    return jnp.concatenate([jnp.where(lo, t, r), jnp.where(lo, r, t)], axis=-1)


def _inproj_p_body(tiles_per_batch, ncast, x_ref, mod_ref, n1_ref, w_ref, qn_ref, kn_ref, gmn_ref, seg_ref,
                   cos_ref, sin_ref, *rest):
    cast_src = rest[:ncast]
    q_ref, kd_ref, vd_ref, u_ref, gv_ref, kl_ref, vl_ref, gvl_ref = rest[ncast:ncast + 8]
    cast_dst = rest[ncast + 8:]
    for src, dst in zip(cast_src, cast_dst):
        dst[...] = src[...].astype(BF)
    i = pl.program_id(0)
    b = i // tiles_per_batch
    sh = mod_ref[0, pl.ds(b, 1), :]
    sc = mod_ref[1, pl.ds(b, 1), :]
    hs = min(ROW_GROUP, x_ref.shape[0])
    for hh in range(x_ref.shape[0] // hs):
        rs = slice(hh * hs, (hh + 1) * hs)
        q, k, v, u, gv = _inproj_compute(_dot_bf, x_ref[rs, :], sh, sc, n1_ref[...], w_ref, qn_ref[...], kn_ref[...],
                                         gmn_ref[...], seg_ref, cos_ref[rs, :], sin_ref[rs, :])
        q_ref[rs, :] = q.astype(BF)
        kd_ref[rs, :] = _dup_heads(k).astype(BF)
        vd_ref[rs, :] = _dup_heads(v).astype(BF)
        u_ref[rs, :] = u.astype(BF)
        gv_ref[rs, :] = gv.astype(BF)

    @pl.when(i % tiles_per_batch == tiles_per_batch - 1)
    def _():
        kl_ref[...] = k[hs - WINDOW:, :]
        vl_ref[...] = v[hs - WINDOW:, :]
        gvl_ref[...] = gv[hs - CHUNK:, :]


BF16_SUBLANES = 16


def _slab_spec(rows, cols, steps):
    s = steps
    while rows % s or (rows // s) % BF16_SUBLANES:
        s //= 2
    return pl.BlockSpec((rows // s, cols), lambda i: (i // (steps // s), 0))


def _inproj_p(l, x, mod, n1, w_bf, qn, kn, gmn, seg, cos, sin, casts, batch, seq):
    t = x.shape[0]
    tm = min(1024, seq)
    tpb = seq // tm
    steps = t // tm
    row = lambda i: (i, 0)
    full = lambda i: (0, 0)
    last = lambda i: (i // tpb, 0, 0)
    cast_specs = [_slab_spec(c.shape[0], c.shape[1], steps) for c in casts]
    return pl.pallas_call(
        functools.partial(_inproj_p_body, tpb, len(casts)),
        grid=(steps,),
        in_specs=[
            pl.BlockSpec((tm, D_MODEL), row),
            pl.BlockSpec((N_ADA, batch, D_MODEL), lambda i: (0, 0, 0)),
            pl.BlockSpec((1, D_MODEL), full),
            pl.BlockSpec((None, D_MODEL, IN_COLS), lambda i: (l, 0, 0)),
            pl.BlockSpec((1, ATTN_WIDTH), full),
            pl.BlockSpec((1, KV_WIDTH), full),
            pl.BlockSpec((1, GM_WIDTH), full),
            pl.BlockSpec((ATTN_WIDTH, ATTN_WIDTH), full),
            pl.BlockSpec((tm, LANES), lambda i: (i % tpb, 0)),
            pl.BlockSpec((tm, LANES), lambda i: (i % tpb, 0)),
        ] + cast_specs,
        out_specs=[
            pl.BlockSpec((tm, ATTN_WIDTH), row),
            pl.BlockSpec((tm, 2 * KV_WIDTH), row),
            pl.BlockSpec((tm, 2 * KV_WIDTH), row),
            pl.BlockSpec((tm, GM_WIDTH), row),
            pl.BlockSpec((tm, GM_WIDTH), row),
            pl.BlockSpec((None, WINDOW, KV_WIDTH), last),
            pl.BlockSpec((None, WINDOW, KV_WIDTH), last),
            pl.BlockSpec((None, CHUNK, GM_WIDTH), last),
        ] + cast_specs,
        out_shape=[
            jax.ShapeDtypeStruct((t, ATTN_WIDTH), BF),
            jax.ShapeDtypeStruct((t, 2 * KV_WIDTH), BF),
            jax.ShapeDtypeStruct((t, 2 * KV_WIDTH), BF),
            jax.ShapeDtypeStruct((t, GM_WIDTH), BF),
            jax.ShapeDtypeStruct((t, GM_WIDTH), BF),
            jax.ShapeDtypeStruct((batch, WINDOW, KV_WIDTH), F32),
            jax.ShapeDtypeStruct((batch, WINDOW, KV_WIDTH), F32),
            jax.ShapeDtypeStruct((batch, CHUNK, GM_WIDTH), F32),
        ] + [jax.ShapeDtypeStruct(c.shape, BF) for c in casts],
        compiler_params=_params(("arbitrary",), 56),
        name="inproj_p",
    )(x, mod, n1, w_bf, qn, kn, gmn, seg, cos, sin, *casts)


def _inproj_s_body(x_ref, mod_ref, n1_ref, w_ref, qn_ref, kn_ref, gmn_ref, seg_ref, cos_ref, sin_ref,
                   q_ref, k_ref, v_ref, u_ref, gv_ref):
    q, k, v, u, gv = _inproj_compute(_dot3, x_ref[...], mod_ref[0], mod_ref[1], n1_ref[...], w_ref, qn_ref[...],
                                     kn_ref[...], gmn_ref[...], seg_ref, cos_ref[...], sin_ref[...])
    q_ref[...] = q
    k_ref[...] = k
    v_ref[...] = v
    u_ref[...] = u
    gv_ref[...] = gv


def _inproj_s(x, mod, n1, w_bf, qn, kn, gmn, seg, cos, sin):
    n = x.shape[0]
    widths = (ATTN_WIDTH, KV_WIDTH, KV_WIDTH, GM_WIDTH, GM_WIDTH)
    return pl.pallas_call(
        _inproj_s_body,
        out_shape=[jax.ShapeDtypeStruct((n, w), F32) for w in widths],
        compiler_params=pltpu.CompilerParams(vmem_limit_bytes=48 * MIB),
        name="inproj_s",
    )(x, mod, n1, w_bf, qn, kn, gmn, seg, cos, sin)


def _mix_p_body(nblk, q_ref, kc_ref, kp_ref, vc_ref, vp_ref, u_ref, gv_ref, ws_ref, bst_ref, sink_ref, o_ref):
    i = pl.program_id(1)
    blk = WINDOW
    lane = lax.broadcasted_iota(jnp.int32, (1, LANES), 1)
    lo = lane < HEAD_DIM
    cols = KV_GROUP * blk
    iq = lax.broadcasted_iota(jnp.int32, (2 * blk, cols), 1) % blk
    jk = lax.broadcasted_iota(jnp.int32, (2 * blk, cols), 0)
    band = (jk > iq) & (jk <= iq + blk)
    bias = jnp.where(band, 0.0, NEG_INF)
    bias_first = jnp.where(band & ((jk >= blk) | (i > 0)), 0.0, NEG_INF)
    tri = (lax.broadcasted_iota(jnp.int32, (CHUNK, CHUNK), 0)
           >= lax.broadcasted_iota(jnp.int32, (CHUNK, CHUNK), 1))
    wm = [jnp.where(tri, ws_ref[g], 0.0).astype(BF) for g in range(GM_GROUPS)]

    for n in range(nblk):
        r0 = n * blk
        if n == 0:
            kk = jnp.concatenate([kp_ref[...], kc_ref[0:blk, :]], axis=0)
            vv = jnp.concatenate([vp_ref[...], vc_ref[0:blk, :]], axis=0)
            mask_bias = bias_first
        else:
            kk = kc_ref[r0 - blk:r0 + blk, :]
            vv = vc_ref[r0 - blk:r0 + blk, :]
            mask_bias = bias
        for kvh in range(N_KV_HEADS):
            c0 = 2 * kvh
            qa = q_ref[r0:r0 + blk, c0 * LANES:(c0 + 1) * LANES]
            qb = q_ref[r0:r0 + blk, (c0 + 1) * LANES:(c0 + 2) * LANES]
            zero = jnp.zeros_like(qa)
            qq = jnp.concatenate([jnp.where(lo, qa, zero), jnp.where(lo, zero, qa),
                                  jnp.where(lo, qb, zero), jnp.where(lo, zero, qb)], axis=0)
            s = _dot_nt(kk[:, kvh * LANES:(kvh + 1) * LANES], qq) + mask_bias
            sink = jnp.concatenate(
                [jnp.full((1, blk), sink_ref[kvh * KV_GROUP + g], F32) for g in range(KV_GROUP)], axis=1)
            m = jnp.maximum(jnp.max(s, axis=0, keepdims=True), sink)
            p = jnp.exp(s - m)
            den = jnp.sum(p, axis=0, keepdims=True) + jnp.exp(sink - m)
            p = (p * (1.0 / den)).astype(BF)
            o = lax.dot_general(p, vv[:, kvh * LANES:(kvh + 1) * LANES], (((0,), (0,)), ((), ())),
                                preferred_element_type=F32)
            o_ref[r0:r0 + blk, c0 * LANES:(c0 + 1) * LANES] = jnp.where(
                lo, o[0:blk], o[blk:2 * blk]).astype(BF)
            o_ref[r0:r0 + blk, (c0 + 1) * LANES:(c0 + 2) * LANES] = jnp.where(
                lo, o[2 * blk:3 * blk], o[3 * blk:4 * blk]).astype(BF)
        for g in range(GM_GROUPS):
            cs = slice(g * LANES, (g + 1) * LANES)
            sp = _dot(wm[g], gv_ref[r0:r0 + blk, cs]) + bst_ref[:, g:g + 1]
            o_ref[r0:r0 + blk, ATTN_WIDTH + g * LANES:ATTN_WIDTH + (g + 1) * LANES] = (
                u_ref[r0:r0 + blk, cs].astype(F32) * sp).astype(BF)


def _mix_p(q, kd, vd, u, gv, ws, bst, sinks, batch, seq):
    t = q.shape[0]
    tq = min(1024, seq)
    nblk = tq // WINDOW
    tpb = seq // tq
    cur = lambda b, i: (b * tpb + i, 0)
    prev = lambda b, i: (jnp.maximum((b * tpb + i) * nblk - 1, b * tpb * nblk), 0)
    return pl.pallas_call(
        functools.partial(_mix_p_body, nblk),
        grid=(batch, tpb),
        in_specs=[
            pl.BlockSpec((tq, ATTN_WIDTH), cur),
            pl.BlockSpec((tq, 2 * KV_WIDTH), cur),
            pl.BlockSpec((WINDOW, 2 * KV_WIDTH), prev),
            pl.BlockSpec((tq, 2 * KV_WIDTH), cur),
            pl.BlockSpec((WINDOW, 2 * KV_WIDTH), prev),
            pl.BlockSpec((tq, GM_WIDTH), cur),
            pl.BlockSpec((tq, GM_WIDTH), cur),
            pl.BlockSpec((GM_GROUPS, CHUNK, CHUNK), lambda b, i: (0, 0, 0)),
            pl.BlockSpec((CHUNK, GM_GROUPS), lambda b, i: (0, 0)),
            pl.BlockSpec(memory_space=pltpu.SMEM),
        ],
        out_specs=pl.BlockSpec((tq, D_MODEL), cur),
        out_shape=jax.ShapeDtypeStruct((t, D_MODEL), BF),
        compiler_params=_params(("arbitrary", "arbitrary"), 48),
        name="mix_p",
    )(q, kd, kd, vd, vd, u, gv, ws, bst, sinks)


def _attn_s_body(q_ref, kn_ref, vn_ref, ck_ref, cv_ref, sink_ref, o_ref, nk_ref, nv_ref):
    nb, hd, w = ck_ref.shape
    pos = lax.broadcasted_iota(jnp.int32, (1, w), 1)
    col = lax.broadcasted_iota(jnp.int32, (1, kn_ref.shape[1]), 1)
    first = pl.program_id(0) * nb
    for p in range(nb):
        batch_elem = (first + p) // N_KV_HEADS
        rows = slice((p % N_KV_HEADS) * hd, (p % N_KV_HEADS + 1) * hd)
        for new_ref, c_ref, n_ref in ((kn_ref, ck_ref, nk_ref), (vn_ref, cv_ref, nv_ref)):
            new = jnp.sum(jnp.where(col == batch_elem, new_ref[rows, :], 0.0), axis=1, keepdims=True)
            n_ref[p] = jnp.where(pos == w - 1, new, pltpu.roll(c_ref[p], w - 1, axis=1))
    nk = nk_ref[...]
    nv = nv_ref[...]

    def bmm3(spec, a, b):
        (ah, al), (bh, bl) = _split(a), _split(b)
        mm = lambda x, y: jnp.einsum(spec, x, y, preferred_element_type=F32)
        return mm(ah, bh) + mm(al, bh) + mm(ah, bl)

    s = bmm3('ngd,ndj->ngj', q_ref[...], nk)
    sink = sink_ref[...]
    m = jnp.maximum(jnp.max(s, axis=-1, keepdims=True), sink)
    p = jnp.exp(s - m)
    den = jnp.sum(p, axis=-1, keepdims=True) + jnp.exp(sink - m)
    o_ref[...] = bmm3('ngj,ndj->ngd', p, nv) * (1.0 / den)


def _attn_s(l, q, k_new, v_new, ck, cv, sink):
    _, n, hd, w = ck.shape
    nb = min(32, n)
    rows = q.shape[1]
    blk = lambda r, c: pl.BlockSpec((nb, r, c), lambda i: (i, 0, 0))
    cache = pl.BlockSpec((None, nb, hd, w), lambda i: (l, i, 0, 0))
    new = pl.BlockSpec(k_new.shape, lambda i: (0, 0))
    return pl.pallas_call(
        _attn_s_body,
        grid=(n // nb,),
        in_specs=[blk(rows, hd), new, new, cache, cache, blk(rows, 1)],
        out_specs=[blk(rows, hd), blk(hd, w), blk(hd, w)],
        out_shape=[jax.ShapeDtypeStruct((n, rows, hd), F32),
                   jax.ShapeDtypeStruct((n, hd, w), F32),
                   jax.ShapeDtypeStruct((n, hd, w), F32)],
        compiler_params=_params(("arbitrary",), 32),
        name="attn_s",
    )(q, k_new, v_new, ck, cv, sink)


def _top2(h2, rw_ref, rb_ref):
    logits = _dot3(h2, rw_ref[...]) + rb_ref[...]
    lane = lax.broadcasted_iota(jnp.int32, logits.shape, 1).astype(F32)
    e = jnp.exp(logits - jnp.max(logits, axis=-1, keepdims=True))
    p = e / jnp.sum(e, axis=-1, keepdims=True)
    m1 = jnp.max(p, axis=-1, keepdims=True)
    i1 = jnp.min(jnp.where(p == m1, lane, float(LANES)), axis=-1, keepdims=True)
    p2 = jnp.where(lane == i1, -1.0, p)
    m2 = jnp.max(p2, axis=-1, keepdims=True)
    i2 = jnp.min(jnp.where(p2 == m2, lane, float(LANES)), axis=-1, keepdims=True)
    tot = m1 + m2
    return lane, i1, i2, m1 / tot, m2 / tot


def _route_gates(h2, rw_ref, rb_ref):
    lane, i1, i2, g1, g2 = _top2(h2, rw_ref, rb_ref)
    return jnp.where(lane == i1, g1, 0.0) + jnp.where(lane == i2, g2, 0.0)


ROUTE_E, ROUTE_RANK, ROUTE_GATE = 0, 2, 4


def _route_ranked(h2, rw_ref, rb_ref, tri_ref, cnt_ref):
    lane, i1, i2, g1, g2 = _top2(h2, rw_ref, rb_ref)
    oh1 = lane == i1
    oh2 = lane == i2
    hit = jnp.where(oh1, 1.0, 0.0) + jnp.where(oh2, 1.0, 0.0)
    before = cnt_ref[...] + _dot(tri_ref[...], hit.astype(BF))
    r1 = jnp.sum(jnp.where(oh1, before, 0.0), axis=-1, keepdims=True)
    r2 = jnp.sum(jnp.where(oh2, before, 0.0), axis=-1, keepdims=True)
    cnt_ref[...] += jnp.sum(hit, axis=0, keepdims=True)
    cols = (i1, i2, r1, r2, g1, g2)
    out = jnp.zeros_like(lane)
    for j, c in enumerate(cols):
        out = jnp.where(lane == float(j), c, out)
    return out


def _outproj_compute(mm, mix, x, ga1, sh2, sc2, w_ref, n2):
    xn = x + ga1 * mm(mix, w_ref[...])
    h2 = _rms(xn, n2) * (1.0 + sc2) + sh2
    return xn, h2


def _outproj_p_body(tiles_per_batch, mix_ref, x_ref, mod_ref, w_ref, n2_ref, rw_ref, rb_ref, tri_ref,
                    xn_ref, h2_ref, route_ref, route_t_ref, cnt_ref):
    i = pl.program_id(0)
    b = i // tiles_per_batch
    mrow = lambda j: mod_ref[j, pl.ds(b, 1), :]

    @pl.when(i == 0)
    def _():
        cnt_ref[...] = jnp.zeros_like(cnt_ref)

    xn, h2 = _outproj_compute(_dot_bf, mix_ref[...], x_ref[...], mrow(2), mrow(3), mrow(4), w_ref, n2_ref[...])
    route = _route_ranked(h2, rw_ref, rb_ref, tri_ref, cnt_ref)
    route_ref[...] = route
    route_t_ref[...] = route.T[:SUB, :]
    h2_ref[...] = h2
    xn_ref[...] = xn


def _outproj_p(l, mix, x, mod, w_bf, n2, rw, rb, batch, seq):
    t = x.shape[0]
    tm = min(512, seq)
    row = lambda i: (i, 0)
    full = lambda i: (0, 0)
    return pl.pallas_call(
        functools.partial(_outproj_p_body, seq // tm),
        grid=(t // tm,),
        in_specs=[
            pl.BlockSpec((tm, D_MODEL), row),
            pl.BlockSpec((tm, D_MODEL), row),
            pl.BlockSpec((N_ADA, batch, D_MODEL), lambda i: (0, 0, 0)),
            pl.BlockSpec((None, D_MODEL, D_MODEL), lambda i: (l, 0, 0)),
            pl.BlockSpec((1, D_MODEL), full),
            pl.BlockSpec((D_MODEL, LANES), full),
            pl.BlockSpec((1, LANES), full),
            pl.BlockSpec((tm, tm), full),
        ],
        out_specs=[pl.BlockSpec((tm, D_MODEL), row), pl.BlockSpec((tm, D_MODEL), row),
                   pl.BlockSpec((tm, LANES), row), pl.BlockSpec((SUB, tm), lambda i: (0, i)),
                   pl.BlockSpec((1, LANES), full)],
        out_shape=[jax.ShapeDtypeStruct((t, D_MODEL), F32), jax.ShapeDtypeStruct((t, D_MODEL), F32),
                   jax.ShapeDtypeStruct((t, LANES), F32), jax.ShapeDtypeStruct((SUB, t), F32),
                   jax.ShapeDtypeStruct((1, LANES), F32)],
        compiler_params=_params(("arbitrary",), 48),
        name="outproj_p",
    )(mix, x, mod, w_bf, n2, rw, rb, jnp.asarray(np.tri(tm, k=-1), BF))


def _outproj_s_body(with_router, o_ref, u_ref, gv_ref, wdiag_ref, bsrow_ref, x_ref, mod_ref, w_ref, n2_ref, *rest):
    gate = u_ref[...] * (wdiag_ref[...] * gv_ref[...] + bsrow_ref[...])
    mix = jnp.concatenate([o_ref[...], gate], axis=-1)
    xn, h2 = _outproj_compute(_dot3, mix, x_ref[...], mod_ref[2], mod_ref[3], mod_ref[4], w_ref, n2_ref[...])
    if with_router:
        rw_ref, rb_ref, xn_ref, h2_ref, gates_ref = rest
        gates_ref[...] = _route_gates(h2, rw_ref, rb_ref)
    else:
        xn_ref, h2_ref = rest
    xn_ref[...] = xn
    h2_ref[...] = h2


def _outproj_s(o, u, gv, wdiag, bsrow, x, mod, w, n2, router):
    n = x.shape[0]
    out_shape = [jax.ShapeDtypeStruct((n, D_MODEL), F32), jax.ShapeDtypeStruct((n, D_MODEL), F32)]
    args = [o, u, gv, wdiag, bsrow, x, mod, w, n2]
    if router is not None:
        out_shape.append(jax.ShapeDtypeStruct((n, LANES), F32))
        args += list(router)
    return pl.pallas_call(
        functools.partial(_outproj_s_body, router is not None),
        out_shape=out_shape,
        compiler_params=pltpu.CompilerParams(vmem_limit_bytes=32 * MIB),
        name="outproj_s",
    )(*args)


def _swiglu(h_bf, wg_ref, wu_ref, wd_ref):
    y = None
    for c in range(len(FF_SPLIT) - 1):
        sl = slice(FF_SPLIT[c], FF_SPLIT[c + 1])
        a = (jax.nn.silu(_dot(h_bf, wg_ref[:, sl])) * _dot(h_bf, wu_ref[:, sl])).astype(BF)
        part = _dot(a, wd_ref[sl, :])
        y = part if y is None else y + part
    return y


def _ffn_s_body(h_ref, x_ref, mod_ref, wg_ref, wu_ref, wd_ref, o_ref):
    h = h_ref[...]
    y = jnp.zeros_like(h)
    for c in range(D_FF // MXU_DIM):
        sl = slice(c * MXU_DIM, (c + 1) * MXU_DIM)
        a = jax.nn.silu(_dot3(h, wg_ref[:, sl])) * _dot3(h, wu_ref[:, sl])
        y = y + _dot3(a, wd_ref[sl, :])
    o_ref[...] = x_ref[...] + mod_ref[5] * y


def _ffn_s(h2, x, mod, wg, wu, wd):
    return pl.pallas_call(
        _ffn_s_body,
        out_shape=jax.ShapeDtypeStruct(x.shape, F32),
        compiler_params=pltpu.CompilerParams(vmem_limit_bytes=56 * MIB),
        name="ffn_s",
    )(h2, x, mod, wg, wu, wd)


def _outffn_p_body(tiles_per_batch, mix_ref, x_ref, mod_ref, w_ref, n2_ref, wg_ref, wu_ref, wd_ref, o_ref):
    b = pl.program_id(0) // tiles_per_batch
    mrow = lambda j: mod_ref[j, pl.ds(b, 1), :]
    xn, h2 = _outproj_compute(_dot_bf, mix_ref[...], x_ref[...], mrow(2), mrow(3), mrow(4), w_ref, n2_ref[...])
    o_ref[...] = xn + mrow(5) * _swiglu(h2.astype(BF), wg_ref, wu_ref, wd_ref)


def _outffn_p(l, mix, x, mod, w_bf, n2, wg, wu, wd, batch, seq):
    t = x.shape[0]
    tm = min(512, seq)
    row = lambda i: (i, 0)
    const = lambda shape: pl.BlockSpec(shape, lambda i: (0,) * len(shape), pipeline_mode=pl.Buffered(1))
    return pl.pallas_call(
        functools.partial(_outffn_p_body, seq // tm),
        grid=(t // tm,),
        in_specs=[
            pl.BlockSpec((tm, D_MODEL), row),
            pl.BlockSpec((tm, D_MODEL), row),
            const((N_ADA, batch, D_MODEL)),
            pl.BlockSpec((None, D_MODEL, D_MODEL), lambda i: (l, 0, 0), pipeline_mode=pl.Buffered(1)),
            const((1, D_MODEL)),
            const((D_MODEL, D_FF)), const((D_MODEL, D_FF)), const((D_FF, D_MODEL)),
        ],
        out_specs=pl.BlockSpec((tm, D_MODEL), row),
        out_shape=jax.ShapeDtypeStruct((t, D_MODEL), F32),
        compiler_params=_params(("arbitrary",), 56),
        name="outffn_p",
    )(mix, x, mod, w_bf, n2, wg, wu, wd)


def _moe_s_body(h_ref, x_ref, gates_ref, mod_ref, wg_ref, wu_ref, wd_ref, o_ref):
    e = pl.program_id(0)

    @pl.when(e == 0)
    def _():
        o_ref[...] = jnp.zeros_like(o_ref)

    lane = lax.broadcasted_iota(jnp.int32, (1, LANES), 1)
    gate = jnp.sum(jnp.where(lane == e, gates_ref[...], 0.0), axis=-1, keepdims=True)
    o_ref[...] += gate * _swiglu(h_ref[...].astype(BF), wg_ref, wu_ref, wd_ref)

    @pl.when(e == N_EXPERTS - 1)
    def _():
        o_ref[...] = x_ref[...] + mod_ref[5] * o_ref[...]


def _moe_s(h2, x, gates, mod, wg, wu, wd):
    n = x.shape[0]
    whole = lambda shape: pl.BlockSpec(shape, lambda e: (0,) * len(shape))
    wspec = lambda shape: pl.BlockSpec((None,) + shape, lambda e: (e, 0, 0))
    return pl.pallas_call(
        _moe_s_body,
        grid=(N_EXPERTS,),
        in_specs=[whole((n, D_MODEL)), whole((n, D_MODEL)), whole((n, LANES)), whole(mod.shape),
                  wspec((D_MODEL, D_FF)), wspec((D_MODEL, D_FF)), wspec((D_FF, D_MODEL))],
        out_specs=whole((n, D_MODEL)),
        out_shape=jax.ShapeDtypeStruct((n, D_MODEL), F32),
        compiler_params=_params(("arbitrary",), 56),
        name="moe_s",
    )(h2, x, gates, mod, wg, wu, wd)


TM_MOE = 512
TD = 512


def _to_token_tiles(ref, x):
    r = x.shape[0]
    for g in range(SUB):
        ref[pl.ds(g, r, stride=SUB), :] = x[:, g * LANES:(g + 1) * LANES]


def _from_token_tiles(ref, first, r):
    return jnp.concatenate([ref[pl.ds(first * SUB + g, r, stride=SUB), :] for g in range(SUB)], axis=-1)


def _token_copy(src, s, dst, d, sem):
    aligned = lambda v: v if isinstance(v, int) else pl.multiple_of(v, SUB)
    return pltpu.make_async_copy(src.at[pl.ds(aligned(s), SUB), :], dst.at[pl.ds(aligned(d), SUB), :], sem)


def _dispatch_body(pos_ref, pad_ref, h_ref, xs_ref, stage, sem, zsem):
    i = pl.program_id(0)
    n = pl.num_programs(0)
    td = h_ref.shape[0]
    slot = i % 2

    def wait_slot(s):
        for _ in range(2):
            pltpu.make_async_copy(stage.at[s], xs_ref.at[pl.ds(0, td * SUB), :], sem.at[s]).wait()

    @pl.when(i >= 2)
    def _():
        wait_slot(slot)

    _to_token_tiles(stage.at[slot], h_ref[...])

    def issue(r, c):
        for k in range(2):
            _token_copy(stage.at[slot], r * SUB, xs_ref, pos_ref[0, 0, k * td + r], sem.at[slot]).start(priority=k)
        return c

    lax.fori_loop(0, td, issue, 0, unroll=8)

    @pl.when(i == n - 1)
    def _():
        wait_slot(slot)

        @pl.when(n > 1)
        def _():
            wait_slot(1 - slot)

        stage[0] = jnp.zeros(stage.shape[1:], stage.dtype)
        for e in range(N_EXPERTS):
            lo = pad_ref[0, e]
            hi = pad_ref[1, e]

            def zero_token(r, c):
                _token_copy(stage.at[0], 0, xs_ref, r * SUB, zsem).start()
                return c

            def wait_token(r, c):
                _token_copy(stage.at[0], 0, xs_ref, 0, zsem).wait()
                return c

            lax.fori_loop(lo, hi, zero_token, 0)
            lax.fori_loop(lo, hi, wait_token, 0)

        def zero_blk(j, c):
            pltpu.make_async_copy(stage.at[0], xs_ref.at[pl.ds(pl.multiple_of(j * (td * SUB), SUB), td * SUB), :],
                                  zsem).start()
            return c

        def wait_blk(j, c):
            pltpu.make_async_copy(stage.at[0], xs_ref.at[pl.ds(0, td * SUB), :], zsem).wait()
            return c

        lax.fori_loop(pad_ref[0, N_EXPERTS], pad_ref[1, N_EXPERTS], zero_blk, 0)
        lax.fori_loop(pad_ref[0, N_EXPERTS], pad_ref[1, N_EXPERTS], wait_blk, 0)


def _dispatch(h2, pos_t, pad, npad):
    t = h2.shape[0]
    td = min(TD, t)
    return pl.pallas_call(
        _dispatch_body,
        grid=(t // td,),
        in_specs=[
            pl.BlockSpec((1, 1, 2 * td), lambda i: (i, 0, 0), memory_space=pltpu.SMEM),
            pl.BlockSpec(memory_space=pltpu.SMEM),
            pl.BlockSpec((td, D_MODEL), lambda i: (i, 0)),
        ],
        out_specs=pl.BlockSpec(memory_space=pl.ANY),
        out_shape=jax.ShapeDtypeStruct((npad * SUB, LANES), F32),
        scratch_shapes=[pltpu.VMEM((2, td * SUB, LANES), F32), pltpu.SemaphoreType.DMA((2,)),
                        pltpu.SemaphoreType.DMA(())],
        compiler_params=_params(("arbitrary",), 32),
        name="dispatch",
    )(pos_t, pad, h2)


def _moe_body(te_ref, src_ref, nv_ref, x_ref, wg_ref, wu_ref, wd_ref, o_ref):
    i = pl.program_id(0)

    @pl.when(nv_ref[i] > 0)
    def _():
        x = _from_token_tiles(x_ref, 0, TM_MOE).astype(BF)
        _to_token_tiles(o_ref, _swiglu(x, wg_ref, wu_ref, wd_ref))

    @pl.when(nv_ref[i] == 0)
    def _():
        o_ref[...] = jnp.zeros_like(o_ref)


def _moe(xs, tile_e, tile_src, tile_nv, wg, wu, wd):
    rows = TM_MOE * SUB
    wspec = lambda shape: pl.BlockSpec((None,) + shape, lambda i, te, src, nv: (te[i], 0, 0))
    return pl.pallas_call(
        _moe_body,
        grid_spec=pltpu.PrefetchScalarGridSpec(
            num_scalar_prefetch=3,
            grid=(xs.shape[0] // rows,),
            in_specs=[
                pl.BlockSpec((rows, LANES), lambda i, te, src, nv: (src[i], 0)),
                wspec((D_MODEL, D_FF)), wspec((D_MODEL, D_FF)), wspec((D_FF, D_MODEL)),
            ],
            out_specs=pl.BlockSpec((rows, LANES), lambda i, te, src, nv: (i, 0)),
        ),
        out_shape=jax.ShapeDtypeStruct(xs.shape, F32),
        compiler_params=_params(("arbitrary",), 56),
        name="moe",
    )(tile_e, tile_src, tile_nv, xs, wg, wu, wd)


def _combine_body(tiles_per_batch, posc_ref, posn_ref, x_ref, route_ref, mod_ref, ys_ref, o_ref, buf, sem):
    i = pl.program_id(0)
    n = pl.num_programs(0)
    tc = x_ref.shape[0]

    def gather(p_ref, s):
        def issue(r, c):
            for k in range(2):
                _token_copy(ys_ref, p_ref[0, 0, k * tc + r], buf.at[s], (k * tc + r) * SUB,
                            sem.at[s]).start(priority=k)
            return c

        lax.fori_loop(0, tc, issue, 0, unroll=8)

    @pl.when(i == 0)
    def _():
        gather(posc_ref, 0)

    @pl.when(i + 1 < n)
    def _():
        gather(posn_ref, (i + 1) % 2)

    slot = i % 2
    pltpu.make_async_copy(ys_ref.at[pl.ds(0, 2 * tc * SUB), :], buf.at[slot], sem.at[slot]).wait()
    lane = lax.broadcasted_iota(jnp.int32, (1, LANES), 1)
    rt = route_ref[...]
    g1 = jnp.sum(jnp.where(lane == ROUTE_GATE, rt, 0.0), axis=-1, keepdims=True)
    g2 = jnp.sum(jnp.where(lane == ROUTE_GATE + 1, rt, 0.0), axis=-1, keepdims=True)
    y = g1 * _from_token_tiles(buf.at[slot], 0, tc) + g2 * _from_token_tiles(buf.at[slot], tc, tc)
    ga2 = mod_ref[5, pl.ds(i // tiles_per_batch, 1), :]
    o_ref[...] = x_ref[...] + ga2 * y


def _combine(ys, pos_t, x, route, mod, seq):
    t = x.shape[0]
    tc = min(TD, t)
    nt = t // tc
    row = lambda i: (i, 0)
    return pl.pallas_call(
        functools.partial(_combine_body, seq // tc),
        grid=(nt,),
        in_specs=[
            pl.BlockSpec((1, 1, 2 * tc), lambda i: (i, 0, 0), memory_space=pltpu.SMEM),
            pl.BlockSpec((1, 1, 2 * tc), lambda i: (jnp.minimum(i + 1, nt - 1), 0, 0), memory_space=pltpu.SMEM),
            pl.BlockSpec((tc, D_MODEL), row),
            pl.BlockSpec((tc, LANES), row),
            pl.BlockSpec(mod.shape, lambda i: (0, 0, 0)),
            pl.BlockSpec(memory_space=pl.ANY),
        ],
        out_specs=pl.BlockSpec((tc, D_MODEL), row),
        out_shape=jax.ShapeDtypeStruct((t, D_MODEL), F32),
        scratch_shapes=[pltpu.VMEM((2, 2 * tc * SUB, LANES), F32), pltpu.SemaphoreType.DMA((2,))],
        compiler_params=_params(("arbitrary",), 32),
        name="combine",
    )(pos_t, pos_t, x, route, mod, ys)


def _moe_routed(h2, xn, route, route_t, cnt, mod, wg, wu, wd, seq):
    t = h2.shape[0]
    td = min(TD, t)
    nt_max = pl.cdiv(2 * t, TM_MOE) + N_EXPERTS
    npad = nt_max * TM_MOE
    counts = cnt[0, :N_EXPERTS].astype(jnp.int32)
    ntile = (counts + TM_MOE - 1) // TM_MOE
    eid = jnp.arange(N_EXPERTS)
    tile_end = jnp.sum(jnp.where(eid[None, :] <= eid[:, None], ntile[None, :], 0), axis=1)
    off = (tile_end - ntile) * TM_MOE
    e12 = route_t[ROUTE_E:ROUTE_E + 2].astype(jnp.int32)
    r12 = route_t[ROUTE_RANK:ROUTE_RANK + 2].astype(jnp.int32)
    onehot = e12[:, :, None] == eid[None, None, :]
    pos = jnp.sum(jnp.where(onehot, off[None, None, :], 0), axis=-1) + r12
    pos_t = (pos * SUB).reshape(2, t // td, td).transpose(1, 0, 2).reshape(t // td, 1, 2 * td)
    total = tile_end[-1]
    tid = jnp.arange(nt_max)
    tile_e = jnp.minimum(jnp.sum(tid[:, None] >= tile_end[None, :], axis=1), N_EXPERTS - 1).astype(jnp.int32)
    tile_nv = (tid < total).astype(jnp.int32)
    tile_src = jnp.minimum(tid, total - 1).astype(jnp.int32)
    pad = jnp.stack([jnp.concatenate([off + counts, (total * (TM_MOE // td))[None]]),
                     jnp.concatenate([off + ntile * TM_MOE, jnp.full((1,), npad // td, jnp.int32)])]).astype(jnp.int32)
    xs = _dispatch(h2, pos_t, pad, npad)
    ys = _moe(xs, tile_e, tile_src, tile_nv, wg, wu, wd)
    return _combine(ys, pos_t, xn, route, mod, seq)


def _rope_tables(pos):
    inv = ROPE_THETA ** (-np.arange(0, HEAD_DIM, 2, dtype=np.float64) / HEAD_DIM)
    ang = np.asarray(pos, np.float64)[:, None] * inv[None, :]
    cos = np.concatenate([np.cos(ang), np.cos(ang)], axis=-1)
    sin = np.concatenate([-np.sin(ang), np.sin(ang)], axis=-1)
    reps = LANES // HEAD_DIM
    return (jnp.asarray(np.tile(cos, (1, reps)), F32), jnp.asarray(np.tile(sin, (1, reps)), F32))


def kernel(x_prompt, x_sample, cache_k, cache_v, c_prompt, c_sample, w_ada, b_ada, norm1_w, norm2_w, w_in,
           q_norm_w, k_norm_w, attn_sinks, gm_norm_w, gm_ws, gm_bs, w_out, dense_w_gate, dense_w_up,
           dense_w_down, router_w, router_b, moe_w_gate, moe_w_up, moe_w_down):
    batch, seq, d = x_prompt.shape
    nd = x_sample.shape[0]
    depth = w_in.shape[0]
    t = batch * seq

    mod = _ada(jnp.concatenate([c_prompt, c_sample], axis=0), w_ada, b_ada)
    mod_p = mod[:, :batch].reshape(depth, batch, N_ADA, d).transpose(0, 2, 1, 3)
    mod_s = mod[:, batch:].reshape(depth, nd, N_ADA, d).transpose(0, 2, 1, 3)

    cos_p, sin_p = _rope_tables(np.arange(seq))
    cos_s, sin_s = _rope_tables(np.array([PAST_LEN]))
    head_of = np.arange(ATTN_WIDTH) // HEAD_DIM
    seg = jnp.asarray(head_of[:, None] == head_of[None, :], BF)
    pairs = nd * N_KV_HEADS
    w = cache_k.shape[2]
    ck_t = cache_k.transpose(0, 1, 3, 4, 2).reshape(depth, pairs, HEAD_DIM, w)
    cv_t = cache_v.transpose(0, 1, 3, 4, 2).reshape(depth, pairs, HEAD_DIM, w)
    uncache = lambda c: jnp.stack(c).reshape(depth, nd, N_KV_HEADS, HEAD_DIM, w).transpose(0, 1, 4, 2, 3)

    w_in_bf = w_in.astype(BF)
    w_out_bf = w_out.astype(BF)
    router_w_pad = jnp.pad(router_w, ((0, 0), (0, 0), (0, LANES - N_EXPERTS)))
    router_b_pad = jnp.pad(router_b, ((0, 0), (0, LANES - N_EXPERTS)), constant_values=NEG_INF)

    cast_plan = []
    for l in range(depth):
        ws = (dense_w_gate, dense_w_up, dense_w_down) if l % 2 == 0 else (moe_w_gate, moe_w_up, moe_w_down)
        cast_plan += [(l - 1 if (l % 2 == 1 and j == 0) else l, (l, j), wj[l // 2]) for j, wj in enumerate(ws)]
    ffn_bf = {}

    xp = x_prompt.reshape(t, d)
    xs = x_sample.reshape(nd, d)
    k_p, v_p, g_p, k_s, v_s, g_s = [], [], [], [], [], []
    for l in range(depth):
        i = l // 2
        n1 = norm1_w[l][None, :]
        n2 = norm2_w[l][None, :]
        qn = jnp.tile(q_norm_w[l], N_HEADS)[None, :]
        kn = jnp.tile(k_norm_w[l], N_KV_HEADS)[None, :]
        gmn = gm_norm_w[l][None, :]
        router = None if l % 2 == 0 else (router_w_pad[i], router_b_pad[i][None, :])

        pending = [c for c in cast_plan if c[0] == l]
        res = _inproj_p(l, xp, mod_p[l], n1, w_in_bf, qn, kn, gmn, seg, cos_p, sin_p,
                        [wt.reshape(-1, wt.shape[-1]) for _, _, wt in pending], batch, seq)
        q, kd, vd, u, gv, kl, vl, gvl = res[:8]
        for (_, name, wt), c in zip(pending, res[8:]):
            ffn_bf[name] = c.reshape(wt.shape)
        wg_bf, wu_bf, wd_bf = (ffn_bf[(l, j)] for j in range(3))
        mix = _mix_p(q, kd, vd, u, gv, gm_ws[l], gm_bs[l].T, attn_sinks[l], batch, seq)
        if router is None:
            xp = _outffn_p(l, mix, xp, mod_p[l], w_out_bf, n2, wg_bf, wu_bf, wd_bf, batch, seq)
        else:
            xn, h2, route, route_t, cnt = _outproj_p(l, mix, xp, mod_p[l], w_out_bf, n2, *router, batch, seq)
            xp = _moe_routed(h2, xn, route, route_t, cnt, mod_p[l], wg_bf, wu_bf, wd_bf, seq)
        k_p.append(kl.reshape(batch, WINDOW, N_KV_HEADS, HEAD_DIM))
        v_p.append(vl.reshape(batch, WINDOW, N_KV_HEADS, HEAD_DIM))
        g_p.append(gvl)

        q, k, v, u, gv = _inproj_s(xs, mod_s[l], n1, w_in[l], qn, kn, gmn, seg, cos_s, sin_s)
        qg = jnp.pad(q.reshape(pairs, KV_GROUP, HEAD_DIM), ((0, 0), (0, SUB - KV_GROUP), (0, 0)))
        sink = jnp.pad(jnp.tile(attn_sinks[l].reshape(N_KV_HEADS, KV_GROUP), (nd, 1)),
                       ((0, 0), (0, SUB - KV_GROUP)))[:, :, None]
        o, nk, nv = _attn_s(l, qg, k.T, v.T, ck_t, cv_t, sink)
        o = o[:, :KV_GROUP, :].reshape(nd, ATTN_WIDTH)
        wdiag = jnp.repeat(gm_ws[l][:, 0, 0], GM_WIDTH // GM_GROUPS)[None, :]
        bsrow = jnp.repeat(gm_bs[l][:, 0], GM_WIDTH // GM_GROUPS)[None, :]
        res = _outproj_s(o, u, gv, wdiag, bsrow, xs, mod_s[l], w_out[l], n2, router)
        if router is None:
            xs = _ffn_s(res[1], res[0], mod_s[l], dense_w_gate[i], dense_w_up[i], dense_w_down[i])
        else:
            xs = _moe_s(res[1], res[0], res[2], mod_s[l], wg_bf, wu_bf, wd_bf)
        k_s.append(nk)
        v_s.append(nv)
        g_s.append(gv[:, None, :])

    return (xp.reshape(batch, seq, d), xs.reshape(nd, 1, d), jnp.stack(k_p), jnp.stack(v_p), jnp.stack(g_p),
            uncache(k_s), uncache(v_s), jnp.stack(g_s))
```

```python
import functools

import numpy as np
import jax
import jax.numpy as jnp
from jax import lax
from jax.experimental import pallas as pl
from jax.experimental.pallas import tpu as pltpu

D_MODEL = 1024
HEAD_DIM = 64
N_HEADS = 8
N_KV_HEADS = 2
KV_GROUP = N_HEADS // N_KV_HEADS
ATTN_WIDTH = N_HEADS * HEAD_DIM
KV_WIDTH = N_KV_HEADS * HEAD_DIM
GM_WIDTH = 512
GM_GROUPS = 4
WINDOW = 128
CHUNK = 128
D_FF = 2816
MXU_DIM = 256
FF_SPLIT = (0, 6 * MXU_DIM, D_FF)
N_EXPERTS = 8
N_ADA = 6
IN_COLS = ATTN_WIDTH + 2 * KV_WIDTH + 2 * GM_WIDTH
PAST_LEN = 16384
ROPE_THETA = 10000.0
EPS = 1e-6
NEG_INF = -1e30
LANES = 128
SUB = 8
assert D_MODEL == SUB * LANES

BF = jnp.bfloat16
F32 = jnp.float32
MIB = 1024 * 1024


ROW_GROUP = 256


def _params(sem, vmem_mib):
    return pltpu.CompilerParams(dimension_semantics=sem, vmem_limit_bytes=vmem_mib * MIB)


def _dot(a, b):
    return jnp.dot(a, b, preferred_element_type=F32)


def _dot_nt(a, b):
    return lax.dot_general(a, b, (((1,), (1,)), ((), ())), preferred_element_type=F32)


def _split(a):
    hi = a.astype(BF)
    return hi, (a - hi.astype(F32)).astype(BF)


def _dot_bf(a, w):
    return _dot(a.astype(BF), w)


def _dot3(a, w):
    ah, al = _split(a)
    if w.dtype == BF:
        return _dot(ah, w) + _dot(al, w)
    wh, wl = _split(w)
    return _dot(ah, wh) + _dot(al, wh) + _dot(ah, wl)


def _rms(x, w):
    ms = jnp.mean(x * x, axis=-1, keepdims=True)
    return x * lax.rsqrt(ms + EPS) * w


def _ada_body(c_ref, w_ref, b_ref, o_ref):
    o_ref[...] = _dot3(jax.nn.silu(c_ref[...]), w_ref[...]) + b_ref[...]


def _ada(c_all, w_ada, b_ada):
    depth, d, cols = w_ada.shape
    n = c_all.shape[0]
    tn = 1024
    return pl.pallas_call(
        _ada_body,
        grid=(depth, cols // tn),
        in_specs=[
            pl.BlockSpec((n, d), lambda l, j: (0, 0)),
            pl.BlockSpec((None, d, tn), lambda l, j: (l, 0, j)),
            pl.BlockSpec((None, 1, tn), lambda l, j: (l, 0, j)),
        ],
        out_specs=pl.BlockSpec((None, n, tn), lambda l, j: (l, 0, j)),
        out_shape=jax.ShapeDtypeStruct((depth, n, cols), F32),
        compiler_params=_params(("arbitrary", "arbitrary"), 32),
        name="ada",
    )(c_all, w_ada, b_ada.reshape(depth, 1, cols))


def _swap_halves(t):
    n = t.shape[-1]
    lane = lax.broadcasted_iota(jnp.int32, (1, n), 1)
    first = (lane % HEAD_DIM) < (HEAD_DIM // 2)
    return jnp.where(first, pltpu.roll(t, n - HEAD_DIM // 2, axis=1), pltpu.roll(t, HEAD_DIM // 2, axis=1))


def _inproj_compute(mm, x, sh, sc, n1, w_ref, qn, kn, gmn, seg_ref, cos, sin):
    h = _rms(x, n1) * (1.0 + sc) + sh
    z = mm(h, w_ref[...])
    q = z[:, :ATTN_WIDTH]
    k = z[:, ATTN_WIDTH:ATTN_WIDTH + KV_WIDTH]
    v = z[:, ATTN_WIDTH + KV_WIDTH:ATTN_WIDTH + 2 * KV_WIDTH]
    gm = z[:, ATTN_WIDTH + 2 * KV_WIDTH:]

    def head_norm(t, seg, wn):
        ms = mm(t * t, seg) * (1.0 / HEAD_DIM)
        return t * lax.rsqrt(ms + EPS) * wn

    def rope(t):
        reps = t.shape[-1] // LANES
        c = jnp.concatenate([cos] * reps, axis=-1) if reps > 1 else cos
        s = jnp.concatenate([sin] * reps, axis=-1) if reps > 1 else sin
        return t * c + _swap_halves(t) * s

    q = rope(head_norm(q, seg_ref[...], qn)) * (HEAD_DIM ** -0.5)
    k = rope(head_norm(k, seg_ref[:KV_WIDTH, :KV_WIDTH], kn))
    g = jax.nn.gelu(gm)
    u = g[:, :GM_WIDTH]
    gv = _rms(g[:, GM_WIDTH:], gmn)
    return q, k, v, u, gv


def _dup_heads(t):
    lane = lax.broadcasted_iota(jnp.int32, (1, LANES), 1)
    lo = lane < HEAD_DIM
    r = pltpu.roll(t, HEAD_DIM, axis=1)
    return jnp.concatenate([jnp.where(lo, t, r), jnp.where(lo, r, t)], axis=-1)


def _inproj_p_body(tiles_per_batch, ncast, x_ref, mod_ref, n1_ref, w_ref, qn_ref, kn_ref, gmn_ref, seg_ref,
                   cos_ref, sin_ref, *rest):
    cast_src = rest[:ncast]
    q_ref, kd_ref, vd_ref, u_ref, gv_ref, kl_ref, vl_ref, gvl_ref = rest[ncast:ncast + 8]
    cast_dst = rest[ncast + 8:]
    for src, dst in zip(cast_src, cast_dst):
        dst[...] = src[...].astype(BF)
    i = pl.program_id(0)
    b = i // tiles_per_batch
    sh = mod_ref[0, pl.ds(b, 1), :]
    sc = mod_ref[1, pl.ds(b, 1), :]
    hs = min(ROW_GROUP, x_ref.shape[0])
    for hh in range(x_ref.shape[0] // hs):
        rs = slice(hh * hs, (hh + 1) * hs)
        q, k, v, u, gv = _inproj_compute(_dot_bf, x_ref[rs, :], sh, sc, n1_ref[...], w_ref, qn_ref[...], kn_ref[...],
                                         gmn_ref[...], seg_ref, cos_ref[rs, :], sin_ref[rs, :])
        q_ref[rs, :] = q.astype(BF)
        kd_ref[rs, :] = _dup_heads(k).astype(BF)
        vd_ref[rs, :] = _dup_heads(v).astype(BF)
        u_ref[rs, :] = u.astype(BF)
        gv_ref[rs, :] = gv.astype(BF)

    @pl.when(i % tiles_per_batch == tiles_per_batch - 1)
    def _():
        kl_ref[...] = k[hs - WINDOW:, :]
        vl_ref[...] = v[hs - WINDOW:, :]
        gvl_ref[...] = gv[hs - CHUNK:, :]


BF16_SUBLANES = 16


def _slab_spec(rows, cols, steps):
    s = steps
    while rows % s or (rows // s) % BF16_SUBLANES:
        s //= 2
    return pl.BlockSpec((rows // s, cols), lambda i: (i // (steps // s), 0))


def _inproj_p(l, x, mod, n1, w_bf, qn, kn, gmn, seg, cos, sin, casts, batch, seq):
    t = x.shape[0]
    tm = min(1024, seq)
    tpb = seq // tm
    steps = t // tm
    row = lambda i: (i, 0)
    full = lambda i: (0, 0)
    last = lambda i: (i // tpb, 0, 0)
    cast_specs = [_slab_spec(c.shape[0], c.shape[1], steps) for c in casts]
    return pl.pallas_call(
        functools.partial(_inproj_p_body, tpb, len(casts)),
        grid=(steps,),
        in_specs=[
            pl.BlockSpec((tm, D_MODEL), row),
            pl.BlockSpec((N_ADA, batch, D_MODEL), lambda i: (0, 0, 0)),
            pl.BlockSpec((1, D_MODEL), full),
            pl.BlockSpec((None, D_MODEL, IN_COLS), lambda i: (l, 0, 0)),
            pl.BlockSpec((1, ATTN_WIDTH), full),
            pl.BlockSpec((1, KV_WIDTH), full),
            pl.BlockSpec((1, GM_WIDTH), full),
            pl.BlockSpec((ATTN_WIDTH, ATTN_WIDTH), full),
            pl.BlockSpec((tm, LANES), lambda i: (i % tpb, 0)),
            pl.BlockSpec((tm, LANES), lambda i: (i % tpb, 0)),
        ] + cast_specs,
        out_specs=[
            pl.BlockSpec((tm, ATTN_WIDTH), row),
            pl.BlockSpec((tm, 2 * KV_WIDTH), row),
            pl.BlockSpec((tm, 2 * KV_WIDTH), row),
            pl.BlockSpec((tm, GM_WIDTH), row),
            pl.BlockSpec((tm, GM_WIDTH), row),
            pl.BlockSpec((None, WINDOW, KV_WIDTH), last),
            pl.BlockSpec((None, WINDOW, KV_WIDTH), last),
            pl.BlockSpec((None, CHUNK, GM_WIDTH), last),
        ] + cast_specs,
        out_shape=[
            jax.ShapeDtypeStruct((t, ATTN_WIDTH), BF),
            jax.ShapeDtypeStruct((t, 2 * KV_WIDTH), BF),
            jax.ShapeDtypeStruct((t, 2 * KV_WIDTH), BF),
            jax.ShapeDtypeStruct((t, GM_WIDTH), BF),
            jax.ShapeDtypeStruct((t, GM_WIDTH), BF),
            jax.ShapeDtypeStruct((batch, WINDOW, KV_WIDTH), F32),
            jax.ShapeDtypeStruct((batch, WINDOW, KV_WIDTH), F32),
            jax.ShapeDtypeStruct((batch, CHUNK, GM_WIDTH), F32),
        ] + [jax.ShapeDtypeStruct(c.shape, BF) for c in casts],
        compiler_params=_params(("arbitrary",), 56),
        name="inproj_p",
    )(x, mod, n1, w_bf, qn, kn, gmn, seg, cos, sin, *casts)


def _inproj_s_body(x_ref, mod_ref, n1_ref, w_ref, qn_ref, kn_ref, gmn_ref, seg_ref, cos_ref, sin_ref,
                   q_ref, k_ref, v_ref, u_ref, gv_ref):
    q, k, v, u, gv = _inproj_compute(_dot3, x_ref[...], mod_ref[0], mod_ref[1], n1_ref[...], w_ref, qn_ref[...],
                                     kn_ref[...], gmn_ref[...], seg_ref, cos_ref[...], sin_ref[...])
    q_ref[...] = q
    k_ref[...] = k
    v_ref[...] = v
    u_ref[...] = u
    gv_ref[...] = gv


def _inproj_s(x, mod, n1, w_bf, qn, kn, gmn, seg, cos, sin):
    n = x.shape[0]
    widths = (ATTN_WIDTH, KV_WIDTH, KV_WIDTH, GM_WIDTH, GM_WIDTH)
    return pl.pallas_call(
        _inproj_s_body,
        out_shape=[jax.ShapeDtypeStruct((n, w), F32) for w in widths],
        compiler_params=pltpu.CompilerParams(vmem_limit_bytes=48 * MIB),
        name="inproj_s",
    )(x, mod, n1, w_bf, qn, kn, gmn, seg, cos, sin)


def _mix_p_body(nblk, q_ref, kc_ref, kp_ref, vc_ref, vp_ref, u_ref, gv_ref, ws_ref, bst_ref, sink_ref, o_ref):
    i = pl.program_id(1)
    blk = WINDOW
    lane = lax.broadcasted_iota(jnp.int32, (1, LANES), 1)
    lo = lane < HEAD_DIM
    cols = KV_GROUP * blk
    iq = lax.broadcasted_iota(jnp.int32, (2 * blk, cols), 1) % blk
    jk = lax.broadcasted_iota(jnp.int32, (2 * blk, cols), 0)
    band = (jk > iq) & (jk <= iq + blk)
    bias = jnp.where(band, 0.0, NEG_INF)
    bias_first = jnp.where(band & ((jk >= blk) | (i > 0)), 0.0, NEG_INF)
    tri = (lax.broadcasted_iota(jnp.int32, (CHUNK, CHUNK), 0)
           >= lax.broadcasted_iota(jnp.int32, (CHUNK, CHUNK), 1))
    wm = [jnp.where(tri, ws_ref[g], 0.0).astype(BF) for g in range(GM_GROUPS)]

    for n in range(nblk):
        r0 = n * blk
        if n == 0:
            kk = jnp.concatenate([kp_ref[...], kc_ref[0:blk, :]], axis=0)
            vv = jnp.concatenate([vp_ref[...], vc_ref[0:blk, :]], axis=0)
            mask_bias = bias_first
        else:
            kk = kc_ref[r0 - blk:r0 + blk, :]
            vv = vc_ref[r0 - blk:r0 + blk, :]
            mask_bias = bias
        for kvh in range(N_KV_HEADS):
            c0 = 2 * kvh
            qa = q_ref[r0:r0 + blk, c0 * LANES:(c0 + 1) * LANES]
            qb = q_ref[r0:r0 + blk, (c0 + 1) * LANES:(c0 + 2) * LANES]
            zero = jnp.zeros_like(qa)
            qq = jnp.concatenate([jnp.where(lo, qa, zero), jnp.where(lo, zero, qa),
                                  jnp.where(lo, qb, zero), jnp.where(lo, zero, qb)], axis=0)
            s = _dot_nt(kk[:, kvh * LANES:(kvh + 1) * LANES], qq) + mask_bias
            sink = jnp.concatenate(
                [jnp.full((1, blk), sink_ref[kvh * KV_GROUP + g], F32) for g in range(KV_GROUP)], axis=1)
            m = jnp.maximum(jnp.max(s, axis=0, keepdims=True), sink)
            p = jnp.exp(s - m)
            den = jnp.sum(p, axis=0, keepdims=True) + jnp.exp(sink - m)
            p = (p * (1.0 / den)).astype(BF)
            o = lax.dot_general(p, vv[:, kvh * LANES:(kvh + 1) * LANES], (((0,), (0,)), ((), ())),
                                preferred_element_type=F32)
            o_ref[r0:r0 + blk, c0 * LANES:(c0 + 1) * LANES] = jnp.where(
                lo, o[0:blk], o[blk:2 * blk]).astype(BF)
            o_ref[r0:r0 + blk, (c0 + 1) * LANES:(c0 + 2) * LANES] = jnp.where(
                lo, o[2 * blk:3 * blk], o[3 * blk:4 * blk]).astype(BF)
        for g in range(GM_GROUPS):
            cs = slice(g * LANES, (g + 1) * LANES)
            sp = _dot(wm[g], gv_ref[r0:r0 + blk, cs]) + bst_ref[:, g:g + 1]
            o_ref[r0:r0 + blk, ATTN_WIDTH + g * LANES:ATTN_WIDTH + (g + 1) * LANES] = (
                u_ref[r0:r0 + blk, cs].astype(F32) * sp).astype(BF)


def _mix_p(q, kd, vd, u, gv, ws, bst, sinks, batch, seq):
    t = q.shape[0]
    tq = min(1024, seq)
    nblk = tq // WINDOW
    tpb = seq // tq
    cur = lambda b, i: (b * tpb + i, 0)
    prev = lambda b, i: (jnp.maximum((b * tpb + i) * nblk - 1, b * tpb * nblk), 0)
    return pl.pallas_call(
        functools.partial(_mix_p_body, nblk),
        grid=(batch, tpb),
        in_specs=[
            pl.BlockSpec((tq, ATTN_WIDTH), cur),
            pl.BlockSpec((tq, 2 * KV_WIDTH), cur),
            pl.BlockSpec((WINDOW, 2 * KV_WIDTH), prev),
            pl.BlockSpec((tq, 2 * KV_WIDTH), cur),
            pl.BlockSpec((WINDOW, 2 * KV_WIDTH), prev),
            pl.BlockSpec((tq, GM_WIDTH), cur),
            pl.BlockSpec((tq, GM_WIDTH), cur),
            pl.BlockSpec((GM_GROUPS, CHUNK, CHUNK), lambda b, i: (0, 0, 0)),
            pl.BlockSpec((CHUNK, GM_GROUPS), lambda b, i: (0, 0)),
            pl.BlockSpec(memory_space=pltpu.SMEM),
        ],
        out_specs=pl.BlockSpec((tq, D_MODEL), cur),
        out_shape=jax.ShapeDtypeStruct((t, D_MODEL), BF),
        compiler_params=_params(("arbitrary", "arbitrary"), 48),
        name="mix_p",
    )(q, kd, kd, vd, vd, u, gv, ws, bst, sinks)


ATTN_S_PAIRS = 32

def _attn_s_body(q_ref, kn_ref, vn_ref, ck_ref, cv_ref, sink_ref, o_ref, nk_ref, nv_ref):
    nb, hd, w = ck_ref.shape
    pos = lax.broadcasted_iota(jnp.int32, (1, w), 1)
    for p in range(nb):
        b, kvh = divmod(p, N_KV_HEADS)
        for new_ref, c_ref, n_ref in ((kn_ref, ck_ref, nk_ref), (vn_ref, cv_ref, nv_ref)):
            new = new_ref[kvh * hd:(kvh + 1) * hd, b:b + 1]
            n_ref[p] = jnp.where(pos == w - 1, new, pltpu.roll(c_ref[p], w - 1, axis=1))
    nk = nk_ref[...]
    nv = nv_ref[...]

    def bmm3(spec, a, b):
        (ah, al), (bh, bl) = _split(a), _split(b)
        mm = lambda x, y: jnp.einsum(spec, x, y, preferred_element_type=F32)
        return mm(ah, bh) + mm(al, bh) + mm(ah, bl)

    s = bmm3('ngd,ndj->ngj', q_ref[...], nk)
    sink = sink_ref[...]
    m = jnp.maximum(jnp.max(s, axis=-1, keepdims=True), sink)
    p = jnp.exp(s - m)
    den = jnp.sum(p, axis=-1, keepdims=True) + jnp.exp(sink - m)
    o_ref[...] = bmm3('ngj,ndj->ngd', p, nv) * (1.0 / den)


def _attn_s(l, q, k_new, v_new, ck, cv, sink):
    _, n, hd, w = ck.shape
    nb = ATTN_S_PAIRS
    assert n % nb == 0
    rows = q.shape[1]
    blk = lambda r, c: pl.BlockSpec((nb, r, c), lambda i: (i, 0, 0))
    cache = pl.BlockSpec((None, nb, hd, w), lambda i: (l, i, 0, 0))
    new = pl.BlockSpec((None,) + k_new.shape[1:], lambda i: (i, 0, 0))
    return pl.pallas_call(
        _attn_s_body,
        grid=(n // nb,),
        in_specs=[blk(rows, hd), new, new, cache, cache, blk(rows, 1)],
        out_specs=[blk(rows, hd), blk(hd, w), blk(hd, w)],
        out_shape=[jax.ShapeDtypeStruct((n, rows, hd), F32),
                   jax.ShapeDtypeStruct((n, hd, w), F32),
                   jax.ShapeDtypeStruct((n, hd, w), F32)],
        compiler_params=_params(("arbitrary",), 32),
        name="attn_s",
    )(q, k_new, v_new, ck, cv, sink)


def _top2(h2, rw_ref, rb_ref):
    hh, hl = _split(h2)
    wh, wl = _split(rw_ref[...])
    both = _dot(hh, jnp.concatenate([wh, wl], axis=1))
    logits = both[:, :LANES] + _dot(hl, wh) + both[:, LANES:] + rb_ref[...]
    lane = lax.broadcasted_iota(jnp.int32, logits.shape, 1).astype(F32)
    e = jnp.exp(logits - jnp.max(logits, axis=-1, keepdims=True))
    p = e / jnp.sum(e, axis=-1, keepdims=True)
    m1 = jnp.max(p, axis=-1, keepdims=True)
    i1 = jnp.min(jnp.where(p == m1, lane, float(LANES)), axis=-1, keepdims=True)
    p2 = jnp.where(lane == i1, -1.0, p)
    m2 = jnp.max(p2, axis=-1, keepdims=True)
    i2 = jnp.min(jnp.where(p2 == m2, lane, float(LANES)), axis=-1, keepdims=True)
    tot = m1 + m2
    return lane, i1, i2, m1 / tot, m2 / tot


def _route_gates(h2, rw_ref, rb_ref):
    lane, i1, i2, g1, g2 = _top2(h2, rw_ref, rb_ref)
    return jnp.where(lane == i1, g1, 0.0) + jnp.where(lane == i2, g2, 0.0)


ROUTE_E, ROUTE_RANK, ROUTE_GATE = 0, 2, 4


def _route_ranked(h2, rw_ref, rb_ref, tri_ref, cnt_ref):
    lane, i1, i2, g1, g2 = _top2(h2, rw_ref, rb_ref)
    oh1 = lane == i1
    oh2 = lane == i2
    hit = jnp.where(oh1, 1.0, 0.0) + jnp.where(oh2, 1.0, 0.0)
    before = cnt_ref[...] + _dot(tri_ref[...], hit.astype(BF))
    r1 = jnp.sum(jnp.where(oh1, before, 0.0), axis=-1, keepdims=True)
    r2 = jnp.sum(jnp.where(oh2, before, 0.0), axis=-1, keepdims=True)
    cnt_ref[...] += jnp.sum(hit, axis=0, keepdims=True)
    cols = (i1, i2, r1, r2, g1, g2)
    out = jnp.zeros_like(lane)
    for j, c in enumerate(cols):
        out = jnp.where(lane == float(j), c, out)
    return out


def _outproj_compute(mm, mix, x, ga1, sh2, sc2, w_ref, n2):
    xn = x + ga1 * mm(mix, w_ref[...])
    h2 = _rms(xn, n2) * (1.0 + sc2) + sh2
    return xn, h2


def _outproj_p_body(tiles_per_batch, mix_ref, x_ref, mod_ref, w_ref, n2_ref, rw_ref, rb_ref, tri_ref,
                    xn_ref, h2_ref, route_ref, route_t_ref, cnt_ref):
    i = pl.program_id(0)
    b = i // tiles_per_batch
    mrow = lambda j: mod_ref[j, pl.ds(b, 1), :]

    @pl.when(i == 0)
    def _():
        cnt_ref[...] = jnp.zeros_like(cnt_ref)

    xn, h2 = _outproj_compute(_dot_bf, mix_ref[...], x_ref[...], mrow(2), mrow(3), mrow(4), w_ref, n2_ref[...])
    route = _route_ranked(h2, rw_ref, rb_ref, tri_ref, cnt_ref)
    route_ref[...] = route
    route_t_ref[...] = route.T[:SUB, :]
    h2_ref[...] = h2
    xn_ref[...] = xn


def _outproj_p(l, mix, x, mod, w_bf, n2, rw, rb, batch, seq):
    t = x.shape[0]
    tm = min(512, seq)
    row = lambda i: (i, 0)
    full = lambda i: (0, 0)
    return pl.pallas_call(
        functools.partial(_outproj_p_body, seq // tm),
        grid=(t // tm,),
        in_specs=[
            pl.BlockSpec((tm, D_MODEL), row),
            pl.BlockSpec((tm, D_MODEL), row),
            pl.BlockSpec((N_ADA, batch, D_MODEL), lambda i: (0, 0, 0)),
            pl.BlockSpec((None, D_MODEL, D_MODEL), lambda i: (l, 0, 0)),
            pl.BlockSpec((1, D_MODEL), full),
            pl.BlockSpec((D_MODEL, LANES), full),
            pl.BlockSpec((1, LANES), full),
            pl.BlockSpec((tm, tm), full),
        ],
        out_specs=[pl.BlockSpec((tm, D_MODEL), row), pl.BlockSpec((tm, D_MODEL), row),
                   pl.BlockSpec((tm, LANES), row), pl.BlockSpec((SUB, tm), lambda i: (0, i)),
                   pl.BlockSpec((1, LANES), full)],
        out_shape=[jax.ShapeDtypeStruct((t, D_MODEL), F32), jax.ShapeDtypeStruct((t, D_MODEL), F32),
                   jax.ShapeDtypeStruct((t, LANES), F32), jax.ShapeDtypeStruct((SUB, t), F32),
                   jax.ShapeDtypeStruct((1, LANES), F32)],
        compiler_params=_params(("arbitrary",), 48),
        name="outproj_p",
    )(mix, x, mod, w_bf, n2, rw, rb, jnp.asarray(np.tri(tm, k=-1), BF))


def _outproj_s_body(with_router, o_ref, u_ref, gv_ref, wdiag_ref, bsrow_ref, x_ref, mod_ref, w_ref, n2_ref, *rest):
    gate = u_ref[...] * (wdiag_ref[...] * gv_ref[...] + bsrow_ref[...])
    mix = jnp.concatenate([o_ref[...], gate], axis=-1)
    xn, h2 = _outproj_compute(_dot3, mix, x_ref[...], mod_ref[2], mod_ref[3], mod_ref[4], w_ref, n2_ref[...])
    if with_router:
        rw_ref, rb_ref, xn_ref, h2_ref, gates_ref = rest
        gates_ref[...] = _route_gates(h2, rw_ref, rb_ref)
    else:
        xn_ref, h2_ref = rest
    xn_ref[...] = xn
    h2_ref[...] = h2


def _outproj_s(o, u, gv, wdiag, bsrow, x, mod, w, n2, router):
    n = x.shape[0]
    out_shape = [jax.ShapeDtypeStruct((n, D_MODEL), F32), jax.ShapeDtypeStruct((n, D_MODEL), F32)]
    args = [o, u, gv, wdiag, bsrow, x, mod, w, n2]
    if router is not None:
        out_shape.append(jax.ShapeDtypeStruct((n, LANES), F32))
        args += list(router)
    return pl.pallas_call(
        functools.partial(_outproj_s_body, router is not None),
        out_shape=out_shape,
        compiler_params=pltpu.CompilerParams(vmem_limit_bytes=32 * MIB),
        name="outproj_s",
    )(*args)


def _swiglu(h_bf, wg_ref, wu_ref, wd_ref):
    y = None
    for c in range(len(FF_SPLIT) - 1):
        sl = slice(FF_SPLIT[c], FF_SPLIT[c + 1])
        a = (jax.nn.silu(_dot(h_bf, wg_ref[:, sl])) * _dot(h_bf, wu_ref[:, sl])).astype(BF)
        part = _dot(a, wd_ref[sl, :])
        y = part if y is None else y + part
    return y


def _ffn_s_body(h_ref, x_ref, mod_ref, wg_ref, wu_ref, wd_ref, o_ref):
    h = h_ref[...]
    y = jnp.zeros_like(h)
    for c in range(D_FF // MXU_DIM):
        sl = slice(c * MXU_DIM, (c + 1) * MXU_DIM)
        a = jax.nn.silu(_dot3(h, wg_ref[:, sl])) * _dot3(h, wu_ref[:, sl])
        y = y + _dot3(a, wd_ref[sl, :])
    o_ref[...] = x_ref[...] + mod_ref[5] * y


def _ffn_s(h2, x, mod, wg, wu, wd):
    return pl.pallas_call(
        _ffn_s_body,
        out_shape=jax.ShapeDtypeStruct(x.shape, F32),
        compiler_params=pltpu.CompilerParams(vmem_limit_bytes=56 * MIB),
        name="ffn_s",
    )(h2, x, mod, wg, wu, wd)


def _outffn_p_body(tiles_per_batch, mix_ref, x_ref, mod_ref, w_ref, n2_ref, wg_ref, wu_ref, wd_ref, o_ref):
    b = pl.program_id(0) // tiles_per_batch
    mrow = lambda j: mod_ref[j, pl.ds(b, 1), :]
    xn, h2 = _outproj_compute(_dot_bf, mix_ref[...], x_ref[...], mrow(2), mrow(3), mrow(4), w_ref, n2_ref[...])
    o_ref[...] = xn + mrow(5) * _swiglu(h2.astype(BF), wg_ref, wu_ref, wd_ref)


def _outffn_p(l, mix, x, mod, w_bf, n2, wg, wu, wd, batch, seq):
    t = x.shape[0]
    tm = min(512, seq)
    row = lambda i: (i, 0)
    const = lambda shape: pl.BlockSpec(shape, lambda i: (0,) * len(shape), pipeline_mode=pl.Buffered(1))
    return pl.pallas_call(
        functools.partial(_outffn_p_body, seq // tm),
        grid=(t // tm,),
        in_specs=[
            pl.BlockSpec((tm, D_MODEL), row),
            pl.BlockSpec((tm, D_MODEL), row),
            const((N_ADA, batch, D_MODEL)),
            pl.BlockSpec((None, D_MODEL, D_MODEL), lambda i: (l, 0, 0), pipeline_mode=pl.Buffered(1)),
            const((1, D_MODEL)),
            const((D_MODEL, D_FF)), const((D_MODEL, D_FF)), const((D_FF, D_MODEL)),
        ],
        out_specs=pl.BlockSpec((tm, D_MODEL), row),
        out_shape=jax.ShapeDtypeStruct((t, D_MODEL), F32),
        compiler_params=_params(("arbitrary",), 56),
        name="outffn_p",
    )(mix, x, mod, w_bf, n2, wg, wu, wd)


def _moe_s_body(h_ref, x_ref, gates_ref, mod_ref, wg_ref, wu_ref, wd_ref, o_ref):
    e = pl.program_id(0)

    @pl.when(e == 0)
    def _():
        o_ref[...] = jnp.zeros_like(o_ref)

    lane = lax.broadcasted_iota(jnp.int32, (1, LANES), 1)
    gate = jnp.sum(jnp.where(lane == e, gates_ref[...], 0.0), axis=-1, keepdims=True)
    o_ref[...] += gate * _swiglu(h_ref[...].astype(BF), wg_ref, wu_ref, wd_ref)

    @pl.when(e == N_EXPERTS - 1)
    def _():
        o_ref[...] = x_ref[...] + mod_ref[5] * o_ref[...]


def _moe_s(h2, x, gates, mod, wg, wu, wd):
    n = x.shape[0]
    whole = lambda shape: pl.BlockSpec(shape, lambda e: (0,) * len(shape))
    wspec = lambda shape: pl.BlockSpec((None,) + shape, lambda e: (e, 0, 0))
    return pl.pallas_call(
        _moe_s_body,
        grid=(N_EXPERTS,),
        in_specs=[whole((n, D_MODEL)), whole((n, D_MODEL)), whole((n, LANES)), whole(mod.shape),
                  wspec((D_MODEL, D_FF)), wspec((D_MODEL, D_FF)), wspec((D_FF, D_MODEL))],
        out_specs=whole((n, D_MODEL)),
        out_shape=jax.ShapeDtypeStruct((n, D_MODEL), F32),
        compiler_params=_params(("arbitrary",), 56),
        name="moe_s",
    )(h2, x, gates, mod, wg, wu, wd)


TM_MOE = 512
TD = 512


def _to_token_tiles(ref, x):
    r = x.shape[0]
    for g in range(SUB):
        ref[pl.ds(g, r, stride=SUB), :] = x[:, g * LANES:(g + 1) * LANES]


def _from_token_tiles(ref, first, r):
    return jnp.concatenate([ref[pl.ds(first * SUB + g, r, stride=SUB), :] for g in range(SUB)], axis=-1)


def _token_copy(src, s, dst, d, sem):
    aligned = lambda v: v if isinstance(v, int) else pl.multiple_of(v, SUB)
    return pltpu.make_async_copy(src.at[pl.ds(aligned(s), SUB), :], dst.at[pl.ds(aligned(d), SUB), :], sem)


def _dispatch_body(pos_ref, pad_ref, h_ref, xs_ref, stage, sem, zsem):
    i = pl.program_id(0)
    n = pl.num_programs(0)
    td = h_ref.shape[0]
    slot = i % 2

    def wait_slot(s):
        for _ in range(2):
            pltpu.make_async_copy(stage.at[s], xs_ref.at[pl.ds(0, td * SUB), :], sem.at[s]).wait()

    @pl.when(i >= 2)
    def _():
        wait_slot(slot)

    _to_token_tiles(stage.at[slot], h_ref[...])

    def issue(r, c):
        for k in range(2):
            _token_copy(stage.at[slot], r * SUB, xs_ref, pos_ref[0, 0, k * td + r], sem.at[slot]).start(priority=k)
        return c

    lax.fori_loop(0, td, issue, 0, unroll=8)

    @pl.when(i == n - 1)
    def _():
        wait_slot(slot)

        @pl.when(n > 1)
        def _():
            wait_slot(1 - slot)

        stage[0] = jnp.zeros(stage.shape[1:], stage.dtype)
        for e in range(N_EXPERTS):
            lo = pad_ref[0, e]
            hi = pad_ref[1, e]

            def zero_token(r, c):
                _token_copy(stage.at[0], 0, xs_ref, r * SUB, zsem).start()
                return c

            def wait_token(r, c):
                _token_copy(stage.at[0], 0, xs_ref, 0, zsem).wait()
                return c

            lax.fori_loop(lo, hi, zero_token, 0)
            lax.fori_loop(lo, hi, wait_token, 0)

        def zero_blk(j, c):
            pltpu.make_async_copy(stage.at[0], xs_ref.at[pl.ds(pl.multiple_of(j * (td * SUB), SUB), td * SUB), :],
                                  zsem).start()
            return c

        def wait_blk(j, c):
            pltpu.make_async_copy(stage.at[0], xs_ref.at[pl.ds(0, td * SUB), :], zsem).wait()
            return c

        lax.fori_loop(pad_ref[0, N_EXPERTS], pad_ref[1, N_EXPERTS], zero_blk, 0)
        lax.fori_loop(pad_ref[0, N_EXPERTS], pad_ref[1, N_EXPERTS], wait_blk, 0)


def _dispatch(h2, pos_t, pad, npad):
    t = h2.shape[0]
    td = min(TD, t)
    return pl.pallas_call(
        _dispatch_body,
        grid=(t // td,),
        in_specs=[
            pl.BlockSpec((1, 1, 2 * td), lambda i: (i, 0, 0), memory_space=pltpu.SMEM),
            pl.BlockSpec(memory_space=pltpu.SMEM),
            pl.BlockSpec((td, D_MODEL), lambda i: (i, 0)),
        ],
        out_specs=pl.BlockSpec(memory_space=pl.ANY),
        out_shape=jax.ShapeDtypeStruct((npad * SUB, LANES), F32),
        scratch_shapes=[pltpu.VMEM((2, td * SUB, LANES), F32), pltpu.SemaphoreType.DMA((2,)),
                        pltpu.SemaphoreType.DMA(())],
        compiler_params=_params(("arbitrary",), 32),
        name="dispatch",
    )(pos_t, pad, h2)


def _moe_body(te_ref, src_ref, nv_ref, x_ref, wg_ref, wu_ref, wd_ref, o_ref):
    i = pl.program_id(0)

    @pl.when(nv_ref[i] > 0)
    def _():
        x = _from_token_tiles(x_ref, 0, TM_MOE).astype(BF)
        _to_token_tiles(o_ref, _swiglu(x, wg_ref, wu_ref, wd_ref))

    @pl.when(nv_ref[i] == 0)
    def _():
        o_ref[...] = jnp.zeros_like(o_ref)


def _moe(xs, tile_e, tile_src, tile_nv, wg, wu, wd):
    rows = TM_MOE * SUB
    wspec = lambda shape: pl.BlockSpec((None,) + shape, lambda i, te, src, nv: (te[i], 0, 0))
    return pl.pallas_call(
        _moe_body,
        grid_spec=pltpu.PrefetchScalarGridSpec(
            num_scalar_prefetch=3,
            grid=(xs.shape[0] // rows,),
            in_specs=[
                pl.BlockSpec((rows, LANES), lambda i, te, src, nv: (src[i], 0)),
                wspec((D_MODEL, D_FF)), wspec((D_MODEL, D_FF)), wspec((D_FF, D_MODEL)),
            ],
            out_specs=pl.BlockSpec((rows, LANES), lambda i, te, src, nv: (i, 0)),
        ),
        out_shape=jax.ShapeDtypeStruct(xs.shape, F32),
        compiler_params=_params(("arbitrary",), 56),
        name="moe",
    )(tile_e, tile_src, tile_nv, xs, wg, wu, wd)


def _combine_body(tiles_per_batch, posc_ref, posn_ref, x_ref, route_ref, mod_ref, ys_ref, o_ref, buf, sem):
    i = pl.program_id(0)
    n = pl.num_programs(0)
    tc = x_ref.shape[0]

    def gather(p_ref, s):
        def issue(r, c):
            for k in range(2):
                _token_copy(ys_ref, p_ref[0, 0, k * tc + r], buf.at[s], (k * tc + r) * SUB,
                            sem.at[s]).start(priority=k)
            return c

        lax.fori_loop(0, tc, issue, 0, unroll=8)

    @pl.when(i == 0)
    def _():
        gather(posc_ref, 0)

    @pl.when(i + 1 < n)
    def _():
        gather(posn_ref, (i + 1) % 2)

    slot = i % 2
    pltpu.make_async_copy(ys_ref.at[pl.ds(0, 2 * tc * SUB), :], buf.at[slot], sem.at[slot]).wait()
    lane = lax.broadcasted_iota(jnp.int32, (1, LANES), 1)
    rt = route_ref[...]
    g1 = jnp.sum(jnp.where(lane == ROUTE_GATE, rt, 0.0), axis=-1, keepdims=True)
    g2 = jnp.sum(jnp.where(lane == ROUTE_GATE + 1, rt, 0.0), axis=-1, keepdims=True)
    y = g1 * _from_token_tiles(buf.at[slot], 0, tc) + g2 * _from_token_tiles(buf.at[slot], tc, tc)
    ga2 = mod_ref[5, pl.ds(i // tiles_per_batch, 1), :]
    o_ref[...] = x_ref[...] + ga2 * y


def _combine(ys, pos_t, x, route, mod, seq):
    t = x.shape[0]
    tc = min(TD, t)
    nt = t // tc
    row = lambda i: (i, 0)
    return pl.pallas_call(
        functools.partial(_combine_body, seq // tc),
        grid=(nt,),
        in_specs=[
            pl.BlockSpec((1, 1, 2 * tc), lambda i: (i, 0, 0), memory_space=pltpu.SMEM),
            pl.BlockSpec((1, 1, 2 * tc), lambda i: (jnp.minimum(i + 1, nt - 1), 0, 0), memory_space=pltpu.SMEM),
            pl.BlockSpec((tc, D_MODEL), row),
            pl.BlockSpec((tc, LANES), row),
            pl.BlockSpec(mod.shape, lambda i: (0, 0, 0)),
            pl.BlockSpec(memory_space=pl.ANY),
        ],
        out_specs=pl.BlockSpec((tc, D_MODEL), row),
        out_shape=jax.ShapeDtypeStruct((t, D_MODEL), F32),
        scratch_shapes=[pltpu.VMEM((2, 2 * tc * SUB, LANES), F32), pltpu.SemaphoreType.DMA((2,))],
        compiler_params=_params(("arbitrary",), 32),
        name="combine",
    )(pos_t, pos_t, x, route, mod, ys)


def _moe_routed(h2, xn, route, route_t, cnt, mod, wg, wu, wd, seq):
    t = h2.shape[0]
    td = min(TD, t)
    nt_max = pl.cdiv(2 * t, TM_MOE) + N_EXPERTS
    npad = nt_max * TM_MOE
    counts = cnt[0, :N_EXPERTS].astype(jnp.int32)
    ntile = (counts + TM_MOE - 1) // TM_MOE
    eid = jnp.arange(N_EXPERTS)
    tile_end = jnp.sum(jnp.where(eid[None, :] <= eid[:, None], ntile[None, :], 0), axis=1)
    off = (tile_end - ntile) * TM_MOE
    e12 = route_t[ROUTE_E:ROUTE_E + 2].astype(jnp.int32)
    r12 = route_t[ROUTE_RANK:ROUTE_RANK + 2].astype(jnp.int32)
    onehot = e12[:, :, None] == eid[None, None, :]
    pos = jnp.sum(jnp.where(onehot, off[None, None, :], 0), axis=-1) + r12
    pos_t = (pos * SUB).reshape(2, t // td, td).transpose(1, 0, 2).reshape(t // td, 1, 2 * td)
    total = tile_end[-1]
    tid = jnp.arange(nt_max)
    tile_e = jnp.minimum(jnp.sum(tid[:, None] >= tile_end[None, :], axis=1), N_EXPERTS - 1).astype(jnp.int32)
    tile_nv = (tid < total).astype(jnp.int32)
    tile_src = jnp.minimum(tid, total - 1).astype(jnp.int32)
    pad = jnp.stack([jnp.concatenate([off + counts, (total * (TM_MOE // td))[None]]),
                     jnp.concatenate([off + ntile * TM_MOE, jnp.full((1,), npad // td, jnp.int32)])]).astype(jnp.int32)
    xs = _dispatch(h2, pos_t, pad, npad)
    ys = _moe(xs, tile_e, tile_src, tile_nv, wg, wu, wd)
    return _combine(ys, pos_t, xn, route, mod, seq)


def _rope_tables(pos):
    inv = ROPE_THETA ** (-np.arange(0, HEAD_DIM, 2, dtype=np.float64) / HEAD_DIM)
    ang = np.asarray(pos, np.float64)[:, None] * inv[None, :]
    cos = np.concatenate([np.cos(ang), np.cos(ang)], axis=-1)
    sin = np.concatenate([-np.sin(ang), np.sin(ang)], axis=-1)
    reps = LANES // HEAD_DIM
    return (jnp.asarray(np.tile(cos, (1, reps)), F32), jnp.asarray(np.tile(sin, (1, reps)), F32))


def kernel(x_prompt, x_sample, cache_k, cache_v, c_prompt, c_sample, w_ada, b_ada, norm1_w, norm2_w, w_in,
           q_norm_w, k_norm_w, attn_sinks, gm_norm_w, gm_ws, gm_bs, w_out, dense_w_gate, dense_w_up,
           dense_w_down, router_w, router_b, moe_w_gate, moe_w_up, moe_w_down):
    batch, seq, d = x_prompt.shape
    nd = x_sample.shape[0]
    depth = w_in.shape[0]
    t = batch * seq

    mod = _ada(jnp.concatenate([c_prompt, c_sample], axis=0), w_ada, b_ada)
    mod_p = mod[:, :batch].reshape(depth, batch, N_ADA, d).transpose(0, 2, 1, 3)
    mod_s = mod[:, batch:].reshape(depth, nd, N_ADA, d).transpose(0, 2, 1, 3)

    cos_p, sin_p = _rope_tables(np.arange(seq))
    cos_s, sin_s = _rope_tables(np.array([PAST_LEN]))
    head_of = np.arange(ATTN_WIDTH) // HEAD_DIM
    seg = jnp.asarray(head_of[:, None] == head_of[None, :], BF)
    pairs = nd * N_KV_HEADS
    w = cache_k.shape[2]
    ck_t = cache_k.transpose(0, 1, 3, 4, 2).reshape(depth, pairs, HEAD_DIM, w)
    cv_t = cache_v.transpose(0, 1, 3, 4, 2).reshape(depth, pairs, HEAD_DIM, w)
    uncache = lambda c: jnp.stack(c).reshape(depth, nd, N_KV_HEADS, HEAD_DIM, w).transpose(0, 1, 4, 2, 3)

    w_in_bf = w_in.astype(BF)
    w_out_bf = w_out.astype(BF)
    router_w_pad = jnp.pad(router_w, ((0, 0), (0, 0), (0, LANES - N_EXPERTS)))
    router_b_pad = jnp.pad(router_b, ((0, 0), (0, LANES - N_EXPERTS)), constant_values=NEG_INF)

    cast_plan = []
    for l in range(depth):
        ws = (dense_w_gate, dense_w_up, dense_w_down) if l % 2 == 0 else (moe_w_gate, moe_w_up, moe_w_down)
        cast_plan += [(l - 1 if (l % 2 == 1 and j == 0) else l, (l, j), wj[l // 2]) for j, wj in enumerate(ws)]
    ffn_bf = {}

    xp = x_prompt.reshape(t, d)
    xs = x_sample.reshape(nd, d)
    k_p, v_p, g_p, k_s, v_s, g_s = [], [], [], [], [], []
    for l in range(depth):
        i = l // 2
        n1 = norm1_w[l][None, :]
        n2 = norm2_w[l][None, :]
        qn = jnp.tile(q_norm_w[l], N_HEADS)[None, :]
        kn = jnp.tile(k_norm_w[l], N_KV_HEADS)[None, :]
        gmn = gm_norm_w[l][None, :]
        router = None if l % 2 == 0 else (router_w_pad[i], router_b_pad[i][None, :])

        pending = [c for c in cast_plan if c[0] == l]
        res = _inproj_p(l, xp, mod_p[l], n1, w_in_bf, qn, kn, gmn, seg, cos_p, sin_p,
                        [wt.reshape(-1, wt.shape[-1]) for _, _, wt in pending], batch, seq)
        q, kd, vd, u, gv, kl, vl, gvl = res[:8]
        for (_, name, wt), c in zip(pending, res[8:]):
            ffn_bf[name] = c.reshape(wt.shape)
        wg_bf, wu_bf, wd_bf = (ffn_bf[(l, j)] for j in range(3))
        mix = _mix_p(q, kd, vd, u, gv, gm_ws[l], gm_bs[l].T, attn_sinks[l], batch, seq)
        if router is None:
            xp = _outffn_p(l, mix, xp, mod_p[l], w_out_bf, n2, wg_bf, wu_bf, wd_bf, batch, seq)
        else:
            xn, h2, route, route_t, cnt = _outproj_p(l, mix, xp, mod_p[l], w_out_bf, n2, *router, batch, seq)
            xp = _moe_routed(h2, xn, route, route_t, cnt, mod_p[l], wg_bf, wu_bf, wd_bf, seq)
        k_p.append(kl.reshape(batch, WINDOW, N_KV_HEADS, HEAD_DIM))
        v_p.append(vl.reshape(batch, WINDOW, N_KV_HEADS, HEAD_DIM))
        g_p.append(gvl)

        q, k, v, u, gv = _inproj_s(xs, mod_s[l], n1, w_in[l], qn, kn, gmn, seg, cos_s, sin_s)
        qg = jnp.pad(q.reshape(pairs, KV_GROUP, HEAD_DIM), ((0, 0), (0, SUB - KV_GROUP), (0, 0)))
        sink = jnp.pad(jnp.tile(attn_sinks[l].reshape(N_KV_HEADS, KV_GROUP), (nd, 1)),
                       ((0, 0), (0, SUB - KV_GROUP)))[:, :, None]
        per_step = lambda a: a.T.reshape(KV_WIDTH, -1, ATTN_S_PAIRS // N_KV_HEADS).transpose(1, 0, 2)
        o, nk, nv = _attn_s(l, qg, per_step(k), per_step(v), ck_t, cv_t, sink)
        o = o[:, :KV_GROUP, :].reshape(nd, ATTN_WIDTH)
        wdiag = jnp.repeat(gm_ws[l][:, 0, 0], GM_WIDTH // GM_GROUPS)[None, :]
        bsrow = jnp.repeat(gm_bs[l][:, 0], GM_WIDTH // GM_GROUPS)[None, :]
        res = _outproj_s(o, u, gv, wdiag, bsrow, xs, mod_s[l], w_out[l], n2, router)
        if router is None:
            xs = _ffn_s(res[1], res[0], mod_s[l], dense_w_gate[i], dense_w_up[i], dense_w_down[i])
        else:
            xs = _moe_s(res[1], res[0], res[2], mod_s[l], wg_bf, wu_bf, wd_bf)
        k_s.append(nk)
        v_s.append(nv)
        g_s.append(gv[:, None, :])

    return (xp.reshape(batch, seq, d), xs.reshape(nd, 1, d), jnp.stack(k_p), jnp.stack(v_p), jnp.stack(g_p),
            uncache(k_s), uncache(v_s), jnp.stack(g_s))
```

```python
import functools

import numpy as np
import jax
import jax.numpy as jnp
from jax import lax
from jax.experimental import pallas as pl
from jax.experimental.pallas import tpu as pltpu

D_MODEL = 1024
HEAD_DIM = 64
N_HEADS = 8
N_KV_HEADS = 2
KV_GROUP = N_HEADS // N_KV_HEADS
ATTN_WIDTH = N_HEADS * HEAD_DIM
KV_WIDTH = N_KV_HEADS * HEAD_DIM
GM_WIDTH = 512
GM_GROUPS = 4
WINDOW = 128
CHUNK = 128
D_FF = 2816
MXU_DIM = 256
FF_SPLIT = (0, 6 * MXU_DIM, D_FF)
N_EXPERTS = 8
N_ADA = 6
IN_COLS = ATTN_WIDTH + 2 * KV_WIDTH + 2 * GM_WIDTH
PAST_LEN = 16384
ROPE_THETA = 10000.0
EPS = 1e-6
NEG_INF = -1e30
LOG2E = 1.4426950408889634
LANES = 128
SUB = 8
assert D_MODEL == SUB * LANES

BF = jnp.bfloat16
F32 = jnp.float32
MIB = 1024 * 1024


ROW_GROUP = 256


def _params(sem, vmem_mib):
    return pltpu.CompilerParams(dimension_semantics=sem, vmem_limit_bytes=vmem_mib * MIB)


def _dot(a, b):
    return jnp.dot(a, b, preferred_element_type=F32)


def _dot_nt(a, b):
    return lax.dot_general(a, b, (((1,), (1,)), ((), ())), preferred_element_type=F32)


def _split(a):
    hi = a.astype(BF)
    return hi, (a - hi.astype(F32)).astype(BF)


def _dot_bf(a, w):
    return _dot(a.astype(BF), w)


def _dot3(a, w):
    ah, al = _split(a)
    if w.dtype == BF:
        return _dot(ah, w) + _dot(al, w)
    wh, wl = _split(w)
    return _dot(ah, wh) + _dot(al, wh) + _dot(ah, wl)


def _rms(x, w):
    ms = jnp.mean(x * x, axis=-1, keepdims=True)
    return x * lax.rsqrt(ms + EPS) * w


def _ada_body(c_ref, w_ref, b_ref, o_ref):
    o_ref[...] = _dot3(jax.nn.silu(c_ref[...]), w_ref[...]) + b_ref[...]


def _ada(c_all, w_ada, b_ada):
    depth, d, cols = w_ada.shape
    n = c_all.shape[0]
    tn = 1024
    return pl.pallas_call(
        _ada_body,
        grid=(depth, cols // tn),
        in_specs=[
            pl.BlockSpec((n, d), lambda l, j: (0, 0)),
            pl.BlockSpec((None, d, tn), lambda l, j: (l, 0, j)),
            pl.BlockSpec((None, 1, tn), lambda l, j: (l, 0, j)),
        ],
        out_specs=pl.BlockSpec((None, n, tn), lambda l, j: (l, 0, j)),
        out_shape=jax.ShapeDtypeStruct((depth, n, cols), F32),
        compiler_params=_params(("arbitrary", "arbitrary"), 32),
        name="ada",
    )(c_all, w_ada, b_ada.reshape(depth, 1, cols))


def _swap_halves(t):
    n = t.shape[-1]
    lane = lax.broadcasted_iota(jnp.int32, (1, n), 1)
    first = (lane % HEAD_DIM) < (HEAD_DIM // 2)
    return jnp.where(first, pltpu.roll(t, n - HEAD_DIM // 2, axis=1), pltpu.roll(t, HEAD_DIM // 2, axis=1))


def _inproj_compute(mm, x, sh, sc, n1, w_ref, qn, kn, gmn, seg_ref, cos, sin):
    h = _rms(x, n1) * (1.0 + sc) + sh
    z = mm(h, w_ref[...])
    q = z[:, :ATTN_WIDTH]
    k = z[:, ATTN_WIDTH:ATTN_WIDTH + KV_WIDTH]
    v = z[:, ATTN_WIDTH + KV_WIDTH:ATTN_WIDTH + 2 * KV_WIDTH]
    gm = z[:, ATTN_WIDTH + 2 * KV_WIDTH:]

    def head_norm(t, seg, wn):
        ms = mm(t * t, seg) * (1.0 / HEAD_DIM)
        return t * lax.rsqrt(ms + EPS) * wn

    def rope(t):
        reps = t.shape[-1] // LANES
        c = jnp.concatenate([cos] * reps, axis=-1) if reps > 1 else cos
        s = jnp.concatenate([sin] * reps, axis=-1) if reps > 1 else sin
        return t * c + _swap_halves(t) * s

    q = rope(head_norm(q, seg_ref[...], qn)) * (HEAD_DIM ** -0.5 * LOG2E)
    k = rope(head_norm(k, seg_ref[:KV_WIDTH, :KV_WIDTH], kn))
    g = jax.nn.gelu(gm)
    u = g[:, :GM_WIDTH]
    gv = _rms(g[:, GM_WIDTH:], gmn)
    return q, k, v, u, gv


def _dup_heads(t):
    lane = lax.broadcasted_iota(jnp.int32, (1, LANES), 1)
    lo = lane < HEAD_DIM
    r = pltpu.roll(t, HEAD_DIM, axis=1)
    return jnp.concatenate([jnp.where(lo, t, r), jnp.where(lo, r, t)], axis=-1)


def _inproj_p_body(tiles_per_batch, ncast, x_ref, mod_ref, n1_ref, w_ref, qn_ref, kn_ref, gmn_ref, seg_ref,
                   cos_ref, sin_ref, *rest):
    cast_src = rest[:ncast]
    q_ref, kd_ref, vd_ref, u_ref, gv_ref, kl_ref, vl_ref, gvl_ref = rest[ncast:ncast + 8]
    cast_dst = rest[ncast + 8:]
    for src, dst in zip(cast_src, cast_dst):
        dst[...] = src[...].astype(BF)
    i = pl.program_id(0)
    b = i // tiles_per_batch
    sh = mod_ref[0, pl.ds(b, 1), :]
    sc = mod_ref[1, pl.ds(b, 1), :]
    hs = min(ROW_GROUP, x_ref.shape[0])
    for hh in range(x_ref.shape[0] // hs):
        rs = slice(hh * hs, (hh + 1) * hs)
        q, k, v, u, gv = _inproj_compute(_dot_bf, x_ref[rs, :], sh, sc, n1_ref[...], w_ref, qn_ref[...], kn_ref[...],
                                         gmn_ref[...], seg_ref, cos_ref[rs, :], sin_ref[rs, :])
        q_ref[rs, :] = q.astype(BF)
        kd_ref[rs, :] = _dup_heads(k).astype(BF)
        vd_ref[rs, :] = _dup_heads(v).astype(BF)
        u_ref[rs, :] = u.astype(BF)
        gv_ref[rs, :] = gv.astype(BF)

    @pl.when(i % tiles_per_batch == tiles_per_batch - 1)
    def _():
        kl_ref[...] = k[hs - WINDOW:, :]
        vl_ref[...] = v[hs - WINDOW:, :]
        gvl_ref[...] = gv[hs - CHUNK:, :]


BF16_SUBLANES = 16


def _slab_spec(rows, cols, steps):
    s = steps
    while rows % s or (rows // s) % BF16_SUBLANES:
        s //= 2
    return pl.BlockSpec((rows // s, cols), lambda i: (i // (steps // s), 0))


def _inproj_p(l, x, mod, n1, w_bf, qn, kn, gmn, seg, cos, sin, casts, batch, seq):
    t = x.shape[0]
    tm = min(1024, seq)
    tpb = seq // tm
    steps = t // tm
    row = lambda i: (i, 0)
    full = lambda i: (0, 0)
    last = lambda i: (i // tpb, 0, 0)
    cast_specs = [_slab_spec(c.shape[0], c.shape[1], steps) for c in casts]
    return pl.pallas_call(
        functools.partial(_inproj_p_body, tpb, len(casts)),
        grid=(steps,),
        in_specs=[
            pl.BlockSpec((tm, D_MODEL), row),
            pl.BlockSpec((N_ADA, batch, D_MODEL), lambda i: (0, 0, 0)),
            pl.BlockSpec((1, D_MODEL), full),
            pl.BlockSpec((None, D_MODEL, IN_COLS), lambda i: (l, 0, 0)),
            pl.BlockSpec((1, ATTN_WIDTH), full),
            pl.BlockSpec((1, KV_WIDTH), full),
            pl.BlockSpec((1, GM_WIDTH), full),
            pl.BlockSpec((ATTN_WIDTH, ATTN_WIDTH), full),
            pl.BlockSpec((tm, LANES), lambda i: (i % tpb, 0)),
            pl.BlockSpec((tm, LANES), lambda i: (i % tpb, 0)),
        ] + cast_specs,
        out_specs=[
            pl.BlockSpec((tm, ATTN_WIDTH), row),
            pl.BlockSpec((tm, 2 * KV_WIDTH), row),
            pl.BlockSpec((tm, 2 * KV_WIDTH), row),
            pl.BlockSpec((tm, GM_WIDTH), row),
            pl.BlockSpec((tm, GM_WIDTH), row),
            pl.BlockSpec((None, WINDOW, KV_WIDTH), last),
            pl.BlockSpec((None, WINDOW, KV_WIDTH), last),
            pl.BlockSpec((None, CHUNK, GM_WIDTH), last),
        ] + cast_specs,
        out_shape=[
            jax.ShapeDtypeStruct((t, ATTN_WIDTH), BF),
            jax.ShapeDtypeStruct((t, 2 * KV_WIDTH), BF),
            jax.ShapeDtypeStruct((t, 2 * KV_WIDTH), BF),
            jax.ShapeDtypeStruct((t, GM_WIDTH), BF),
            jax.ShapeDtypeStruct((t, GM_WIDTH), BF),
            jax.ShapeDtypeStruct((batch, WINDOW, KV_WIDTH), F32),
            jax.ShapeDtypeStruct((batch, WINDOW, KV_WIDTH), F32),
            jax.ShapeDtypeStruct((batch, CHUNK, GM_WIDTH), F32),
        ] + [jax.ShapeDtypeStruct(c.shape, BF) for c in casts],
        compiler_params=_params(("arbitrary",), 56),
        name="inproj_p",
    )(x, mod, n1, w_bf, qn, kn, gmn, seg, cos, sin, *casts)


def _inproj_s_body(x_ref, mod_ref, n1_ref, w_ref, qn_ref, kn_ref, gmn_ref, seg_ref, cos_ref, sin_ref,
                   q_ref, k_ref, v_ref, u_ref, gv_ref):
    q, k, v, u, gv = _inproj_compute(_dot3, x_ref[...], mod_ref[0], mod_ref[1], n1_ref[...], w_ref, qn_ref[...],
                                     kn_ref[...], gmn_ref[...], seg_ref, cos_ref[...], sin_ref[...])
    q_ref[...] = q
    k_ref[...] = k
    v_ref[...] = v
    u_ref[...] = u
    gv_ref[...] = gv


def _whole(a):
    return pl.BlockSpec(a.shape, lambda i: (0,) * a.ndim)


def _layer_of(w, l):
    return pl.BlockSpec((None,) + w.shape[1:], lambda i: (l,) + (0,) * (w.ndim - 1))


def _inproj_s(l, x, mod, n1, w, qn, kn, gmn, seg, cos, sin):
    n = x.shape[0]
    widths = (ATTN_WIDTH, KV_WIDTH, KV_WIDTH, GM_WIDTH, GM_WIDTH)
    args = (x, mod, n1, w, qn, kn, gmn, seg, cos, sin)
    return pl.pallas_call(
        _inproj_s_body,
        grid=(1,),
        in_specs=[_layer_of(a, l) if a is w else _whole(a) for a in args],
        out_specs=[pl.BlockSpec((n, c), lambda i: (0, 0)) for c in widths],
        out_shape=[jax.ShapeDtypeStruct((n, c), F32) for c in widths],
        compiler_params=_params(("arbitrary",), 48),
        name="inproj_s",
    )(*args)


def _mix_p_body(nblk, q_ref, kc_ref, kp_ref, vc_ref, vp_ref, u_ref, gv_ref, ws_ref, bst_ref, sink_ref, o_ref):
    i = pl.program_id(1)
    blk = WINDOW
    lane = lax.broadcasted_iota(jnp.int32, (1, LANES), 1)
    lo = lane < HEAD_DIM
    cols = KV_GROUP * blk
    iq = lax.broadcasted_iota(jnp.int32, (2 * blk, cols), 1) % blk
    jk = lax.broadcasted_iota(jnp.int32, (2 * blk, cols), 0)
    band = (jk > iq) & (jk <= iq + blk)
    bias = jnp.where(band, 0.0, NEG_INF)
    bias_first = jnp.where(band & ((jk >= blk) | (i > 0)), 0.0, NEG_INF)
    tri = (lax.broadcasted_iota(jnp.int32, (CHUNK, CHUNK), 0)
           >= lax.broadcasted_iota(jnp.int32, (CHUNK, CHUNK), 1))
    wm = [jnp.where(tri, ws_ref[g], 0.0).astype(BF) for g in range(GM_GROUPS)]

    for n in range(nblk):
        r0 = n * blk
        if n == 0:
            kk = jnp.concatenate([kp_ref[...], kc_ref[0:blk, :]], axis=0)
            vv = jnp.concatenate([vp_ref[...], vc_ref[0:blk, :]], axis=0)
            mask_bias = bias_first
        else:
            kk = kc_ref[r0 - blk:r0 + blk, :]
            vv = vc_ref[r0 - blk:r0 + blk, :]
            mask_bias = bias
        for kvh in range(N_KV_HEADS):
            c0 = 2 * kvh
            qa = q_ref[r0:r0 + blk, c0 * LANES:(c0 + 1) * LANES]
            qb = q_ref[r0:r0 + blk, (c0 + 1) * LANES:(c0 + 2) * LANES]
            zero = jnp.zeros_like(qa)
            qq = jnp.concatenate([jnp.where(lo, qa, zero), jnp.where(lo, zero, qa),
                                  jnp.where(lo, qb, zero), jnp.where(lo, zero, qb)], axis=0)
            s = _dot_nt(kk[:, kvh * LANES:(kvh + 1) * LANES], qq) + mask_bias
            sink = jnp.concatenate(
                [jnp.full((1, blk), sink_ref[kvh * KV_GROUP + g] * LOG2E, F32) for g in range(KV_GROUP)], axis=1)
            m = jnp.maximum(jnp.max(s, axis=0, keepdims=True), sink)
            p = jnp.exp2(s - m)
            den = jnp.sum(p, axis=0, keepdims=True) + jnp.exp2(sink - m)
            p = (p * (1.0 / den)).astype(BF)
            o = lax.dot_general(p, vv[:, kvh * LANES:(kvh + 1) * LANES], (((0,), (0,)), ((), ())),
                                preferred_element_type=F32)
            o_ref[r0:r0 + blk, c0 * LANES:(c0 + 1) * LANES] = jnp.where(
                lo, o[0:blk], o[blk:2 * blk]).astype(BF)
            o_ref[r0:r0 + blk, (c0 + 1) * LANES:(c0 + 2) * LANES] = jnp.where(
                lo, o[2 * blk:3 * blk], o[3 * blk:4 * blk]).astype(BF)
        for g in range(GM_GROUPS):
            cs = slice(g * LANES, (g + 1) * LANES)
            sp = _dot(wm[g], gv_ref[r0:r0 + blk, cs]) + bst_ref[:, g:g + 1]
            o_ref[r0:r0 + blk, ATTN_WIDTH + g * LANES:ATTN_WIDTH + (g + 1) * LANES] = (
                u_ref[r0:r0 + blk, cs].astype(F32) * sp).astype(BF)


def _mix_p(q, kd, vd, u, gv, ws, bst, sinks, batch, seq):
    t = q.shape[0]
    tq = min(1024, seq)
    nblk = tq // WINDOW
    tpb = seq // tq
    cur = lambda b, i: (b * tpb + i, 0)
    prev = lambda b, i: (jnp.maximum((b * tpb + i) * nblk - 1, b * tpb * nblk), 0)
    return pl.pallas_call(
        functools.partial(_mix_p_body, nblk),
        grid=(batch, tpb),
        in_specs=[
            pl.BlockSpec((tq, ATTN_WIDTH), cur),
            pl.BlockSpec((tq, 2 * KV_WIDTH), cur),
            pl.BlockSpec((WINDOW, 2 * KV_WIDTH), prev),
            pl.BlockSpec((tq, 2 * KV_WIDTH), cur),
            pl.BlockSpec((WINDOW, 2 * KV_WIDTH), prev),
            pl.BlockSpec((tq, GM_WIDTH), cur),
            pl.BlockSpec((tq, GM_WIDTH), cur),
            pl.BlockSpec((GM_GROUPS, CHUNK, CHUNK), lambda b, i: (0, 0, 0)),
            pl.BlockSpec((CHUNK, GM_GROUPS), lambda b, i: (0, 0)),
            pl.BlockSpec(memory_space=pltpu.SMEM),
        ],
        out_specs=pl.BlockSpec((tq, D_MODEL), cur),
        out_shape=jax.ShapeDtypeStruct((t, D_MODEL), BF),
        compiler_params=_params(("arbitrary", "arbitrary"), 48),
        name="mix_p",
    )(q, kd, kd, vd, vd, u, gv, ws, bst, sinks)


ATTN_S_PAIRS = 32

def _attn_s_body(q_ref, kn_ref, vn_ref, ck_ref, cv_ref, sink_ref, o_ref, nk_ref, nv_ref):
    nb, hd, w = ck_ref.shape
    pos = lax.broadcasted_iota(jnp.int32, (1, w), 1)
    for p in range(nb):
        b, kvh = divmod(p, N_KV_HEADS)
        for new_ref, c_ref, n_ref in ((kn_ref, ck_ref, nk_ref), (vn_ref, cv_ref, nv_ref)):
            new = new_ref[kvh * hd:(kvh + 1) * hd, b:b + 1]
            n_ref[p] = jnp.where(pos == w - 1, new, pltpu.roll(c_ref[p], w - 1, axis=1))
    nk = nk_ref[...]
    nv = nv_ref[...]

    def bmm3(spec, a, b):
        (ah, al), (bh, bl) = _split(a), _split(b)
        mm = lambda x, y: jnp.einsum(spec, x, y, preferred_element_type=F32)
        return mm(ah, bh) + mm(al, bh) + mm(ah, bl)

    s = bmm3('ngd,ndj->ngj', q_ref[...], nk)
    sink = sink_ref[...] * LOG2E
    m = jnp.maximum(jnp.max(s, axis=-1, keepdims=True), sink)
    p = jnp.exp2(s - m)
    den = jnp.sum(p, axis=-1, keepdims=True) + jnp.exp2(sink - m)
    o_ref[...] = bmm3('ngj,ndj->ngd', p, nv) * (1.0 / den)


def _attn_s(l, q, k_new, v_new, ck, cv, sink):
    _, n, hd, w = ck.shape
    nb = ATTN_S_PAIRS
    assert n % nb == 0
    rows = q.shape[1]
    blk = lambda r, c: pl.BlockSpec((nb, r, c), lambda i: (i, 0, 0))
    cache = pl.BlockSpec((None, nb, hd, w), lambda i: (l, i, 0, 0))
    new = pl.BlockSpec((None,) + k_new.shape[1:], lambda i: (i, 0, 0))
    return pl.pallas_call(
        _attn_s_body,
        grid=(n // nb,),
        in_specs=[blk(rows, hd), new, new, cache, cache, blk(rows, 1)],
        out_specs=[blk(rows, hd), blk(hd, w), blk(hd, w)],
        out_shape=[jax.ShapeDtypeStruct((n, rows, hd), F32),
                   jax.ShapeDtypeStruct((n, hd, w), F32),
                   jax.ShapeDtypeStruct((n, hd, w), F32)],
        compiler_params=_params(("arbitrary",), 32),
        name="attn_s",
    )(q, k_new, v_new, ck, cv, sink)


def _top2(h2, rw_ref, rb_ref):
    hh, hl = _split(h2)
    wh, wl = _split(rw_ref[...])
    both = _dot(hh, jnp.concatenate([wh, wl], axis=1))
    logits = both[:, :LANES] + _dot(hl, wh) + both[:, LANES:] + rb_ref[...]
    lane = lax.broadcasted_iota(jnp.int32, logits.shape, 1).astype(F32)
    e = jnp.exp(logits - jnp.max(logits, axis=-1, keepdims=True))
    p = e / jnp.sum(e, axis=-1, keepdims=True)
    m1 = jnp.max(p, axis=-1, keepdims=True)
    i1 = jnp.min(jnp.where(p == m1, lane, float(LANES)), axis=-1, keepdims=True)
    p2 = jnp.where(lane == i1, -1.0, p)
    m2 = jnp.max(p2, axis=-1, keepdims=True)
    i2 = jnp.min(jnp.where(p2 == m2, lane, float(LANES)), axis=-1, keepdims=True)
    tot = m1 + m2
    return lane, i1, i2, m1 / tot, m2 / tot


def _route_gates(h2, rw_ref, rb_ref):
    lane, i1, i2, g1, g2 = _top2(h2, rw_ref, rb_ref)
    return jnp.where(lane == i1, g1, 0.0) + jnp.where(lane == i2, g2, 0.0)


ROUTE_E, ROUTE_RANK, ROUTE_GATE = 0, 2, 4


def _route_ranked(h2, rw_ref, rb_ref, tri_ref, cnt_ref):
    lane, i1, i2, g1, g2 = _top2(h2, rw_ref, rb_ref)
    oh1 = lane == i1
    oh2 = lane == i2
    hit = jnp.where(oh1, 1.0, 0.0) + jnp.where(oh2, 1.0, 0.0)
    before = cnt_ref[...] + _dot(tri_ref[...], hit.astype(BF))
    r1 = jnp.sum(jnp.where(oh1, before, 0.0), axis=-1, keepdims=True)
    r2 = jnp.sum(jnp.where(oh2, before, 0.0), axis=-1, keepdims=True)
    cnt_ref[...] += jnp.sum(hit, axis=0, keepdims=True)
    cols = (i1, i2, r1, r2, g1, g2)
    out = jnp.zeros_like(lane)
    for j, c in enumerate(cols):
        out = jnp.where(lane == float(j), c, out)
    return out


def _outproj_compute(mm, mix, x, ga1, sh2, sc2, w_ref, n2):
    xn = x + ga1 * mm(mix, w_ref[...])
    h2 = _rms(xn, n2) * (1.0 + sc2) + sh2
    return xn, h2


def _outproj_p_body(tiles_per_batch, mix_ref, x_ref, mod_ref, w_ref, n2_ref, rw_ref, rb_ref, tri_ref,
                    xn_ref, h2_ref, route_ref, route_t_ref, cnt_ref):
    i = pl.program_id(0)
    b = i // tiles_per_batch
    mrow = lambda j: mod_ref[j, pl.ds(b, 1), :]

    @pl.when(i == 0)
    def _():
        cnt_ref[...] = jnp.zeros_like(cnt_ref)

    xn, h2 = _outproj_compute(_dot_bf, mix_ref[...], x_ref[...], mrow(2), mrow(3), mrow(4), w_ref, n2_ref[...])
    route = _route_ranked(h2, rw_ref, rb_ref, tri_ref, cnt_ref)
    route_ref[...] = route
    route_t_ref[...] = route.T[:SUB, :]
    h2_ref[...] = h2
    xn_ref[...] = xn


def _outproj_p(l, mix, x, mod, w_bf, n2, rw, rb, batch, seq):
    t = x.shape[0]
    tm = min(512, seq)
    row = lambda i: (i, 0)
    full = lambda i: (0, 0)
    return pl.pallas_call(
        functools.partial(_outproj_p_body, seq // tm),
        grid=(t // tm,),
        in_specs=[
            pl.BlockSpec((tm, D_MODEL), row),
            pl.BlockSpec((tm, D_MODEL), row),
            pl.BlockSpec((N_ADA, batch, D_MODEL), lambda i: (0, 0, 0)),
            pl.BlockSpec((None, D_MODEL, D_MODEL), lambda i: (l, 0, 0)),
            pl.BlockSpec((1, D_MODEL), full),
            pl.BlockSpec((D_MODEL, LANES), full),
            pl.BlockSpec((1, LANES), full),
            pl.BlockSpec((tm, tm), full),
        ],
        out_specs=[pl.BlockSpec((tm, D_MODEL), row), pl.BlockSpec((tm, D_MODEL), row),
                   pl.BlockSpec((tm, LANES), row), pl.BlockSpec((SUB, tm), lambda i: (0, i)),
                   pl.BlockSpec((1, LANES), full)],
        out_shape=[jax.ShapeDtypeStruct((t, D_MODEL), F32), jax.ShapeDtypeStruct((t, D_MODEL), F32),
                   jax.ShapeDtypeStruct((t, LANES), F32), jax.ShapeDtypeStruct((SUB, t), F32),
                   jax.ShapeDtypeStruct((1, LANES), F32)],
        compiler_params=_params(("arbitrary",), 48),
        name="outproj_p",
    )(mix, x, mod, w_bf, n2, rw, rb, jnp.asarray(np.tri(tm, k=-1), BF))


def _outproj_s_body(with_router, o_ref, u_ref, gv_ref, wdiag_ref, bsrow_ref, x_ref, mod_ref, w_ref, n2_ref, *rest):
    gate = u_ref[...] * (wdiag_ref[...] * gv_ref[...] + bsrow_ref[...])
    mix = jnp.concatenate([o_ref[...], gate], axis=-1)
    xn, h2 = _outproj_compute(_dot3, mix, x_ref[...], mod_ref[2], mod_ref[3], mod_ref[4], w_ref, n2_ref[...])
    if with_router:
        rw_ref, rb_ref, xn_ref, h2_ref, gates_ref = rest
        gates_ref[...] = _route_gates(h2, rw_ref, rb_ref)
    else:
        xn_ref, h2_ref = rest
    xn_ref[...] = xn
    h2_ref[...] = h2


def _outproj_s(l, o, u, gv, wdiag, bsrow, x, mod, w, n2, router):
    n = x.shape[0]
    out_shape = [jax.ShapeDtypeStruct((n, D_MODEL), F32), jax.ShapeDtypeStruct((n, D_MODEL), F32)]
    args = [o, u, gv, wdiag, bsrow, x, mod, w, n2]
    if router is not None:
        out_shape.append(jax.ShapeDtypeStruct((n, LANES), F32))
        args += list(router)
    return pl.pallas_call(
        functools.partial(_outproj_s_body, router is not None),
        grid=(1,),
        in_specs=[_layer_of(a, l) if a is w else _whole(a) for a in args],
        out_specs=[pl.BlockSpec(s.shape, lambda i: (0, 0)) for s in out_shape],
        out_shape=out_shape,
        compiler_params=_params(("arbitrary",), 32),
        name="outproj_s",
    )(*args)


def _swiglu(h_bf, wg_ref, wu_ref, wd_ref):
    y = None
    for c in range(len(FF_SPLIT) - 1):
        sl = slice(FF_SPLIT[c], FF_SPLIT[c + 1])
        a = (jax.nn.silu(_dot(h_bf, wg_ref[:, sl])) * _dot(h_bf, wu_ref[:, sl])).astype(BF)
        part = _dot(a, wd_ref[sl, :])
        y = part if y is None else y + part
    return y


def _ffn_s_body(h_ref, x_ref, mod_ref, wg_ref, wu_ref, wd_ref, o_ref):
    h = h_ref[...]
    y = jnp.zeros_like(h)
    for c in range(D_FF // MXU_DIM):
        sl = slice(c * MXU_DIM, (c + 1) * MXU_DIM)
        a = jax.nn.silu(_dot3(h, wg_ref[:, sl])) * _dot3(h, wu_ref[:, sl])
        y = y + _dot3(a, wd_ref[sl, :])
    o_ref[...] = x_ref[...] + mod_ref[5] * y


def _ffn_s(h2, x, mod, wg, wu, wd):
    return pl.pallas_call(
        _ffn_s_body,
        out_shape=jax.ShapeDtypeStruct(x.shape, F32),
        compiler_params=pltpu.CompilerParams(vmem_limit_bytes=56 * MIB),
        name="ffn_s",
    )(h2, x, mod, wg, wu, wd)


def _outffn_p_body(tiles_per_batch, mix_ref, x_ref, mod_ref, w_ref, n2_ref, wg_ref, wu_ref, wd_ref, o_ref):
    b = pl.program_id(0) // tiles_per_batch
    mrow = lambda j: mod_ref[j, pl.ds(b, 1), :]
    xn, h2 = _outproj_compute(_dot_bf, mix_ref[...], x_ref[...], mrow(2), mrow(3), mrow(4), w_ref, n2_ref[...])
    o_ref[...] = xn + mrow(5) * _swiglu(h2.astype(BF), wg_ref, wu_ref, wd_ref)


def _outffn_p(l, mix, x, mod, w_bf, n2, wg, wu, wd, batch, seq):
    t = x.shape[0]
    tm = min(512, seq)
    row = lambda i: (i, 0)
    const = lambda shape: pl.BlockSpec(shape, lambda i: (0,) * len(shape), pipeline_mode=pl.Buffered(1))
    return pl.pallas_call(
        functools.partial(_outffn_p_body, seq // tm),
        grid=(t // tm,),
        in_specs=[
            pl.BlockSpec((tm, D_MODEL), row),
            pl.BlockSpec((tm, D_MODEL), row),
            const((N_ADA, batch, D_MODEL)),
            pl.BlockSpec((None, D_MODEL, D_MODEL), lambda i: (l, 0, 0), pipeline_mode=pl.Buffered(1)),
            const((1, D_MODEL)),
            const((D_MODEL, D_FF)), const((D_MODEL, D_FF)), const((D_FF, D_MODEL)),
        ],
        out_specs=pl.BlockSpec((tm, D_MODEL), row),
        out_shape=jax.ShapeDtypeStruct((t, D_MODEL), F32),
        compiler_params=_params(("arbitrary",), 56),
        name="outffn_p",
    )(mix, x, mod, w_bf, n2, wg, wu, wd)


def _moe_s_body(h_ref, x_ref, gates_ref, mod_ref, wg_ref, wu_ref, wd_ref, o_ref):
    e = pl.program_id(0)

    @pl.when(e == 0)
    def _():
        o_ref[...] = jnp.zeros_like(o_ref)

    lane = lax.broadcasted_iota(jnp.int32, (1, LANES), 1)
    gate = jnp.sum(jnp.where(lane == e, gates_ref[...], 0.0), axis=-1, keepdims=True)
    o_ref[...] += gate * _swiglu(h_ref[...].astype(BF), wg_ref, wu_ref, wd_ref)

    @pl.when(e == N_EXPERTS - 1)
    def _():
        o_ref[...] = x_ref[...] + mod_ref[5] * o_ref[...]


def _moe_s(h2, x, gates, mod, wg, wu, wd):
    n = x.shape[0]
    whole = lambda shape: pl.BlockSpec(shape, lambda e: (0,) * len(shape))
    wspec = lambda shape: pl.BlockSpec((None,) + shape, lambda e: (e, 0, 0))
    return pl.pallas_call(
        _moe_s_body,
        grid=(N_EXPERTS,),
        in_specs=[whole((n, D_MODEL)), whole((n, D_MODEL)), whole((n, LANES)), whole(mod.shape),
                  wspec((D_MODEL, D_FF)), wspec((D_MODEL, D_FF)), wspec((D_FF, D_MODEL))],
        out_specs=whole((n, D_MODEL)),
        out_shape=jax.ShapeDtypeStruct((n, D_MODEL), F32),
        compiler_params=_params(("arbitrary",), 56),
        name="moe_s",
    )(h2, x, gates, mod, wg, wu, wd)


TM_MOE = 512
TD = 512


def _to_token_tiles(ref, x):
    r = x.shape[0]
    for g in range(SUB):
        ref[pl.ds(g, r, stride=SUB), :] = x[:, g * LANES:(g + 1) * LANES]


def _from_token_tiles(ref, first, r):
    return jnp.concatenate([ref[pl.ds(first * SUB + g, r, stride=SUB), :] for g in range(SUB)], axis=-1)


def _token_copy(src, s, dst, d, sem):
    aligned = lambda v: v if isinstance(v, int) else pl.multiple_of(v, SUB)
    return pltpu.make_async_copy(src.at[pl.ds(aligned(s), SUB), :], dst.at[pl.ds(aligned(d), SUB), :], sem)


def _dispatch_body(pos_ref, pad_ref, h_ref, xs_ref, stage, sem, zsem):
    i = pl.program_id(0)
    n = pl.num_programs(0)
    td = h_ref.shape[0]
    slot = i % 2

    def wait_slot(s):
        for _ in range(2):
            pltpu.make_async_copy(stage.at[s], xs_ref.at[pl.ds(0, td * SUB), :], sem.at[s]).wait()

    @pl.when(i >= 2)
    def _():
        wait_slot(slot)

    _to_token_tiles(stage.at[slot], h_ref[...])

    def issue(r, c):
        for k in range(2):
            _token_copy(stage.at[slot], r * SUB, xs_ref, pos_ref[0, 0, k * td + r], sem.at[slot]).start(priority=k)
        return c

    lax.fori_loop(0, td, issue, 0, unroll=8)

    @pl.when(i == n - 1)
    def _():
        wait_slot(slot)

        @pl.when(n > 1)
        def _():
            wait_slot(1 - slot)

        stage[0] = jnp.zeros(stage.shape[1:], stage.dtype)
        for e in range(N_EXPERTS):
            lo = pad_ref[0, e]
            hi = pad_ref[1, e]

            def zero_token(r, c):
                _token_copy(stage.at[0], 0, xs_ref, r * SUB, zsem).start()
                return c

            def wait_token(r, c):
                _token_copy(stage.at[0], 0, xs_ref, 0, zsem).wait()
                return c

            lax.fori_loop(lo, hi, zero_token, 0)
            lax.fori_loop(lo, hi, wait_token, 0)

        def zero_blk(j, c):
            pltpu.make_async_copy(stage.at[0], xs_ref.at[pl.ds(pl.multiple_of(j * (td * SUB), SUB), td * SUB), :],
                                  zsem).start()
            return c

        def wait_blk(j, c):
            pltpu.make_async_copy(stage.at[0], xs_ref.at[pl.ds(0, td * SUB), :], zsem).wait()
            return c

        lax.fori_loop(pad_ref[0, N_EXPERTS], pad_ref[1, N_EXPERTS], zero_blk, 0)
        lax.fori_loop(pad_ref[0, N_EXPERTS], pad_ref[1, N_EXPERTS], wait_blk, 0)


def _dispatch(h2, pos_t, pad, npad):
    t = h2.shape[0]
    td = min(TD, t)
    return pl.pallas_call(
        _dispatch_body,
        grid=(t // td,),
        in_specs=[
            pl.BlockSpec((1, 1, 2 * td), lambda i: (i, 0, 0), memory_space=pltpu.SMEM),
            pl.BlockSpec(memory_space=pltpu.SMEM),
            pl.BlockSpec((td, D_MODEL), lambda i: (i, 0)),
        ],
        out_specs=pl.BlockSpec(memory_space=pl.ANY),
        out_shape=jax.ShapeDtypeStruct((npad * SUB, LANES), F32),
        scratch_shapes=[pltpu.VMEM((2, td * SUB, LANES), F32), pltpu.SemaphoreType.DMA((2,)),
                        pltpu.SemaphoreType.DMA(())],
        compiler_params=_params(("arbitrary",), 32),
        name="dispatch",
    )(pos_t, pad, h2)


def _moe_body(te_ref, src_ref, nv_ref, x_ref, wg_ref, wu_ref, wd_ref, o_ref):
    i = pl.program_id(0)

    @pl.when(nv_ref[i] > 0)
    def _():
        x = _from_token_tiles(x_ref, 0, TM_MOE).astype(BF)
        _to_token_tiles(o_ref, _swiglu(x, wg_ref, wu_ref, wd_ref))

    @pl.when(nv_ref[i] == 0)
    def _():
        o_ref[...] = jnp.zeros_like(o_ref)


def _moe(xs, tile_e, tile_src, tile_nv, wg, wu, wd):
    rows = TM_MOE * SUB
    wspec = lambda shape: pl.BlockSpec((None,) + shape, lambda i, te, src, nv: (te[i], 0, 0))
    return pl.pallas_call(
        _moe_body,
        grid_spec=pltpu.PrefetchScalarGridSpec(
            num_scalar_prefetch=3,
            grid=(xs.shape[0] // rows,),
            in_specs=[
                pl.BlockSpec((rows, LANES), lambda i, te, src, nv: (src[i], 0)),
                wspec((D_MODEL, D_FF)), wspec((D_MODEL, D_FF)), wspec((D_FF, D_MODEL)),
            ],
            out_specs=pl.BlockSpec((rows, LANES), lambda i, te, src, nv: (i, 0)),
        ),
        out_shape=jax.ShapeDtypeStruct(xs.shape, F32),
        compiler_params=_params(("arbitrary",), 56),
        name="moe",
    )(tile_e, tile_src, tile_nv, xs, wg, wu, wd)


def _combine_body(tiles_per_batch, posc_ref, posn_ref, x_ref, route_ref, mod_ref, ys_ref, o_ref, buf, sem):
    i = pl.program_id(0)
    n = pl.num_programs(0)
    tc = x_ref.shape[0]

    def gather(p_ref, s):
        def issue(r, c):
            for k in range(2):
                _token_copy(ys_ref, p_ref[0, 0, k * tc + r], buf.at[s], (k * tc + r) * SUB,
                            sem.at[s]).start(priority=k)
            return c

        lax.fori_loop(0, tc, issue, 0, unroll=8)

    @pl.when(i == 0)
    def _():
        gather(posc_ref, 0)

    @pl.when(i + 1 < n)
    def _():
        gather(posn_ref, (i + 1) % 2)

    slot = i % 2
    pltpu.make_async_copy(ys_ref.at[pl.ds(0, 2 * tc * SUB), :], buf.at[slot], sem.at[slot]).wait()
    lane = lax.broadcasted_iota(jnp.int32, (1, LANES), 1)
    rt = route_ref[...]
    g1 = jnp.sum(jnp.where(lane == ROUTE_GATE, rt, 0.0), axis=-1, keepdims=True)
    g2 = jnp.sum(jnp.where(lane == ROUTE_GATE + 1, rt, 0.0), axis=-1, keepdims=True)
    y = g1 * _from_token_tiles(buf.at[slot], 0, tc) + g2 * _from_token_tiles(buf.at[slot], tc, tc)
    ga2 = mod_ref[5, pl.ds(i // tiles_per_batch, 1), :]
    o_ref[...] = x_ref[...] + ga2 * y


def _combine(ys, pos_t, x, route, mod, seq):
    t = x.shape[0]
    tc = min(TD, t)
    nt = t // tc
    row = lambda i: (i, 0)
    return pl.pallas_call(
        functools.partial(_combine_body, seq // tc),
        grid=(nt,),
        in_specs=[
            pl.BlockSpec((1, 1, 2 * tc), lambda i: (i, 0, 0), memory_space=pltpu.SMEM),
            pl.BlockSpec((1, 1, 2 * tc), lambda i: (jnp.minimum(i + 1, nt - 1), 0, 0), memory_space=pltpu.SMEM),
            pl.BlockSpec((tc, D_MODEL), row),
            pl.BlockSpec((tc, LANES), row),
            pl.BlockSpec(mod.shape, lambda i: (0, 0, 0)),
            pl.BlockSpec(memory_space=pl.ANY),
        ],
        out_specs=pl.BlockSpec((tc, D_MODEL), row),
        out_shape=jax.ShapeDtypeStruct((t, D_MODEL), F32),
        scratch_shapes=[pltpu.VMEM((2, 2 * tc * SUB, LANES), F32), pltpu.SemaphoreType.DMA((2,))],
        compiler_params=_params(("arbitrary",), 32),
        name="combine",
    )(pos_t, pos_t, x, route, mod, ys)


def _moe_routed(h2, xn, route, route_t, cnt, mod, wg, wu, wd, seq):
    t = h2.shape[0]
    td = min(TD, t)
    nt_max = pl.cdiv(2 * t, TM_MOE) + N_EXPERTS
    npad = nt_max * TM_MOE
    counts = cnt[0, :N_EXPERTS].astype(jnp.int32)
    ntile = (counts + TM_MOE - 1) // TM_MOE
    eid = jnp.arange(N_EXPERTS)
    tile_end = jnp.sum(jnp.where(eid[None, :] <= eid[:, None], ntile[None, :], 0), axis=1)
    off = (tile_end - ntile) * TM_MOE
    e12 = route_t[ROUTE_E:ROUTE_E + 2].astype(jnp.int32)
    r12 = route_t[ROUTE_RANK:ROUTE_RANK + 2].astype(jnp.int32)
    onehot = e12[:, :, None] == eid[None, None, :]
    pos = jnp.sum(jnp.where(onehot, off[None, None, :], 0), axis=-1) + r12
    pos_t = (pos * SUB).reshape(2, t // td, td).transpose(1, 0, 2).reshape(t // td, 1, 2 * td)
    total = tile_end[-1]
    tid = jnp.arange(nt_max)
    tile_e = jnp.minimum(jnp.sum(tid[:, None] >= tile_end[None, :], axis=1), N_EXPERTS - 1).astype(jnp.int32)
    tile_nv = (tid < total).astype(jnp.int32)
    tile_src = jnp.minimum(tid, total - 1).astype(jnp.int32)
    pad = jnp.stack([jnp.concatenate([off + counts, (total * (TM_MOE // td))[None]]),
                     jnp.concatenate([off + ntile * TM_MOE, jnp.full((1,), npad // td, jnp.int32)])]).astype(jnp.int32)
    xs = _dispatch(h2, pos_t, pad, npad)
    ys = _moe(xs, tile_e, tile_src, tile_nv, wg, wu, wd)
    return _combine(ys, pos_t, xn, route, mod, seq)


def _rope_tables(pos):
    inv = ROPE_THETA ** (-np.arange(0, HEAD_DIM, 2, dtype=np.float64) / HEAD_DIM)
    ang = np.asarray(pos, np.float64)[:, None] * inv[None, :]
    cos = np.concatenate([np.cos(ang), np.cos(ang)], axis=-1)
    sin = np.concatenate([-np.sin(ang), np.sin(ang)], axis=-1)
    reps = LANES // HEAD_DIM
    return (jnp.asarray(np.tile(cos, (1, reps)), F32), jnp.asarray(np.tile(sin, (1, reps)), F32))


def kernel(x_prompt, x_sample, cache_k, cache_v, c_prompt, c_sample, w_ada, b_ada, norm1_w, norm2_w, w_in,
           q_norm_w, k_norm_w, attn_sinks, gm_norm_w, gm_ws, gm_bs, w_out, dense_w_gate, dense_w_up,
           dense_w_down, router_w, router_b, moe_w_gate, moe_w_up, moe_w_down):
    batch, seq, d = x_prompt.shape
    nd = x_sample.shape[0]
    depth = w_in.shape[0]
    t = batch * seq

    mod = _ada(jnp.concatenate([c_prompt, c_sample], axis=0), w_ada, b_ada)
    mod_p = mod[:, :batch].reshape(depth, batch, N_ADA, d).transpose(0, 2, 1, 3)
    mod_s = mod[:, batch:].reshape(depth, nd, N_ADA, d).transpose(0, 2, 1, 3)

    cos_p, sin_p = _rope_tables(np.arange(seq))
    cos_s, sin_s = _rope_tables(np.array([PAST_LEN]))
    head_of = np.arange(ATTN_WIDTH) // HEAD_DIM
    seg = jnp.asarray(head_of[:, None] == head_of[None, :], BF)
    pairs = nd * N_KV_HEADS
    w = cache_k.shape[2]
    ck_t = cache_k.transpose(0, 1, 3, 4, 2).reshape(depth, pairs, HEAD_DIM, w)
    cv_t = cache_v.transpose(0, 1, 3, 4, 2).reshape(depth, pairs, HEAD_DIM, w)
    uncache = lambda c: jnp.stack(c).reshape(depth, nd, N_KV_HEADS, HEAD_DIM, w).transpose(0, 1, 4, 2, 3)

    w_in_bf = w_in.astype(BF)
    w_out_bf = w_out.astype(BF)
    router_w_pad = jnp.pad(router_w, ((0, 0), (0, 0), (0, LANES - N_EXPERTS)))
    router_b_pad = jnp.pad(router_b, ((0, 0), (0, LANES - N_EXPERTS)), constant_values=NEG_INF)

    cast_plan = []
    for l in range(depth):
        ws = (dense_w_gate, dense_w_up, dense_w_down) if l % 2 == 0 else (moe_w_gate, moe_w_up, moe_w_down)
        cast_plan += [(l - 1 if (l % 2 == 1 and j == 0) else l, (l, j), wj[l // 2]) for j, wj in enumerate(ws)]
    ffn_bf = {}

    xp = x_prompt.reshape(t, d)
    xs = x_sample.reshape(nd, d)
    k_p, v_p, g_p, k_s, v_s, g_s = [], [], [], [], [], []
    for l in range(depth):
        i = l // 2
        n1 = norm1_w[l][None, :]
        n2 = norm2_w[l][None, :]
        qn = jnp.tile(q_norm_w[l], N_HEADS)[None, :]
        kn = jnp.tile(k_norm_w[l], N_KV_HEADS)[None, :]
        gmn = gm_norm_w[l][None, :]
        router = None if l % 2 == 0 else (router_w_pad[i], router_b_pad[i][None, :])

        pending = [c for c in cast_plan if c[0] == l]
        res = _inproj_p(l, xp, mod_p[l], n1, w_in_bf, qn, kn, gmn, seg, cos_p, sin_p,
                        [wt.reshape(-1, wt.shape[-1]) for _, _, wt in pending], batch, seq)
        q, kd, vd, u, gv, kl, vl, gvl = res[:8]
        for (_, name, wt), c in zip(pending, res[8:]):
            ffn_bf[name] = c.reshape(wt.shape)
        wg_bf, wu_bf, wd_bf = (ffn_bf[(l, j)] for j in range(3))
        mix = _mix_p(q, kd, vd, u, gv, gm_ws[l], gm_bs[l].T, attn_sinks[l], batch, seq)
        if router is None:
            xp = _outffn_p(l, mix, xp, mod_p[l], w_out_bf, n2, wg_bf, wu_bf, wd_bf, batch, seq)
        else:
            xn, h2, route, route_t, cnt = _outproj_p(l, mix, xp, mod_p[l], w_out_bf, n2, *router, batch, seq)
            xp = _moe_routed(h2, xn, route, route_t, cnt, mod_p[l], wg_bf, wu_bf, wd_bf, seq)
        k_p.append(kl.reshape(batch, WINDOW, N_KV_HEADS, HEAD_DIM))
        v_p.append(vl.reshape(batch, WINDOW, N_KV_HEADS, HEAD_DIM))
        g_p.append(gvl)

        q, k, v, u, gv = _inproj_s(l, xs, mod_s[l], n1, w_in, qn, kn, gmn, seg, cos_s, sin_s)
        qg = jnp.pad(q.reshape(pairs, KV_GROUP, HEAD_DIM), ((0, 0), (0, SUB - KV_GROUP), (0, 0)))
        sink = jnp.pad(jnp.tile(attn_sinks[l].reshape(N_KV_HEADS, KV_GROUP), (nd, 1)),
                       ((0, 0), (0, SUB - KV_GROUP)))[:, :, None]
        per_step = lambda a: a.T.reshape(KV_WIDTH, -1, ATTN_S_PAIRS // N_KV_HEADS).transpose(1, 0, 2)
        o, nk, nv = _attn_s(l, qg, per_step(k), per_step(v), ck_t, cv_t, sink)
        o = o[:, :KV_GROUP, :].reshape(nd, ATTN_WIDTH)
        wdiag = jnp.repeat(gm_ws[l][:, 0, 0], GM_WIDTH // GM_GROUPS)[None, :]
        bsrow = jnp.repeat(gm_bs[l][:, 0], GM_WIDTH // GM_GROUPS)[None, :]
        res = _outproj_s(l, o, u, gv, wdiag, bsrow, xs, mod_s[l], w_out, n2, router)
        if router is None:
            xs = _ffn_s(res[1], res[0], mod_s[l], dense_w_gate[i], dense_w_up[i], dense_w_down[i])
        else:
            xs = _moe_s(res[1], res[0], res[2], mod_s[l], wg_bf, wu_bf, wd_bf)
        k_s.append(nk)
        v_s.append(nv)
        g_s.append(gv[:, None, :])

    return (xp.reshape(batch, seq, d), xs.reshape(nd, 1, d), jnp.stack(k_p), jnp.stack(v_p), jnp.stack(g_p),
            uncache(k_s), uncache(v_s), jnp.stack(g_s))
```

```python
import functools

import numpy as np
import jax
import jax.numpy as jnp
from jax import lax
from jax.experimental import pallas as pl
from jax.experimental.pallas import tpu as pltpu

D_MODEL = 1024
HEAD_DIM = 64
N_HEADS = 8
N_KV_HEADS = 2
KV_GROUP = N_HEADS // N_KV_HEADS
ATTN_WIDTH = N_HEADS * HEAD_DIM
KV_WIDTH = N_KV_HEADS * HEAD_DIM
GM_WIDTH = 512
GM_GROUPS = 4
WINDOW = 128
CHUNK = 128
D_FF = 2816
MXU_DIM = 256
FF_SPLIT = (0, 6 * MXU_DIM, D_FF)
N_EXPERTS = 8
N_ADA = 6
IN_COLS = ATTN_WIDTH + 2 * KV_WIDTH + 2 * GM_WIDTH
PAST_LEN = 16384
ROPE_THETA = 10000.0
EPS = 1e-6
NEG_INF = -1e30
LOG2E = 1.4426950408889634
LANES = 128
SUB = 8
assert D_MODEL == SUB * LANES

BF = jnp.bfloat16
F32 = jnp.float32
MIB = 1024 * 1024


WIDE_TILE = 1024
ROW_TILE = 512
ROW_GROUP = 256
ADA_COLS = 1024
VMEM_SMALL, VMEM_MID, VMEM_BIG = 32, 48, 56


def _params(sem, vmem_mib):
    return pltpu.CompilerParams(dimension_semantics=sem, vmem_limit_bytes=vmem_mib * MIB)


def _dot(a, b):
    return jnp.dot(a, b, preferred_element_type=F32)


def _dot_nt(a, b):
    return lax.dot_general(a, b, (((1,), (1,)), ((), ())), preferred_element_type=F32)


def _split(a):
    hi = a.astype(BF)
    return hi, (a - hi.astype(F32)).astype(BF)


def _dot_bf(a, w):
    return _dot(a.astype(BF), w)


def _dot3(a, w):
    ah, al = _split(a)
    if w.dtype == BF:
        return _dot(ah, w) + _dot(al, w)
    wh, wl = _split(w)
    return _dot(ah, wh) + _dot(al, wh) + _dot(ah, wl)


def _rms(x, w):
    ms = jnp.mean(x * x, axis=-1, keepdims=True)
    return x * lax.rsqrt(ms + EPS) * w


def _ada_body(c_ref, w_ref, b_ref, o_ref):
    o_ref[...] = _dot3(jax.nn.silu(c_ref[...]), w_ref[...]) + b_ref[...]


def _ada(c_all, w_ada, b_ada):
    depth, d, cols = w_ada.shape
    n = c_all.shape[0]
    tn = ADA_COLS
    return pl.pallas_call(
        _ada_body,
        grid=(depth, cols // tn),
        in_specs=[
            pl.BlockSpec((n, d), lambda l, j: (0, 0)),
            pl.BlockSpec((None, d, tn), lambda l, j: (l, 0, j)),
            pl.BlockSpec((None, 1, tn), lambda l, j: (l, 0, j)),
        ],
        out_specs=pl.BlockSpec((None, n, tn), lambda l, j: (l, 0, j)),
        out_shape=jax.ShapeDtypeStruct((depth, n, cols), F32),
        compiler_params=_params(("arbitrary", "arbitrary"), VMEM_SMALL),
        name="ada",
    )(c_all, w_ada, b_ada.reshape(depth, 1, cols))


def _swap_halves(t):
    n = t.shape[-1]
    lane = lax.broadcasted_iota(jnp.int32, (1, n), 1)
    first = (lane % HEAD_DIM) < (HEAD_DIM // 2)
    return jnp.where(first, pltpu.roll(t, n - HEAD_DIM // 2, axis=1), pltpu.roll(t, HEAD_DIM // 2, axis=1))


def _inproj_compute(mm, x, sh, sc, n1, w_ref, qn, kn, gmn, seg_ref, cos, sin):
    h = _rms(x, n1) * (1.0 + sc) + sh
    z = mm(h, w_ref[...])
    q = z[:, :ATTN_WIDTH]
    k = z[:, ATTN_WIDTH:ATTN_WIDTH + KV_WIDTH]
    v = z[:, ATTN_WIDTH + KV_WIDTH:ATTN_WIDTH + 2 * KV_WIDTH]
    gm = z[:, ATTN_WIDTH + 2 * KV_WIDTH:]

    def head_norm(t, seg, wn):
        ms = mm(t * t, seg) * (1.0 / HEAD_DIM)
        return t * lax.rsqrt(ms + EPS) * wn

    def rope(t):
        reps = t.shape[-1] // LANES
        c = jnp.concatenate([cos] * reps, axis=-1) if reps > 1 else cos
        s = jnp.concatenate([sin] * reps, axis=-1) if reps > 1 else sin
        return t * c + _swap_halves(t) * s

    q = rope(head_norm(q, seg_ref[...], qn)) * (HEAD_DIM ** -0.5 * LOG2E)
    k = rope(head_norm(k, seg_ref[:KV_WIDTH, :KV_WIDTH], kn))
    g = jax.nn.gelu(gm)
    u = g[:, :GM_WIDTH]
    gv = _rms(g[:, GM_WIDTH:], gmn)
    return q, k, v, u, gv


def _dup_heads(t):
    lane = lax.broadcasted_iota(jnp.int32, (1, LANES), 1)
    lo = lane < HEAD_DIM
    r = pltpu.roll(t, HEAD_DIM, axis=1)
    return jnp.concatenate([jnp.where(lo, t, r), jnp.where(lo, r, t)], axis=-1)


def _inproj_p_body(tiles_per_batch, ncast, x_ref, mod_ref, n1_ref, w_ref, qn_ref, kn_ref, gmn_ref, seg_ref,
                   cos_ref, sin_ref, *rest):
    cast_src = rest[:ncast]
    q_ref, kd_ref, vd_ref, u_ref, gv_ref, kl_ref, vl_ref, gvl_ref = rest[ncast:ncast + 8]
    cast_dst = rest[ncast + 8:]
    for src, dst in zip(cast_src, cast_dst):
        dst[...] = src[...].astype(BF)
    i = pl.program_id(0)
    b = i // tiles_per_batch
    sh = mod_ref[0, pl.ds(b, 1), :]
    sc = mod_ref[1, pl.ds(b, 1), :]
    hs = min(ROW_GROUP, x_ref.shape[0])
    for hh in range(x_ref.shape[0] // hs):
        rs = slice(hh * hs, (hh + 1) * hs)
        q, k, v, u, gv = _inproj_compute(_dot_bf, x_ref[rs, :], sh, sc, n1_ref[...], w_ref, qn_ref[...], kn_ref[...],
                                         gmn_ref[...], seg_ref, cos_ref[rs, :], sin_ref[rs, :])
        q_ref[rs, :] = q.astype(BF)
        kd_ref[rs, :] = _dup_heads(k).astype(BF)
        vd_ref[rs, :] = _dup_heads(v).astype(BF)
        u_ref[rs, :] = u.astype(BF)
        gv_ref[rs, :] = gv.astype(BF)

    @pl.when(i % tiles_per_batch == tiles_per_batch - 1)
    def _():
        kl_ref[...] = k[hs - WINDOW:, :]
        vl_ref[...] = v[hs - WINDOW:, :]
        gvl_ref[...] = gv[hs - CHUNK:, :]


BF16_SUBLANES = 16


def _slab_spec(rows, cols, steps):
    s = steps
    while rows % s or (rows // s) % BF16_SUBLANES:
        s //= 2
    return pl.BlockSpec((rows // s, cols), lambda i: (i // (steps // s), 0))


def _inproj_p(l, x, mod, n1, w_bf, qn, kn, gmn, seg, cos, sin, casts, batch, seq):
    t = x.shape[0]
    tm = min(WIDE_TILE, seq)
    tpb = seq // tm
    steps = t // tm
    row = lambda i: (i, 0)
    full = lambda i: (0, 0)
    last = lambda i: (i // tpb, 0, 0)
    cast_specs = [_slab_spec(c.shape[0], c.shape[1], steps) for c in casts]
    return pl.pallas_call(
        functools.partial(_inproj_p_body, tpb, len(casts)),
        grid=(steps,),
        in_specs=[
            pl.BlockSpec((tm, D_MODEL), row),
            pl.BlockSpec((N_ADA, batch, D_MODEL), lambda i: (0, 0, 0)),
            pl.BlockSpec((1, D_MODEL), full),
            pl.BlockSpec((None, D_MODEL, IN_COLS), lambda i: (l, 0, 0)),
            pl.BlockSpec((1, ATTN_WIDTH), full),
            pl.BlockSpec((1, KV_WIDTH), full),
            pl.BlockSpec((1, GM_WIDTH), full),
            pl.BlockSpec((ATTN_WIDTH, ATTN_WIDTH), full),
            pl.BlockSpec((tm, LANES), lambda i: (i % tpb, 0)),
            pl.BlockSpec((tm, LANES), lambda i: (i % tpb, 0)),
        ] + cast_specs,
        out_specs=[
            pl.BlockSpec((tm, ATTN_WIDTH), row),
            pl.BlockSpec((tm, 2 * KV_WIDTH), row),
            pl.BlockSpec((tm, 2 * KV_WIDTH), row),
            pl.BlockSpec((tm, GM_WIDTH), row),
            pl.BlockSpec((tm, GM_WIDTH), row),
            pl.BlockSpec((None, WINDOW, KV_WIDTH), last),
            pl.BlockSpec((None, WINDOW, KV_WIDTH), last),
            pl.BlockSpec((None, CHUNK, GM_WIDTH), last),
        ] + cast_specs,
        out_shape=[
            jax.ShapeDtypeStruct((t, ATTN_WIDTH), BF),
            jax.ShapeDtypeStruct((t, 2 * KV_WIDTH), BF),
            jax.ShapeDtypeStruct((t, 2 * KV_WIDTH), BF),
            jax.ShapeDtypeStruct((t, GM_WIDTH), BF),
            jax.ShapeDtypeStruct((t, GM_WIDTH), BF),
            jax.ShapeDtypeStruct((batch, WINDOW, KV_WIDTH), F32),
            jax.ShapeDtypeStruct((batch, WINDOW, KV_WIDTH), F32),
            jax.ShapeDtypeStruct((batch, CHUNK, GM_WIDTH), F32),
        ] + [jax.ShapeDtypeStruct(c.shape, BF) for c in casts],
        compiler_params=_params(("arbitrary",), VMEM_BIG),
        name="inproj_p",
    )(x, mod, n1, w_bf, qn, kn, gmn, seg, cos, sin, *casts)


def _inproj_s_body(x_ref, mod_ref, n1_ref, w_ref, qn_ref, kn_ref, gmn_ref, seg_ref, cos_ref, sin_ref,
                   q_ref, k_ref, v_ref, u_ref, gv_ref):
    q, k, v, u, gv = _inproj_compute(_dot3, x_ref[...], mod_ref[0], mod_ref[1], n1_ref[...], w_ref, qn_ref[...],
                                     kn_ref[...], gmn_ref[...], seg_ref, cos_ref[...], sin_ref[...])
    q_ref[...] = q
    k_ref[...] = k
    v_ref[...] = v
    u_ref[...] = u
    gv_ref[...] = gv


def _whole(a):
    return pl.BlockSpec(a.shape, lambda i: (0,) * a.ndim)


def _layer_of(w, l):
    return pl.BlockSpec((None,) + w.shape[1:], lambda i: (l,) + (0,) * (w.ndim - 1))


def _inproj_s(l, x, mod, n1, w, qn, kn, gmn, seg, cos, sin):
    n = x.shape[0]
    widths = (ATTN_WIDTH, KV_WIDTH, KV_WIDTH, GM_WIDTH, GM_WIDTH)
    args = (x, mod, n1, w, qn, kn, gmn, seg, cos, sin)
    return pl.pallas_call(
        _inproj_s_body,
        grid=(1,),
        in_specs=[_layer_of(a, l) if a is w else _whole(a) for a in args],
        out_specs=[pl.BlockSpec((n, c), lambda i: (0, 0)) for c in widths],
        out_shape=[jax.ShapeDtypeStruct((n, c), F32) for c in widths],
        compiler_params=_params(("arbitrary",), VMEM_MID),
        name="inproj_s",
    )(*args)


def _mix_p_body(nblk, q_ref, kc_ref, kp_ref, vc_ref, vp_ref, u_ref, gv_ref, ws_ref, bst_ref, sink_ref, o_ref):
    i = pl.program_id(1)
    blk = WINDOW
    lane = lax.broadcasted_iota(jnp.int32, (1, LANES), 1)
    lo = lane < HEAD_DIM
    cols = KV_GROUP * blk
    iq = lax.broadcasted_iota(jnp.int32, (2 * blk, cols), 1) % blk
    jk = lax.broadcasted_iota(jnp.int32, (2 * blk, cols), 0)
    band = (jk > iq) & (jk <= iq + blk)
    bias = jnp.where(band, 0.0, NEG_INF)
    bias_first = jnp.where(band & ((jk >= blk) | (i > 0)), 0.0, NEG_INF)
    tri = (lax.broadcasted_iota(jnp.int32, (CHUNK, CHUNK), 0)
           >= lax.broadcasted_iota(jnp.int32, (CHUNK, CHUNK), 1))
    wm = [jnp.where(tri, ws_ref[g], 0.0).astype(BF) for g in range(GM_GROUPS)]

    for n in range(nblk):
        r0 = n * blk
        if n == 0:
            kk = jnp.concatenate([kp_ref[...], kc_ref[0:blk, :]], axis=0)
            vv = jnp.concatenate([vp_ref[...], vc_ref[0:blk, :]], axis=0)
            mask_bias = bias_first
        else:
            kk = kc_ref[r0 - blk:r0 + blk, :]
            vv = vc_ref[r0 - blk:r0 + blk, :]
            mask_bias = bias
        for kvh in range(N_KV_HEADS):
            c0 = 2 * kvh
            qa = q_ref[r0:r0 + blk, c0 * LANES:(c0 + 1) * LANES]
            qb = q_ref[r0:r0 + blk, (c0 + 1) * LANES:(c0 + 2) * LANES]
            zero = jnp.zeros_like(qa)
            qq = jnp.concatenate([jnp.where(lo, qa, zero), jnp.where(lo, zero, qa),
                                  jnp.where(lo, qb, zero), jnp.where(lo, zero, qb)], axis=0)
            s = _dot_nt(kk[:, kvh * LANES:(kvh + 1) * LANES], qq) + mask_bias
            sink = jnp.concatenate(
                [jnp.full((1, blk), sink_ref[kvh * KV_GROUP + g] * LOG2E, F32) for g in range(KV_GROUP)], axis=1)
            m = jnp.maximum(jnp.max(s, axis=0, keepdims=True), sink)
            p = jnp.exp2(s - m)
            den = jnp.sum(p, axis=0, keepdims=True) + jnp.exp2(sink - m)
            p = (p * (1.0 / den)).astype(BF)
            o = lax.dot_general(p, vv[:, kvh * LANES:(kvh + 1) * LANES], (((0,), (0,)), ((), ())),
                                preferred_element_type=F32)
            o_ref[r0:r0 + blk, c0 * LANES:(c0 + 1) * LANES] = jnp.where(
                lo, o[0:blk], o[blk:2 * blk]).astype(BF)
            o_ref[r0:r0 + blk, (c0 + 1) * LANES:(c0 + 2) * LANES] = jnp.where(
                lo, o[2 * blk:3 * blk], o[3 * blk:4 * blk]).astype(BF)
        for g in range(GM_GROUPS):
            cs = slice(g * LANES, (g + 1) * LANES)
            sp = _dot(wm[g], gv_ref[r0:r0 + blk, cs]) + bst_ref[:, g:g + 1]
            o_ref[r0:r0 + blk, ATTN_WIDTH + g * LANES:ATTN_WIDTH + (g + 1) * LANES] = (
                u_ref[r0:r0 + blk, cs].astype(F32) * sp).astype(BF)


def _mix_p(q, kd, vd, u, gv, ws, bst, sinks, batch, seq):
    t = q.shape[0]
    tq = min(WIDE_TILE, seq)
    nblk = tq // WINDOW
    tpb = seq // tq
    cur = lambda b, i: (b * tpb + i, 0)
    prev = lambda b, i: (jnp.maximum((b * tpb + i) * nblk - 1, b * tpb * nblk), 0)
    return pl.pallas_call(
        functools.partial(_mix_p_body, nblk),
        grid=(batch, tpb),
        in_specs=[
            pl.BlockSpec((tq, ATTN_WIDTH), cur),
            pl.BlockSpec((tq, 2 * KV_WIDTH), cur),
            pl.BlockSpec((WINDOW, 2 * KV_WIDTH), prev),
            pl.BlockSpec((tq, 2 * KV_WIDTH), cur),
            pl.BlockSpec((WINDOW, 2 * KV_WIDTH), prev),
            pl.BlockSpec((tq, GM_WIDTH), cur),
            pl.BlockSpec((tq, GM_WIDTH), cur),
            pl.BlockSpec((GM_GROUPS, CHUNK, CHUNK), lambda b, i: (0, 0, 0)),
            pl.BlockSpec((CHUNK, GM_GROUPS), lambda b, i: (0, 0)),
            pl.BlockSpec(memory_space=pltpu.SMEM),
        ],
        out_specs=pl.BlockSpec((tq, D_MODEL), cur),
        out_shape=jax.ShapeDtypeStruct((t, D_MODEL), BF),
        compiler_params=_params(("arbitrary", "arbitrary"), VMEM_MID),
        name="mix_p",
    )(q, kd, kd, vd, vd, u, gv, ws, bst, sinks)


ATTN_S_PAIRS = 32

def _attn_s_body(q_ref, kn_ref, vn_ref, ck_ref, cv_ref, sink_ref, o_ref, nk_ref, nv_ref):
    nb, hd, w = ck_ref.shape
    pos = lax.broadcasted_iota(jnp.int32, (1, w), 1)
    for p in range(nb):
        b, kvh = divmod(p, N_KV_HEADS)
        for new_ref, c_ref, n_ref in ((kn_ref, ck_ref, nk_ref), (vn_ref, cv_ref, nv_ref)):
            new = new_ref[kvh * hd:(kvh + 1) * hd, b:b + 1]
            n_ref[p] = jnp.where(pos == w - 1, new, pltpu.roll(c_ref[p], w - 1, axis=1))
    nk = nk_ref[...]
    nv = nv_ref[...]

    def bmm3(spec, a, b):
        (ah, al), (bh, bl) = _split(a), _split(b)
        mm = lambda x, y: jnp.einsum(spec, x, y, preferred_element_type=F32)
        return mm(ah, bh) + mm(al, bh) + mm(ah, bl)

    s = bmm3('ngd,ndj->ngj', q_ref[...], nk)
    sink = sink_ref[...] * LOG2E
    m = jnp.maximum(jnp.max(s, axis=-1, keepdims=True), sink)
    p = jnp.exp2(s - m)
    den = jnp.sum(p, axis=-1, keepdims=True) + jnp.exp2(sink - m)
    o_ref[...] = bmm3('ngj,ndj->ngd', p, nv) * (1.0 / den)


def _attn_s(l, q, k_new, v_new, ck, cv, sink):
    _, n, hd, w = ck.shape
    nb = ATTN_S_PAIRS
    assert n % nb == 0
    rows = q.shape[1]
    blk = lambda r, c: pl.BlockSpec((nb, r, c), lambda i: (i, 0, 0))
    cache = pl.BlockSpec((None, nb, hd, w), lambda i: (l, i, 0, 0))
    new = pl.BlockSpec((None,) + k_new.shape[1:], lambda i: (i, 0, 0))
    return pl.pallas_call(
        _attn_s_body,
        grid=(n // nb,),
        in_specs=[blk(rows, hd), new, new, cache, cache, blk(rows, 1)],
        out_specs=[blk(rows, hd), blk(hd, w), blk(hd, w)],
        out_shape=[jax.ShapeDtypeStruct((n, rows, hd), F32),
                   jax.ShapeDtypeStruct((n, hd, w), F32),
                   jax.ShapeDtypeStruct((n, hd, w), F32)],
        compiler_params=_params(("arbitrary",), VMEM_SMALL),
        name="attn_s",
    )(q, k_new, v_new, ck, cv, sink)


def _top2(h2, rw_ref, rb_ref):
    hh, hl = _split(h2)
    wh, wl = _split(rw_ref[...])
    both = _dot(hh, jnp.concatenate([wh, wl], axis=1))
    logits = both[:, :LANES] + _dot(hl, wh) + both[:, LANES:] + rb_ref[...]
    lane = lax.broadcasted_iota(jnp.int32, logits.shape, 1).astype(F32)
    e = jnp.exp(logits - jnp.max(logits, axis=-1, keepdims=True))
    p = e / jnp.sum(e, axis=-1, keepdims=True)
    m1 = jnp.max(p, axis=-1, keepdims=True)
    i1 = jnp.min(jnp.where(p == m1, lane, float(LANES)), axis=-1, keepdims=True)
    p2 = jnp.where(lane == i1, -1.0, p)
    m2 = jnp.max(p2, axis=-1, keepdims=True)
    i2 = jnp.min(jnp.where(p2 == m2, lane, float(LANES)), axis=-1, keepdims=True)
    tot = m1 + m2
    return lane, i1, i2, m1 / tot, m2 / tot


def _route_gates(h2, rw_ref, rb_ref):
    lane, i1, i2, g1, g2 = _top2(h2, rw_ref, rb_ref)
    return jnp.where(lane == i1, g1, 0.0) + jnp.where(lane == i2, g2, 0.0)


ROUTE_E, ROUTE_RANK, ROUTE_GATE = 0, 2, 4


def _route_ranked(h2, rw_ref, rb_ref, tri_ref, cnt_ref):
    lane, i1, i2, g1, g2 = _top2(h2, rw_ref, rb_ref)
    oh1 = lane == i1
    oh2 = lane == i2
    hit = jnp.where(oh1, 1.0, 0.0) + jnp.where(oh2, 1.0, 0.0)
    before = cnt_ref[...] + _dot(tri_ref[...], hit.astype(BF))
    r1 = jnp.sum(jnp.where(oh1, before, 0.0), axis=-1, keepdims=True)
    r2 = jnp.sum(jnp.where(oh2, before, 0.0), axis=-1, keepdims=True)
    cnt_ref[...] += jnp.sum(hit, axis=0, keepdims=True)
    cols = (i1, i2, r1, r2, g1, g2)
    out = jnp.zeros_like(lane)
    for j, c in enumerate(cols):
        out = jnp.where(lane == float(j), c, out)
    return out


def _outproj_compute(mm, mix, x, ga1, sh2, sc2, w_ref, n2):
    xn = x + ga1 * mm(mix, w_ref[...])
    h2 = _rms(xn, n2) * (1.0 + sc2) + sh2
    return xn, h2


def _outproj_p_body(tiles_per_batch, mix_ref, x_ref, mod_ref, w_ref, n2_ref, rw_ref, rb_ref, tri_ref,
                    xn_ref, h2_ref, route_ref, route_t_ref, cnt_ref):
    i = pl.program_id(0)
    b = i // tiles_per_batch
    mrow = lambda j: mod_ref[j, pl.ds(b, 1), :]

    @pl.when(i == 0)
    def _():
        cnt_ref[...] = jnp.zeros_like(cnt_ref)

    xn, h2 = _outproj_compute(_dot_bf, mix_ref[...], x_ref[...], mrow(2), mrow(3), mrow(4), w_ref, n2_ref[...])
    route = _route_ranked(h2, rw_ref, rb_ref, tri_ref, cnt_ref)
    route_ref[...] = route
    route_t_ref[...] = route.T[:SUB, :]
    h2_ref[...] = h2
    xn_ref[...] = xn


def _outproj_p(l, mix, x, mod, w_bf, n2, rw, rb, batch, seq):
    t = x.shape[0]
    tm = min(ROW_TILE, seq)
    row = lambda i: (i, 0)
    full = lambda i: (0, 0)
    return pl.pallas_call(
        functools.partial(_outproj_p_body, seq // tm),
        grid=(t // tm,),
        in_specs=[
            pl.BlockSpec((tm, D_MODEL), row),
            pl.BlockSpec((tm, D_MODEL), row),
            pl.BlockSpec((N_ADA, batch, D_MODEL), lambda i: (0, 0, 0)),
            pl.BlockSpec((None, D_MODEL, D_MODEL), lambda i: (l, 0, 0)),
            pl.BlockSpec((1, D_MODEL), full),
            pl.BlockSpec((D_MODEL, LANES), full),
            pl.BlockSpec((1, LANES), full),
            pl.BlockSpec((tm, tm), full),
        ],
        out_specs=[pl.BlockSpec((tm, D_MODEL), row), pl.BlockSpec((tm, D_MODEL), row),
                   pl.BlockSpec((tm, LANES), row), pl.BlockSpec((SUB, tm), lambda i: (0, i)),
                   pl.BlockSpec((1, LANES), full)],
        out_shape=[jax.ShapeDtypeStruct((t, D_MODEL), F32), jax.ShapeDtypeStruct((t, D_MODEL), F32),
                   jax.ShapeDtypeStruct((t, LANES), F32), jax.ShapeDtypeStruct((SUB, t), F32),
                   jax.ShapeDtypeStruct((1, LANES), F32)],
        compiler_params=_params(("arbitrary",), VMEM_MID),
        name="outproj_p",
    )(mix, x, mod, w_bf, n2, rw, rb, jnp.asarray(np.tri(tm, k=-1), BF))


def _outproj_s_body(with_router, o_ref, u_ref, gv_ref, wdiag_ref, bsrow_ref, x_ref, mod_ref, w_ref, n2_ref, *rest):
    gate = u_ref[...] * (wdiag_ref[...] * gv_ref[...] + bsrow_ref[...])
    mix = jnp.concatenate([o_ref[...], gate], axis=-1)
    xn, h2 = _outproj_compute(_dot3, mix, x_ref[...], mod_ref[2], mod_ref[3], mod_ref[4], w_ref, n2_ref[...])
    if with_router:
        rw_ref, rb_ref, xn_ref, h2_ref, gates_ref = rest
        gates_ref[...] = _route_gates(h2, rw_ref, rb_ref)
    else:
        xn_ref, h2_ref = rest
    xn_ref[...] = xn
    h2_ref[...] = h2


def _outproj_s(l, o, u, gv, wdiag, bsrow, x, mod, w, n2, router):
    n = x.shape[0]
    out_shape = [jax.ShapeDtypeStruct((n, D_MODEL), F32), jax.ShapeDtypeStruct((n, D_MODEL), F32)]
    args = [o, u, gv, wdiag, bsrow, x, mod, w, n2]
    if router is not None:
        out_shape.append(jax.ShapeDtypeStruct((n, LANES), F32))
        args += list(router)
    return pl.pallas_call(
        functools.partial(_outproj_s_body, router is not None),
        grid=(1,),
        in_specs=[_layer_of(a, l) if a is w else _whole(a) for a in args],
        out_specs=[pl.BlockSpec(s.shape, lambda i: (0, 0)) for s in out_shape],
        out_shape=out_shape,
        compiler_params=_params(("arbitrary",), VMEM_SMALL),
        name="outproj_s",
    )(*args)


def _swiglu(h_bf, wg_ref, wu_ref, wd_ref):
    y = None
    for c in range(len(FF_SPLIT) - 1):
        sl = slice(FF_SPLIT[c], FF_SPLIT[c + 1])
        a = (jax.nn.silu(_dot(h_bf, wg_ref[:, sl])) * _dot(h_bf, wu_ref[:, sl])).astype(BF)
        part = _dot(a, wd_ref[sl, :])
        y = part if y is None else y + part
    return y


def _ffn_s_body(h_ref, x_ref, mod_ref, wg_ref, wu_ref, wd_ref, o_ref):
    h = h_ref[...]
    y = jnp.zeros_like(h)
    for c in range(D_FF // MXU_DIM):
        sl = slice(c * MXU_DIM, (c + 1) * MXU_DIM)
        a = jax.nn.silu(_dot3(h, wg_ref[:, sl])) * _dot3(h, wu_ref[:, sl])
        y = y + _dot3(a, wd_ref[sl, :])
    o_ref[...] = x_ref[...] + mod_ref[5] * y


def _ffn_s(h2, x, mod, wg, wu, wd):
    return pl.pallas_call(
        _ffn_s_body,
        out_shape=jax.ShapeDtypeStruct(x.shape, F32),
        compiler_params=pltpu.CompilerParams(vmem_limit_bytes=VMEM_BIG * MIB),
        name="ffn_s",
    )(h2, x, mod, wg, wu, wd)


def _outffn_p_body(tiles_per_batch, mix_ref, x_ref, mod_ref, w_ref, n2_ref, wg_ref, wu_ref, wd_ref, o_ref):
    b = pl.program_id(0) // tiles_per_batch
    mrow = lambda j: mod_ref[j, pl.ds(b, 1), :]
    xn, h2 = _outproj_compute(_dot_bf, mix_ref[...], x_ref[...], mrow(2), mrow(3), mrow(4), w_ref, n2_ref[...])
    o_ref[...] = xn + mrow(5) * _swiglu(h2.astype(BF), wg_ref, wu_ref, wd_ref)


def _outffn_p(l, mix, x, mod, w_bf, n2, wg, wu, wd, batch, seq):
    t = x.shape[0]
    tm = min(ROW_TILE, seq)
    row = lambda i: (i, 0)
    const = lambda shape: pl.BlockSpec(shape, lambda i: (0,) * len(shape), pipeline_mode=pl.Buffered(1))
    return pl.pallas_call(
        functools.partial(_outffn_p_body, seq // tm),
        grid=(t // tm,),
        in_specs=[
            pl.BlockSpec((tm, D_MODEL), row),
            pl.BlockSpec((tm, D_MODEL), row),
            const((N_ADA, batch, D_MODEL)),
            pl.BlockSpec((None, D_MODEL, D_MODEL), lambda i: (l, 0, 0), pipeline_mode=pl.Buffered(1)),
            const((1, D_MODEL)),
            const((D_MODEL, D_FF)), const((D_MODEL, D_FF)), const((D_FF, D_MODEL)),
        ],
        out_specs=pl.BlockSpec((tm, D_MODEL), row),
        out_shape=jax.ShapeDtypeStruct((t, D_MODEL), F32),
        compiler_params=_params(("arbitrary",), VMEM_BIG),
        name="outffn_p",
    )(mix, x, mod, w_bf, n2, wg, wu, wd)


def _moe_s_body(h_ref, x_ref, gates_ref, mod_ref, wg_ref, wu_ref, wd_ref, o_ref):
    e = pl.program_id(0)

    @pl.when(e == 0)
    def _():
        o_ref[...] = jnp.zeros_like(o_ref)

    lane = lax.broadcasted_iota(jnp.int32, (1, LANES), 1)
    gate = jnp.sum(jnp.where(lane == e, gates_ref[...], 0.0), axis=-1, keepdims=True)
    o_ref[...] += gate * _swiglu(h_ref[...].astype(BF), wg_ref, wu_ref, wd_ref)

    @pl.when(e == N_EXPERTS - 1)
    def _():
        o_ref[...] = x_ref[...] + mod_ref[5] * o_ref[...]


def _moe_s(h2, x, gates, mod, wg, wu, wd):
    n = x.shape[0]
    whole = lambda shape: pl.BlockSpec(shape, lambda e: (0,) * len(shape))
    wspec = lambda shape: pl.BlockSpec((None,) + shape, lambda e: (e, 0, 0))
    return pl.pallas_call(
        _moe_s_body,
        grid=(N_EXPERTS,),
        in_specs=[whole((n, D_MODEL)), whole((n, D_MODEL)), whole((n, LANES)), whole(mod.shape),
                  wspec((D_MODEL, D_FF)), wspec((D_MODEL, D_FF)), wspec((D_FF, D_MODEL))],
        out_specs=whole((n, D_MODEL)),
        out_shape=jax.ShapeDtypeStruct((n, D_MODEL), F32),
        compiler_params=_params(("arbitrary",), VMEM_BIG),
        name="moe_s",
    )(h2, x, gates, mod, wg, wu, wd)


TM_MOE = 512
TD = 512


def _to_token_tiles(ref, x):
    r = x.shape[0]
    for g in range(SUB):
        ref[pl.ds(g, r, stride=SUB), :] = x[:, g * LANES:(g + 1) * LANES]


def _from_token_tiles(ref, first, r):
    return jnp.concatenate([ref[pl.ds(first * SUB + g, r, stride=SUB), :] for g in range(SUB)], axis=-1)


def _token_copy(src, s, dst, d, sem):
    aligned = lambda v: v if isinstance(v, int) else pl.multiple_of(v, SUB)
    return pltpu.make_async_copy(src.at[pl.ds(aligned(s), SUB), :], dst.at[pl.ds(aligned(d), SUB), :], sem)


def _dispatch_body(pos_ref, pad_ref, h_ref, xs_ref, stage, sem, zsem):
    i = pl.program_id(0)
    n = pl.num_programs(0)
    td = h_ref.shape[0]
    slot = i % 2

    def wait_slot(s):
        for _ in range(2):
            pltpu.make_async_copy(stage.at[s], xs_ref.at[pl.ds(0, td * SUB), :], sem.at[s]).wait()

    @pl.when(i >= 2)
    def _():
        wait_slot(slot)

    _to_token_tiles(stage.at[slot], h_ref[...])

    def issue(r, c):
        for k in range(2):
            _token_copy(stage.at[slot], r * SUB, xs_ref, pos_ref[0, 0, k * td + r], sem.at[slot]).start(priority=k)
        return c

    lax.fori_loop(0, td, issue, 0, unroll=8)

    @pl.when(i == n - 1)
    def _():
        wait_slot(slot)

        @pl.when(n > 1)
        def _():
            wait_slot(1 - slot)

        stage[0] = jnp.zeros(stage.shape[1:], stage.dtype)
        for e in range(N_EXPERTS):
            lo = pad_ref[0, e]
            hi = pad_ref[1, e]

            def zero_token(r, c):
                _token_copy(stage.at[0], 0, xs_ref, r * SUB, zsem).start()
                return c

            def wait_token(r, c):
                _token_copy(stage.at[0], 0, xs_ref, 0, zsem).wait()
                return c

            lax.fori_loop(lo, hi, zero_token, 0)
            lax.fori_loop(lo, hi, wait_token, 0)

        def zero_blk(j, c):
            pltpu.make_async_copy(stage.at[0], xs_ref.at[pl.ds(pl.multiple_of(j * (td * SUB), SUB), td * SUB), :],
                                  zsem).start()
            return c

        def wait_blk(j, c):
            pltpu.make_async_copy(stage.at[0], xs_ref.at[pl.ds(0, td * SUB), :], zsem).wait()
            return c

        lax.fori_loop(pad_ref[0, N_EXPERTS], pad_ref[1, N_EXPERTS], zero_blk, 0)
        lax.fori_loop(pad_ref[0, N_EXPERTS], pad_ref[1, N_EXPERTS], wait_blk, 0)


def _dispatch(h2, pos_t, pad, npad):
    t = h2.shape[0]
    td = min(TD, t)
    return pl.pallas_call(
        _dispatch_body,
        grid=(t // td,),
        in_specs=[
            pl.BlockSpec((1, 1, 2 * td), lambda i: (i, 0, 0), memory_space=pltpu.SMEM),
            pl.BlockSpec(memory_space=pltpu.SMEM),
            pl.BlockSpec((td, D_MODEL), lambda i: (i, 0)),
        ],
        out_specs=pl.BlockSpec(memory_space=pl.ANY),
        out_shape=jax.ShapeDtypeStruct((npad * SUB, LANES), F32),
        scratch_shapes=[pltpu.VMEM((2, td * SUB, LANES), F32), pltpu.SemaphoreType.DMA((2,)),
                        pltpu.SemaphoreType.DMA(())],
        compiler_params=_params(("arbitrary",), VMEM_SMALL),
        name="dispatch",
    )(pos_t, pad, h2)


def _moe_body(te_ref, src_ref, nv_ref, x_ref, wg_ref, wu_ref, wd_ref, o_ref):
    i = pl.program_id(0)

    @pl.when(nv_ref[i] > 0)
    def _():
        x = _from_token_tiles(x_ref, 0, TM_MOE).astype(BF)
        _to_token_tiles(o_ref, _swiglu(x, wg_ref, wu_ref, wd_ref))

    @pl.when(nv_ref[i] == 0)
    def _():
        o_ref[...] = jnp.zeros_like(o_ref)


def _moe(xs, tile_e, tile_src, tile_nv, wg, wu, wd):
    rows = TM_MOE * SUB
    wspec = lambda shape: pl.BlockSpec((None,) + shape, lambda i, te, src, nv: (te[i], 0, 0))
    return pl.pallas_call(
        _moe_body,
        grid_spec=pltpu.PrefetchScalarGridSpec(
            num_scalar_prefetch=3,
            grid=(xs.shape[0] // rows,),
            in_specs=[
                pl.BlockSpec((rows, LANES), lambda i, te, src, nv: (src[i], 0)),
                wspec((D_MODEL, D_FF)), wspec((D_MODEL, D_FF)), wspec((D_FF, D_MODEL)),
            ],
            out_specs=pl.BlockSpec((rows, LANES), lambda i, te, src, nv: (i, 0)),
        ),
        out_shape=jax.ShapeDtypeStruct(xs.shape, F32),
        compiler_params=_params(("arbitrary",), VMEM_BIG),
        name="moe",
    )(tile_e, tile_src, tile_nv, xs, wg, wu, wd)


def _combine_body(tiles_per_batch, posc_ref, posn_ref, x_ref, route_ref, mod_ref, ys_ref, o_ref, buf, sem):
    i = pl.program_id(0)
    n = pl.num_programs(0)
    tc = x_ref.shape[0]

    def gather(p_ref, s):
        def issue(r, c):
            for k in range(2):
                _token_copy(ys_ref, p_ref[0, 0, k * tc + r], buf.at[s], (k * tc + r) * SUB,
                            sem.at[s]).start(priority=k)
            return c

        lax.fori_loop(0, tc, issue, 0, unroll=8)

    @pl.when(i == 0)
    def _():
        gather(posc_ref, 0)

    @pl.when(i + 1 < n)
    def _():
        gather(posn_ref, (i + 1) % 2)

    slot = i % 2
    pltpu.make_async_copy(ys_ref.at[pl.ds(0, 2 * tc * SUB), :], buf.at[slot], sem.at[slot]).wait()
    lane = lax.broadcasted_iota(jnp.int32, (1, LANES), 1)
    rt = route_ref[...]
    g1 = jnp.sum(jnp.where(lane == ROUTE_GATE, rt, 0.0), axis=-1, keepdims=True)
    g2 = jnp.sum(jnp.where(lane == ROUTE_GATE + 1, rt, 0.0), axis=-1, keepdims=True)
    y = g1 * _from_token_tiles(buf.at[slot], 0, tc) + g2 * _from_token_tiles(buf.at[slot], tc, tc)
    ga2 = mod_ref[5, pl.ds(i // tiles_per_batch, 1), :]
    o_ref[...] = x_ref[...] + ga2 * y


def _combine(ys, pos_t, x, route, mod, seq):
    t = x.shape[0]
    tc = min(TD, t)
    nt = t // tc
    row = lambda i: (i, 0)
    return pl.pallas_call(
        functools.partial(_combine_body, seq // tc),
        grid=(nt,),
        in_specs=[
            pl.BlockSpec((1, 1, 2 * tc), lambda i: (i, 0, 0), memory_space=pltpu.SMEM),
            pl.BlockSpec((1, 1, 2 * tc), lambda i: (jnp.minimum(i + 1, nt - 1), 0, 0), memory_space=pltpu.SMEM),
            pl.BlockSpec((tc, D_MODEL), row),
            pl.BlockSpec((tc, LANES), row),
            pl.BlockSpec(mod.shape, lambda i: (0, 0, 0)),
            pl.BlockSpec(memory_space=pl.ANY),
        ],
        out_specs=pl.BlockSpec((tc, D_MODEL), row),
        out_shape=jax.ShapeDtypeStruct((t, D_MODEL), F32),
        scratch_shapes=[pltpu.VMEM((2, 2 * tc * SUB, LANES), F32), pltpu.SemaphoreType.DMA((2,))],
        compiler_params=_params(("arbitrary",), VMEM_SMALL),
        name="combine",
    )(pos_t, pos_t, x, route, mod, ys)


def _moe_routed(h2, xn, route, route_t, cnt, mod, wg, wu, wd, seq):
    t = h2.shape[0]
    td = min(TD, t)
    nt_max = pl.cdiv(2 * t, TM_MOE) + N_EXPERTS
    npad = nt_max * TM_MOE
    counts = cnt[0, :N_EXPERTS].astype(jnp.int32)
    ntile = (counts + TM_MOE - 1) // TM_MOE
    eid = jnp.arange(N_EXPERTS)
    tile_end = jnp.sum(jnp.where(eid[None, :] <= eid[:, None], ntile[None, :], 0), axis=1)
    off = (tile_end - ntile) * TM_MOE
    e12 = route_t[ROUTE_E:ROUTE_E + 2].astype(jnp.int32)
    r12 = route_t[ROUTE_RANK:ROUTE_RANK + 2].astype(jnp.int32)
    onehot = e12[:, :, None] == eid[None, None, :]
    pos = jnp.sum(jnp.where(onehot, off[None, None, :], 0), axis=-1) + r12
    pos_t = (pos * SUB).reshape(2, t // td, td).transpose(1, 0, 2).reshape(t // td, 1, 2 * td)
    total = tile_end[-1]
    tid = jnp.arange(nt_max)
    tile_e = jnp.minimum(jnp.sum(tid[:, None] >= tile_end[None, :], axis=1), N_EXPERTS - 1).astype(jnp.int32)
    tile_nv = (tid < total).astype(jnp.int32)
    tile_src = jnp.minimum(tid, total - 1).astype(jnp.int32)
    pad = jnp.stack([jnp.concatenate([off + counts, (total * (TM_MOE // td))[None]]),
                     jnp.concatenate([off + ntile * TM_MOE, jnp.full((1,), npad // td, jnp.int32)])]).astype(jnp.int32)
    xs = _dispatch(h2, pos_t, pad, npad)
    ys = _moe(xs, tile_e, tile_src, tile_nv, wg, wu, wd)
    return _combine(ys, pos_t, xn, route, mod, seq)


def _rope_tables(pos):
    inv = ROPE_THETA ** (-np.arange(0, HEAD_DIM, 2, dtype=np.float64) / HEAD_DIM)
    ang = np.asarray(pos, np.float64)[:, None] * inv[None, :]
    cos = np.concatenate([np.cos(ang), np.cos(ang)], axis=-1)
    sin = np.concatenate([-np.sin(ang), np.sin(ang)], axis=-1)
    reps = LANES // HEAD_DIM
    return (jnp.asarray(np.tile(cos, (1, reps)), F32), jnp.asarray(np.tile(sin, (1, reps)), F32))


def kernel(x_prompt, x_sample, cache_k, cache_v, c_prompt, c_sample, w_ada, b_ada, norm1_w, norm2_w, w_in,
           q_norm_w, k_norm_w, attn_sinks, gm_norm_w, gm_ws, gm_bs, w_out, dense_w_gate, dense_w_up,
           dense_w_down, router_w, router_b, moe_w_gate, moe_w_up, moe_w_down):
    batch, seq, d = x_prompt.shape
    nd = x_sample.shape[0]
    depth = w_in.shape[0]
    t = batch * seq

    mod = _ada(jnp.concatenate([c_prompt, c_sample], axis=0), w_ada, b_ada)
    mod_p = mod[:, :batch].reshape(depth, batch, N_ADA, d).transpose(0, 2, 1, 3)
    mod_s = mod[:, batch:].reshape(depth, nd, N_ADA, d).transpose(0, 2, 1, 3)

    cos_p, sin_p = _rope_tables(np.arange(seq))
    cos_s, sin_s = _rope_tables(np.array([PAST_LEN]))
    head_of = np.arange(ATTN_WIDTH) // HEAD_DIM
    seg = jnp.asarray(head_of[:, None] == head_of[None, :], BF)
    pairs = nd * N_KV_HEADS
    w = cache_k.shape[2]
    ck_t = cache_k.transpose(0, 1, 3, 4, 2).reshape(depth, pairs, HEAD_DIM, w)
    cv_t = cache_v.transpose(0, 1, 3, 4, 2).reshape(depth, pairs, HEAD_DIM, w)
    uncache = lambda c: jnp.stack(c).reshape(depth, nd, N_KV_HEADS, HEAD_DIM, w).transpose(0, 1, 4, 2, 3)

    w_in_bf = w_in.astype(BF)
    w_out_bf = w_out.astype(BF)
    router_w_pad = jnp.pad(router_w, ((0, 0), (0, 0), (0, LANES - N_EXPERTS)))
    router_b_pad = jnp.pad(router_b, ((0, 0), (0, LANES - N_EXPERTS)), constant_values=NEG_INF)

    cast_plan = []
    for l in range(depth):
        ws = (dense_w_gate, dense_w_up, dense_w_down) if l % 2 == 0 else (moe_w_gate, moe_w_up, moe_w_down)
        cast_plan += [(l - 1 if (l % 2 == 1 and j == 0) else l, (l, j), wj[l // 2]) for j, wj in enumerate(ws)]
    ffn_bf = {}

    xp = x_prompt.reshape(t, d)
    xs = x_sample.reshape(nd, d)
    k_p, v_p, g_p, k_s, v_s, g_s = [], [], [], [], [], []
    for l in range(depth):
        i = l // 2
        n1 = norm1_w[l][None, :]
        n2 = norm2_w[l][None, :]
        qn = jnp.tile(q_norm_w[l], N_HEADS)[None, :]
        kn = jnp.tile(k_norm_w[l], N_KV_HEADS)[None, :]
        gmn = gm_norm_w[l][None, :]
        router = None if l % 2 == 0 else (router_w_pad[i], router_b_pad[i][None, :])

        pending = [c for c in cast_plan if c[0] == l]
        res = _inproj_p(l, xp, mod_p[l], n1, w_in_bf, qn, kn, gmn, seg, cos_p, sin_p,
                        [wt.reshape(-1, wt.shape[-1]) for _, _, wt in pending], batch, seq)
        q, kd, vd, u, gv, kl, vl, gvl = res[:8]
        for (_, name, wt), c in zip(pending, res[8:]):
            ffn_bf[name] = c.reshape(wt.shape)
        wg_bf, wu_bf, wd_bf = (ffn_bf[(l, j)] for j in range(3))
        mix = _mix_p(q, kd, vd, u, gv, gm_ws[l], gm_bs[l].T, attn_sinks[l], batch, seq)
        if router is None:
            xp = _outffn_p(l, mix, xp, mod_p[l], w_out_bf, n2, wg_bf, wu_bf, wd_bf, batch, seq)
        else:
            xn, h2, route, route_t, cnt = _outproj_p(l, mix, xp, mod_p[l], w_out_bf, n2, *router, batch, seq)
            xp = _moe_routed(h2, xn, route, route_t, cnt, mod_p[l], wg_bf, wu_bf, wd_bf, seq)
        k_p.append(kl.reshape(batch, WINDOW, N_KV_HEADS, HEAD_DIM))
        v_p.append(vl.reshape(batch, WINDOW, N_KV_HEADS, HEAD_DIM))
        g_p.append(gvl)

        q, k, v, u, gv = _inproj_s(l, xs, mod_s[l], n1, w_in, qn, kn, gmn, seg, cos_s, sin_s)
        qg = jnp.pad(q.reshape(pairs, KV_GROUP, HEAD_DIM), ((0, 0), (0, SUB - KV_GROUP), (0, 0)))
        sink = jnp.pad(jnp.tile(attn_sinks[l].reshape(N_KV_HEADS, KV_GROUP), (nd, 1)),
                       ((0, 0), (0, SUB - KV_GROUP)))[:, :, None]
        per_step = lambda a: a.T.reshape(KV_WIDTH, -1, ATTN_S_PAIRS // N_KV_HEADS).transpose(1, 0, 2)
        o, nk, nv = _attn_s(l, qg, per_step(k), per_step(v), ck_t, cv_t, sink)
        o = o[:, :KV_GROUP, :].reshape(nd, ATTN_WIDTH)
        wdiag = jnp.repeat(gm_ws[l][:, 0, 0], GM_WIDTH // GM_GROUPS)[None, :]
        bsrow = jnp.repeat(gm_bs[l][:, 0], GM_WIDTH // GM_GROUPS)[None, :]
        res = _outproj_s(l, o, u, gv, wdiag, bsrow, xs, mod_s[l], w_out, n2, router)
        if router is None:
            xs = _ffn_s(res[1], res[0], mod_s[l], dense_w_gate[i], dense_w_up[i], dense_w_down[i])
        else:
            xs = _moe_s(res[1], res[0], res[2], mod_s[l], wg_bf, wu_bf, wd_bf)
        k_s.append(nk)
        v_s.append(nv)
        g_s.append(gv[:, None, :])

    return (xp.reshape(batch, seq, d), xs.reshape(nd, 1, d), jnp.stack(k_p), jnp.stack(v_p), jnp.stack(g_p),
            uncache(k_s), uncache(v_s), jnp.stack(g_s))
```

```python
import functools

import numpy as np
import jax
import jax.numpy as jnp
from jax import lax
from jax.experimental import pallas as pl
from jax.experimental.pallas import tpu as pltpu

D_MODEL = 1024
HEAD_DIM = 64
N_HEADS = 8
N_KV_HEADS = 2
KV_GROUP = N_HEADS // N_KV_HEADS
ATTN_WIDTH = N_HEADS * HEAD_DIM
KV_WIDTH = N_KV_HEADS * HEAD_DIM
GM_WIDTH = 512
GM_GROUPS = 4
WINDOW = 128
CHUNK = 128
D_FF = 2816
MXU_DIM = 256
FF_SPLIT = (0, 6 * MXU_DIM, D_FF)
N_EXPERTS = 8
N_ADA = 6
IN_COLS = ATTN_WIDTH + 2 * KV_WIDTH + 2 * GM_WIDTH
PAST_LEN = 16384
ROPE_THETA = 10000.0
EPS = 1e-6
NEG_INF = -1e30
LOG2E = 1.4426950408889634
LANES = 128
SUB = 8
assert D_MODEL == SUB * LANES

BF = jnp.bfloat16
F32 = jnp.float32
MIB = 1024 * 1024


WIDE_TILE = 1024
ROW_TILE = 512
ROW_GROUP = 256
ADA_COLS = 1024
VMEM_SMALL, VMEM_MID, VMEM_BIG = 32, 48, 56


def _params(sem, vmem_mib):
    return pltpu.CompilerParams(dimension_semantics=sem, vmem_limit_bytes=vmem_mib * MIB)


def _dot(a, b):
    return jnp.dot(a, b, preferred_element_type=F32)


def _dot_nt(a, b):
    return lax.dot_general(a, b, (((1,), (1,)), ((), ())), preferred_element_type=F32)


def _split(a):
    hi = a.astype(BF)
    return hi, (a - hi.astype(F32)).astype(BF)


def _dot_bf(a, w):
    return _dot(a.astype(BF), w)


def _dot3(a, w):
    ah, al = _split(a)
    if w.dtype == BF:
        return _dot(ah, w) + _dot(al, w)
    wh, wl = _split(w)
    return _dot(ah, wh) + _dot(al, wh) + _dot(ah, wl)


def _rms(x, w):
    ms = jnp.mean(x * x, axis=-1, keepdims=True)
    return x * lax.rsqrt(ms + EPS) * w


def _ada_body(c_ref, w_ref, b_ref, o_ref):
    o_ref[...] = _dot3(jax.nn.silu(c_ref[...]), w_ref[...]) + b_ref[...]


def _ada(c_all, w_ada, b_ada):
    depth, d, cols = w_ada.shape
    n = c_all.shape[0]
    tn = ADA_COLS
    return pl.pallas_call(
        _ada_body,
        grid=(depth, cols // tn),
        in_specs=[
            pl.BlockSpec((n, d), lambda l, j: (0, 0)),
            pl.BlockSpec((None, d, tn), lambda l, j: (l, 0, j)),
            pl.BlockSpec((None, 1, tn), lambda l, j: (l, 0, j)),
        ],
        out_specs=pl.BlockSpec((None, n, tn), lambda l, j: (l, 0, j)),
        out_shape=jax.ShapeDtypeStruct((depth, n, cols), F32),
        compiler_params=_params(("arbitrary", "arbitrary"), VMEM_SMALL),
        name="ada",
    )(c_all, w_ada, b_ada.reshape(depth, 1, cols))


def _swap_halves(t):
    n = t.shape[-1]
    lane = lax.broadcasted_iota(jnp.int32, (1, n), 1)
    first = (lane % HEAD_DIM) < (HEAD_DIM // 2)
    return jnp.where(first, pltpu.roll(t, n - HEAD_DIM // 2, axis=1), pltpu.roll(t, HEAD_DIM // 2, axis=1))


def _inproj_compute(mm, x, sh, sc, n1, w_ref, qn, kn, gmn, seg_ref, cos, sin):
    h = _rms(x, n1) * (1.0 + sc) + sh
    z = mm(h, w_ref[...])
    q = z[:, :ATTN_WIDTH]
    k = z[:, ATTN_WIDTH:ATTN_WIDTH + KV_WIDTH]
    v = z[:, ATTN_WIDTH + KV_WIDTH:ATTN_WIDTH + 2 * KV_WIDTH]
    gm = z[:, ATTN_WIDTH + 2 * KV_WIDTH:]

    def head_norm(t, seg, wn):
        ms = mm(t * t, seg) * (1.0 / HEAD_DIM)
        return t * lax.rsqrt(ms + EPS) * wn

    def rope(t):
        reps = t.shape[-1] // LANES
        c = jnp.concatenate([cos] * reps, axis=-1) if reps > 1 else cos
        s = jnp.concatenate([sin] * reps, axis=-1) if reps > 1 else sin
        return t * c + _swap_halves(t) * s

    q = rope(head_norm(q, seg_ref[...], qn)) * (HEAD_DIM ** -0.5 * LOG2E)
    k = rope(head_norm(k, seg_ref[:KV_WIDTH, :KV_WIDTH], kn))
    g = jax.nn.gelu(gm)
    u = g[:, :GM_WIDTH]
    gv = _rms(g[:, GM_WIDTH:], gmn)
    return q, k, v, u, gv


def _dup_heads(t):
    lane = lax.broadcasted_iota(jnp.int32, (1, LANES), 1)
    lo = lane < HEAD_DIM
    r = pltpu.roll(t, HEAD_DIM, axis=1)
    return jnp.concatenate([jnp.where(lo, t, r), jnp.where(lo, r, t)], axis=-1)


def _inproj_p_body(tiles_per_batch, ncast, x_ref, mod_ref, n1_ref, w_ref, qn_ref, kn_ref, gmn_ref, seg_ref,
                   cos_ref, sin_ref, *rest):
    cast_src = rest[:ncast]
    q_ref, kd_ref, vd_ref, u_ref, gv_ref, kl_ref, vl_ref, gvl_ref = rest[ncast:ncast + 8]
    cast_dst = rest[ncast + 8:]
    for src, dst in zip(cast_src, cast_dst):
        dst[...] = src[...].astype(BF)
    i = pl.program_id(0)
    b = i // tiles_per_batch
    sh = mod_ref[0, pl.ds(b, 1), :]
    sc = mod_ref[1, pl.ds(b, 1), :]
    hs = min(ROW_GROUP, x_ref.shape[0])
    for hh in range(x_ref.shape[0] // hs):
        rs = slice(hh * hs, (hh + 1) * hs)
        q, k, v, u, gv = _inproj_compute(_dot_bf, x_ref[rs, :], sh, sc, n1_ref[...], w_ref, qn_ref[...], kn_ref[...],
                                         gmn_ref[...], seg_ref, cos_ref[rs, :], sin_ref[rs, :])
        q_ref[rs, :] = q.astype(BF)
        kd_ref[rs, :] = _dup_heads(k).astype(BF)
        vd_ref[rs, :] = _dup_heads(v).astype(BF)
        u_ref[rs, :] = u.astype(BF)
        gv_ref[rs, :] = gv.astype(BF)

    @pl.when(i % tiles_per_batch == tiles_per_batch - 1)
    def _():
        kl_ref[...] = k[hs - WINDOW:, :]
        vl_ref[...] = v[hs - WINDOW:, :]
        gvl_ref[...] = gv[hs - CHUNK:, :]


BF16_SUBLANES = 16


def _slab_spec(rows, cols, steps):
    s = steps
    while rows % s or (rows // s) % BF16_SUBLANES:
        s //= 2
    return pl.BlockSpec((rows // s, cols), lambda i: (i // (steps // s), 0))


def _inproj_p(l, x, mod, n1, w_bf, qn, kn, gmn, seg, cos, sin, casts, batch, seq):
    t = x.shape[0]
    tm = min(WIDE_TILE, seq)
    tpb = seq // tm
    steps = t // tm
    row = lambda i: (i, 0)
    full = lambda i: (0, 0)
    last = lambda i: (i // tpb, 0, 0)
    cast_specs = [_slab_spec(c.shape[0], c.shape[1], steps) for c in casts]
    return pl.pallas_call(
        functools.partial(_inproj_p_body, tpb, len(casts)),
        grid=(steps,),
        in_specs=[
            pl.BlockSpec((tm, D_MODEL), row),
            pl.BlockSpec((N_ADA, batch, D_MODEL), lambda i: (0, 0, 0)),
            pl.BlockSpec((1, D_MODEL), full),
            pl.BlockSpec((None, D_MODEL, IN_COLS), lambda i: (l, 0, 0)),
            pl.BlockSpec((1, ATTN_WIDTH), full),
            pl.BlockSpec((1, KV_WIDTH), full),
            pl.BlockSpec((1, GM_WIDTH), full),
            pl.BlockSpec((ATTN_WIDTH, ATTN_WIDTH), full),
            pl.BlockSpec((tm, LANES), lambda i: (i % tpb, 0)),
            pl.BlockSpec((tm, LANES), lambda i: (i % tpb, 0)),
        ] + cast_specs,
        out_specs=[
            pl.BlockSpec((tm, ATTN_WIDTH), row),
            pl.BlockSpec((tm, 2 * KV_WIDTH), row),
            pl.BlockSpec((tm, 2 * KV_WIDTH), row),
            pl.BlockSpec((tm, GM_WIDTH), row),
            pl.BlockSpec((tm, GM_WIDTH), row),
            pl.BlockSpec((None, WINDOW, KV_WIDTH), last),
            pl.BlockSpec((None, WINDOW, KV_WIDTH), last),
            pl.BlockSpec((None, CHUNK, GM_WIDTH), last),
        ] + cast_specs,
        out_shape=[
            jax.ShapeDtypeStruct((t, ATTN_WIDTH), BF),
            jax.ShapeDtypeStruct((t, 2 * KV_WIDTH), BF),
            jax.ShapeDtypeStruct((t, 2 * KV_WIDTH), BF),
            jax.ShapeDtypeStruct((t, GM_WIDTH), BF),
            jax.ShapeDtypeStruct((t, GM_WIDTH), BF),
            jax.ShapeDtypeStruct((batch, WINDOW, KV_WIDTH), F32),
            jax.ShapeDtypeStruct((batch, WINDOW, KV_WIDTH), F32),
            jax.ShapeDtypeStruct((batch, CHUNK, GM_WIDTH), F32),
        ] + [jax.ShapeDtypeStruct(c.shape, BF) for c in casts],
        compiler_params=_params(("arbitrary",), VMEM_BIG),
        name="inproj_p",
    )(x, mod, n1, w_bf, qn, kn, gmn, seg, cos, sin, *casts)


def _inproj_s_body(x_ref, mod_ref, n1_ref, w_ref, qn_ref, kn_ref, gmn_ref, seg_ref, cos_ref, sin_ref,
                   q_ref, k_ref, v_ref, u_ref, gv_ref):
    q, k, v, u, gv = _inproj_compute(_dot3, x_ref[...], mod_ref[0], mod_ref[1], n1_ref[...], w_ref, qn_ref[...],
                                     kn_ref[...], gmn_ref[...], seg_ref, cos_ref[...], sin_ref[...])
    q_ref[...] = q
    k_ref[...] = k
    v_ref[...] = v
    u_ref[...] = u
    gv_ref[...] = gv


def _whole(a):
    return pl.BlockSpec(a.shape, lambda i: (0,) * a.ndim)


def _layer_of(w, l):
    return pl.BlockSpec((None,) + w.shape[1:], lambda i: (l,) + (0,) * (w.ndim - 1))


def _inproj_s(l, x, mod, n1, w, qn, kn, gmn, seg, cos, sin):
    n = x.shape[0]
    widths = (ATTN_WIDTH, KV_WIDTH, KV_WIDTH, GM_WIDTH, GM_WIDTH)
    args = (x, mod, n1, w, qn, kn, gmn, seg, cos, sin)
    return pl.pallas_call(
        _inproj_s_body,
        grid=(1,),
        in_specs=[_layer_of(a, l) if a is w else _whole(a) for a in args],
        out_specs=[pl.BlockSpec((n, c), lambda i: (0, 0)) for c in widths],
        out_shape=[jax.ShapeDtypeStruct((n, c), F32) for c in widths],
        compiler_params=_params(("arbitrary",), VMEM_MID),
        name="inproj_s",
    )(*args)


def _mix_p_body(nblk, q_ref, kc_ref, kp_ref, vc_ref, vp_ref, u_ref, gv_ref, ws_ref, bst_ref, sink_ref, o_ref):
    i = pl.program_id(1)
    blk = WINDOW
    lane = lax.broadcasted_iota(jnp.int32, (1, LANES), 1)
    lo = lane < HEAD_DIM
    cols = KV_GROUP * blk
    iq = lax.broadcasted_iota(jnp.int32, (2 * blk, cols), 1) % blk
    jk = lax.broadcasted_iota(jnp.int32, (2 * blk, cols), 0)
    band = (jk > iq) & (jk <= iq + blk)
    bias = jnp.where(band, 0.0, NEG_INF)
    bias_first = jnp.where(band & ((jk >= blk) | (i > 0)), 0.0, NEG_INF)
    tri = (lax.broadcasted_iota(jnp.int32, (CHUNK, CHUNK), 0)
           >= lax.broadcasted_iota(jnp.int32, (CHUNK, CHUNK), 1))
    wm = [jnp.where(tri, ws_ref[g], 0.0).astype(BF) for g in range(GM_GROUPS)]

    for n in range(nblk):
        r0 = n * blk
        if n == 0:
            kk = jnp.concatenate([kp_ref[...], kc_ref[0:blk, :]], axis=0)
            vv = jnp.concatenate([vp_ref[...], vc_ref[0:blk, :]], axis=0)
            mask_bias = bias_first
        else:
            kk = kc_ref[r0 - blk:r0 + blk, :]
            vv = vc_ref[r0 - blk:r0 + blk, :]
            mask_bias = bias
        for kvh in range(N_KV_HEADS):
            c0 = 2 * kvh
            qa = q_ref[r0:r0 + blk, c0 * LANES:(c0 + 1) * LANES]
            qb = q_ref[r0:r0 + blk, (c0 + 1) * LANES:(c0 + 2) * LANES]
            zero = jnp.zeros_like(qa)
            qq = jnp.concatenate([jnp.where(lo, qa, zero), jnp.where(lo, zero, qa),
                                  jnp.where(lo, qb, zero), jnp.where(lo, zero, qb)], axis=0)
            s = _dot_nt(kk[:, kvh * LANES:(kvh + 1) * LANES], qq) + mask_bias
            sink = jnp.concatenate(
                [jnp.full((1, blk), sink_ref[kvh * KV_GROUP + g] * LOG2E, F32) for g in range(KV_GROUP)], axis=1)
            m = jnp.maximum(jnp.max(s, axis=0, keepdims=True), sink)
            p = jnp.exp2(s - m)
            den = jnp.sum(p, axis=0, keepdims=True) + jnp.exp2(sink - m)
            p = (p * (1.0 / den)).astype(BF)
            o = lax.dot_general(p, vv[:, kvh * LANES:(kvh + 1) * LANES], (((0,), (0,)), ((), ())),
                                preferred_element_type=F32)
            o_ref[r0:r0 + blk, c0 * LANES:(c0 + 1) * LANES] = jnp.where(
                lo, o[0:blk], o[blk:2 * blk]).astype(BF)
            o_ref[r0:r0 + blk, (c0 + 1) * LANES:(c0 + 2) * LANES] = jnp.where(
                lo, o[2 * blk:3 * blk], o[3 * blk:4 * blk]).astype(BF)
        for g in range(GM_GROUPS):
            cs = slice(g * LANES, (g + 1) * LANES)
            sp = _dot(wm[g], gv_ref[r0:r0 + blk, cs]) + bst_ref[:, g:g + 1]
            o_ref[r0:r0 + blk, ATTN_WIDTH + g * LANES:ATTN_WIDTH + (g + 1) * LANES] = (
                u_ref[r0:r0 + blk, cs].astype(F32) * sp).astype(BF)


def _mix_p(q, kd, vd, u, gv, ws, bst, sinks, batch, seq):
    t = q.shape[0]
    tq = min(WIDE_TILE, seq)
    nblk = tq // WINDOW
    tpb = seq // tq
    cur = lambda b, i: (b * tpb + i, 0)
    prev = lambda b, i: (jnp.maximum((b * tpb + i) * nblk - 1, b * tpb * nblk), 0)
    return pl.pallas_call(
        functools.partial(_mix_p_body, nblk),
        grid=(batch, tpb),
        in_specs=[
            pl.BlockSpec((tq, ATTN_WIDTH), cur),
            pl.BlockSpec((tq, 2 * KV_WIDTH), cur),
            pl.BlockSpec((WINDOW, 2 * KV_WIDTH), prev),
            pl.BlockSpec((tq, 2 * KV_WIDTH), cur),
            pl.BlockSpec((WINDOW, 2 * KV_WIDTH), prev),
            pl.BlockSpec((tq, GM_WIDTH), cur),
            pl.BlockSpec((tq, GM_WIDTH), cur),
            pl.BlockSpec((GM_GROUPS, CHUNK, CHUNK), lambda b, i: (0, 0, 0)),
            pl.BlockSpec((CHUNK, GM_GROUPS), lambda b, i: (0, 0)),
            pl.BlockSpec(memory_space=pltpu.SMEM),
        ],
        out_specs=pl.BlockSpec((tq, D_MODEL), cur),
        out_shape=jax.ShapeDtypeStruct((t, D_MODEL), BF),
        compiler_params=_params(("arbitrary", "arbitrary"), VMEM_MID),
        name="mix_p",
    )(q, kd, kd, vd, vd, u, gv, ws, bst, sinks)


ATTN_S_PAIRS = 32

def _attn_s_body(q_ref, kn_ref, vn_ref, ck_ref, cv_ref, sink_ref, o_ref, nk_ref, nv_ref):
    nb, hd, w = ck_ref.shape
    pos = lax.broadcasted_iota(jnp.int32, (1, w), 1)
    for p in range(nb):
        b, kvh = divmod(p, N_KV_HEADS)
        for new_ref, c_ref, n_ref in ((kn_ref, ck_ref, nk_ref), (vn_ref, cv_ref, nv_ref)):
            new = new_ref[kvh * hd:(kvh + 1) * hd, b:b + 1]
            n_ref[p] = jnp.where(pos == w - 1, new, pltpu.roll(c_ref[p], w - 1, axis=1))
    nk = nk_ref[...]
    nv = nv_ref[...]

    def bmm3(spec, a, b):
        (ah, al), (bh, bl) = _split(a), _split(b)
        mm = lambda x, y: jnp.einsum(spec, x, y, preferred_element_type=F32)
        return mm(ah, bh) + mm(al, bh) + mm(ah, bl)

    s = bmm3('ngd,ndj->ngj', q_ref[...], nk)
    sink = sink_ref[...] * LOG2E
    m = jnp.maximum(jnp.max(s, axis=-1, keepdims=True), sink)
    p = jnp.exp2(s - m)
    den = jnp.sum(p, axis=-1, keepdims=True) + jnp.exp2(sink - m)
    o_ref[...] = bmm3('ngj,ndj->ngd', p, nv) * (1.0 / den)


def _attn_s(l, q, k_new, v_new, ck, cv, sink):
    _, n, hd, w = ck.shape
    nb = ATTN_S_PAIRS
    assert n % nb == 0
    rows = q.shape[1]
    blk = lambda r, c: pl.BlockSpec((nb, r, c), lambda i: (i, 0, 0))
    cache = pl.BlockSpec((None, nb, hd, w), lambda i: (l, i, 0, 0))
    new = pl.BlockSpec((None,) + k_new.shape[1:], lambda i: (i, 0, 0))
    return pl.pallas_call(
        _attn_s_body,
        grid=(n // nb,),
        in_specs=[blk(rows, hd), new, new, cache, cache, blk(rows, 1)],
        out_specs=[blk(rows, hd), blk(hd, w), blk(hd, w)],
        out_shape=[jax.ShapeDtypeStruct((n, rows, hd), F32),
                   jax.ShapeDtypeStruct((n, hd, w), F32),
                   jax.ShapeDtypeStruct((n, hd, w), F32)],
        compiler_params=_params(("arbitrary",), VMEM_SMALL),
        name="attn_s",
    )(q, k_new, v_new, ck, cv, sink)


def _top2(h2, rw_ref, rb_ref):
    hh, hl = _split(h2)
    wh, wl = _split(rw_ref[...])
    both = _dot(hh, jnp.concatenate([wh, wl], axis=1))
    logits = both[:, :LANES] + _dot(hl, wh) + both[:, LANES:] + rb_ref[...]
    lane = lax.broadcasted_iota(jnp.int32, logits.shape, 1).astype(F32)
    e = jnp.exp(logits - jnp.max(logits, axis=-1, keepdims=True))
    p = e / jnp.sum(e, axis=-1, keepdims=True)
    m1 = jnp.max(p, axis=-1, keepdims=True)
    i1 = jnp.min(jnp.where(p == m1, lane, float(LANES)), axis=-1, keepdims=True)
    p2 = jnp.where(lane == i1, -1.0, p)
    m2 = jnp.max(p2, axis=-1, keepdims=True)
    i2 = jnp.min(jnp.where(p2 == m2, lane, float(LANES)), axis=-1, keepdims=True)
    tot = m1 + m2
    return lane, i1, i2, m1 / tot, m2 / tot


def _route_gates(h2, rw_ref, rb_ref):
    lane, i1, i2, g1, g2 = _top2(h2, rw_ref, rb_ref)
    return jnp.where(lane == i1, g1, 0.0) + jnp.where(lane == i2, g2, 0.0)


ROUTE_E, ROUTE_RANK, ROUTE_GATE = 0, 2, 4


def _route_ranked(h2, rw_ref, rb_ref, tri_ref, cnt_ref):
    lane, i1, i2, g1, g2 = _top2(h2, rw_ref, rb_ref)
    oh1 = lane == i1
    oh2 = lane == i2
    hit = jnp.where(oh1, 1.0, 0.0) + jnp.where(oh2, 1.0, 0.0)
    before = cnt_ref[...] + _dot(tri_ref[...], hit.astype(BF))
    r1 = jnp.sum(jnp.where(oh1, before, 0.0), axis=-1, keepdims=True)
    r2 = jnp.sum(jnp.where(oh2, before, 0.0), axis=-1, keepdims=True)
    cnt_ref[...] += jnp.sum(hit, axis=0, keepdims=True)
    cols = (i1, i2, r1, r2, g1, g2)
    out = jnp.zeros_like(lane)
    for j, c in enumerate(cols):
        out = jnp.where(lane == float(j), c, out)
    return out


def _outproj_compute(mm, mix, x, ga1, sh2, sc2, w_ref, n2):
    xn = x + ga1 * mm(mix, w_ref[...])
    h2 = _rms(xn, n2) * (1.0 + sc2) + sh2
    return xn, h2


def _outproj_p_body(tiles_per_batch, mix_ref, x_ref, mod_ref, w_ref, n2_ref, rw_ref, rb_ref, tri_ref,
                    xn_ref, h2_ref, route_ref, route_t_ref, cnt_ref):
    i = pl.program_id(0)
    b = i // tiles_per_batch
    mrow = lambda j: mod_ref[j, pl.ds(b, 1), :]

    @pl.when(i == 0)
    def _():
        cnt_ref[...] = jnp.zeros_like(cnt_ref)

    hs = tri_ref.shape[0]
    for hh in range(x_ref.shape[0] // hs):
        rs = slice(hh * hs, (hh + 1) * hs)
        xn, h2 = _outproj_compute(_dot_bf, mix_ref[rs, :], x_ref[rs, :], mrow(2), mrow(3), mrow(4), w_ref, n2_ref[...])
        route = _route_ranked(h2, rw_ref, rb_ref, tri_ref, cnt_ref)
        route_ref[rs, :] = route
        route_t_ref[:, rs] = route.T[:SUB, :]
        h2_ref[rs, :] = h2
        xn_ref[rs, :] = xn


def _outproj_p(l, mix, x, mod, w_bf, n2, rw, rb, batch, seq):
    t = x.shape[0]
    tm = min(WIDE_TILE, seq)
    group = min(ROW_TILE, tm)
    row = lambda i: (i, 0)
    full = lambda i: (0, 0)
    return pl.pallas_call(
        functools.partial(_outproj_p_body, seq // tm),
        grid=(t // tm,),
        in_specs=[
            pl.BlockSpec((tm, D_MODEL), row),
            pl.BlockSpec((tm, D_MODEL), row),
            pl.BlockSpec((N_ADA, batch, D_MODEL), lambda i: (0, 0, 0)),
            pl.BlockSpec((None, D_MODEL, D_MODEL), lambda i: (l, 0, 0)),
            pl.BlockSpec((1, D_MODEL), full),
            pl.BlockSpec((D_MODEL, LANES), full),
            pl.BlockSpec((1, LANES), full),
            pl.BlockSpec((group, group), full),
        ],
        out_specs=[pl.BlockSpec((tm, D_MODEL), row), pl.BlockSpec((tm, D_MODEL), row),
                   pl.BlockSpec((tm, LANES), row), pl.BlockSpec((SUB, tm), lambda i: (0, i)),
                   pl.BlockSpec((1, LANES), full)],
        out_shape=[jax.ShapeDtypeStruct((t, D_MODEL), F32), jax.ShapeDtypeStruct((t, D_MODEL), F32),
                   jax.ShapeDtypeStruct((t, LANES), F32), jax.ShapeDtypeStruct((SUB, t), F32),
                   jax.ShapeDtypeStruct((1, LANES), F32)],
        compiler_params=_params(("arbitrary",), VMEM_MID),
        name="outproj_p",
    )(mix, x, mod, w_bf, n2, rw, rb, jnp.asarray(np.tri(group, k=-1), BF))


def _outproj_s_body(with_router, o_ref, u_ref, gv_ref, wdiag_ref, bsrow_ref, x_ref, mod_ref, w_ref, n2_ref, *rest):
    gate = u_ref[...] * (wdiag_ref[...] * gv_ref[...] + bsrow_ref[...])
    mix = jnp.concatenate([o_ref[...], gate], axis=-1)
    xn, h2 = _outproj_compute(_dot3, mix, x_ref[...], mod_ref[2], mod_ref[3], mod_ref[4], w_ref, n2_ref[...])
    if with_router:
        rw_ref, rb_ref, xn_ref, h2_ref, gates_ref = rest
        gates_ref[...] = _route_gates(h2, rw_ref, rb_ref)
    else:
        xn_ref, h2_ref = rest
    xn_ref[...] = xn
    h2_ref[...] = h2


def _outproj_s(l, o, u, gv, wdiag, bsrow, x, mod, w, n2, router):
    n = x.shape[0]
    out_shape = [jax.ShapeDtypeStruct((n, D_MODEL), F32), jax.ShapeDtypeStruct((n, D_MODEL), F32)]
    args = [o, u, gv, wdiag, bsrow, x, mod, w, n2]
    if router is not None:
        out_shape.append(jax.ShapeDtypeStruct((n, LANES), F32))
        args += list(router)
    return pl.pallas_call(
        functools.partial(_outproj_s_body, router is not None),
        grid=(1,),
        in_specs=[_layer_of(a, l) if a is w else _whole(a) for a in args],
        out_specs=[pl.BlockSpec(s.shape, lambda i: (0, 0)) for s in out_shape],
        out_shape=out_shape,
        compiler_params=_params(("arbitrary",), VMEM_SMALL),
        name="outproj_s",
    )(*args)


def _swiglu(h_bf, wg_ref, wu_ref, wd_ref):
    y = None
    for c in range(len(FF_SPLIT) - 1):
        sl = slice(FF_SPLIT[c], FF_SPLIT[c + 1])
        a = (jax.nn.silu(_dot(h_bf, wg_ref[:, sl])) * _dot(h_bf, wu_ref[:, sl])).astype(BF)
        part = _dot(a, wd_ref[sl, :])
        y = part if y is None else y + part
    return y


def _ffn_s_body(h_ref, x_ref, mod_ref, wg_ref, wu_ref, wd_ref, o_ref):
    h = h_ref[...]
    y = jnp.zeros_like(h)
    for c in range(D_FF // MXU_DIM):
        sl = slice(c * MXU_DIM, (c + 1) * MXU_DIM)
        a = jax.nn.silu(_dot3(h, wg_ref[:, sl])) * _dot3(h, wu_ref[:, sl])
        y = y + _dot3(a, wd_ref[sl, :])
    o_ref[...] = x_ref[...] + mod_ref[5] * y


def _ffn_s(h2, x, mod, wg, wu, wd):
    return pl.pallas_call(
        _ffn_s_body,
        out_shape=jax.ShapeDtypeStruct(x.shape, F32),
        compiler_params=pltpu.CompilerParams(vmem_limit_bytes=VMEM_BIG * MIB),
        name="ffn_s",
    )(h2, x, mod, wg, wu, wd)


def _outffn_p_body(tiles_per_batch, mix_ref, x_ref, mod_ref, w_ref, n2_ref, wg_ref, wu_ref, wd_ref, o_ref):
    b = pl.program_id(0) // tiles_per_batch
    mrow = lambda j: mod_ref[j, pl.ds(b, 1), :]
    xn, h2 = _outproj_compute(_dot_bf, mix_ref[...], x_ref[...], mrow(2), mrow(3), mrow(4), w_ref, n2_ref[...])
    o_ref[...] = xn + mrow(5) * _swiglu(h2.astype(BF), wg_ref, wu_ref, wd_ref)


def _outffn_p(l, mix, x, mod, w_bf, n2, wg, wu, wd, batch, seq):
    t = x.shape[0]
    tm = min(ROW_TILE, seq)
    row = lambda i: (i, 0)
    const = lambda shape: pl.BlockSpec(shape, lambda i: (0,) * len(shape), pipeline_mode=pl.Buffered(1))
    return pl.pallas_call(
        functools.partial(_outffn_p_body, seq // tm),
        grid=(t // tm,),
        in_specs=[
            pl.BlockSpec((tm, D_MODEL), row),
            pl.BlockSpec((tm, D_MODEL), row),
            const((N_ADA, batch, D_MODEL)),
            pl.BlockSpec((None, D_MODEL, D_MODEL), lambda i: (l, 0, 0), pipeline_mode=pl.Buffered(1)),
            const((1, D_MODEL)),
            const((D_MODEL, D_FF)), const((D_MODEL, D_FF)), const((D_FF, D_MODEL)),
        ],
        out_specs=pl.BlockSpec((tm, D_MODEL), row),
        out_shape=jax.ShapeDtypeStruct((t, D_MODEL), F32),
        compiler_params=_params(("arbitrary",), VMEM_BIG),
        name="outffn_p",
    )(mix, x, mod, w_bf, n2, wg, wu, wd)


def _moe_s_body(h_ref, x_ref, gates_ref, mod_ref, wg_ref, wu_ref, wd_ref, o_ref):
    e = pl.program_id(0)

    @pl.when(e == 0)
    def _():
        o_ref[...] = jnp.zeros_like(o_ref)

    lane = lax.broadcasted_iota(jnp.int32, (1, LANES), 1)
    gate = jnp.sum(jnp.where(lane == e, gates_ref[...], 0.0), axis=-1, keepdims=True)
    o_ref[...] += gate * _swiglu(h_ref[...].astype(BF), wg_ref, wu_ref, wd_ref)

    @pl.when(e == N_EXPERTS - 1)
    def _():
        o_ref[...] = x_ref[...] + mod_ref[5] * o_ref[...]


def _moe_s(h2, x, gates, mod, wg, wu, wd):
    n = x.shape[0]
    whole = lambda shape: pl.BlockSpec(shape, lambda e: (0,) * len(shape))
    wspec = lambda shape: pl.BlockSpec((None,) + shape, lambda e: (e, 0, 0))
    return pl.pallas_call(
        _moe_s_body,
        grid=(N_EXPERTS,),
        in_specs=[whole((n, D_MODEL)), whole((n, D_MODEL)), whole((n, LANES)), whole(mod.shape),
                  wspec((D_MODEL, D_FF)), wspec((D_MODEL, D_FF)), wspec((D_FF, D_MODEL))],
        out_specs=whole((n, D_MODEL)),
        out_shape=jax.ShapeDtypeStruct((n, D_MODEL), F32),
        compiler_params=_params(("arbitrary",), VMEM_BIG),
        name="moe_s",
    )(h2, x, gates, mod, wg, wu, wd)


TM_MOE = 512
TD = 512


def _to_token_tiles(ref, x):
    r = x.shape[0]
    for g in range(SUB):
        ref[pl.ds(g, r, stride=SUB), :] = x[:, g * LANES:(g + 1) * LANES]


def _from_token_tiles(ref, first, r):
    return jnp.concatenate([ref[pl.ds(first * SUB + g, r, stride=SUB), :] for g in range(SUB)], axis=-1)


def _token_copy(src, s, dst, d, sem):
    aligned = lambda v: v if isinstance(v, int) else pl.multiple_of(v, SUB)
    return pltpu.make_async_copy(src.at[pl.ds(aligned(s), SUB), :], dst.at[pl.ds(aligned(d), SUB), :], sem)


def _dispatch_body(pos_ref, pad_ref, h_ref, xs_ref, stage, sem, zsem):
    i = pl.program_id(0)
    n = pl.num_programs(0)
    td = h_ref.shape[0]
    slot = i % 2

    def wait_slot(s):
        for _ in range(2):
            pltpu.make_async_copy(stage.at[s], xs_ref.at[pl.ds(0, td * SUB), :], sem.at[s]).wait()

    @pl.when(i >= 2)
    def _():
        wait_slot(slot)

    _to_token_tiles(stage.at[slot], h_ref[...])

    def issue(r, c):
        for k in range(2):
            _token_copy(stage.at[slot], r * SUB, xs_ref, pos_ref[0, 0, k * td + r], sem.at[slot]).start(priority=k)
        return c

    lax.fori_loop(0, td, issue, 0, unroll=8)

    @pl.when(i == n - 1)
    def _():
        wait_slot(slot)

        @pl.when(n > 1)
        def _():
            wait_slot(1 - slot)

        stage[0] = jnp.zeros(stage.shape[1:], stage.dtype)
        for e in range(N_EXPERTS):
            lo = pad_ref[0, e]
            hi = pad_ref[1, e]

            def zero_token(r, c):
                _token_copy(stage.at[0], 0, xs_ref, r * SUB, zsem).start()
                return c

            def wait_token(r, c):
                _token_copy(stage.at[0], 0, xs_ref, 0, zsem).wait()
                return c

            lax.fori_loop(lo, hi, zero_token, 0)
            lax.fori_loop(lo, hi, wait_token, 0)

        def zero_blk(j, c):
            pltpu.make_async_copy(stage.at[0], xs_ref.at[pl.ds(pl.multiple_of(j * (td * SUB), SUB), td * SUB), :],
                                  zsem).start()
            return c

        def wait_blk(j, c):
            pltpu.make_async_copy(stage.at[0], xs_ref.at[pl.ds(0, td * SUB), :], zsem).wait()
            return c

        lax.fori_loop(pad_ref[0, N_EXPERTS], pad_ref[1, N_EXPERTS], zero_blk, 0)
        lax.fori_loop(pad_ref[0, N_EXPERTS], pad_ref[1, N_EXPERTS], wait_blk, 0)


def _dispatch(h2, pos_t, pad, npad):
    t = h2.shape[0]
    td = min(TD, t)
    return pl.pallas_call(
        _dispatch_body,
        grid=(t // td,),
        in_specs=[
            pl.BlockSpec((1, 1, 2 * td), lambda i: (i, 0, 0), memory_space=pltpu.SMEM),
            pl.BlockSpec(memory_space=pltpu.SMEM),
            pl.BlockSpec((td, D_MODEL), lambda i: (i, 0)),
        ],
        out_specs=pl.BlockSpec(memory_space=pl.ANY),
        out_shape=jax.ShapeDtypeStruct((npad * SUB, LANES), F32),
        scratch_shapes=[pltpu.VMEM((2, td * SUB, LANES), F32), pltpu.SemaphoreType.DMA((2,)),
                        pltpu.SemaphoreType.DMA(())],
        compiler_params=_params(("arbitrary",), VMEM_SMALL),
        name="dispatch",
    )(pos_t, pad, h2)


def _moe_body(te_ref, src_ref, nv_ref, x_ref, wg_ref, wu_ref, wd_ref, o_ref):
    i = pl.program_id(0)

    @pl.when(nv_ref[i] > 0)
    def _():
        x = _from_token_tiles(x_ref, 0, TM_MOE).astype(BF)
        _to_token_tiles(o_ref, _swiglu(x, wg_ref, wu_ref, wd_ref))

    @pl.when(nv_ref[i] == 0)
    def _():
        o_ref[...] = jnp.zeros_like(o_ref)


def _moe(xs, tile_e, tile_src, tile_nv, wg, wu, wd):
    rows = TM_MOE * SUB
    wspec = lambda shape: pl.BlockSpec((None,) + shape, lambda i, te, src, nv: (te[i], 0, 0))
    return pl.pallas_call(
        _moe_body,
        grid_spec=pltpu.PrefetchScalarGridSpec(
            num_scalar_prefetch=3,
            grid=(xs.shape[0] // rows,),
            in_specs=[
                pl.BlockSpec((rows, LANES), lambda i, te, src, nv: (src[i], 0)),
                wspec((D_MODEL, D_FF)), wspec((D_MODEL, D_FF)), wspec((D_FF, D_MODEL)),
            ],
            out_specs=pl.BlockSpec((rows, LANES), lambda i, te, src, nv: (i, 0)),
        ),
        out_shape=jax.ShapeDtypeStruct(xs.shape, F32),
        compiler_params=_params(("arbitrary",), VMEM_BIG),
        name="moe",
    )(tile_e, tile_src, tile_nv, xs, wg, wu, wd)


def _combine_body(tiles_per_batch, posc_ref, posn_ref, x_ref, route_ref, mod_ref, ys_ref, o_ref, buf, sem):
    i = pl.program_id(0)
    n = pl.num_programs(0)
    tc = x_ref.shape[0]

    def gather(p_ref, s):
        def issue(r, c):
            for k in range(2):
                _token_copy(ys_ref, p_ref[0, 0, k * tc + r], buf.at[s], (k * tc + r) * SUB,
                            sem.at[s]).start(priority=k)
            return c

        lax.fori_loop(0, tc, issue, 0, unroll=8)

    @pl.when(i == 0)
    def _():
        gather(posc_ref, 0)

    @pl.when(i + 1 < n)
    def _():
        gather(posn_ref, (i + 1) % 2)

    slot = i % 2
    pltpu.make_async_copy(ys_ref.at[pl.ds(0, 2 * tc * SUB), :], buf.at[slot], sem.at[slot]).wait()
    lane = lax.broadcasted_iota(jnp.int32, (1, LANES), 1)
    rt = route_ref[...]
    g1 = jnp.sum(jnp.where(lane == ROUTE_GATE, rt, 0.0), axis=-1, keepdims=True)
    g2 = jnp.sum(jnp.where(lane == ROUTE_GATE + 1, rt, 0.0), axis=-1, keepdims=True)
    y = g1 * _from_token_tiles(buf.at[slot], 0, tc) + g2 * _from_token_tiles(buf.at[slot], tc, tc)
    ga2 = mod_ref[5, pl.ds(i // tiles_per_batch, 1), :]
    o_ref[...] = x_ref[...] + ga2 * y


def _combine(ys, pos_t, x, route, mod, seq):
    t = x.shape[0]
    tc = min(TD, t)
    nt = t // tc
    row = lambda i: (i, 0)
    return pl.pallas_call(
        functools.partial(_combine_body, seq // tc),
        grid=(nt,),
        in_specs=[
            pl.BlockSpec((1, 1, 2 * tc), lambda i: (i, 0, 0), memory_space=pltpu.SMEM),
            pl.BlockSpec((1, 1, 2 * tc), lambda i: (jnp.minimum(i + 1, nt - 1), 0, 0), memory_space=pltpu.SMEM),
            pl.BlockSpec((tc, D_MODEL), row),
            pl.BlockSpec((tc, LANES), row),
            pl.BlockSpec(mod.shape, lambda i: (0, 0, 0)),
            pl.BlockSpec(memory_space=pl.ANY),
        ],
        out_specs=pl.BlockSpec((tc, D_MODEL), row),
        out_shape=jax.ShapeDtypeStruct((t, D_MODEL), F32),
        scratch_shapes=[pltpu.VMEM((2, 2 * tc * SUB, LANES), F32), pltpu.SemaphoreType.DMA((2,))],
        compiler_params=_params(("arbitrary",), VMEM_SMALL),
        name="combine",
    )(pos_t, pos_t, x, route, mod, ys)


def _moe_routed(h2, xn, route, route_t, cnt, mod, wg, wu, wd, seq):
    t = h2.shape[0]
    td = min(TD, t)
    nt_max = pl.cdiv(2 * t, TM_MOE) + N_EXPERTS
    npad = nt_max * TM_MOE
    counts = cnt[0, :N_EXPERTS].astype(jnp.int32)
    ntile = (counts + TM_MOE - 1) // TM_MOE
    eid = jnp.arange(N_EXPERTS)
    tile_end = jnp.sum(jnp.where(eid[None, :] <= eid[:, None], ntile[None, :], 0), axis=1)
    off = (tile_end - ntile) * TM_MOE
    e12 = route_t[ROUTE_E:ROUTE_E + 2].astype(jnp.int32)
    r12 = route_t[ROUTE_RANK:ROUTE_RANK + 2].astype(jnp.int32)
    onehot = e12[:, :, None] == eid[None, None, :]
    pos = jnp.sum(jnp.where(onehot, off[None, None, :], 0), axis=-1) + r12
    pos_t = (pos * SUB).reshape(2, t // td, td).transpose(1, 0, 2).reshape(t // td, 1, 2 * td)
    total = tile_end[-1]
    tid = jnp.arange(nt_max)
    tile_e = jnp.minimum(jnp.sum(tid[:, None] >= tile_end[None, :], axis=1), N_EXPERTS - 1).astype(jnp.int32)
    tile_nv = (tid < total).astype(jnp.int32)
    tile_src = jnp.minimum(tid, total - 1).astype(jnp.int32)
    pad = jnp.stack([jnp.concatenate([off + counts, (total * (TM_MOE // td))[None]]),
                     jnp.concatenate([off + ntile * TM_MOE, jnp.full((1,), npad // td, jnp.int32)])]).astype(jnp.int32)
    xs = _dispatch(h2, pos_t, pad, npad)
    ys = _moe(xs, tile_e, tile_src, tile_nv, wg, wu, wd)
    return _combine(ys, pos_t, xn, route, mod, seq)


def _rope_tables(pos):
    inv = ROPE_THETA ** (-np.arange(0, HEAD_DIM, 2, dtype=np.float64) / HEAD_DIM)
    ang = np.asarray(pos, np.float64)[:, None] * inv[None, :]
    cos = np.concatenate([np.cos(ang), np.cos(ang)], axis=-1)
    sin = np.concatenate([-np.sin(ang), np.sin(ang)], axis=-1)
    reps = LANES // HEAD_DIM
    return (jnp.asarray(np.tile(cos, (1, reps)), F32), jnp.asarray(np.tile(sin, (1, reps)), F32))


def kernel(x_prompt, x_sample, cache_k, cache_v, c_prompt, c_sample, w_ada, b_ada, norm1_w, norm2_w, w_in,
           q_norm_w, k_norm_w, attn_sinks, gm_norm_w, gm_ws, gm_bs, w_out, dense_w_gate, dense_w_up,
           dense_w_down, router_w, router_b, moe_w_gate, moe_w_up, moe_w_down):
    batch, seq, d = x_prompt.shape
    nd = x_sample.shape[0]
    depth = w_in.shape[0]
    t = batch * seq

    mod = _ada(jnp.concatenate([c_prompt, c_sample], axis=0), w_ada, b_ada)
    mod_p = mod[:, :batch].reshape(depth, batch, N_ADA, d).transpose(0, 2, 1, 3)
    mod_s = mod[:, batch:].reshape(depth, nd, N_ADA, d).transpose(0, 2, 1, 3)

    cos_p, sin_p = _rope_tables(np.arange(seq))
    cos_s, sin_s = _rope_tables(np.array([PAST_LEN]))
    head_of = np.arange(ATTN_WIDTH) // HEAD_DIM
    seg = jnp.asarray(head_of[:, None] == head_of[None, :], BF)
    pairs = nd * N_KV_HEADS
    w = cache_k.shape[2]
    ck_t = cache_k.transpose(0, 1, 3, 4, 2).reshape(depth, pairs, HEAD_DIM, w)
    cv_t = cache_v.transpose(0, 1, 3, 4, 2).reshape(depth, pairs, HEAD_DIM, w)
    uncache = lambda c: jnp.stack(c).reshape(depth, nd, N_KV_HEADS, HEAD_DIM, w).transpose(0, 1, 4, 2, 3)

    w_in_bf = w_in.astype(BF)
    w_out_bf = w_out.astype(BF)
    router_w_pad = jnp.pad(router_w, ((0, 0), (0, 0), (0, LANES - N_EXPERTS)))
    router_b_pad = jnp.pad(router_b, ((0, 0), (0, LANES - N_EXPERTS)), constant_values=NEG_INF)

    cast_plan = []
    for l in range(depth):
        ws = (dense_w_gate, dense_w_up, dense_w_down) if l % 2 == 0 else (moe_w_gate, moe_w_up, moe_w_down)
        cast_plan += [(l - 1 if (l % 2 == 1 and j == 0) else l, (l, j), wj[l // 2]) for j, wj in enumerate(ws)]
    ffn_bf = {}

    xp = x_prompt.reshape(t, d)
    xs = x_sample.reshape(nd, d)
    k_p, v_p, g_p, k_s, v_s, g_s = [], [], [], [], [], []
    for l in range(depth):
        i = l // 2
        n1 = norm1_w[l][None, :]
        n2 = norm2_w[l][None, :]
        qn = jnp.tile(q_norm_w[l], N_HEADS)[None, :]
        kn = jnp.tile(k_norm_w[l], N_KV_HEADS)[None, :]
        gmn = gm_norm_w[l][None, :]
        router = None if l % 2 == 0 else (router_w_pad[i], router_b_pad[i][None, :])

        pending = [c for c in cast_plan if c[0] == l]
        res = _inproj_p(l, xp, mod_p[l], n1, w_in_bf, qn, kn, gmn, seg, cos_p, sin_p,
                        [wt.reshape(-1, wt.shape[-1]) for _, _, wt in pending], batch, seq)
        q, kd, vd, u, gv, kl, vl, gvl = res[:8]
        for (_, name, wt), c in zip(pending, res[8:]):
            ffn_bf[name] = c.reshape(wt.shape)
        wg_bf, wu_bf, wd_bf = (ffn_bf[(l, j)] for j in range(3))
        mix = _mix_p(q, kd, vd, u, gv, gm_ws[l], gm_bs[l].T, attn_sinks[l], batch, seq)
        if router is None:
            xp = _outffn_p(l, mix, xp, mod_p[l], w_out_bf, n2, wg_bf, wu_bf, wd_bf, batch, seq)
        else:
            xn, h2, route, route_t, cnt = _outproj_p(l, mix, xp, mod_p[l], w_out_bf, n2, *router, batch, seq)
            xp = _moe_routed(h2, xn, route, route_t, cnt, mod_p[l], wg_bf, wu_bf, wd_bf, seq)
        k_p.append(kl.reshape(batch, WINDOW, N_KV_HEADS, HEAD_DIM))
        v_p.append(vl.reshape(batch, WINDOW, N_KV_HEADS, HEAD_DIM))
        g_p.append(gvl)

        q, k, v, u, gv = _inproj_s(l, xs, mod_s[l], n1, w_in, qn, kn, gmn, seg, cos_s, sin_s)
        qg = jnp.pad(q.reshape(pairs, KV_GROUP, HEAD_DIM), ((0, 0), (0, SUB - KV_GROUP), (0, 0)))
        sink = jnp.pad(jnp.tile(attn_sinks[l].reshape(N_KV_HEADS, KV_GROUP), (nd, 1)),
                       ((0, 0), (0, SUB - KV_GROUP)))[:, :, None]
        per_step = lambda a: a.T.reshape(KV_WIDTH, -1, ATTN_S_PAIRS // N_KV_HEADS).transpose(1, 0, 2)
        o, nk, nv = _attn_s(l, qg, per_step(k), per_step(v), ck_t, cv_t, sink)
        o = o[:, :KV_GROUP, :].reshape(nd, ATTN_WIDTH)
        wdiag = jnp.repeat(gm_ws[l][:, 0, 0], GM_WIDTH // GM_GROUPS)[None, :]
        bsrow = jnp.repeat(gm_bs[l][:, 0], GM_WIDTH // GM_GROUPS)[None, :]
        res = _outproj_s(l, o, u, gv, wdiag, bsrow, xs, mod_s[l], w_out, n2, router)
        if router is None:
            xs = _ffn_s(res[1], res[0], mod_s[l], dense_w_gate[i], dense_w_up[i], dense_w_down[i])
        else:
            xs = _moe_s(res[1], res[0], res[2], mod_s[l], wg_bf, wu_bf, wd_bf)
        k_s.append(nk)
        v_s.append(nv)
        g_s.append(gv[:, None, :])

    return (xp.reshape(batch, seq, d), xs.reshape(nd, 1, d), jnp.stack(k_p), jnp.stack(v_p), jnp.stack(g_p),
            uncache(k_s), uncache(v_s), jnp.stack(g_s))
```
